```python
import math
import jax, jax.numpy as jnp
from jax import lax
import numpy as np

D_MODEL = 2048
BATCH = 2
SEQ = 4096
DEPTH = 1
DEC_BATCH = 8
DEC_SEQ = 2048
PAST_LEN = 128

FNET_GROUPS = 4
FNET_GROUP_DIM = D_MODEL // 8
FNET_WIDTH = FNET_GROUPS * FNET_GROUP_DIM
GLA_HEADS = 4
GLA_KEY_DIM = D_MODEL // 2
GLA_VALUE_DIM = D_MODEL
GLA_HEAD_K = GLA_KEY_DIM // GLA_HEADS
GLA_HEAD_V = GLA_VALUE_DIM // GLA_HEADS
GATE_LOW_RANK = 16
GATE_LOGIT_NORMALIZER = 16.0
CHUNK = 64
N_BRANCHES = 2
IN_SIZES = (FNET_WIDTH, GLA_KEY_DIM, GLA_KEY_DIM, GLA_VALUE_DIM, GLA_VALUE_DIM,
            GATE_LOW_RANK, GATE_LOW_RANK, N_BRANCHES * D_MODEL)
IN_SPLITS = tuple(int(s) for s in np.cumsum(IN_SIZES)[:-1])
D_IN = int(sum(IN_SIZES))
N_EXPERTS = 32
TOP_K = 4
D_EXPERT = D_MODEL
SWIGLU_LIMIT = 7.0
SWIGLU_ALPHA = 1.702
EPS = 1e-5

kernel_name = 'fnet_gla_gated_moe_encoder'


def rmsnorm(x, gain):
    xf = x.astype(jnp.float32)
    var = jnp.mean(xf * xf, axis=-1, keepdims=True)
    return (xf * lax.rsqrt(var + EPS) * gain.astype(jnp.float32)).astype(x.dtype)


def gla_chunked(q, k, v, g, strict):
    B, H, S, DK = q.shape
    DV = v.shape[-1]
    N = S // CHUNK
    q = q.reshape(B, H, N, CHUNK, DK)
    k = k.reshape(B, H, N, CHUNK, DK)
    v = v.reshape(B, H, N, CHUNK, DV)
    g = g.reshape(B, H, N, CHUNK, DK)
    gc = jnp.cumsum(g, axis=3)
    g_ref = gc[:, :, :, CHUNK // 2 - 1:CHUNK // 2, :]
    q_in = q * jnp.exp(gc - g_ref)
    k_in = k * jnp.exp(g_ref - gc)
    scores = jnp.einsum('bhnid,bhnjd->bhnij', q_in, k_in)
    pos = jnp.arange(CHUNK)
    mask = (pos[:, None] > pos[None, :]) if strict else (pos[:, None] >= pos[None, :])
    scores = jnp.where(mask, scores, 0.0)
    o_intra = jnp.einsum('bhnij,bhnjv->bhniv', scores, v)
    g_tot = gc[:, :, :, -1, :]
    q_inter = q * jnp.exp(gc)
    k_inter = k * jnp.exp(g_tot[:, :, :, None, :] - gc)

    def step(state, xs):
        qc, kc, vc, gt = xs
        o = jnp.einsum('bhid,bhdv->bhiv', qc, state)
        state = state * jnp.exp(gt)[..., None] + jnp.einsum('bhjd,bhjv->bhdv', kc, vc)
        return state, o

    xs = (jnp.moveaxis(q_inter, 2, 0), jnp.moveaxis(k_inter, 2, 0),
          jnp.moveaxis(v, 2, 0), jnp.moveaxis(g_tot, 2, 0))
    _, o_inter = lax.scan(step, jnp.zeros((B, H, DK, DV), jnp.float32), xs)
    o_inter = jnp.moveaxis(o_inter, 0, 2)
    return (o_intra + o_inter).reshape(B, H, S, DV)


def to_heads(t, dh):
    B, S, _ = t.shape
    return t.reshape(B, S, GLA_HEADS, dh).transpose(0, 2, 1, 3).astype(jnp.float32)


def token_mixer(xn, w_in, w_gk_up_fwd, b_gk_fwd, w_gk_up_bwd, b_gk_bwd, gla_head_norm,
                w_fnet_out, w_gla_out, w_out):
    B, S, _ = xn.shape
    proj = xn @ w_in
    u_f, q, k, v, og, lr_f, lr_b, gate_logits = jnp.split(proj, IN_SPLITS, axis=-1)

    u = u_f.astype(jnp.float32).reshape(B, S, FNET_GROUPS, FNET_GROUP_DIM)
    u = jnp.fft.fft2(u, axes=(1, 3), norm='ortho').real
    y_a = u.reshape(B, S, FNET_WIDTH).astype(xn.dtype) @ w_fnet_out

    g_f = jax.nn.log_sigmoid((lr_f @ w_gk_up_fwd + b_gk_fwd).astype(jnp.float32)) / GATE_LOGIT_NORMALIZER
    g_b = jax.nn.log_sigmoid((lr_b @ w_gk_up_bwd + b_gk_bwd).astype(jnp.float32)) / GATE_LOGIT_NORMALIZER
    qh = to_heads(q, GLA_HEAD_K) * (GLA_HEAD_K ** -0.5)
    kh = to_heads(k, GLA_HEAD_K)
    vh = to_heads(v, GLA_HEAD_V)
    gfh = to_heads(g_f, GLA_HEAD_K)
    gbh = to_heads(g_b, GLA_HEAD_K)
    o_fwd = gla_chunked(qh, kh, vh, gfh, strict=False)
    o_bwd = jnp.flip(gla_chunked(jnp.flip(qh, 2), jnp.flip(kh, 2), jnp.flip(vh, 2),
                                 jnp.flip(gbh, 2), strict=True), 2)
    o = rmsnorm(o_fwd + o_bwd, gla_head_norm)
    o = o.transpose(0, 2, 1, 3).reshape(B, S, GLA_VALUE_DIM).astype(xn.dtype)
    y_b = (o * jax.nn.silu(og)) @ w_gla_out

    gates = jax.nn.sigmoid(gate_logits.astype(jnp.float32)).reshape(B, S, N_BRANCHES, D_MODEL)
    merged = (gates[:, :, 0] * y_a.astype(jnp.float32) + gates[:, :, 1] * y_b.astype(jnp.float32)).astype(xn.dtype)
    return merged @ w_out


def moe(xn, w_router, b_router, w_gate_up, b_gate_up, w_down, b_down):
    B, S, D = xn.shape
    xt = xn.reshape(B * S, D)
    logits = (xt @ w_router + b_router).astype(jnp.float32)
    top_vals, top_idx = lax.top_k(logits, TOP_K)
    top_w = jax.nn.softmax(top_vals, axis=-1)
    combine = jnp.sum(jax.nn.one_hot(top_idx, N_EXPERTS, dtype=jnp.float32) * top_w[..., None], axis=1)

    def expert_step(acc, xs):
        wgu, bgu, wd, bd, c = xs
        h = xt @ wgu + bgu
        gate = jnp.minimum(h[:, 0::2], SWIGLU_LIMIT)
        up = jnp.clip(h[:, 1::2], -SWIGLU_LIMIT, SWIGLU_LIMIT)
        act = gate * jax.nn.sigmoid(SWIGLU_ALPHA * gate) * (up + 1)
        out = act @ wd + bd
        return acc + c[:, None] * out.astype(jnp.float32), None

    acc, _ = lax.scan(expert_step, jnp.zeros((B * S, D), jnp.float32),
                      (w_gate_up, b_gate_up, w_down, b_down, combine.T))
    return acc.astype(xn.dtype).reshape(B, S, D)


def encoder_trunk(x, norm_mix, w_in, w_gk_up_fwd, b_gk_fwd, w_gk_up_bwd, b_gk_bwd, gla_head_norm,
                  w_fnet_out, w_gla_out, w_out, norm_ffn, w_router, b_router, w_gate_up, b_gate_up,
                  w_down, b_down, norm_final):
    for layer in range(DEPTH):
        x = x + token_mixer(rmsnorm(x, norm_mix[layer]), w_in[layer], w_gk_up_fwd[layer], b_gk_fwd[layer],
                            w_gk_up_bwd[layer], b_gk_bwd[layer], gla_head_norm[layer],
                            w_fnet_out[layer], w_gla_out[layer], w_out[layer])
        x = x + moe(rmsnorm(x, norm_ffn[layer]), w_router[layer], b_router[layer], w_gate_up[layer],
                    b_gate_up[layer], w_down[layer], b_down[layer])
    return rmsnorm(x, norm_final)


def setup_inputs(seed: int = 0) -> dict:
    key = jax.random.key(seed)
    ks = jax.random.split(key, 20)
    f32 = jnp.float32

    def nrm(k, shape, scale):
        return jax.random.normal(k, shape, f32) * scale

    L = DEPTH
    return {
        'x_prompt': nrm(ks[0], (BATCH, SEQ, D_MODEL), 1.0),
        'x_sample': nrm(ks[1], (DEC_BATCH, DEC_SEQ, D_MODEL), 1.0),
        'norm_mix': 1.0 + nrm(ks[2], (L, D_MODEL), 0.02),
        'w_in': nrm(ks[3], (L, D_MODEL, D_IN), D_MODEL ** -0.5),
        'w_gk_up_fwd': nrm(ks[4], (L, GATE_LOW_RANK, GLA_KEY_DIM), GATE_LOW_RANK ** -0.5),
        'b_gk_fwd': nrm(ks[5], (L, GLA_KEY_DIM), 0.02),
        'w_gk_up_bwd': nrm(ks[6], (L, GATE_LOW_RANK, GLA_KEY_DIM), GATE_LOW_RANK ** -0.5),
        'b_gk_bwd': nrm(ks[7], (L, GLA_KEY_DIM), 0.02),
        'gla_head_norm': 1.0 + nrm(ks[8], (L, GLA_HEAD_V), 0.02),
        'w_fnet_out': nrm(ks[9], (L, FNET_WIDTH, D_MODEL), FNET_WIDTH ** -0.5),
        'w_gla_out': nrm(ks[10], (L, GLA_VALUE_DIM, D_MODEL), GLA_VALUE_DIM ** -0.5),
        'w_out': nrm(ks[11], (L, D_MODEL, D_MODEL), D_MODEL ** -0.5),
        'norm_ffn': 1.0 + nrm(ks[12], (L, D_MODEL), 0.02),
        'w_router': nrm(ks[13], (L, D_MODEL, N_EXPERTS), D_MODEL ** -0.5),
        'b_router': nrm(ks[14], (L, N_EXPERTS), 0.01),
        'w_gate_up': nrm(ks[15], (L, N_EXPERTS, D_MODEL, 2 * D_EXPERT), D_MODEL ** -0.5),
        'b_gate_up': nrm(ks[16], (L, N_EXPERTS, 2 * D_EXPERT), 0.02),
        'w_down': nrm(ks[17], (L, N_EXPERTS, D_EXPERT, D_MODEL), D_EXPERT ** -0.5),
        'b_down': nrm(ks[18], (L, N_EXPERTS, D_MODEL), 0.02),
        'norm_final': 1.0 + nrm(ks[19], (D_MODEL,), 0.02),
    }


def reference(x_prompt, x_sample, norm_mix, w_in, w_gk_up_fwd, b_gk_fwd, w_gk_up_bwd, b_gk_bwd,
              gla_head_norm, w_fnet_out, w_gla_out, w_out, norm_ffn, w_router, b_router, w_gate_up,
              b_gate_up, w_down, b_down, norm_final):
    y_prompt = encoder_trunk(x_prompt, norm_mix, w_in, w_gk_up_fwd, b_gk_fwd, w_gk_up_bwd, b_gk_bwd,
                             gla_head_norm, w_fnet_out, w_gla_out, w_out, norm_ffn, w_router, b_router,
                             w_gate_up, b_gate_up, w_down, b_down, norm_final)
    y_sample = encoder_trunk(x_sample, norm_mix, w_in, w_gk_up_fwd, b_gk_fwd, w_gk_up_bwd, b_gk_bwd,
                             gla_head_norm, w_fnet_out, w_gla_out, w_out, norm_ffn, w_router, b_router,
                             w_gate_up, b_gate_up, w_down, b_down, norm_final)
    return (y_prompt, y_sample)
```

```python
import functools
import math

import numpy as np
import jax
import jax.numpy as jnp
from jax import lax
from jax.experimental import pallas as pl
from jax.experimental.pallas import tpu as pltpu

F32 = jnp.float32
BF16 = jnp.bfloat16
HIGHEST = lax.Precision.HIGHEST

EPS = 1e-5
FNET_GROUPS = 4
GLA_HEADS = 4
GATE_LOW_RANK = 16
GATE_LOGIT_NORMALIZER = 16.0
CHUNK = 64
TOP_K = 4
SWIGLU_LIMIT = 7.0
SWIGLU_ALPHA = 1.702

VMEM_LIMIT_BYTES = 56 * 1024 * 1024
GROUP_TILE = 256


def _cparams(sem):
    return pltpu.CompilerParams(dimension_semantics=sem, vmem_limit_bytes=VMEM_LIMIT_BYTES)


def _inproj_kernel(x_ref, g_ref, w_ref, wlr_ref, o_ref, lr_ref, xn_ref):
    @pl.when(pl.program_id(1) == 0)
    def _():
        x = x_ref[...]
        var = jnp.mean(x * x, axis=-1, keepdims=True)
        xn = x * lax.rsqrt(var + EPS) * g_ref[...]
        xn_ref[...] = xn.astype(BF16)
        lr_ref[...] = jnp.dot(xn, wlr_ref[...], precision=HIGHEST, preferred_element_type=F32)

    o_ref[...] = jnp.dot(xn_ref[...], w_ref[...], preferred_element_type=F32).astype(o_ref.dtype)


def _inproj(x, gain, w_main, w_lr, tm=512, tn=1024):
    T, D = x.shape
    N = w_main.shape[1]
    R = w_lr.shape[1]
    return pl.pallas_call(
        _inproj_kernel,
        grid=(T // tm, N // tn),
        in_specs=[
            pl.BlockSpec((tm, D), lambda i, j: (i, 0)),
            pl.BlockSpec((1, D), lambda i, j: (0, 0)),
            pl.BlockSpec((D, tn), lambda i, j: (0, j)),
            pl.BlockSpec((D, R), lambda i, j: (0, 0)),
        ],
        out_specs=[
            pl.BlockSpec((tm, tn), lambda i, j: (i, j)),
            pl.BlockSpec((tm, R), lambda i, j: (i, 0)),
        ],
        out_shape=[jax.ShapeDtypeStruct((T, N), BF16), jax.ShapeDtypeStruct((T, R), F32)],
        scratch_shapes=[pltpu.VMEM((tm, D), BF16)],
        compiler_params=_cparams(("parallel", "arbitrary")),
        name="inproj",
    )(x, gain.reshape(1, D), w_main, w_lr)


def _chan_dft_kernel(u_ref, cs_ref, zc_ref, zs_ref, *, gd):
    for g in range(FNET_GROUPS):
        r = jnp.dot(u_ref[:, g * gd:(g + 1) * gd], cs_ref[...], preferred_element_type=F32)
        zc_ref[:, g * gd:(g + 1) * gd] = r[:, :gd].astype(BF16)
        zs_ref[:, g * gd:(g + 1) * gd] = r[:, gd:].astype(BF16)


def _chan_dft(proj, u_col_block, width, cs, tm=512):
    T = proj.shape[0]
    gd = width // FNET_GROUPS
    return pl.pallas_call(
        functools.partial(_chan_dft_kernel, gd=gd),
        grid=(T // tm,),
        in_specs=[
            pl.BlockSpec((tm, width), lambda i: (i, u_col_block)),
            pl.BlockSpec((gd, 2 * gd), lambda i: (0, 0)),
        ],
        out_specs=[pl.BlockSpec((tm, width), lambda i: (i, 0))] * 2,
        out_shape=[jax.ShapeDtypeStruct((T, width), BF16)] * 2,
        compiler_params=_cparams(("parallel",)),
        name="chan_dft",
    )(proj, cs)


def _seq_dft_kernel(c_ref, s_ref, zc_ref, zs_ref, o_ref, acc_ref):
    k = pl.program_id(2)

    @pl.when(k == 0)
    def _():
        acc_ref[...] = jnp.zeros_like(acc_ref)

    acc_ref[...] += (jnp.dot(c_ref[...], zc_ref[...], preferred_element_type=F32)
                     + jnp.dot(s_ref[...], zs_ref[...], preferred_element_type=F32))

    @pl.when(k == pl.num_programs(2) - 1)
    def _():
        o_ref[...] = acc_ref[...].astype(o_ref.dtype)


def _seq_dft(zc, zs, cmat, nsmat, row0, B, S, tm=512, tk=512):
    W = zc.shape[1]
    tm, tk = min(tm, S), min(tk, S)
    nm, nk = S // tm, S // tk
    kb0 = row0 // tk
    return pl.pallas_call(
        _seq_dft_kernel,
        grid=(B, nm, nk),
        in_specs=[
            pl.BlockSpec((tm, tk), lambda b, i, k: (i, k)),
            pl.BlockSpec((tm, tk), lambda b, i, k: (i, k)),
            pl.BlockSpec((tk, W), lambda b, i, k: (kb0 + b * nk + k, 0)),
            pl.BlockSpec((tk, W), lambda b, i, k: (kb0 + b * nk + k, 0)),
        ],
        out_specs=pl.BlockSpec((tm, W), lambda b, i, k: (b * nm + i, 0)),
        out_shape=jax.ShapeDtypeStruct((B * S, W), BF16),
        scratch_shapes=[pltpu.VMEM((tm, W), F32)],
        compiler_params=_cparams(("parallel", "parallel", "arbitrary")),
        name="seq_dft",
    )(cmat, nsmat, zc, zs)


def _dft_mats(n, scale):
    idx = jnp.arange(n, dtype=jnp.int32)
    ph = (idx[:, None] * idx[None, :]) % n
    ang = ph.astype(F32) * (2.0 * math.pi / n)
    return jnp.cos(ang) * scale, jnp.sin(ang) * scale


def _gla_kernel(q_ref, k_ref, v_ref, lr_ref, wup_ref, b_ref, o_ref, st_ref, *, reverse, nchunk, qscale):
    @pl.when(pl.program_id(2) == 0)
    def _():
        st_ref[...] = jnp.zeros_like(st_ref)

    row = lax.broadcasted_iota(jnp.int32, (CHUNK, CHUNK), 0)
    col = lax.broadcasted_iota(jnp.int32, (CHUNK, CHUNK), 1)
    if reverse:
        cum_mat = (col >= row).astype(F32)
        mask = col > row
        ref_row, tot_row = CHUNK // 2, 0
    else:
        cum_mat = (col <= row).astype(F32)
        mask = col <= row
        ref_row, tot_row = CHUNK // 2 - 1, CHUNK - 1

    order = range(nchunk - 1, -1, -1) if reverse else range(nchunk)
    for c in order:
        rows = pl.ds(c * CHUNK, CHUNK)
        z = jnp.dot(lr_ref[rows, :], wup_ref[...], precision=HIGHEST, preferred_element_type=F32) + b_ref[...]
        g = (jnp.minimum(z, 0.0) - jnp.log1p(jnp.exp(-jnp.abs(z)))) * (1.0 / GATE_LOGIT_NORMALIZER)
        gc = jnp.dot(cum_mat, g, precision=HIGHEST, preferred_element_type=F32)
        gref = gc[ref_row:ref_row + 1, :]
        gtot = gc[tot_row:tot_row + 1, :]
        q = q_ref[rows, :].astype(F32) * qscale
        k = k_ref[rows, :].astype(F32)
        v = v_ref[rows, :]
        q_in = (q * jnp.exp(gc - gref)).astype(BF16)
        k_in = (k * jnp.exp(gref - gc)).astype(BF16)
        s = lax.dot_general(q_in, k_in, (((1,), (1,)), ((), ())), preferred_element_type=F32)
        s = jnp.where(mask, s, 0.0).astype(BF16)
        o = jnp.dot(s, v, preferred_element_type=F32)
        q_it = (q * jnp.exp(gc)).astype(BF16)
        k_it = (k * jnp.exp(gtot - gc)).astype(BF16)
        st = st_ref[...]
        o = o + lax.dot_general(q_it, st.astype(BF16), (((1,), (1,)), ((), ())), preferred_element_type=F32)
        o_ref[rows, :] = o
        st_ref[...] = st * jnp.exp(gtot) + lax.dot_general(
            v, k_it, (((0,), (0,)), ((), ())), preferred_element_type=F32)


def _gla_dir(proj, lr, wup, bias, row0, B, S, q_blk0, k_blk0, v_blk0, dk, dv, reverse, rows=256):
    nb = S // rows
    rb0 = row0 // rows

    def rmap(b, n):
        return rb0 + b * nb + ((nb - 1 - n) if reverse else n)

    def omap(b, n):
        return b * nb + ((nb - 1 - n) if reverse else n)

    kern = functools.partial(_gla_kernel, reverse=reverse, nchunk=rows // CHUNK, qscale=dk ** -0.5)
    return pl.pallas_call(
        kern,
        grid=(B, GLA_HEADS, nb),
        in_specs=[
            pl.BlockSpec((rows, dk), lambda b, h, n: (rmap(b, n), q_blk0 + h)),
            pl.BlockSpec((rows, dk), lambda b, h, n: (rmap(b, n), k_blk0 + h)),
            pl.BlockSpec((rows, dv), lambda b, h, n: (rmap(b, n), v_blk0 + h)),
            pl.BlockSpec((rows, GATE_LOW_RANK), lambda b, h, n: (rmap(b, n), 0)),
            pl.BlockSpec((GATE_LOW_RANK, dk), lambda b, h, n: (0, h)),
            pl.BlockSpec((1, dk), lambda b, h, n: (0, h)),
        ],
        out_specs=pl.BlockSpec((rows, dv), lambda b, h, n: (omap(b, n), h)),
        out_shape=jax.ShapeDtypeStruct((B * S, GLA_HEADS * dv), F32),
        scratch_shapes=[pltpu.VMEM((dv, dk), F32)],
        compiler_params=_cparams(("parallel", "parallel", "arbitrary")),
        name="gla_bwd" if reverse else "gla_fwd",
    )(proj, proj, proj, lr, wup, bias.reshape(1, -1))


def _merge_kernel(fft_ref, of_ref, ob_ref, og_ref, g0_ref, g1_ref, hn_ref, wf_ref, wg_ref, o_ref, a_ref, *, dv):
    ya = jnp.dot(fft_ref[...], wf_ref[...], preferred_element_type=F32)
    for h in range(GLA_HEADS):
        cs = slice(h * dv, (h + 1) * dv)
        o = of_ref[:, cs] + ob_ref[:, cs]
        var = jnp.mean(o * o, axis=-1, keepdims=True)
        on = o * lax.rsqrt(var + EPS) * hn_ref[...]
        og = og_ref[:, cs].astype(F32)
        a_ref[:, cs] = (on * (og * jax.nn.sigmoid(og))).astype(BF16)
    yb = jnp.dot(a_ref[...], wg_ref[...], preferred_element_type=F32)
    m = jax.nn.sigmoid(g0_ref[...].astype(F32)) * ya + jax.nn.sigmoid(g1_ref[...].astype(F32)) * yb
    o_ref[...] = m.astype(BF16)


def _merge(fft, o_f, o_b, proj, og_blk, g0_blk, g1_blk, hn, wf, wg, tm=256):
    T, D = o_f.shape
    FW = fft.shape[1]
    dv = D // GLA_HEADS
    const = dict(pipeline_mode=pl.Buffered(1))
    return pl.pallas_call(
        functools.partial(_merge_kernel, dv=dv),
        grid=(T // tm,),
        in_specs=[
            pl.BlockSpec((tm, FW), lambda i: (i, 0)),
            pl.BlockSpec((tm, D), lambda i: (i, 0)),
            pl.BlockSpec((tm, D), lambda i: (i, 0)),
            pl.BlockSpec((tm, D), lambda i: (i, og_blk)),
            pl.BlockSpec((tm, D), lambda i: (i, g0_blk)),
            pl.BlockSpec((tm, D), lambda i: (i, g1_blk)),
            pl.BlockSpec((1, dv), lambda i: (0, 0)),
            pl.BlockSpec((FW, D), lambda i: (0, 0), **const),
            pl.BlockSpec((D, D), lambda i: (0, 0), **const),
        ],
        out_specs=pl.BlockSpec((tm, D), lambda i: (i, 0)),
        out_shape=jax.ShapeDtypeStruct((T, D), BF16),
        scratch_shapes=[pltpu.VMEM((tm, D), BF16)],
        compiler_params=_cparams(("parallel",)),
        name="merge",
    )(fft, o_f, o_b, proj, proj, proj, hn.reshape(1, dv), wf, wg)


def _outproj_router_kernel(m_ref, x_ref, wo_ref, g_ref, wr_ref, br_ref,
                           x1_ref, xn_ref, idx_ref, tw_ref, rank_ref, cnt_ref, run_ref, *, n_exp):
    i = pl.program_id(0)

    @pl.when(i == 0)
    def _():
        run_ref[...] = jnp.zeros_like(run_ref)

    tm = m_ref.shape[0]
    x1 = x_ref[...] + jnp.dot(m_ref[...], wo_ref[...], preferred_element_type=F32)
    x1_ref[...] = x1
    var = jnp.mean(x1 * x1, axis=-1, keepdims=True)
    xn = x1 * lax.rsqrt(var + EPS) * g_ref[...]
    xn_ref[...] = xn
    lg = jnp.dot(xn, wr_ref[...], precision=HIGHEST, preferred_element_type=F32) + br_ref[...]

    lane = lax.broadcasted_iota(jnp.int32, (tm, n_exp), 1)
    vals, hots = [], []
    for _ in range(TOP_K):
        mx = jnp.max(lg, axis=-1, keepdims=True)
        ik = jnp.min(jnp.where(lg == mx, lane, n_exp), axis=-1, keepdims=True)
        hot = lane == ik
        vals.append(mx)
        hots.append(hot)
        lg = jnp.where(hot, -jnp.inf, lg)
    exps = [jnp.exp(v - vals[0]) for v in vals]
    denom = exps[0] + exps[1] + exps[2] + exps[3]

    sel = hots[0] | hots[1] | hots[2] | hots[3]
    sel_f = sel.astype(F32)
    r = lax.broadcasted_iota(jnp.int32, (tm, tm), 0)
    c = lax.broadcasted_iota(jnp.int32, (tm, tm), 1)
    strict = (c < r).astype(BF16)
    before = jnp.dot(strict, sel_f.astype(BF16), preferred_element_type=F32) + run_ref[...]
    run_ref[...] += jnp.sum(sel_f, axis=0, keepdims=True)
    cnt_ref[...] = run_ref[...].astype(jnp.int32)

    k4 = lax.broadcasted_iota(jnp.int32, (tm, TOP_K), 1)
    idx4 = jnp.zeros((tm, TOP_K), jnp.int32)
    w4 = jnp.zeros((tm, TOP_K), F32)
    rk4 = jnp.zeros((tm, TOP_K), jnp.int32)
    for k in range(TOP_K):
        ik = jnp.sum(jnp.where(hots[k], lane, 0), axis=-1, keepdims=True)
        rk = jnp.sum(jnp.where(hots[k], before, 0.0), axis=-1, keepdims=True).astype(jnp.int32)
        idx4 = jnp.where(k4 == k, ik, idx4)
        w4 = jnp.where(k4 == k, exps[k] / denom, w4)
        rk4 = jnp.where(k4 == k, rk, rk4)
    idx_ref[...] = idx4
    tw_ref[...] = w4
    rank_ref[...] = rk4


def _outproj_router(merged, x, wo, gain, wr, br, tm=256):
    T, D = x.shape
    E = wr.shape[1]
    const = dict(pipeline_mode=pl.Buffered(1))
    row = lambda i: (i, 0)
    fix = lambda i: (0, 0)
    return pl.pallas_call(
        functools.partial(_outproj_router_kernel, n_exp=E),
        grid=(T // tm,),
        in_specs=[
            pl.BlockSpec((tm, D), row),
            pl.BlockSpec((tm, D), row),
            pl.BlockSpec((D, D), fix, **const),
            pl.BlockSpec((1, D), fix),
            pl.BlockSpec((D, E), fix),
            pl.BlockSpec((1, E), fix),
        ],
        out_specs=[
            pl.BlockSpec((tm, D), row),
            pl.BlockSpec((tm, D), row),
            pl.BlockSpec((tm, TOP_K), row),
            pl.BlockSpec((tm, TOP_K), row),
            pl.BlockSpec((tm, TOP_K), row),
            pl.BlockSpec((1, E), fix),
        ],
        out_shape=[
            jax.ShapeDtypeStruct((T, D), F32),
            jax.ShapeDtypeStruct((T, D), F32),
            jax.ShapeDtypeStruct((T, TOP_K), jnp.int32),
            jax.ShapeDtypeStruct((T, TOP_K), F32),
            jax.ShapeDtypeStruct((T, TOP_K), jnp.int32),
            jax.ShapeDtypeStruct((1, E), jnp.int32),
        ],
        scratch_shapes=[pltpu.VMEM((1, E), F32)],
        compiler_params=_cparams(("arbitrary",)),
        name="outproj_router",
    )(merged, x, wo, gain.reshape(1, D), wr, br.reshape(1, E))


def _row_copy(src, dst, sem):
    return pltpu.make_async_copy(src, dst, sem)


def _dispatch_kernel(pos_ref, x_ref, zero_ref, xs_ref, sem):
    del zero_ref
    tm = x_ref.shape[0]

    def body(t, carry):
        for k in range(TOP_K):
            p = pos_ref[t * TOP_K + k]
            _row_copy(x_ref.at[pl.ds(t, 1), :], xs_ref.at[pl.ds(p, 1), :], sem).start()
        return carry

    lax.fori_loop(0, tm, body, 0)
    for _ in range(TOP_K):
        _row_copy(x_ref, xs_ref.at[pl.ds(0, tm), :], sem).wait()


def _dispatch(xn, pos_flat, n_slots, tm=256):
    T, D = xn.shape
    zeros = jnp.zeros((n_slots, D), xn.dtype)
    return pl.pallas_call(
        _dispatch_kernel,
        grid=(T // tm,),
        in_specs=[
            pl.BlockSpec((tm * TOP_K,), lambda i: (i,), memory_space=pltpu.SMEM),
            pl.BlockSpec((tm, D), lambda i: (i, 0)),
            pl.BlockSpec(memory_space=pl.ANY),
        ],
        out_specs=pl.BlockSpec(memory_space=pl.ANY),
        out_shape=jax.ShapeDtypeStruct((n_slots, D), xn.dtype),
        scratch_shapes=[pltpu.SemaphoreType.DMA(())],
        input_output_aliases={2: 0},
        compiler_params=_cparams(("arbitrary",)),
        name="dispatch",
    )(pos_flat, xn, zeros)


def _combine_kernel(pos_ref, tw_ref, x1_ref, g_ref, ys_ref, o_ref, buf_ref, sem):
    tm = x1_ref.shape[0]

    def body(t, carry):
        for k in range(TOP_K):
            p = pos_ref[t * TOP_K + k]
            _row_copy(ys_ref.at[pl.ds(p, 1), :], buf_ref.at[k, pl.ds(t, 1), :], sem).start()
        return carry

    lax.fori_loop(0, tm, body, 0)
    for k in range(TOP_K):
        _row_copy(ys_ref.at[pl.ds(0, tm), :], buf_ref.at[k], sem).wait()

    tw = tw_ref[...]
    acc = jnp.zeros(x1_ref.shape, F32)
    for k in range(TOP_K):
        acc = acc + tw[:, k:k + 1] * buf_ref[k]
    x2 = x1_ref[...] + acc
    var = jnp.mean(x2 * x2, axis=-1, keepdims=True)
    o_ref[...] = x2 * lax.rsqrt(var + EPS) * g_ref[...]


def _combine(pos_flat, tw, x1, gain, ys, tm=256):
    T, D = x1.shape
    return pl.pallas_call(
        _combine_kernel,
        grid=(T // tm,),
        in_specs=[
            pl.BlockSpec((tm * TOP_K,), lambda i: (i,), memory_space=pltpu.SMEM),
            pl.BlockSpec((tm, TOP_K), lambda i: (i, 0)),
            pl.BlockSpec((tm, D), lambda i: (i, 0)),
            pl.BlockSpec((1, D), lambda i: (0, 0)),
            pl.BlockSpec(memory_space=pl.ANY),
        ],
        out_specs=pl.BlockSpec((tm, D), lambda i: (i, 0)),
        out_shape=jax.ShapeDtypeStruct((T, D), F32),
        scratch_shapes=[pltpu.VMEM((TOP_K, tm, D), F32), pltpu.SemaphoreType.DMA(())],
        compiler_params=_cparams(("arbitrary",)),
        name="combine",
    )(pos_flat, tw, x1, gain.reshape(1, D), ys)


def _gate_up_kernel(te_ref, nu_ref, x_ref, wg_ref, wu_ref, bg_ref, bu_ref, o_ref):
    @pl.when(pl.program_id(1) < nu_ref[0])
    def _():
        x = x_ref[...].astype(BF16)
        hg = jnp.dot(x, wg_ref[0], preferred_element_type=F32) + bg_ref[0]
        hu = jnp.dot(x, wu_ref[0], preferred_element_type=F32) + bu_ref[0]
        gate = jnp.minimum(hg, SWIGLU_LIMIT)
        up = jnp.clip(hu, -SWIGLU_LIMIT, SWIGLU_LIMIT)
        o_ref[...] = (gate * jax.nn.sigmoid(SWIGLU_ALPHA * gate) * (up + 1.0)).astype(o_ref.dtype)


def _gate_up(tile_exp, n_used, xs, wg, wu, bg, bu, tn=1024):
    P, D = xs.shape
    E, _, H = wg.shape
    tm = GROUP_TILE
    nt = P // tm

    def xmap(j, i, te, nu):
        return (jnp.minimum(i, nu[0] - 1), 0)

    def wmap(j, i, te, nu):
        return (te[i], 0, j)

    def omap(j, i, te, nu):
        return (jnp.minimum(i, nu[0] - 1), j)

    gs = pltpu.PrefetchScalarGridSpec(
        num_scalar_prefetch=2,
        grid=(H // tn, nt),
        in_specs=[
            pl.BlockSpec((tm, D), xmap),
            pl.BlockSpec((1, D, tn), wmap),
            pl.BlockSpec((1, D, tn), wmap),
            pl.BlockSpec((1, 1, tn), wmap),
            pl.BlockSpec((1, 1, tn), wmap),
        ],
        out_specs=pl.BlockSpec((tm, tn), omap),
    )
    return pl.pallas_call(
        _gate_up_kernel,
        grid_spec=gs,
        out_shape=jax.ShapeDtypeStruct((P, H), BF16),
        compiler_params=_cparams(("arbitrary", "arbitrary")),
        name="moe_gate_up",
    )(tile_exp, n_used, xs, wg, wu, bg, bu)


def _down_kernel(te_ref, nu_ref, a_ref, wd_ref, bd_ref, o_ref):
    @pl.when(pl.program_id(1) < nu_ref[0])
    def _():
        o_ref[...] = jnp.dot(a_ref[...], wd_ref[0], preferred_element_type=F32) + bd_ref[0]


def _down(tile_exp, n_used, act, wd, bd, tn=1024):
    P, H = act.shape
    E, _, D = wd.shape
    tm = GROUP_TILE
    nt = P // tm

    def amap(j, i, te, nu):
        return (jnp.minimum(i, nu[0] - 1), 0)

    def wmap(j, i, te, nu):
        return (te[i], 0, j)

    def omap(j, i, te, nu):
        return (jnp.minimum(i, nu[0] - 1), j)

    gs = pltpu.PrefetchScalarGridSpec(
        num_scalar_prefetch=2,
        grid=(D // tn, nt),
        in_specs=[
            pl.BlockSpec((tm, H), amap),
            pl.BlockSpec((1, H, tn), wmap),
            pl.BlockSpec((1, 1, tn), wmap),
        ],
        out_specs=pl.BlockSpec((tm, tn), omap),
    )
    return pl.pallas_call(
        _down_kernel,
        grid_spec=gs,
        out_shape=jax.ShapeDtypeStruct((P, D), F32),
        compiler_params=_cparams(("arbitrary", "arbitrary")),
        name="moe_down",
    )(tile_exp, n_used, act, wd, bd)


def _trunk(xs, seq_shapes, norm_mix, w_in, w_gk_up_fwd, b_gk_fwd, w_gk_up_bwd, b_gk_bwd, gla_head_norm,
           w_fnet_out, w_gla_out, w_out, norm_ffn, w_router, b_router, w_gate_up, b_gate_up,
           w_down, b_down, norm_final):
    T, D = xs.shape
    fw = w_fnet_out.shape[0]
    dkk = w_gk_up_fwd.shape[1]
    dvv = w_gla_out.shape[0]
    dk, dv = dkk // GLA_HEADS, dvv // GLA_HEADS
    sizes = (fw, dkk, dkk, dvv, dvv, GATE_LOW_RANK, GATE_LOW_RANK, 2 * D)
    offs = np.concatenate([[0], np.cumsum(sizes)])
    sl = lambda n: slice(int(offs[n]), int(offs[n + 1]))
    w_u, w_q, w_k, w_v, w_og = (w_in[:, sl(n)] for n in range(5))
    w_lr = w_in[:, int(offs[5]):int(offs[7])]
    w_g = w_in[:, sl(7)]
    w_main = jnp.concatenate([w_og, w_g, w_v, w_u, w_q, w_k], axis=1).astype(BF16)
    og_blk, g0_blk, g1_blk = 0, dvv // D, dvv // D + 1
    v_off = dvv + 2 * D
    u_off = v_off + dvv
    q_off = u_off + fw
    k_off = q_off + dkk

    proj, lr = _inproj(xs, norm_mix, w_main, w_lr)
    lr_f, lr_b = lr[:, :GATE_LOW_RANK], lr[:, GATE_LOW_RANK:]

    gd = fw // FNET_GROUPS
    cc, sc = _dft_mats(gd, gd ** -0.5)
    cs = jnp.concatenate([cc, sc], axis=1).astype(BF16)
    zc, zs = _chan_dft(proj, u_off // fw, fw, cs)
    ffts, ofs, obs = [], [], []
    row0 = 0
    for (B, S) in seq_shapes:
        cm, sm = _dft_mats(S, S ** -0.5)
        ffts.append(_seq_dft(zc, zs, cm.astype(BF16), (-sm).astype(BF16), row0, B, S))
        args = (row0, B, S, q_off // dk, k_off // dk, v_off // dv, dk, dv)
        ofs.append(_gla_dir(proj, lr_f, w_gk_up_fwd, b_gk_fwd, *args, reverse=False))
        obs.append(_gla_dir(proj, lr_b, w_gk_up_bwd, b_gk_bwd, *args, reverse=True))
        row0 += B * S
    fft = jnp.concatenate(ffts, axis=0)
    o_f = jnp.concatenate(ofs, axis=0)
    o_b = jnp.concatenate(obs, axis=0)

    merged = _merge(fft, o_f, o_b, proj, og_blk, g0_blk, g1_blk, gla_head_norm,
                    w_fnet_out.astype(BF16), w_gla_out.astype(BF16))
    x1, xn2, idx, tw, rank, cnt = _outproj_router(merged, xs, w_out.astype(BF16), norm_ffn, w_router, b_router)

    E = w_router.shape[1]
    cnt = cnt.reshape(E)
    gsz = ((cnt + GROUP_TILE - 1) // GROUP_TILE) * GROUP_TILE
    gend = jnp.cumsum(gsz)
    gstart = gend - gsz
    pos = (gstart[idx] + rank).reshape(-1).astype(jnp.int32)
    n_slots = T * TOP_K + E * GROUP_TILE
    nt = n_slots // GROUP_TILE
    tile_exp = jnp.minimum(
        jnp.searchsorted(gend, jnp.arange(nt, dtype=jnp.int32) * GROUP_TILE, side="right"), E - 1).astype(jnp.int32)
    n_used = (gend[-1:] // GROUP_TILE).astype(jnp.int32)

    x_sorted = _dispatch(xn2, pos, n_slots)
    H = w_down.shape[1]
    wg = w_gate_up[:, :, 0::2].astype(BF16)
    wu = w_gate_up[:, :, 1::2].astype(BF16)
    bg = b_gate_up[:, 0::2].reshape(E, 1, H)
    bu = b_gate_up[:, 1::2].reshape(E, 1, H)
    act = _gate_up(tile_exp, n_used, x_sorted, wg, wu, bg, bu)
    ys = _down(tile_exp, n_used, act, w_down.astype(BF16), b_down.reshape(E, 1, D))
    return _combine(pos, tw, x1, norm_final, ys)


def kernel(x_prompt, x_sample, norm_mix, w_in, w_gk_up_fwd, b_gk_fwd, w_gk_up_bwd, b_gk_bwd, gla_head_norm,
           w_fnet_out, w_gla_out, w_out, norm_ffn, w_router, b_router, w_gate_up, b_gate_up, w_down,
           b_down, norm_final):
    D = x_prompt.shape[-1]
    shapes = (x_prompt.shape[:2], x_sample.shape[:2])
    xs = jnp.concatenate([x_prompt.reshape(-1, D), x_sample.reshape(-1, D)], axis=0)
    y = _trunk(xs, shapes, norm_mix[0], w_in[0], w_gk_up_fwd[0], b_gk_fwd[0], w_gk_up_bwd[0], b_gk_bwd[0],
               gla_head_norm[0], w_fnet_out[0], w_gla_out[0], w_out[0], norm_ffn[0], w_router[0],
               b_router[0], w_gate_up[0], b_gate_up[0], w_down[0], b_down[0], norm_final)
    n0 = x_prompt.shape[0] * x_prompt.shape[1]
    return (y[:n0].reshape(x_prompt.shape), y[n0:].reshape(x_sample.shape))
```

```python
import functools
import math

import numpy as np
import jax
import jax.numpy as jnp
from jax import lax
from jax.experimental import pallas as pl
from jax.experimental.pallas import tpu as pltpu

F32 = jnp.float32
BF16 = jnp.bfloat16
HIGHEST = lax.Precision.HIGHEST

EPS = 1e-5
FNET_GROUPS = 4
GLA_HEADS = 4
GATE_LOW_RANK = 16
GATE_LOGIT_NORMALIZER = 16.0
CHUNK = 64
TOP_K = 4
SWIGLU_LIMIT = 7.0
SWIGLU_ALPHA = 1.702

VMEM_LIMIT_BYTES = 56 * 1024 * 1024
MXU_DIM = 256
LANES = 128
GROUP_TILE = 512


def _cparams(sem):
    return pltpu.CompilerParams(dimension_semantics=sem, vmem_limit_bytes=VMEM_LIMIT_BYTES)


def _split3(x):
    hi = x.astype(BF16)
    r = x - hi.astype(F32)
    mid = r.astype(BF16)
    lo = (r - mid.astype(F32)).astype(BF16)
    return hi, mid, lo


def _inproj_kernel(x_ref, g_ref, w_ref, wlr_ref, o_ref, lr_ref, xn_ref):
    @pl.when(pl.program_id(1) == 0)
    def _():
        x = x_ref[...]
        var = jnp.mean(x * x, axis=-1, keepdims=True)
        xn = x * lax.rsqrt(var + EPS) * g_ref[...]
        xn_ref[...] = xn.astype(BF16)
        lr_ref[...] = jnp.dot(xn, wlr_ref[...], precision=HIGHEST, preferred_element_type=F32)

    o_ref[...] = jnp.dot(xn_ref[...], w_ref[...], preferred_element_type=F32).astype(o_ref.dtype)


def _inproj(x, gain, w_main, w_lr, tm=512, tn=1024):
    T, D = x.shape
    N = w_main.shape[1]
    R = w_lr.shape[1]
    return pl.pallas_call(
        _inproj_kernel,
        grid=(T // tm, N // tn),
        in_specs=[
            pl.BlockSpec((tm, D), lambda i, j: (i, 0)),
            pl.BlockSpec((1, D), lambda i, j: (0, 0)),
            pl.BlockSpec((D, tn), lambda i, j: (0, j)),
            pl.BlockSpec((D, R), lambda i, j: (0, 0)),
        ],
        out_specs=[
            pl.BlockSpec((tm, tn), lambda i, j: (i, j)),
            pl.BlockSpec((tm, R), lambda i, j: (i, 0)),
        ],
        out_shape=[jax.ShapeDtypeStruct((T, N), BF16), jax.ShapeDtypeStruct((T, R), F32)],
        scratch_shapes=[pltpu.VMEM((tm, D), BF16)],
        compiler_params=_cparams(("parallel", "arbitrary")),
        name="inproj",
    )(x, gain.reshape(1, D), w_main, w_lr)


def _chan_dft_kernel(u_ref, cs_ref, zc_ref, zs_ref, *, gd):
    for g in range(FNET_GROUPS):
        r = jnp.dot(u_ref[:, g * gd:(g + 1) * gd], cs_ref[...], preferred_element_type=F32)
        zc_ref[:, g * gd:(g + 1) * gd] = r[:, :gd].astype(BF16)
        zs_ref[:, g * gd:(g + 1) * gd] = r[:, gd:].astype(BF16)


def _chan_dft(proj, u_col_block, width, cs, tm=512):
    T = proj.shape[0]
    gd = width // FNET_GROUPS
    return pl.pallas_call(
        functools.partial(_chan_dft_kernel, gd=gd),
        grid=(T // tm,),
        in_specs=[
            pl.BlockSpec((tm, width), lambda i: (i, u_col_block)),
            pl.BlockSpec((gd, 2 * gd), lambda i: (0, 0)),
        ],
        out_specs=[pl.BlockSpec((tm, width), lambda i: (i, 0))] * 2,
        out_shape=[jax.ShapeDtypeStruct((T, width), BF16)] * 2,
        compiler_params=_cparams(("parallel",)),
        name="chan_dft",
    )(proj, cs)


def _seq_dft_kernel(c_ref, s_ref, zc_ref, zs_ref, *rest):
    o_ref, acc_ref = rest[-2:]
    k = pl.program_id(2)

    @pl.when(k == 0)
    def _():
        acc_ref[...] = jnp.zeros_like(acc_ref)

    acc_ref[...] += (jnp.dot(c_ref[...], zc_ref[...], preferred_element_type=F32)
                     + jnp.dot(s_ref[...], zs_ref[...], preferred_element_type=F32))

    @pl.when(k == pl.num_programs(2) - 1)
    def _():
        o_ref[...] = acc_ref[...].astype(o_ref.dtype)


def _seq_dft(zc, zs, cmat, nsmat, row0, B, S, prev=None, tm=512, tk=512):
    T, W = zc.shape
    tm, tk = min(tm, S), min(tk, S)
    nm, nk = S // tm, S // tk
    kb0, mb0 = row0 // tk, row0 // tm
    in_specs = [
        pl.BlockSpec((tm, tk), lambda b, i, k: (i, k)),
        pl.BlockSpec((tm, tk), lambda b, i, k: (i, k)),
        pl.BlockSpec((tk, W), lambda b, i, k: (kb0 + b * nk + k, 0)),
        pl.BlockSpec((tk, W), lambda b, i, k: (kb0 + b * nk + k, 0)),
    ]
    args = [cmat, nsmat, zc, zs]
    aliases = {}
    if prev is not None:
        in_specs.append(pl.BlockSpec(memory_space=pl.ANY))
        args.append(prev)
        aliases = {4: 0}
    return pl.pallas_call(
        _seq_dft_kernel,
        grid=(B, nm, nk),
        in_specs=in_specs,
        out_specs=pl.BlockSpec((tm, W), lambda b, i, k: (mb0 + b * nm + i, 0)),
        out_shape=jax.ShapeDtypeStruct((T, W), BF16),
        scratch_shapes=[pltpu.VMEM((tm, W), F32)],
        input_output_aliases=aliases,
        compiler_params=_cparams(("parallel", "parallel", "arbitrary")),
        name="seq_dft",
    )(*args)


def _dft_mats(n, scale, split=64):
    split = split if n % split == 0 else 1
    k = jnp.arange(n, dtype=jnp.int32)[None, :]
    j1 = jnp.arange(n // split, dtype=jnp.int32)[:, None]
    j2 = jnp.arange(split, dtype=jnp.int32)[:, None]
    w = 2.0 * math.pi / n
    ang_a = ((split * j1 * k) % n).astype(F32) * w
    ang_b = ((j2 * k) % n).astype(F32) * w
    ca, sa = jnp.cos(ang_a)[:, None, :], jnp.sin(ang_a)[:, None, :]
    cb, sb = (jnp.cos(ang_b) * scale)[None, :, :], (jnp.sin(ang_b) * scale)[None, :, :]
    c = (ca * cb - sa * sb).reshape(n, n)
    s = (sa * cb + ca * sb).reshape(n, n)
    return c, s


def _gla_gates(lr, wup, bias, cum):
    z = jnp.dot(lr, wup, precision=HIGHEST, preferred_element_type=F32) + bias
    g = (jnp.minimum(z, 0.0) - jnp.log1p(jnp.exp(-jnp.abs(z)))) * (1.0 / GATE_LOGIT_NORMALIZER)
    hi, mid, lo = _split3(g)
    return (jnp.dot(cum, hi, preferred_element_type=F32) + jnp.dot(cum, mid, preferred_element_type=F32)
            + jnp.dot(cum, lo, preferred_element_type=F32))


def _gla_chunk(q_ref, k_ref, v_ref, gc, st_ref, c, reverse, qscale, mask):
    ref_row, tot_row = (CHUNK // 2, 0) if reverse else (CHUNK // 2 - 1, CHUNK - 1)
    r0 = c * CHUNK
    gcc = gc[r0:r0 + CHUNK, :]
    gref = gcc[ref_row:ref_row + 1, :]
    gtot = gcc[tot_row:tot_row + 1, :]
    q = q_ref[r0:r0 + CHUNK, :].astype(F32) * qscale
    k = k_ref[r0:r0 + CHUNK, :].astype(F32)
    v = v_ref[r0:r0 + CHUNK, :]
    q_in = (q * jnp.exp(gcc - gref)).astype(BF16)
    k_in = (k * jnp.exp(gref - gcc)).astype(BF16)
    s = lax.dot_general(q_in, k_in, (((1,), (1,)), ((), ())), preferred_element_type=F32)
    s = jnp.where(mask, s, 0.0).astype(BF16)
    o = jnp.dot(s, v, preferred_element_type=F32)
    q_it = (q * jnp.exp(gcc)).astype(BF16)
    k_it = (k * jnp.exp(gtot - gcc)).astype(BF16)
    st = st_ref[...]
    o = o + lax.dot_general(q_it, st.astype(BF16), (((1,), (1,)), ((), ())), preferred_element_type=F32)
    st_ref[...] = st * jnp.exp(gtot) + lax.dot_general(
        v, k_it, (((0,), (0,)), ((), ())), preferred_element_type=F32)
    return o


def _gla_kernel(qf_ref, kf_ref, vf_ref, qb_ref, kb_ref, vb_ref, lrf_ref, lrb_ref,
                wf_ref, bf_ref, wb_ref, bb_ref, *rest, nchunk, qscale):
    o_ref, stf_ref, stb_ref = rest[-3:]
    n = pl.program_id(2)
    nb = pl.num_programs(2)
    rows = nchunk * CHUNK

    @pl.when(n == 0)
    def _():
        o_ref[...] = jnp.zeros_like(o_ref)
        stf_ref[...] = jnp.zeros_like(stf_ref)
        stb_ref[...] = jnp.zeros_like(stb_ref)

    r = lax.broadcasted_iota(jnp.int32, (rows, rows), 0)
    c = lax.broadcasted_iota(jnp.int32, (rows, rows), 1)
    same = (r // CHUNK) == (c // CHUNK)
    cum_f = (same & (c <= r)).astype(BF16)
    cum_b = (same & (c >= r)).astype(BF16)
    gc_f = _gla_gates(lrf_ref[...], wf_ref[...], bf_ref[...], cum_f)
    gc_b = _gla_gates(lrb_ref[...], wb_ref[...], bb_ref[...], cum_b)

    ri = lax.broadcasted_iota(jnp.int32, (CHUNK, CHUNK), 0)
    ci = lax.broadcasted_iota(jnp.int32, (CHUNK, CHUNK), 1)
    base_f = n * rows
    base_b = (nb - 1 - n) * rows
    for c_f in range(nchunk):
        c_b = nchunk - 1 - c_f
        o_f = _gla_chunk(qf_ref, kf_ref, vf_ref, gc_f, stf_ref, c_f, False, qscale, ci <= ri)
        o_b = _gla_chunk(qb_ref, kb_ref, vb_ref, gc_b, stb_ref, c_b, True, qscale, ci > ri)
        o_ref[pl.ds(pl.multiple_of(base_f + c_f * CHUNK, CHUNK), CHUNK), :] += o_f
        o_ref[pl.ds(pl.multiple_of(base_b + c_b * CHUNK, CHUNK), CHUNK), :] += o_b


def _gla(proj, lr_f, lr_b, wup_f, b_f, wup_b, b_b, row0, B, S, q_blk0, k_blk0, v_blk0, dk, dv,
         prev=None, rows=256):
    T = proj.shape[0]
    assert S % rows == 0 and row0 % S == 0
    nb = S // rows
    rb0, sb0 = row0 // rows, row0 // S
    fmap = lambda b, n: rb0 + b * nb + n
    bmap = lambda b, n: rb0 + b * nb + (nb - 1 - n)
    in_specs = [
        pl.BlockSpec((rows, dk), lambda b, h, n: (fmap(b, n), q_blk0 + h)),
        pl.BlockSpec((rows, dk), lambda b, h, n: (fmap(b, n), k_blk0 + h)),
        pl.BlockSpec((rows, dv), lambda b, h, n: (fmap(b, n), v_blk0 + h)),
        pl.BlockSpec((rows, dk), lambda b, h, n: (bmap(b, n), q_blk0 + h)),
        pl.BlockSpec((rows, dk), lambda b, h, n: (bmap(b, n), k_blk0 + h)),
        pl.BlockSpec((rows, dv), lambda b, h, n: (bmap(b, n), v_blk0 + h)),
        pl.BlockSpec((rows, GATE_LOW_RANK), lambda b, h, n: (fmap(b, n), 0)),
        pl.BlockSpec((rows, GATE_LOW_RANK), lambda b, h, n: (bmap(b, n), 0)),
        pl.BlockSpec((GATE_LOW_RANK, dk), lambda b, h, n: (0, h)),
        pl.BlockSpec((1, dk), lambda b, h, n: (0, h)),
        pl.BlockSpec((GATE_LOW_RANK, dk), lambda b, h, n: (0, h)),
        pl.BlockSpec((1, dk), lambda b, h, n: (0, h)),
    ]
    args = [proj, proj, proj, proj, proj, proj, lr_f, lr_b,
            wup_f, b_f.reshape(1, -1), wup_b, b_b.reshape(1, -1)]
    aliases = {}
    if prev is not None:
        in_specs.append(pl.BlockSpec(memory_space=pl.ANY))
        args.append(prev)
        aliases = {len(args) - 1: 0}
    return pl.pallas_call(
        functools.partial(_gla_kernel, nchunk=rows // CHUNK, qscale=dk ** -0.5),
        grid=(B, GLA_HEADS, nb),
        in_specs=in_specs,
        out_specs=pl.BlockSpec((S, dv), lambda b, h, n: (sb0 + b, h)),
        out_shape=jax.ShapeDtypeStruct((T, GLA_HEADS * dv), F32),
        scratch_shapes=[pltpu.VMEM((dv, dk), F32), pltpu.VMEM((dv, dk), F32)],
        input_output_aliases=aliases,
        compiler_params=_cparams(("parallel", "parallel", "arbitrary")),
        name="gla",
    )(*args)


def _merge_kernel(fft_ref, o_ref_in, og_ref, g0_ref, g1_ref, hn_ref, wf_ref, wg_ref, o_ref, a_ref, *, dv):
    ya = jnp.dot(fft_ref[...], wf_ref[...], preferred_element_type=F32)
    for h in range(GLA_HEADS):
        cs = slice(h * dv, (h + 1) * dv)
        o = o_ref_in[:, cs]
        var = jnp.mean(o * o, axis=-1, keepdims=True)
        on = o * lax.rsqrt(var + EPS) * hn_ref[...]
        og = og_ref[:, cs].astype(F32)
        a_ref[:, cs] = (on * (og * jax.nn.sigmoid(og))).astype(BF16)
    yb = jnp.dot(a_ref[...], wg_ref[...], preferred_element_type=F32)
    m = jax.nn.sigmoid(g0_ref[...].astype(F32)) * ya + jax.nn.sigmoid(g1_ref[...].astype(F32)) * yb
    o_ref[...] = m.astype(BF16)


def _merge(fft, o_gla, proj, og_blk, g0_blk, g1_blk, hn, wf, wg, tm=256):
    T, D = o_gla.shape
    FW = fft.shape[1]
    dv = D // GLA_HEADS
    const = dict(pipeline_mode=pl.Buffered(1))
    return pl.pallas_call(
        functools.partial(_merge_kernel, dv=dv),
        grid=(T // tm,),
        in_specs=[
            pl.BlockSpec((tm, FW), lambda i: (i, 0)),
            pl.BlockSpec((tm, D), lambda i: (i, 0)),
            pl.BlockSpec((tm, D), lambda i: (i, og_blk)),
            pl.BlockSpec((tm, D), lambda i: (i, g0_blk)),
            pl.BlockSpec((tm, D), lambda i: (i, g1_blk)),
            pl.BlockSpec((1, dv), lambda i: (0, 0)),
            pl.BlockSpec((FW, D), lambda i: (0, 0), **const),
            pl.BlockSpec((D, D), lambda i: (0, 0), **const),
        ],
        out_specs=pl.BlockSpec((tm, D), lambda i: (i, 0)),
        out_shape=jax.ShapeDtypeStruct((T, D), BF16),
        scratch_shapes=[pltpu.VMEM((tm, D), BF16)],
        compiler_params=_cparams(("parallel",)),
        name="merge",
    )(fft, o_gla, proj, proj, proj, hn.reshape(1, dv), wf, wg)


def _outproj_router_kernel(m_ref, x_ref, wo_ref, g_ref, wr_ref, br_ref,
                           x1_ref, xn_ref, idx_ref, tw_ref, rank_ref, cnt_ref, run_ref, *, n_exp):
    i = pl.program_id(0)

    @pl.when(i == 0)
    def _():
        run_ref[...] = jnp.zeros_like(run_ref)

    tm = m_ref.shape[0]
    x1 = x_ref[...] + jnp.dot(m_ref[...], wo_ref[...], preferred_element_type=F32)
    x1_ref[...] = x1
    var = jnp.mean(x1 * x1, axis=-1, keepdims=True)
    xn = x1 * lax.rsqrt(var + EPS) * g_ref[...]
    xn_ref[...] = xn
    lg = jnp.dot(xn, wr_ref[...], precision=HIGHEST, preferred_element_type=F32) + br_ref[...]

    lane = lax.broadcasted_iota(jnp.int32, (tm, n_exp), 1)
    vals, hots = [], []
    for _ in range(TOP_K):
        mx = jnp.max(lg, axis=-1, keepdims=True)
        ik = jnp.min(jnp.where(lg == mx, lane, n_exp), axis=-1, keepdims=True)
        hot = lane == ik
        vals.append(mx)
        hots.append(hot)
        lg = jnp.where(hot, -jnp.inf, lg)
    exps = [jnp.exp(v - vals[0]) for v in vals]
    denom = exps[0] + exps[1] + exps[2] + exps[3]

    sel = hots[0] | hots[1] | hots[2] | hots[3]
    sel_f = sel.astype(F32)
    r = lax.broadcasted_iota(jnp.int32, (tm, tm), 0)
    c = lax.broadcasted_iota(jnp.int32, (tm, tm), 1)
    strict = (c < r).astype(BF16)
    before = jnp.dot(strict, sel_f.astype(BF16), preferred_element_type=F32) + run_ref[...]
    run_ref[...] += jnp.sum(sel_f, axis=0, keepdims=True)
    cnt_ref[...] = run_ref[...].astype(jnp.int32)

    k4 = lax.broadcasted_iota(jnp.int32, (tm, TOP_K), 1)
    idx4 = jnp.zeros((tm, TOP_K), jnp.int32)
    w4 = jnp.zeros((tm, TOP_K), F32)
    rk4 = jnp.zeros((tm, TOP_K), jnp.int32)
    for k in range(TOP_K):
        ik = jnp.sum(jnp.where(hots[k], lane, 0), axis=-1, keepdims=True)
        rk = jnp.sum(jnp.where(hots[k], before, 0.0), axis=-1, keepdims=True).astype(jnp.int32)
        idx4 = jnp.where(k4 == k, ik, idx4)
        w4 = jnp.where(k4 == k, exps[k] / denom, w4)
        rk4 = jnp.where(k4 == k, rk, rk4)
    idx_ref[...] = idx4
    tw_ref[...] = w4
    rank_ref[...] = rk4


def _outproj_router(merged, x, wo, gain, wr, br, tm=256):
    T, D = x.shape
    E = wr.shape[1]
    const = dict(pipeline_mode=pl.Buffered(1))
    row = lambda i: (i, 0)
    fix = lambda i: (0, 0)
    return pl.pallas_call(
        functools.partial(_outproj_router_kernel, n_exp=E),
        grid=(T // tm,),
        in_specs=[
            pl.BlockSpec((tm, D), row),
            pl.BlockSpec((tm, D), row),
            pl.BlockSpec((D, D), fix, **const),
            pl.BlockSpec((1, D), fix),
            pl.BlockSpec((D, E), fix),
            pl.BlockSpec((1, E), fix),
        ],
        out_specs=[
            pl.BlockSpec((tm, D), row),
            pl.BlockSpec((tm, D), row),
            pl.BlockSpec((tm, TOP_K), row),
            pl.BlockSpec((tm, TOP_K), row),
            pl.BlockSpec((tm, TOP_K), row),
            pl.BlockSpec((1, E), fix),
        ],
        out_shape=[
            jax.ShapeDtypeStruct((T, D), F32),
            jax.ShapeDtypeStruct((T, D), F32),
            jax.ShapeDtypeStruct((T, TOP_K), jnp.int32),
            jax.ShapeDtypeStruct((T, TOP_K), F32),
            jax.ShapeDtypeStruct((T, TOP_K), jnp.int32),
            jax.ShapeDtypeStruct((1, E), jnp.int32),
        ],
        scratch_shapes=[pltpu.VMEM((1, E), F32)],
        compiler_params=_cparams(("arbitrary",)),
        name="outproj_router",
    )(merged, x, wo, gain.reshape(1, D), wr, br.reshape(1, E))


def _row_copy(src, dst, sem):
    return pltpu.make_async_copy(src, dst, sem)


def _dispatch_kernel(pos_ref, x_ref, xs_ref, sem):
    tm = x_ref.shape[0]

    def body(t, carry):
        for k in range(TOP_K):
            p = pos_ref[t * TOP_K + k]
            _row_copy(x_ref.at[pl.ds(t, 1), :], xs_ref.at[pl.ds(p, 1), :], sem).start()
        return carry

    lax.fori_loop(0, tm, body, 0)
    for _ in range(TOP_K):
        _row_copy(x_ref, xs_ref.at[pl.ds(0, tm), :], sem).wait()


def _dispatch(xn, pos_flat, n_slots, tm=256):
    T, D = xn.shape
    return pl.pallas_call(
        _dispatch_kernel,
        grid=(T // tm,),
        in_specs=[
            pl.BlockSpec((tm * TOP_K,), lambda i: (i,), memory_space=pltpu.SMEM),
            pl.BlockSpec((tm, D), lambda i: (i, 0)),
        ],
        out_specs=pl.BlockSpec(memory_space=pl.ANY),
        out_shape=jax.ShapeDtypeStruct((n_slots, D), xn.dtype),
        scratch_shapes=[pltpu.SemaphoreType.DMA(())],
        compiler_params=_cparams(("arbitrary",)),
        name="dispatch",
    )(pos_flat, xn)


def _combine_kernel(pos_ref, tw_ref, x1_ref, g_ref, ys_ref, o_ref, buf_ref, sem):
    tm = x1_ref.shape[0]

    def body(t, carry):
        for k in range(TOP_K):
            p = pos_ref[t * TOP_K + k]
            _row_copy(ys_ref.at[pl.ds(p, 1), :], buf_ref.at[k, pl.ds(t, 1), :], sem).start()
        return carry

    lax.fori_loop(0, tm, body, 0)
    for k in range(TOP_K):
        _row_copy(ys_ref.at[pl.ds(0, tm), :], buf_ref.at[k], sem).wait()

    tw = tw_ref[...]
    acc = jnp.zeros(x1_ref.shape, F32)
    for k in range(TOP_K):
        acc = acc + tw[:, k:k + 1] * buf_ref[k]
    x2 = x1_ref[...] + acc
    var = jnp.mean(x2 * x2, axis=-1, keepdims=True)
    o_ref[...] = x2 * lax.rsqrt(var + EPS) * g_ref[...]


def _combine(pos_flat, tw, x1, gain, ys, tm=256):
    T, D = x1.shape
    return pl.pallas_call(
        _combine_kernel,
        grid=(T // tm,),
        in_specs=[
            pl.BlockSpec((tm * TOP_K,), lambda i: (i,), memory_space=pltpu.SMEM),
            pl.BlockSpec((tm, TOP_K), lambda i: (i, 0)),
            pl.BlockSpec((tm, D), lambda i: (i, 0)),
            pl.BlockSpec((1, D), lambda i: (0, 0)),
            pl.BlockSpec(memory_space=pl.ANY),
        ],
        out_specs=pl.BlockSpec((tm, D), lambda i: (i, 0)),
        out_shape=jax.ShapeDtypeStruct((T, D), F32),
        scratch_shapes=[pltpu.VMEM((TOP_K, tm, D), F32), pltpu.SemaphoreType.DMA(())],
        compiler_params=_cparams(("arbitrary",)),
        name="combine",
    )(pos_flat, tw, x1, gain.reshape(1, D), ys)


def _expert_changed(te_ref, i):
    return (i == 0) | (te_ref[i] != te_ref[jnp.maximum(i - 1, 0)])


def _gate_up_kernel(te_ref, nu_ref, nv_ref, x_ref, w_ref, bg_ref, bu_ref, o_ref, wp_ref, xb_ref, *, tn):
    i = pl.program_id(1)
    tm, D = x_ref.shape
    half = MXU_DIM // 2

    @pl.when(i < nu_ref[0])
    def _():
        @pl.when(_expert_changed(te_ref, i))
        def _():
            r = lax.broadcasted_iota(jnp.int32, (MXU_DIM, MXU_DIM), 0)
            c = lax.broadcasted_iota(jnp.int32, (MXU_DIM, MXU_DIM), 1)
            perm = (((c < half) & (r == 2 * c)) | ((c >= half) & (r == 2 * (c - half) + 1))).astype(BF16)
            for cb in range(tn // MXU_DIM):
                for rb in range(D // 512):
                    w = w_ref[0, rb * 512:(rb + 1) * 512, cb * MXU_DIM:(cb + 1) * MXU_DIM].astype(BF16)
                    wp_ref[rb * 512:(rb + 1) * 512, cb * MXU_DIM:(cb + 1) * MXU_DIM] = jnp.dot(
                        w, perm, preferred_element_type=F32).astype(BF16)

        rows = lax.broadcasted_iota(jnp.int32, (tm, 1), 0)
        xb_ref[...] = jnp.where(rows < nv_ref[i], x_ref[...], 0.0).astype(BF16)
        for cb in range(tn // MXU_DIM):
            h = jnp.dot(xb_ref[...], wp_ref[:, cb * MXU_DIM:(cb + 1) * MXU_DIM], preferred_element_type=F32)
            hg = h[:, :half] + bg_ref[0, :, cb * half:(cb + 1) * half]
            hu = h[:, half:] + bu_ref[0, :, cb * half:(cb + 1) * half]
            gate = jnp.minimum(hg, SWIGLU_LIMIT)
            up = jnp.clip(hu, -SWIGLU_LIMIT, SWIGLU_LIMIT)
            act = gate * jax.nn.sigmoid(SWIGLU_ALPHA * gate) * (up + 1.0)
            o_ref[:, cb * half:(cb + 1) * half] = act.astype(o_ref.dtype)


def _gate_up(tile_exp, n_used, n_valid, xs, w_gate_up, bg, bu, tn=1024):
    P, D = xs.shape
    E, _, H2 = w_gate_up.shape
    H = H2 // 2
    tm = GROUP_TILE
    nt = P // tm

    def xmap(j, i, te, nu, nv):
        return (jnp.minimum(i, nu[0] - 1), 0)

    def wmap(j, i, te, nu, nv):
        return (te[i], 0, j)

    def omap(j, i, te, nu, nv):
        return (jnp.minimum(i, nu[0] - 1), j)

    gs = pltpu.PrefetchScalarGridSpec(
        num_scalar_prefetch=3,
        grid=(H2 // tn, nt),
        in_specs=[
            pl.BlockSpec((tm, D), xmap),
            pl.BlockSpec((1, D, tn), wmap),
            pl.BlockSpec((1, 1, tn // 2), wmap),
            pl.BlockSpec((1, 1, tn // 2), wmap),
        ],
        out_specs=pl.BlockSpec((tm, tn // 2), omap),
        scratch_shapes=[pltpu.VMEM((D, tn), BF16), pltpu.VMEM((tm, D), BF16)],
    )
    return pl.pallas_call(
        functools.partial(_gate_up_kernel, tn=tn),
        grid_spec=gs,
        out_shape=jax.ShapeDtypeStruct((P, H), BF16),
        compiler_params=_cparams(("arbitrary", "arbitrary")),
        name="moe_gate_up",
    )(tile_exp, n_used, n_valid, xs, w_gate_up, bg, bu)


def _down_kernel(te_ref, nu_ref, a_ref, w_ref, b_ref, o_ref, wb_ref):
    i = pl.program_id(1)

    @pl.when(i < nu_ref[0])
    def _():
        @pl.when(_expert_changed(te_ref, i))
        def _():
            H = w_ref.shape[1]
            for rb in range(H // 512):
                wb_ref[rb * 512:(rb + 1) * 512, :] = w_ref[0, rb * 512:(rb + 1) * 512, :].astype(BF16)

        o_ref[...] = jnp.dot(a_ref[...], wb_ref[...], preferred_element_type=F32) + b_ref[0]


def _down(tile_exp, n_used, act, wd, bd, tn=1024):
    P, H = act.shape
    E, _, D = wd.shape
    tm = GROUP_TILE
    nt = P // tm

    def amap(j, i, te, nu):
        return (jnp.minimum(i, nu[0] - 1), 0)

    def wmap(j, i, te, nu):
        return (te[i], 0, j)

    def omap(j, i, te, nu):
        return (jnp.minimum(i, nu[0] - 1), j)

    gs = pltpu.PrefetchScalarGridSpec(
        num_scalar_prefetch=2,
        grid=(D // tn, nt),
        in_specs=[
            pl.BlockSpec((tm, H), amap),
            pl.BlockSpec((1, H, tn), wmap),
            pl.BlockSpec((1, 1, tn), wmap),
        ],
        out_specs=pl.BlockSpec((tm, tn), omap),
        scratch_shapes=[pltpu.VMEM((H, tn), BF16)],
    )
    return pl.pallas_call(
        _down_kernel,
        grid_spec=gs,
        out_shape=jax.ShapeDtypeStruct((P, D), F32),
        compiler_params=_cparams(("arbitrary", "arbitrary")),
        name="moe_down",
    )(tile_exp, n_used, act, wd, bd)


def _trunk(xs, seq_shapes, norm_mix, w_in, w_gk_up_fwd, b_gk_fwd, w_gk_up_bwd, b_gk_bwd, gla_head_norm,
           w_fnet_out, w_gla_out, w_out, norm_ffn, w_router, b_router, w_gate_up, b_gate_up,
           w_down, b_down, norm_final):
    T, D = xs.shape
    fw = w_fnet_out.shape[0]
    dkk = w_gk_up_fwd.shape[1]
    dvv = w_gla_out.shape[0]
    dk, dv = dkk // GLA_HEADS, dvv // GLA_HEADS
    sizes = (fw, dkk, dkk, dvv, dvv, GATE_LOW_RANK, GATE_LOW_RANK, 2 * D)
    offs = np.concatenate([[0], np.cumsum(sizes)])
    sl = lambda n: slice(int(offs[n]), int(offs[n + 1]))
    w_u, w_q, w_k, w_v, w_og = (w_in[:, sl(n)] for n in range(5))
    w_lr = w_in[:, int(offs[5]):int(offs[7])]
    w_g = w_in[:, sl(7)]
    w_main = jnp.concatenate([w_og, w_g, w_v, w_u, w_q, w_k], axis=1).astype(BF16)
    og_blk, g0_blk, g1_blk = 0, dvv // D, dvv // D + 1
    v_off = dvv + 2 * D
    u_off = v_off + dvv
    q_off = u_off + fw
    k_off = q_off + dkk

    proj, lr = _inproj(xs, norm_mix, w_main, w_lr)
    lr_f, lr_b = lr[:, :GATE_LOW_RANK], lr[:, GATE_LOW_RANK:]

    gd = fw // FNET_GROUPS
    cc, sc = _dft_mats(gd, gd ** -0.5)
    cs = jnp.concatenate([cc, sc], axis=1).astype(BF16)
    zc, zs = _chan_dft(proj, u_off // fw, fw, cs)
    fft, o_gla = None, None
    row0 = 0
    for (B, S) in seq_shapes:
        cm, sm = _dft_mats(S, S ** -0.5)
        fft = _seq_dft(zc, zs, cm.astype(BF16), (-sm).astype(BF16), row0, B, S, prev=fft)
        o_gla = _gla(proj, lr_f, lr_b, w_gk_up_fwd, b_gk_fwd, w_gk_up_bwd, b_gk_bwd, row0, B, S,
                     q_off // dk, k_off // dk, v_off // dv, dk, dv, prev=o_gla)
        row0 += B * S

    merged = _merge(fft, o_gla, proj, og_blk, g0_blk, g1_blk, gla_head_norm,
                    w_fnet_out.astype(BF16), w_gla_out.astype(BF16))
    x1, xn2, idx, tw, rank, cnt = _outproj_router(merged, xs, w_out.astype(BF16), norm_ffn, w_router, b_router)

    E = w_router.shape[1]
    cnt = cnt.reshape(E)
    gsz = ((cnt + GROUP_TILE - 1) // GROUP_TILE) * GROUP_TILE
    gend = jnp.cumsum(gsz)
    gstart = gend - gsz
    pos = (gstart[idx] + rank).reshape(-1).astype(jnp.int32)
    n_slots = T * TOP_K + E * GROUP_TILE
    nt = n_slots // GROUP_TILE
    tile_id = jnp.arange(nt, dtype=jnp.int32)
    n_used = (gend[-1:] // GROUP_TILE).astype(jnp.int32)
    tile_id_c = jnp.minimum(tile_id, n_used[0] - 1)
    tile_row0 = tile_id_c * GROUP_TILE
    tile_exp = jnp.minimum(jnp.sum(tile_row0[:, None] >= gend[None, :], axis=1), E - 1).astype(jnp.int32)
    n_valid = jnp.clip(cnt[tile_exp] - (tile_row0 - gstart[tile_exp]), 0, GROUP_TILE).astype(jnp.int32)

    x_sorted = _dispatch(xn2, pos, n_slots)
    H = w_down.shape[1]
    bg = b_gate_up[:, 0::2].reshape(E, 1, H)
    bu = b_gate_up[:, 1::2].reshape(E, 1, H)
    act = _gate_up(tile_exp, n_used, n_valid, x_sorted, w_gate_up, bg, bu)
    ys = _down(tile_exp, n_used, act, w_down, b_down.reshape(E, 1, D))
    return _combine(pos, tw, x1, norm_final, ys)


def kernel(x_prompt, x_sample, norm_mix, w_in, w_gk_up_fwd, b_gk_fwd, w_gk_up_bwd, b_gk_bwd, gla_head_norm,
           w_fnet_out, w_gla_out, w_out, norm_ffn, w_router, b_router, w_gate_up, b_gate_up, w_down,
           b_down, norm_final):
    D = x_prompt.shape[-1]
    shapes = (x_prompt.shape[:2], x_sample.shape[:2])
    xs = jnp.concatenate([x_prompt.reshape(-1, D), x_sample.reshape(-1, D)], axis=0)
    y = _trunk(xs, shapes, norm_mix[0], w_in[0], w_gk_up_fwd[0], b_gk_fwd[0], w_gk_up_bwd[0], b_gk_bwd[0],
               gla_head_norm[0], w_fnet_out[0], w_gla_out[0], w_out[0], norm_ffn[0], w_router[0],
               b_router[0], w_gate_up[0], b_gate_up[0], w_down[0], b_down[0], norm_final)
    n0 = x_prompt.shape[0] * x_prompt.shape[1]
    return (y[:n0].reshape(x_prompt.shape), y[n0:].reshape(x_sample.shape))
```

```python
import functools
import math

import numpy as np
import jax
import jax.numpy as jnp
from jax import lax
from jax.experimental import pallas as pl
from jax.experimental.pallas import tpu as pltpu

F32 = jnp.float32
BF16 = jnp.bfloat16
HIGHEST = lax.Precision.HIGHEST

EPS = 1e-5
FNET_GROUPS = 4
GLA_HEADS = 4
GATE_LOW_RANK = 16
GATE_LOGIT_NORMALIZER = 16.0
CHUNK = 64
TOP_K = 4
SWIGLU_LIMIT = 7.0
SWIGLU_ALPHA = 1.702

VMEM_LIMIT_BYTES = 56 * 1024 * 1024
MXU_DIM = 256
LANES = 128
GROUP_TILE = 512


def _cparams(sem):
    return pltpu.CompilerParams(dimension_semantics=sem, vmem_limit_bytes=VMEM_LIMIT_BYTES)


def _split3(x):
    hi = x.astype(BF16)
    r = x - hi.astype(F32)
    mid = r.astype(BF16)
    lo = (r - mid.astype(F32)).astype(BF16)
    return hi, mid, lo


def _hi_lo(w):
    hi = w.astype(BF16)
    lo = (w - hi.astype(F32)).astype(BF16)
    return jnp.concatenate([hi, lo], axis=1)


def _inproj_kernel(xp_ref, xs_ref, g_ref, w_ref, wlr_ref, o_ref, lr_ref, xn_ref, *, n0):
    @pl.when(pl.program_id(1) == 0)
    def _():
        x = jnp.where(pl.program_id(0) < n0, xp_ref[...], xs_ref[...])
        var = jnp.mean(x * x, axis=-1, keepdims=True)
        xn = (x * lax.rsqrt(var + EPS) * g_ref[...]).astype(BF16)
        xn_ref[...] = xn
        r = jnp.dot(xn, wlr_ref[...], preferred_element_type=F32)
        nlr = lr_ref.shape[1]
        lr_ref[...] = r[:, :nlr] + r[:, nlr:]

    o_ref[...] = jnp.dot(xn_ref[...], w_ref[...], preferred_element_type=F32).astype(o_ref.dtype)


def _inproj(xp, xs, gain, w_main, w_lr2, tm=512, tn=1024):
    D = xp.shape[1]
    T = xp.shape[0] + xs.shape[0]
    n0 = xp.shape[0] // tm
    N = w_main.shape[1]
    R = w_lr2.shape[1] // 2
    return pl.pallas_call(
        functools.partial(_inproj_kernel, n0=n0),
        grid=(T // tm, N // tn),
        in_specs=[
            pl.BlockSpec((tm, D), lambda i, j: (jnp.minimum(i, n0 - 1), 0)),
            pl.BlockSpec((tm, D), lambda i, j: (jnp.maximum(i - n0, 0), 0)),
            pl.BlockSpec((1, D), lambda i, j: (0, 0)),
            pl.BlockSpec((D, tn), lambda i, j: (0, j)),
            pl.BlockSpec((D, 2 * R), lambda i, j: (0, 0)),
        ],
        out_specs=[
            pl.BlockSpec((tm, tn), lambda i, j: (i, j)),
            pl.BlockSpec((tm, R), lambda i, j: (i, 0)),
        ],
        out_shape=[jax.ShapeDtypeStruct((T, N), BF16), jax.ShapeDtypeStruct((T, R), F32)],
        scratch_shapes=[pltpu.VMEM((tm, D), BF16)],
        compiler_params=_cparams(("parallel", "arbitrary")),
        name="inproj",
    )(xp, xs, gain.reshape(1, D), w_main, w_lr2)


def _chan_dft_kernel(u_ref, cs_ref, zc_ref, zs_ref, *, gd):
    for g in range(FNET_GROUPS):
        r = jnp.dot(u_ref[:, g * gd:(g + 1) * gd], cs_ref[...], preferred_element_type=F32)
        zc_ref[:, g * gd:(g + 1) * gd] = r[:, :gd].astype(BF16)
        zs_ref[:, g * gd:(g + 1) * gd] = r[:, gd:].astype(BF16)


def _chan_dft(proj, u_col_block, width, cs, tm=512):
    T = proj.shape[0]
    gd = width // FNET_GROUPS
    return pl.pallas_call(
        functools.partial(_chan_dft_kernel, gd=gd),
        grid=(T // tm,),
        in_specs=[
            pl.BlockSpec((tm, width), lambda i: (i, u_col_block)),
            pl.BlockSpec((gd, 2 * gd), lambda i: (0, 0)),
        ],
        out_specs=[pl.BlockSpec((tm, width), lambda i: (i, 0))] * 2,
        out_shape=[jax.ShapeDtypeStruct((T, width), BF16)] * 2,
        compiler_params=_cparams(("parallel",)),
        name="chan_dft",
    )(proj, cs)


def _seq_dft_kernel(c_ref, s_ref, zc_ref, zs_ref, *rest):
    o_ref, acc_ref = rest[-2:]
    k = pl.program_id(2)

    @pl.when(k == 0)
    def _():
        acc_ref[...] = jnp.zeros_like(acc_ref)

    acc_ref[...] += (jnp.dot(c_ref[...], zc_ref[...], preferred_element_type=F32)
                     + jnp.dot(s_ref[...], zs_ref[...], preferred_element_type=F32))

    @pl.when(k == pl.num_programs(2) - 1)
    def _():
        o_ref[...] = acc_ref[...].astype(o_ref.dtype)


def _seq_dft(zc, zs, cmat, nsmat, row0, B, S, prev=None, tm=512, tk=512):
    T, W = zc.shape
    tm, tk = min(tm, S), min(tk, S)
    nm, nk = S // tm, S // tk
    kb0, mb0 = row0 // tk, row0 // tm
    in_specs = [
        pl.BlockSpec((tm, tk), lambda b, i, k: (i, k)),
        pl.BlockSpec((tm, tk), lambda b, i, k: (i, k)),
        pl.BlockSpec((tk, W), lambda b, i, k: (kb0 + b * nk + k, 0)),
        pl.BlockSpec((tk, W), lambda b, i, k: (kb0 + b * nk + k, 0)),
    ]
    args = [cmat, nsmat, zc, zs]
    aliases = {}
    if prev is not None:
        in_specs.append(pl.BlockSpec(memory_space=pl.ANY))
        args.append(prev)
        aliases = {4: 0}
    return pl.pallas_call(
        _seq_dft_kernel,
        grid=(B, nm, nk),
        in_specs=in_specs,
        out_specs=pl.BlockSpec((tm, W), lambda b, i, k: (mb0 + b * nm + i, 0)),
        out_shape=jax.ShapeDtypeStruct((T, W), BF16),
        scratch_shapes=[pltpu.VMEM((tm, W), F32)],
        input_output_aliases=aliases,
        compiler_params=_cparams(("parallel", "parallel", "arbitrary")),
        name="seq_dft",
    )(*args)


def _dft_mats(n, scale, split=64):
    split = split if n % split == 0 else 1
    k = jnp.arange(n, dtype=jnp.int32)[None, :]
    j1 = jnp.arange(n // split, dtype=jnp.int32)[:, None]
    j2 = jnp.arange(split, dtype=jnp.int32)[:, None]
    w = 2.0 * math.pi / n
    ang_a = ((split * j1 * k) % n).astype(F32) * w
    ang_b = ((j2 * k) % n).astype(F32) * w
    ca, sa = jnp.cos(ang_a)[:, None, :], jnp.sin(ang_a)[:, None, :]
    cb, sb = (jnp.cos(ang_b) * scale)[None, :, :], (jnp.sin(ang_b) * scale)[None, :, :]
    c = (ca * cb - sa * sb).reshape(n, n)
    s = (sa * cb + ca * sb).reshape(n, n)
    return c, s


def _gla_gates(lr, wup, bias, cum):
    z = jnp.dot(lr, wup, precision=HIGHEST, preferred_element_type=F32) + bias
    g = (jnp.minimum(z, 0.0) - jnp.log1p(jnp.exp(-jnp.abs(z)))) * (1.0 / GATE_LOGIT_NORMALIZER)
    hi, mid, lo = _split3(g)
    return (jnp.dot(cum, hi, preferred_element_type=F32) + jnp.dot(cum, mid, preferred_element_type=F32)
            + jnp.dot(cum, lo, preferred_element_type=F32))


def _gla_chunk(q_ref, k_ref, v_ref, gc, st_ref, c, reverse, qscale, mask):
    ref_row, tot_row = (CHUNK // 2, 0) if reverse else (CHUNK // 2 - 1, CHUNK - 1)
    r0 = c * CHUNK
    gcc = gc[r0:r0 + CHUNK, :]
    gref = gcc[ref_row:ref_row + 1, :]
    gtot = gcc[tot_row:tot_row + 1, :]
    q = q_ref[r0:r0 + CHUNK, :].astype(F32) * qscale
    k = k_ref[r0:r0 + CHUNK, :].astype(F32)
    v = v_ref[r0:r0 + CHUNK, :]
    q_in = (q * jnp.exp(gcc - gref)).astype(BF16)
    k_in = (k * jnp.exp(gref - gcc)).astype(BF16)
    s = lax.dot_general(q_in, k_in, (((1,), (1,)), ((), ())), preferred_element_type=F32)
    s = jnp.where(mask, s, 0.0).astype(BF16)
    o = jnp.dot(s, v, preferred_element_type=F32)
    q_it = (q * jnp.exp(gcc)).astype(BF16)
    k_it = (k * jnp.exp(gtot - gcc)).astype(BF16)
    st = st_ref[...]
    o = o + lax.dot_general(q_it, st.astype(BF16), (((1,), (1,)), ((), ())), preferred_element_type=F32)
    st_ref[...] = st * jnp.exp(gtot) + lax.dot_general(
        v, k_it, (((0,), (0,)), ((), ())), preferred_element_type=F32)
    return o


def _gla_kernel(qf_ref, kf_ref, vf_ref, qb_ref, kb_ref, vb_ref, lrf_ref, lrb_ref,
                wf_ref, bf_ref, wb_ref, bb_ref, *rest, nchunk, qscale):
    o_ref, stf_ref, stb_ref = rest[-3:]
    n = pl.program_id(2)
    nb = pl.num_programs(2)
    rows = nchunk * CHUNK

    @pl.when(n == 0)
    def _():
        o_ref[...] = jnp.zeros_like(o_ref)
        stf_ref[...] = jnp.zeros_like(stf_ref)
        stb_ref[...] = jnp.zeros_like(stb_ref)

    r = lax.broadcasted_iota(jnp.int32, (rows, rows), 0)
    c = lax.broadcasted_iota(jnp.int32, (rows, rows), 1)
    same = (r // CHUNK) == (c // CHUNK)
    cum_f = (same & (c <= r)).astype(BF16)
    cum_b = (same & (c >= r)).astype(BF16)
    gc_f = _gla_gates(lrf_ref[...], wf_ref[...], bf_ref[...], cum_f)
    gc_b = _gla_gates(lrb_ref[...], wb_ref[...], bb_ref[...], cum_b)

    ri = lax.broadcasted_iota(jnp.int32, (CHUNK, CHUNK), 0)
    ci = lax.broadcasted_iota(jnp.int32, (CHUNK, CHUNK), 1)
    base_f = n * rows
    base_b = (nb - 1 - n) * rows
    for c_f in range(nchunk):
        c_b = nchunk - 1 - c_f
        o_f = _gla_chunk(qf_ref, kf_ref, vf_ref, gc_f, stf_ref, c_f, False, qscale, ci <= ri)
        o_b = _gla_chunk(qb_ref, kb_ref, vb_ref, gc_b, stb_ref, c_b, True, qscale, ci > ri)
        o_ref[pl.ds(pl.multiple_of(base_f + c_f * CHUNK, CHUNK), CHUNK), :] += o_f
        o_ref[pl.ds(pl.multiple_of(base_b + c_b * CHUNK, CHUNK), CHUNK), :] += o_b


def _gla(proj, lr_f, lr_b, wup_f, b_f, wup_b, b_b, row0, B, S, q_blk0, k_blk0, v_blk0, dk, dv,
         prev=None, rows=256):
    T = proj.shape[0]
    assert S % rows == 0 and row0 % S == 0
    nb = S // rows
    rb0, sb0 = row0 // rows, row0 // S
    fmap = lambda b, n: rb0 + b * nb + n
    bmap = lambda b, n: rb0 + b * nb + (nb - 1 - n)
    in_specs = [
        pl.BlockSpec((rows, dk), lambda b, h, n: (fmap(b, n), q_blk0 + h)),
        pl.BlockSpec((rows, dk), lambda b, h, n: (fmap(b, n), k_blk0 + h)),
        pl.BlockSpec((rows, dv), lambda b, h, n: (fmap(b, n), v_blk0 + h)),
        pl.BlockSpec((rows, dk), lambda b, h, n: (bmap(b, n), q_blk0 + h)),
        pl.BlockSpec((rows, dk), lambda b, h, n: (bmap(b, n), k_blk0 + h)),
        pl.BlockSpec((rows, dv), lambda b, h, n: (bmap(b, n), v_blk0 + h)),
        pl.BlockSpec((rows, GATE_LOW_RANK), lambda b, h, n: (fmap(b, n), 0)),
        pl.BlockSpec((rows, GATE_LOW_RANK), lambda b, h, n: (bmap(b, n), 0)),
        pl.BlockSpec((GATE_LOW_RANK, dk), lambda b, h, n: (0, h)),
        pl.BlockSpec((1, dk), lambda b, h, n: (0, h)),
        pl.BlockSpec((GATE_LOW_RANK, dk), lambda b, h, n: (0, h)),
        pl.BlockSpec((1, dk), lambda b, h, n: (0, h)),
    ]
    args = [proj, proj, proj, proj, proj, proj, lr_f, lr_b,
            wup_f, b_f.reshape(1, -1), wup_b, b_b.reshape(1, -1)]
    aliases = {}
    if prev is not None:
        in_specs.append(pl.BlockSpec(memory_space=pl.ANY))
        args.append(prev)
        aliases = {len(args) - 1: 0}
    return pl.pallas_call(
        functools.partial(_gla_kernel, nchunk=rows // CHUNK, qscale=dk ** -0.5),
        grid=(B, GLA_HEADS, nb),
        in_specs=in_specs,
        out_specs=pl.BlockSpec((S, dv), lambda b, h, n: (sb0 + b, h)),
        out_shape=jax.ShapeDtypeStruct((T, GLA_HEADS * dv), F32),
        scratch_shapes=[pltpu.VMEM((dv, dk), F32), pltpu.VMEM((dv, dk), F32)],
        input_output_aliases=aliases,
        compiler_params=_cparams(("parallel", "parallel", "arbitrary")),
        name="gla",
    )(*args)


def _merge_kernel(fft_ref, o_ref_in, og_ref, g0_ref, g1_ref, hn_ref, wf_ref, wg_ref, o_ref, a_ref, *, dv):
    ya = jnp.dot(fft_ref[...], wf_ref[...], preferred_element_type=F32)
    for h in range(GLA_HEADS):
        cs = slice(h * dv, (h + 1) * dv)
        o = o_ref_in[:, cs]
        var = jnp.mean(o * o, axis=-1, keepdims=True)
        on = o * lax.rsqrt(var + EPS) * hn_ref[...]
        og = og_ref[:, cs].astype(F32)
        a_ref[:, cs] = (on * (og * jax.nn.sigmoid(og))).astype(BF16)
    yb = jnp.dot(a_ref[...], wg_ref[...], preferred_element_type=F32)
    m = jax.nn.sigmoid(g0_ref[...].astype(F32)) * ya + jax.nn.sigmoid(g1_ref[...].astype(F32)) * yb
    o_ref[...] = m.astype(BF16)


def _merge(fft, o_gla, proj, og_blk, g0_blk, g1_blk, hn, wf, wg, tm=256):
    T, D = o_gla.shape
    FW = fft.shape[1]
    dv = D // GLA_HEADS
    const = dict(pipeline_mode=pl.Buffered(1))
    return pl.pallas_call(
        functools.partial(_merge_kernel, dv=dv),
        grid=(T // tm,),
        in_specs=[
            pl.BlockSpec((tm, FW), lambda i: (i, 0)),
            pl.BlockSpec((tm, D), lambda i: (i, 0)),
            pl.BlockSpec((tm, D), lambda i: (i, og_blk)),
            pl.BlockSpec((tm, D), lambda i: (i, g0_blk)),
            pl.BlockSpec((tm, D), lambda i: (i, g1_blk)),
            pl.BlockSpec((1, dv), lambda i: (0, 0)),
            pl.BlockSpec((FW, D), lambda i: (0, 0), **const),
            pl.BlockSpec((D, D), lambda i: (0, 0), **const),
        ],
        out_specs=pl.BlockSpec((tm, D), lambda i: (i, 0)),
        out_shape=jax.ShapeDtypeStruct((T, D), BF16),
        scratch_shapes=[pltpu.VMEM((tm, D), BF16)],
        compiler_params=_cparams(("parallel",)),
        name="merge",
    )(fft, o_gla, proj, proj, proj, hn.reshape(1, dv), wf, wg)


HI16 = 0xFFFF0000


def _pack_bf16_pair(lo, hi):
    lo_bits = lax.bitcast_convert_type(lo.astype(BF16).astype(F32), jnp.uint32)
    hi_bits = lax.bitcast_convert_type(hi.astype(BF16).astype(F32), jnp.uint32)
    return (hi_bits & jnp.uint32(HI16)) | (lo_bits >> 16)


def _unpack_bf16_pair(w):
    lo = lax.bitcast_convert_type(w << 16, F32)
    hi = lax.bitcast_convert_type(w & jnp.uint32(HI16), F32)
    return lo, hi


def _outproj_router_kernel(m_ref, xp_ref, xs_ref, wo_ref, g_ref, wr_ref, br_ref,
                           x1_ref, xn_ref, idx_ref, tw_ref, rank_ref, cnt_ref, run_ref, *, n_exp, n0):
    i = pl.program_id(0)

    @pl.when(i == 0)
    def _():
        run_ref[...] = jnp.zeros_like(run_ref)

    tm, D = m_ref.shape
    x = jnp.where(i < n0, xp_ref[...], xs_ref[...])
    x1 = x + jnp.dot(m_ref[...], wo_ref[...], preferred_element_type=F32)
    x1_ref[...] = x1
    var = jnp.mean(x1 * x1, axis=-1, keepdims=True)
    xn = x1 * lax.rsqrt(var + EPS) * g_ref[...]
    xn_ref[...] = _pack_bf16_pair(xn[:, :D // 2], xn[:, D // 2:])
    xh = xn.astype(BF16)
    xl = (xn - xh.astype(F32)).astype(BF16)
    r = jnp.dot(xh, wr_ref[...], preferred_element_type=F32)
    lg = (r[:, :n_exp] + r[:, n_exp:] + jnp.dot(xl, wr_ref[:, :n_exp], preferred_element_type=F32)
          + br_ref[...])

    lane = lax.broadcasted_iota(jnp.int32, (tm, n_exp), 1)
    vals, hots = [], []
    for _ in range(TOP_K):
        mx = jnp.max(lg, axis=-1, keepdims=True)
        ik = jnp.min(jnp.where(lg == mx, lane, n_exp), axis=-1, keepdims=True)
        hot = lane == ik
        vals.append(mx)
        hots.append(hot)
        lg = jnp.where(hot, -jnp.inf, lg)
    exps = [jnp.exp(v - vals[0]) for v in vals]
    denom = exps[0] + exps[1] + exps[2] + exps[3]

    sel = hots[0] | hots[1] | hots[2] | hots[3]
    sel_f = sel.astype(F32)
    r = lax.broadcasted_iota(jnp.int32, (tm, tm), 0)
    c = lax.broadcasted_iota(jnp.int32, (tm, tm), 1)
    strict = (c < r).astype(BF16)
    before = jnp.dot(strict, sel_f.astype(BF16), preferred_element_type=F32) + run_ref[...]
    run_ref[...] += jnp.sum(sel_f, axis=0, keepdims=True)
    cnt_ref[...] = run_ref[...].astype(jnp.int32)

    k4 = lax.broadcasted_iota(jnp.int32, (tm, TOP_K), 1)
    idx4 = jnp.zeros((tm, TOP_K), jnp.int32)
    w4 = jnp.zeros((tm, TOP_K), F32)
    rk4 = jnp.zeros((tm, TOP_K), jnp.int32)
    for k in range(TOP_K):
        ik = jnp.sum(jnp.where(hots[k], lane, 0), axis=-1, keepdims=True)
        rk = jnp.sum(jnp.where(hots[k], before, 0.0), axis=-1, keepdims=True).astype(jnp.int32)
        idx4 = jnp.where(k4 == k, ik, idx4)
        w4 = jnp.where(k4 == k, exps[k] / denom, w4)
        rk4 = jnp.where(k4 == k, rk, rk4)
    idx_ref[...] = idx4
    tw_ref[...] = w4
    rank_ref[...] = rk4


def _outproj_router(merged, xp, xs, wo, gain, wr2, br, tm=256):
    T, D = merged.shape
    E = wr2.shape[1] // 2
    n0 = xp.shape[0] // tm
    const = dict(pipeline_mode=pl.Buffered(1))
    row = lambda i: (i, 0)
    fix = lambda i: (0, 0)
    return pl.pallas_call(
        functools.partial(_outproj_router_kernel, n_exp=E, n0=n0),
        grid=(T // tm,),
        in_specs=[
            pl.BlockSpec((tm, D), row),
            pl.BlockSpec((tm, D), lambda i: (jnp.minimum(i, n0 - 1), 0)),
            pl.BlockSpec((tm, D), lambda i: (jnp.maximum(i - n0, 0), 0)),
            pl.BlockSpec((D, D), fix, **const),
            pl.BlockSpec((1, D), fix),
            pl.BlockSpec((D, 2 * E), fix),
            pl.BlockSpec((1, E), fix),
        ],
        out_specs=[
            pl.BlockSpec((tm, D), row),
            pl.BlockSpec((tm, D // 2), row),
            pl.BlockSpec((tm, TOP_K), row),
            pl.BlockSpec((tm, TOP_K), row),
            pl.BlockSpec((tm, TOP_K), row),
            pl.BlockSpec((1, E), fix),
        ],
        out_shape=[
            jax.ShapeDtypeStruct((T, D), F32),
            jax.ShapeDtypeStruct((T, D // 2), jnp.uint32),
            jax.ShapeDtypeStruct((T, TOP_K), jnp.int32),
            jax.ShapeDtypeStruct((T, TOP_K), F32),
            jax.ShapeDtypeStruct((T, TOP_K), jnp.int32),
            jax.ShapeDtypeStruct((1, E), jnp.int32),
        ],
        scratch_shapes=[pltpu.VMEM((1, E), F32)],
        compiler_params=_cparams(("arbitrary",)),
        name="outproj_router",
    )(merged, xp, xs, wo, gain.reshape(1, D), wr2, br.reshape(1, E))


def _row_copy(src, dst, sem):
    return pltpu.make_async_copy(src, dst, sem)


def _dispatch_kernel(pos_ref, x_ref, xs_ref, sem):
    tm = x_ref.shape[0]

    def body(t, carry):
        for k in range(TOP_K):
            p = pos_ref[t * TOP_K + k]
            _row_copy(x_ref.at[pl.ds(t, 1), :], xs_ref.at[pl.ds(p, 1), :], sem).start()
        return carry

    lax.fori_loop(0, tm, body, 0)
    for _ in range(TOP_K):
        _row_copy(x_ref, xs_ref.at[pl.ds(0, tm), :], sem).wait()


def _dispatch(xn, pos_flat, n_slots, tm=256):
    T, D = xn.shape
    return pl.pallas_call(
        _dispatch_kernel,
        grid=(T // tm,),
        in_specs=[
            pl.BlockSpec((tm * TOP_K,), lambda i: (i,), memory_space=pltpu.SMEM),
            pl.BlockSpec((tm, D), lambda i: (i, 0)),
        ],
        out_specs=pl.BlockSpec(memory_space=pl.ANY),
        out_shape=jax.ShapeDtypeStruct((n_slots, D), xn.dtype),
        scratch_shapes=[pltpu.SemaphoreType.DMA(())],
        compiler_params=_cparams(("arbitrary",)),
        name="dispatch",
    )(pos_flat, xn)


def _combine_kernel(pos_ref, posn_ref, tw_ref, x1_ref, g_ref, ys_ref, op_ref, os_ref, buf_ref, sem, *, n0, tn):
    i = pl.program_id(0)
    n = pl.num_programs(0)
    tm, D = x1_ref.shape
    slot = i % 2

    def gather(p_ref, s):
        def body(t, carry):
            for k in range(TOP_K):
                p = p_ref[t * TOP_K + k]
                _row_copy(ys_ref.at[pl.ds(p, 1), :], buf_ref.at[s, k, pl.ds(t, 1), :], sem.at[s]).start()
            return carry

        lax.fori_loop(0, tm, body, 0, unroll=2)

    @pl.when(i == 0)
    def _():
        gather(pos_ref, 0)

    @pl.when(i + 1 < n)
    def _():
        gather(posn_ref, 1 - slot)

    for k in range(TOP_K):
        _row_copy(ys_ref.at[pl.ds(0, tm), :], buf_ref.at[slot, k], sem.at[slot]).wait()

    tw = tw_ref[...]
    half = tn // 2
    pieces = [jnp.zeros((tm, half), F32) for _ in range(D // half)]
    for k in range(TOP_K):
        wk = tw[:, k:k + 1]
        for jb in range(D // tn):
            lo, hi = _unpack_bf16_pair(buf_ref[slot, k, :, jb * half:(jb + 1) * half])
            pieces[2 * jb] = pieces[2 * jb] + wk * lo
            pieces[2 * jb + 1] = pieces[2 * jb + 1] + wk * hi
    x2 = x1_ref[...] + jnp.concatenate(pieces, axis=1)
    var = jnp.mean(x2 * x2, axis=-1, keepdims=True)
    y = x2 * lax.rsqrt(var + EPS) * g_ref[...]

    @pl.when(i < n0)
    def _():
        op_ref[...] = y

    @pl.when(i >= n0)
    def _():
        os_ref[...] = y


def _combine(pos_flat, tw, x1, gain, ys, t_prompt, tn, tm=256):
    T, D = x1.shape
    n0 = t_prompt // tm
    nlast = T // tm - 1
    return pl.pallas_call(
        functools.partial(_combine_kernel, n0=n0, tn=tn),
        grid=(T // tm,),
        in_specs=[
            pl.BlockSpec((tm * TOP_K,), lambda i: (i,), memory_space=pltpu.SMEM),
            pl.BlockSpec((tm * TOP_K,), lambda i: (jnp.minimum(i + 1, nlast),), memory_space=pltpu.SMEM),
            pl.BlockSpec((tm, TOP_K), lambda i: (i, 0)),
            pl.BlockSpec((tm, D), lambda i: (i, 0)),
            pl.BlockSpec((1, D), lambda i: (0, 0)),
            pl.BlockSpec(memory_space=pl.ANY),
        ],
        out_specs=[
            pl.BlockSpec((tm, D), lambda i: (jnp.minimum(i, n0 - 1), 0)),
            pl.BlockSpec((tm, D), lambda i: (jnp.maximum(i - n0, 0), 0)),
        ],
        out_shape=[jax.ShapeDtypeStruct((t_prompt, D), F32), jax.ShapeDtypeStruct((T - t_prompt, D), F32)],
        scratch_shapes=[pltpu.VMEM((2, TOP_K, tm, D // 2), jnp.uint32), pltpu.SemaphoreType.DMA((2,))],
        compiler_params=_cparams(("arbitrary",)),
        name="combine",
    )(pos_flat, pos_flat, tw, x1, gain.reshape(1, D), ys)


def _expert_changed(te_ref, i):
    return (i == 0) | (te_ref[i] != te_ref[jnp.maximum(i - 1, 0)])


def _gate_up_kernel(te_ref, nu_ref, nv_ref, x_ref, w_ref, bg_ref, bu_ref, o_ref, wp_ref, xb_ref, *, tn):
    i = pl.program_id(1)
    tm, D = xb_ref.shape
    half = MXU_DIM // 2

    @pl.when(i < nu_ref[0])
    def _():
        @pl.when(_expert_changed(te_ref, i))
        def _():
            r = lax.broadcasted_iota(jnp.int32, (MXU_DIM, MXU_DIM), 0)
            c = lax.broadcasted_iota(jnp.int32, (MXU_DIM, MXU_DIM), 1)
            perm = (((c < half) & (r == 2 * c)) | ((c >= half) & (r == 2 * (c - half) + 1))).astype(BF16)
            for cb in range(tn // MXU_DIM):
                for rb in range(D // 512):
                    w = w_ref[0, rb * 512:(rb + 1) * 512, cb * MXU_DIM:(cb + 1) * MXU_DIM].astype(BF16)
                    wp_ref[rb * 512:(rb + 1) * 512, cb * MXU_DIM:(cb + 1) * MXU_DIM] = jnp.dot(
                        w, perm, preferred_element_type=F32).astype(BF16)

        valid = lax.broadcasted_iota(jnp.int32, (tm, 1), 0) < nv_ref[i]
        lo, hi = _unpack_bf16_pair(x_ref[...])
        xb_ref[:, :D // 2] = jnp.where(valid, lo, 0.0).astype(BF16)
        xb_ref[:, D // 2:] = jnp.where(valid, hi, 0.0).astype(BF16)
        for cb in range(tn // MXU_DIM):
            h = jnp.dot(xb_ref[...], wp_ref[:, cb * MXU_DIM:(cb + 1) * MXU_DIM], preferred_element_type=F32)
            hg = h[:, :half] + bg_ref[0, :, cb * half:(cb + 1) * half]
            hu = h[:, half:] + bu_ref[0, :, cb * half:(cb + 1) * half]
            gate = jnp.minimum(hg, SWIGLU_LIMIT)
            up = jnp.clip(hu, -SWIGLU_LIMIT, SWIGLU_LIMIT)
            act = gate * jax.nn.sigmoid(SWIGLU_ALPHA * gate) * (up + 1.0)
            o_ref[:, cb * half:(cb + 1) * half] = act.astype(o_ref.dtype)


def _gate_up(tile_exp, n_used, n_valid, xs, w_gate_up, bg, bu, tn=1024):
    P = xs.shape[0]
    E, D, H2 = w_gate_up.shape
    H = H2 // 2
    tm = GROUP_TILE
    nt = P // tm

    def xmap(j, i, te, nu, nv):
        return (jnp.minimum(i, nu[0] - 1), 0)

    def wmap(j, i, te, nu, nv):
        return (te[i], 0, j)

    def omap(j, i, te, nu, nv):
        return (jnp.minimum(i, nu[0] - 1), j)

    gs = pltpu.PrefetchScalarGridSpec(
        num_scalar_prefetch=3,
        grid=(H2 // tn, nt),
        in_specs=[
            pl.BlockSpec((tm, D // 2), xmap),
            pl.BlockSpec((1, D, tn), wmap),
            pl.BlockSpec((1, 1, tn // 2), wmap),
            pl.BlockSpec((1, 1, tn // 2), wmap),
        ],
        out_specs=pl.BlockSpec((tm, tn // 2), omap),
        scratch_shapes=[pltpu.VMEM((D, tn), BF16), pltpu.VMEM((tm, D), BF16)],
    )
    return pl.pallas_call(
        functools.partial(_gate_up_kernel, tn=tn),
        grid_spec=gs,
        out_shape=jax.ShapeDtypeStruct((P, H), BF16),
        compiler_params=_cparams(("arbitrary", "arbitrary")),
        name="moe_gate_up",
    )(tile_exp, n_used, n_valid, xs, w_gate_up, bg, bu)


def _down_kernel(te_ref, nu_ref, a_ref, w_ref, b_ref, o_ref, wb_ref):
    i = pl.program_id(1)

    @pl.when(i < nu_ref[0])
    def _():
        @pl.when(_expert_changed(te_ref, i))
        def _():
            H = w_ref.shape[1]
            for rb in range(H // 512):
                wb_ref[rb * 512:(rb + 1) * 512, :] = w_ref[0, rb * 512:(rb + 1) * 512, :].astype(BF16)

        y = jnp.dot(a_ref[...], wb_ref[...], preferred_element_type=F32) + b_ref[0]
        half = y.shape[1] // 2
        o_ref[...] = _pack_bf16_pair(y[:, :half], y[:, half:])


def _down(tile_exp, n_used, act, wd, bd, tn):
    P, H = act.shape
    E, _, D = wd.shape
    tm = GROUP_TILE
    nt = P // tm

    def amap(j, i, te, nu):
        return (jnp.minimum(i, nu[0] - 1), 0)

    def wmap(j, i, te, nu):
        return (te[i], 0, j)

    def omap(j, i, te, nu):
        return (jnp.minimum(i, nu[0] - 1), j)

    gs = pltpu.PrefetchScalarGridSpec(
        num_scalar_prefetch=2,
        grid=(D // tn, nt),
        in_specs=[
            pl.BlockSpec((tm, H), amap),
            pl.BlockSpec((1, H, tn), wmap),
            pl.BlockSpec((1, 1, tn), wmap),
        ],
        out_specs=pl.BlockSpec((tm, tn // 2), omap),
        scratch_shapes=[pltpu.VMEM((H, tn), BF16)],
    )
    return pl.pallas_call(
        _down_kernel,
        grid_spec=gs,
        out_shape=jax.ShapeDtypeStruct((P, D // 2), jnp.uint32),
        compiler_params=_cparams(("arbitrary", "arbitrary")),
        name="moe_down",
    )(tile_exp, n_used, act, wd, bd)


def _trunk(xp, xs, seq_shapes, norm_mix, w_in, w_gk_up_fwd, b_gk_fwd, w_gk_up_bwd, b_gk_bwd, gla_head_norm,
           w_fnet_out, w_gla_out, w_out, norm_ffn, w_router, b_router, w_gate_up, b_gate_up,
           w_down, b_down, norm_final):
    D = xp.shape[1]
    T = xp.shape[0] + xs.shape[0]
    fw = w_fnet_out.shape[0]
    dkk = w_gk_up_fwd.shape[1]
    dvv = w_gla_out.shape[0]
    dk, dv = dkk // GLA_HEADS, dvv // GLA_HEADS
    sizes = (fw, dkk, dkk, dvv, dvv, GATE_LOW_RANK, GATE_LOW_RANK, 2 * D)
    offs = np.concatenate([[0], np.cumsum(sizes)])
    sl = lambda n: slice(int(offs[n]), int(offs[n + 1]))
    w_u, w_q, w_k, w_v, w_og = (w_in[:, sl(n)] for n in range(5))
    w_lr = w_in[:, int(offs[5]):int(offs[7])]
    w_g = w_in[:, sl(7)]
    w_main = jnp.concatenate([w_og, w_g, w_v, w_u, w_q, w_k], axis=1).astype(BF16)
    og_blk, g0_blk, g1_blk = 0, dvv // D, dvv // D + 1
    v_off = dvv + 2 * D
    u_off = v_off + dvv
    q_off = u_off + fw
    k_off = q_off + dkk

    proj, lr = _inproj(xp, xs, norm_mix, w_main, _hi_lo(w_lr))
    lr_f, lr_b = lr[:, :GATE_LOW_RANK], lr[:, GATE_LOW_RANK:]

    gd = fw // FNET_GROUPS
    cc, sc = _dft_mats(gd, gd ** -0.5)
    cs = jnp.concatenate([cc, sc], axis=1).astype(BF16)
    zc, zs = _chan_dft(proj, u_off // fw, fw, cs)
    fft, o_gla = None, None
    row0 = 0
    for (B, S) in seq_shapes:
        cm, sm = _dft_mats(S, S ** -0.5)
        fft = _seq_dft(zc, zs, cm.astype(BF16), (-sm).astype(BF16), row0, B, S, prev=fft)
        o_gla = _gla(proj, lr_f, lr_b, w_gk_up_fwd, b_gk_fwd, w_gk_up_bwd, b_gk_bwd, row0, B, S,
                     q_off // dk, k_off // dk, v_off // dv, dk, dv, prev=o_gla)
        row0 += B * S

    merged = _merge(fft, o_gla, proj, og_blk, g0_blk, g1_blk, gla_head_norm,
                    w_fnet_out.astype(BF16), w_gla_out.astype(BF16))
    x1, xn2, idx, tw, rank, cnt = _outproj_router(merged, xp, xs, w_out.astype(BF16), norm_ffn,
                                                  _hi_lo(w_router), b_router)

    E = w_router.shape[1]
    cnt = cnt.reshape(E)
    gsz = ((cnt + GROUP_TILE - 1) // GROUP_TILE) * GROUP_TILE
    gend = jnp.cumsum(gsz)
    gstart = gend - gsz
    pos = (gstart[idx] + rank).reshape(-1).astype(jnp.int32)
    n_slots = T * TOP_K + E * GROUP_TILE
    nt = n_slots // GROUP_TILE
    tile_id = jnp.arange(nt, dtype=jnp.int32)
    n_used = (gend[-1:] // GROUP_TILE).astype(jnp.int32)
    tile_id_c = jnp.minimum(tile_id, n_used[0] - 1)
    tile_row0 = tile_id_c * GROUP_TILE
    tile_exp = jnp.minimum(jnp.sum(tile_row0[:, None] >= gend[None, :], axis=1), E - 1).astype(jnp.int32)
    n_valid = jnp.clip(cnt[tile_exp] - (tile_row0 - gstart[tile_exp]), 0, GROUP_TILE).astype(jnp.int32)

    x_sorted = _dispatch(xn2, pos, n_slots)
    H = w_down.shape[1]
    bg = b_gate_up[:, 0::2].reshape(E, 1, H)
    bu = b_gate_up[:, 1::2].reshape(E, 1, H)
    act = _gate_up(tile_exp, n_used, n_valid, x_sorted, w_gate_up, bg, bu)
    tn_down = 1024
    ys = _down(tile_exp, n_used, act, w_down, b_down.reshape(E, 1, D), tn_down)
    return _combine(pos, tw, x1, norm_final, ys, xp.shape[0], tn_down)


def kernel(x_prompt, x_sample, norm_mix, w_in, w_gk_up_fwd, b_gk_fwd, w_gk_up_bwd, b_gk_bwd, gla_head_norm,
           w_fnet_out, w_gla_out, w_out, norm_ffn, w_router, b_router, w_gate_up, b_gate_up, w_down,
           b_down, norm_final):
    D = x_prompt.shape[-1]
    shapes = (x_prompt.shape[:2], x_sample.shape[:2])
    yp, ys = _trunk(x_prompt.reshape(-1, D), x_sample.reshape(-1, D), shapes, norm_mix[0], w_in[0], w_gk_up_fwd[0], b_gk_fwd[0], w_gk_up_bwd[0], b_gk_bwd[0],
               gla_head_norm[0], w_fnet_out[0], w_gla_out[0], w_out[0], norm_ffn[0], w_router[0],
               b_router[0], w_gate_up[0], b_gate_up[0], w_down[0], b_down[0], norm_final)
    return (yp.reshape(x_prompt.shape), ys.reshape(x_sample.shape))
```

```python
import functools
import math

import numpy as np
import jax
import jax.numpy as jnp
from jax import lax
from jax.experimental import pallas as pl
from jax.experimental.pallas import tpu as pltpu

F32 = jnp.float32
BF16 = jnp.bfloat16
HIGHEST = lax.Precision.HIGHEST

EPS = 1e-5
FNET_GROUPS = 4
GLA_HEADS = 4
GATE_LOW_RANK = 16
GATE_LOGIT_NORMALIZER = 16.0
CHUNK = 64
TOP_K = 4
SWIGLU_LIMIT = 7.0
SWIGLU_ALPHA = 1.702

VMEM_LIMIT_BYTES = 56 * 1024 * 1024
MXU_DIM = 256
LANES = 128
GROUP_TILE = 256
GATE_UP_TN = 2048
DOWN_TN = 2048


def _cparams(sem):
    return pltpu.CompilerParams(dimension_semantics=sem, vmem_limit_bytes=VMEM_LIMIT_BYTES)


def _split3(x):
    hi = x.astype(BF16)
    r = x - hi.astype(F32)
    mid = r.astype(BF16)
    lo = (r - mid.astype(F32)).astype(BF16)
    return hi, mid, lo


def _hi_lo(w):
    hi = w.astype(BF16)
    lo = (w - hi.astype(F32)).astype(BF16)
    return jnp.concatenate([hi, lo], axis=1)


def _inproj_kernel(xp_ref, xs_ref, g_ref, w_ref, wlr_ref, o_ref, lr_ref, xn_ref, *, n0):
    @pl.when(pl.program_id(1) == 0)
    def _():
        x = jnp.where(pl.program_id(0) < n0, xp_ref[...], xs_ref[...])
        var = jnp.mean(x * x, axis=-1, keepdims=True)
        xn = (x * lax.rsqrt(var + EPS) * g_ref[...]).astype(BF16)
        xn_ref[...] = xn
        r = jnp.dot(xn, wlr_ref[...], preferred_element_type=F32)
        nlr = lr_ref.shape[1]
        lr_ref[...] = r[:, :nlr] + r[:, nlr:]

    o_ref[...] = jnp.dot(xn_ref[...], w_ref[...], preferred_element_type=F32).astype(o_ref.dtype)


def _inproj(xp, xs, gain, w_main, w_lr2, tm=512, tn=1024):
    D = xp.shape[1]
    T = xp.shape[0] + xs.shape[0]
    n0 = xp.shape[0] // tm
    N = w_main.shape[1]
    R = w_lr2.shape[1] // 2
    return pl.pallas_call(
        functools.partial(_inproj_kernel, n0=n0),
        grid=(T // tm, N // tn),
        in_specs=[
            pl.BlockSpec((tm, D), lambda i, j: (jnp.minimum(i, n0 - 1), 0)),
            pl.BlockSpec((tm, D), lambda i, j: (jnp.maximum(i - n0, 0), 0)),
            pl.BlockSpec((1, D), lambda i, j: (0, 0)),
            pl.BlockSpec((D, tn), lambda i, j: (0, j)),
            pl.BlockSpec((D, 2 * R), lambda i, j: (0, 0)),
        ],
        out_specs=[
            pl.BlockSpec((tm, tn), lambda i, j: (i, j)),
            pl.BlockSpec((tm, R), lambda i, j: (i, 0)),
        ],
        out_shape=[jax.ShapeDtypeStruct((T, N), BF16), jax.ShapeDtypeStruct((T, R), F32)],
        scratch_shapes=[pltpu.VMEM((tm, D), BF16)],
        compiler_params=_cparams(("parallel", "arbitrary")),
        name="inproj",
    )(xp, xs, gain.reshape(1, D), w_main, w_lr2)


def _chan_dft_kernel(u_ref, cs_ref, zc_ref, zs_ref, *, gd):
    for g in range(FNET_GROUPS):
        r = jnp.dot(u_ref[:, g * gd:(g + 1) * gd], cs_ref[...], preferred_element_type=F32)
        zc_ref[:, g * gd:(g + 1) * gd] = r[:, :gd].astype(BF16)
        zs_ref[:, g * gd:(g + 1) * gd] = r[:, gd:].astype(BF16)


def _chan_dft(proj, u_col_block, width, cs, tm=512):
    T = proj.shape[0]
    gd = width // FNET_GROUPS
    return pl.pallas_call(
        functools.partial(_chan_dft_kernel, gd=gd),
        grid=(T // tm,),
        in_specs=[
            pl.BlockSpec((tm, width), lambda i: (i, u_col_block)),
            pl.BlockSpec((gd, 2 * gd), lambda i: (0, 0)),
        ],
        out_specs=[pl.BlockSpec((tm, width), lambda i: (i, 0))] * 2,
        out_shape=[jax.ShapeDtypeStruct((T, width), BF16)] * 2,
        compiler_params=_cparams(("parallel",)),
        name="chan_dft",
    )(proj, cs)


def _seq_dft_kernel(c_ref, s_ref, zc_ref, zs_ref, *rest):
    o_ref, acc_ref = rest[-2:]
    k = pl.program_id(2)

    @pl.when(k == 0)
    def _():
        acc_ref[...] = jnp.zeros_like(acc_ref)

    acc_ref[...] += (jnp.dot(c_ref[...], zc_ref[...], preferred_element_type=F32)
                     + jnp.dot(s_ref[...], zs_ref[...], preferred_element_type=F32))

    @pl.when(k == pl.num_programs(2) - 1)
    def _():
        o_ref[...] = acc_ref[...].astype(o_ref.dtype)


def _seq_dft(zc, zs, cmat, nsmat, row0, B, S, prev=None, tm=512, tk=512):
    T, W = zc.shape
    tm, tk = min(tm, S), min(tk, S)
    nm, nk = S // tm, S // tk
    kb0, mb0 = row0 // tk, row0 // tm
    in_specs = [
        pl.BlockSpec((tm, tk), lambda b, i, k: (i, k)),
        pl.BlockSpec((tm, tk), lambda b, i, k: (i, k)),
        pl.BlockSpec((tk, W), lambda b, i, k: (kb0 + b * nk + k, 0)),
        pl.BlockSpec((tk, W), lambda b, i, k: (kb0 + b * nk + k, 0)),
    ]
    args = [cmat, nsmat, zc, zs]
    aliases = {}
    if prev is not None:
        in_specs.append(pl.BlockSpec(memory_space=pl.ANY))
        args.append(prev)
        aliases = {4: 0}
    return pl.pallas_call(
        _seq_dft_kernel,
        grid=(B, nm, nk),
        in_specs=in_specs,
        out_specs=pl.BlockSpec((tm, W), lambda b, i, k: (mb0 + b * nm + i, 0)),
        out_shape=jax.ShapeDtypeStruct((T, W), BF16),
        scratch_shapes=[pltpu.VMEM((tm, W), F32)],
        input_output_aliases=aliases,
        compiler_params=_cparams(("parallel", "parallel", "arbitrary")),
        name="seq_dft",
    )(*args)


def _dft_mats(n, scale, split=64):
    split = split if n % split == 0 else 1
    k = jnp.arange(n, dtype=jnp.int32)[None, :]
    j1 = jnp.arange(n // split, dtype=jnp.int32)[:, None]
    j2 = jnp.arange(split, dtype=jnp.int32)[:, None]
    w = 2.0 * math.pi / n
    ang_a = ((split * j1 * k) % n).astype(F32) * w
    ang_b = ((j2 * k) % n).astype(F32) * w
    ca, sa = jnp.cos(ang_a)[:, None, :], jnp.sin(ang_a)[:, None, :]
    cb, sb = (jnp.cos(ang_b) * scale)[None, :, :], (jnp.sin(ang_b) * scale)[None, :, :]
    c = (ca * cb - sa * sb).reshape(n, n)
    s = (sa * cb + ca * sb).reshape(n, n)
    return c, s


_NT = (((1,), (1,)), ((), ()))
_TN = (((0,), (0,)), ((), ()))


def _gla_block(q_ref, k_ref, v_ref, lr_ref, w3_ref, b_ref, st_ref, reverse, qscale, nchunk):
    R = nchunk * CHUNK
    lr = lr_ref[...]
    lr_hi = lr.astype(BF16)
    lr_lo = (lr - lr_hi.astype(F32)).astype(BF16)
    z = jnp.dot(jnp.concatenate([lr_hi, lr_lo, lr_hi], axis=1), w3_ref[...],
                preferred_element_type=F32) + b_ref[...]
    g = (jnp.minimum(z, 0.0) - jnp.log(1.0 + jnp.exp(-jnp.abs(z)))) * (1.0 / GATE_LOGIT_NORMALIZER)
    ri = lax.broadcasted_iota(jnp.int32, (R, R), 0)
    ci = lax.broadcasted_iota(jnp.int32, (R, R), 1)
    cum = ((ci >= ri) if reverse else (ci <= ri)).astype(BF16)
    g_hi = g.astype(BF16)
    g_lo = (g - g_hi.astype(F32)).astype(BF16)
    G = jnp.dot(cum, g_hi, preferred_element_type=F32) + jnp.dot(cum, g_lo, preferred_element_type=F32)

    dk = G.shape[1]
    zero_row = jnp.zeros((1, dk), F32)
    if reverse:
        starts = [G[(c + 1) * CHUNK:(c + 1) * CHUNK + 1, :] if c + 1 < nchunk else zero_row for c in range(nchunk)]
        ref_row, g_tot = CHUNK // 2, G[0:1, :]
    else:
        starts = [G[c * CHUNK - 1:c * CHUNK, :] if c > 0 else zero_row for c in range(nchunk)]
        ref_row, g_tot = CHUNK // 2 - 1, G[R - 1:R, :]
    bcast = lambda rows_: jnp.concatenate([jnp.broadcast_to(r_, (CHUNK, dk)) for r_ in rows_], axis=0)
    gc = G - bcast(starts)
    gref = bcast([gc[c * CHUNK + ref_row:c * CHUNK + ref_row + 1, :] for c in range(nchunk)])

    q = q_ref[...].astype(F32) * qscale
    k = k_ref[...].astype(F32)
    v = v_ref[...]
    q_in = (q * jnp.exp(gc - gref)).astype(BF16)
    k_in = (k * jnp.exp(gref - gc)).astype(BF16)
    q_it = (q * jnp.exp(gc)).astype(BF16)
    q_st = (q * jnp.exp(G)).astype(BF16)
    k_st = (k * jnp.exp(g_tot - G)).astype(BF16)

    s_diag = lax.dot_general(q_in, k_in, _NT, preferred_element_type=F32)
    same = (ri // CHUNK) == (ci // CHUNK)
    keep = same & ((ci > ri) if reverse else (ci <= ri))
    s_rows = []
    for c in range(nchunk):
        rows = slice(c * CHUNK, (c + 1) * CHUNK)
        s = jnp.where(keep[rows, :], s_diag[rows, :], 0.0)
        lo_, hi_ = ((c + 1) * CHUNK, R) if reverse else (0, c * CHUNK)
        if hi_ > lo_:
            kx = (k[lo_:hi_, :] * jnp.exp(starts[c] - G[lo_:hi_, :])).astype(BF16)
            pad = jnp.zeros((R - (hi_ - lo_), dk), BF16)
            kx = jnp.concatenate([pad, kx] if reverse else [kx, pad], axis=0)
            s = s + lax.dot_general(q_it[rows, :], kx, _NT, preferred_element_type=F32)
        s_rows.append(s.astype(BF16))
    scores = jnp.concatenate(s_rows, axis=0)

    st = st_ref[...]
    o = (jnp.dot(scores, v, preferred_element_type=F32)
         + lax.dot_general(q_st, st.astype(BF16), _NT, preferred_element_type=F32))
    st_ref[...] = st * jnp.exp(g_tot) + lax.dot_general(v, k_st, _TN, preferred_element_type=F32)
    return o


def _gla_kernel(qf_ref, kf_ref, vf_ref, qb_ref, kb_ref, vb_ref, lrf_ref, lrb_ref,
                wf_ref, bf_ref, wb_ref, bb_ref, *rest, nchunk, nsub, qscale):
    o_ref, stf_ref, stb_ref = rest[-3:]
    n = pl.program_id(2)
    nb = pl.num_programs(2)
    blk = nchunk * CHUNK
    rows = nsub * blk

    @pl.when(n == 0)
    def _():
        o_ref[...] = jnp.zeros_like(o_ref)
        stf_ref[...] = jnp.zeros_like(stf_ref)
        stb_ref[...] = jnp.zeros_like(stb_ref)

    for s_f in range(nsub):
        s_b = nsub - 1 - s_f
        sub_f, sub_b = pl.ds(s_f * blk, blk), pl.ds(s_b * blk, blk)
        o_f = _gla_block(qf_ref.at[sub_f, :], kf_ref.at[sub_f, :], vf_ref.at[sub_f, :], lrf_ref.at[sub_f, :],
                         wf_ref, bf_ref, stf_ref, False, qscale, nchunk)
        o_b = _gla_block(qb_ref.at[sub_b, :], kb_ref.at[sub_b, :], vb_ref.at[sub_b, :], lrb_ref.at[sub_b, :],
                         wb_ref, bb_ref, stb_ref, True, qscale, nchunk)
        o_ref[pl.ds(pl.multiple_of(n * rows + s_f * blk, blk), blk), :] += o_f
        o_ref[pl.ds(pl.multiple_of((nb - 1 - n) * rows + s_b * blk, blk), blk), :] += o_b


def _hi_hi_lo(w):
    hi = w.astype(BF16)
    lo = (w - hi.astype(F32)).astype(BF16)
    return jnp.concatenate([hi, hi, lo], axis=0)


def _gla(proj, lr_f, lr_b, wup_f, b_f, wup_b, b_b, row0, B, S, q_blk0, k_blk0, v_blk0, dk, dv,
         prev=None, blk=256, nsub=2):
    T = proj.shape[0]
    nsub = nsub if S % (nsub * blk) == 0 else 1
    rows = nsub * blk
    assert S % rows == 0 and row0 % S == 0
    nb = S // rows
    rb0, sb0 = row0 // rows, row0 // S
    fmap = lambda b, n: rb0 + b * nb + n
    bmap = lambda b, n: rb0 + b * nb + (nb - 1 - n)
    in_specs = [
        pl.BlockSpec((rows, dk), lambda b, h, n: (fmap(b, n), q_blk0 + h)),
        pl.BlockSpec((rows, dk), lambda b, h, n: (fmap(b, n), k_blk0 + h)),
        pl.BlockSpec((rows, dv), lambda b, h, n: (fmap(b, n), v_blk0 + h)),
        pl.BlockSpec((rows, dk), lambda b, h, n: (bmap(b, n), q_blk0 + h)),
        pl.BlockSpec((rows, dk), lambda b, h, n: (bmap(b, n), k_blk0 + h)),
        pl.BlockSpec((rows, dv), lambda b, h, n: (bmap(b, n), v_blk0 + h)),
        pl.BlockSpec((rows, GATE_LOW_RANK), lambda b, h, n: (fmap(b, n), 0)),
        pl.BlockSpec((rows, GATE_LOW_RANK), lambda b, h, n: (bmap(b, n), 0)),
        pl.BlockSpec((3 * GATE_LOW_RANK, dk), lambda b, h, n: (0, h)),
        pl.BlockSpec((1, dk), lambda b, h, n: (0, h)),
        pl.BlockSpec((3 * GATE_LOW_RANK, dk), lambda b, h, n: (0, h)),
        pl.BlockSpec((1, dk), lambda b, h, n: (0, h)),
    ]
    args = [proj, proj, proj, proj, proj, proj, lr_f, lr_b,
            _hi_hi_lo(wup_f), b_f.reshape(1, -1), _hi_hi_lo(wup_b), b_b.reshape(1, -1)]
    aliases = {}
    if prev is not None:
        in_specs.append(pl.BlockSpec(memory_space=pl.ANY))
        args.append(prev)
        aliases = {len(args) - 1: 0}
    return pl.pallas_call(
        functools.partial(_gla_kernel, nchunk=blk // CHUNK, nsub=nsub, qscale=dk ** -0.5),
        grid=(B, GLA_HEADS, nb),
        in_specs=in_specs,
        out_specs=pl.BlockSpec((S, dv), lambda b, h, n: (sb0 + b, h)),
        out_shape=jax.ShapeDtypeStruct((T, GLA_HEADS * dv), F32),
        scratch_shapes=[pltpu.VMEM((dv, dk), F32), pltpu.VMEM((dv, dk), F32)],
        input_output_aliases=aliases,
        compiler_params=_cparams(("parallel", "parallel", "arbitrary")),
        name="gla",
    )(*args)


def _merge_kernel(fft_ref, o_ref_in, og_ref, g0_ref, g1_ref, hn_ref, wf_ref, wg_ref, o_ref, a_ref, *, dv):
    ya = jnp.dot(fft_ref[...], wf_ref[...], preferred_element_type=F32)
    for h in range(GLA_HEADS):
        cs = slice(h * dv, (h + 1) * dv)
        o = o_ref_in[:, cs]
        var = jnp.mean(o * o, axis=-1, keepdims=True)
        on = o * lax.rsqrt(var + EPS) * hn_ref[...]
        og = og_ref[:, cs].astype(F32)
        a_ref[:, cs] = (on * (og * jax.nn.sigmoid(og))).astype(BF16)
    yb = jnp.dot(a_ref[...], wg_ref[...], preferred_element_type=F32)
    m = jax.nn.sigmoid(g0_ref[...].astype(F32)) * ya + jax.nn.sigmoid(g1_ref[...].astype(F32)) * yb
    o_ref[...] = m.astype(BF16)


def _merge(fft, o_gla, proj, og_blk, g0_blk, g1_blk, hn, wf, wg, tm=256):
    T, D = o_gla.shape
    FW = fft.shape[1]
    dv = D // GLA_HEADS
    const = dict(pipeline_mode=pl.Buffered(1))
    return pl.pallas_call(
        functools.partial(_merge_kernel, dv=dv),
        grid=(T // tm,),
        in_specs=[
            pl.BlockSpec((tm, FW), lambda i: (i, 0)),
            pl.BlockSpec((tm, D), lambda i: (i, 0)),
            pl.BlockSpec((tm, D), lambda i: (i, og_blk)),
            pl.BlockSpec((tm, D), lambda i: (i, g0_blk)),
            pl.BlockSpec((tm, D), lambda i: (i, g1_blk)),
            pl.BlockSpec((1, dv), lambda i: (0, 0)),
            pl.BlockSpec((FW, D), lambda i: (0, 0), **const),
            pl.BlockSpec((D, D), lambda i: (0, 0), **const),
        ],
        out_specs=pl.BlockSpec((tm, D), lambda i: (i, 0)),
        out_shape=jax.ShapeDtypeStruct((T, D), BF16),
        scratch_shapes=[pltpu.VMEM((tm, D), BF16)],
        compiler_params=_cparams(("parallel",)),
        name="merge",
    )(fft, o_gla, proj, proj, proj, hn.reshape(1, dv), wf, wg)


HI16 = 0xFFFF0000


def _pack_bf16_pair(lo, hi):
    lo_bits = lax.bitcast_convert_type(lo.astype(BF16).astype(F32), jnp.uint32)
    hi_bits = lax.bitcast_convert_type(hi.astype(BF16).astype(F32), jnp.uint32)
    return (hi_bits & jnp.uint32(HI16)) | (lo_bits >> 16)


def _unpack_bf16_pair(w):
    lo = lax.bitcast_convert_type(w << 16, F32)
    hi = lax.bitcast_convert_type(w & jnp.uint32(HI16), F32)
    return lo, hi


def _outproj_router_kernel(m_ref, xp_ref, xs_ref, wo_ref, g_ref, wr_ref, br_ref,
                           x1_ref, xn_ref, idx_ref, tw_ref, rank_ref, cnt_ref, run_ref, *, n_exp, n0):
    i = pl.program_id(0)

    @pl.when(i == 0)
    def _():
        run_ref[...] = jnp.zeros_like(run_ref)

    tm, D = m_ref.shape
    x = jnp.where(i < n0, xp_ref[...], xs_ref[...])
    x1 = x + jnp.dot(m_ref[...], wo_ref[...], preferred_element_type=F32)
    x1_ref[...] = x1
    var = jnp.mean(x1 * x1, axis=-1, keepdims=True)
    xn = x1 * lax.rsqrt(var + EPS) * g_ref[...]
    xn_ref[...] = _pack_bf16_pair(xn[:, :D // 2], xn[:, D // 2:])
    xh = xn.astype(BF16)
    xl = (xn - xh.astype(F32)).astype(BF16)
    r = jnp.dot(xh, wr_ref[...], preferred_element_type=F32)
    lg = (r[:, :n_exp] + r[:, n_exp:] + jnp.dot(xl, wr_ref[:, :n_exp], preferred_element_type=F32)
          + br_ref[...])

    lane = lax.broadcasted_iota(jnp.int32, (tm, n_exp), 1)
    vals, hots = [], []
    for _ in range(TOP_K):
        mx = jnp.max(lg, axis=-1, keepdims=True)
        ik = jnp.min(jnp.where(lg == mx, lane, n_exp), axis=-1, keepdims=True)
        hot = lane == ik
        vals.append(mx)
        hots.append(hot)
        lg = jnp.where(hot, -jnp.inf, lg)
    exps = [jnp.exp(v - vals[0]) for v in vals]
    denom = exps[0] + exps[1] + exps[2] + exps[3]

    sel = hots[0] | hots[1] | hots[2] | hots[3]
    sel_f = sel.astype(F32)
    r = lax.broadcasted_iota(jnp.int32, (tm, tm), 0)
    c = lax.broadcasted_iota(jnp.int32, (tm, tm), 1)
    strict = (c < r).astype(BF16)
    before = jnp.dot(strict, sel_f.astype(BF16), preferred_element_type=F32) + run_ref[...]
    run_ref[...] += jnp.sum(sel_f, axis=0, keepdims=True)
    cnt_ref[...] = run_ref[...].astype(jnp.int32)

    k4 = lax.broadcasted_iota(jnp.int32, (tm, TOP_K), 1)
    idx4 = jnp.zeros((tm, TOP_K), jnp.int32)
    w4 = jnp.zeros((tm, TOP_K), F32)
    rk4 = jnp.zeros((tm, TOP_K), jnp.int32)
    for k in range(TOP_K):
        ik = jnp.sum(jnp.where(hots[k], lane, 0), axis=-1, keepdims=True)
        rk = jnp.sum(jnp.where(hots[k], before, 0.0), axis=-1, keepdims=True).astype(jnp.int32)
        idx4 = jnp.where(k4 == k, ik, idx4)
        w4 = jnp.where(k4 == k, exps[k] / denom, w4)
        rk4 = jnp.where(k4 == k, rk, rk4)
    idx_ref[...] = idx4
    tw_ref[...] = w4
    rank_ref[...] = rk4


def _outproj_router(merged, xp, xs, wo, gain, wr2, br, tm=256):
    T, D = merged.shape
    E = wr2.shape[1] // 2
    n0 = xp.shape[0] // tm
    const = dict(pipeline_mode=pl.Buffered(1))
    row = lambda i: (i, 0)
    fix = lambda i: (0, 0)
    return pl.pallas_call(
        functools.partial(_outproj_router_kernel, n_exp=E, n0=n0),
        grid=(T // tm,),
        in_specs=[
            pl.BlockSpec((tm, D), row),
            pl.BlockSpec((tm, D), lambda i: (jnp.minimum(i, n0 - 1), 0)),
            pl.BlockSpec((tm, D), lambda i: (jnp.maximum(i - n0, 0), 0)),
            pl.BlockSpec((D, D), fix, **const),
            pl.BlockSpec((1, D), fix),
            pl.BlockSpec((D, 2 * E), fix),
            pl.BlockSpec((1, E), fix),
        ],
        out_specs=[
            pl.BlockSpec((tm, D), row),
            pl.BlockSpec((tm, D // 2), row),
            pl.BlockSpec((tm, TOP_K), row),
            pl.BlockSpec((tm, TOP_K), row),
            pl.BlockSpec((tm, TOP_K), row),
            pl.BlockSpec((1, E), fix),
        ],
        out_shape=[
            jax.ShapeDtypeStruct((T, D), F32),
            jax.ShapeDtypeStruct((T, D // 2), jnp.uint32),
            jax.ShapeDtypeStruct((T, TOP_K), jnp.int32),
            jax.ShapeDtypeStruct((T, TOP_K), F32),
            jax.ShapeDtypeStruct((T, TOP_K), jnp.int32),
            jax.ShapeDtypeStruct((1, E), jnp.int32),
        ],
        scratch_shapes=[pltpu.VMEM((1, E), F32)],
        compiler_params=_cparams(("arbitrary",)),
        name="outproj_router",
    )(merged, xp, xs, wo, gain.reshape(1, D), wr2, br.reshape(1, E))


def _row_copy(src, dst, sem):
    return pltpu.make_async_copy(src, dst, sem)


def _dispatch_kernel(pos_ref, x_ref, xs_ref, sem):
    tm = x_ref.shape[0]

    def body(t, carry):
        for k in range(TOP_K):
            p = pos_ref[t * TOP_K + k]
            _row_copy(x_ref.at[pl.ds(t, 1), :], xs_ref.at[pl.ds(p, 1), :], sem).start()
        return carry

    lax.fori_loop(0, tm, body, 0)
    for _ in range(TOP_K):
        _row_copy(x_ref, xs_ref.at[pl.ds(0, tm), :], sem).wait()


def _dispatch(xn, pos_flat, n_slots, tm=256):
    T, D = xn.shape
    return pl.pallas_call(
        _dispatch_kernel,
        grid=(T // tm,),
        in_specs=[
            pl.BlockSpec((tm * TOP_K,), lambda i: (i,), memory_space=pltpu.SMEM),
            pl.BlockSpec((tm, D), lambda i: (i, 0)),
        ],
        out_specs=pl.BlockSpec(memory_space=pl.ANY),
        out_shape=jax.ShapeDtypeStruct((n_slots, D), xn.dtype),
        scratch_shapes=[pltpu.SemaphoreType.DMA(())],
        compiler_params=_cparams(("arbitrary",)),
        name="dispatch",
    )(pos_flat, xn)


def _combine_kernel(pos_ref, posn_ref, tw_ref, x1_ref, g_ref, ys_ref, op_ref, os_ref, buf_ref, sem, *, n0, tn):
    i = pl.program_id(0)
    n = pl.num_programs(0)
    tm, D = x1_ref.shape
    slot = i % 2

    def gather(p_ref, s):
        def body(t, carry):
            for k in range(TOP_K):
                p = p_ref[t * TOP_K + k]
                _row_copy(ys_ref.at[pl.ds(p, 1), :], buf_ref.at[s, k, pl.ds(t, 1), :], sem.at[s]).start()
            return carry

        lax.fori_loop(0, tm, body, 0, unroll=2)

    @pl.when(i == 0)
    def _():
        gather(pos_ref, 0)

    @pl.when(i + 1 < n)
    def _():
        gather(posn_ref, 1 - slot)

    for k in range(TOP_K):
        _row_copy(ys_ref.at[pl.ds(0, tm), :], buf_ref.at[slot, k], sem.at[slot]).wait()

    tw = tw_ref[...]
    half = tn // 2
    pieces = [jnp.zeros((tm, half), F32) for _ in range(D // half)]
    for k in range(TOP_K):
        wk = tw[:, k:k + 1]
        for jb in range(D // tn):
            lo, hi = _unpack_bf16_pair(buf_ref[slot, k, :, jb * half:(jb + 1) * half])
            pieces[2 * jb] = pieces[2 * jb] + wk * lo
            pieces[2 * jb + 1] = pieces[2 * jb + 1] + wk * hi
    x2 = x1_ref[...] + jnp.concatenate(pieces, axis=1)
    var = jnp.mean(x2 * x2, axis=-1, keepdims=True)
    y = x2 * lax.rsqrt(var + EPS) * g_ref[...]

    @pl.when(i < n0)
    def _():
        op_ref[...] = y

    @pl.when(i >= n0)
    def _():
        os_ref[...] = y


def _combine(pos_flat, tw, x1, gain, ys, t_prompt, tn, tm=256):
    T, D = x1.shape
    n0 = t_prompt // tm
    nlast = T // tm - 1
    return pl.pallas_call(
        functools.partial(_combine_kernel, n0=n0, tn=tn),
        grid=(T // tm,),
        in_specs=[
            pl.BlockSpec((tm * TOP_K,), lambda i: (i,), memory_space=pltpu.SMEM),
            pl.BlockSpec((tm * TOP_K,), lambda i: (jnp.minimum(i + 1, nlast),), memory_space=pltpu.SMEM),
            pl.BlockSpec((tm, TOP_K), lambda i: (i, 0)),
            pl.BlockSpec((tm, D), lambda i: (i, 0)),
            pl.BlockSpec((1, D), lambda i: (0, 0)),
            pl.BlockSpec(memory_space=pl.ANY),
        ],
        out_specs=[
            pl.BlockSpec((tm, D), lambda i: (jnp.minimum(i, n0 - 1), 0)),
            pl.BlockSpec((tm, D), lambda i: (jnp.maximum(i - n0, 0), 0)),
        ],
        out_shape=[jax.ShapeDtypeStruct((t_prompt, D), F32), jax.ShapeDtypeStruct((T - t_prompt, D), F32)],
        scratch_shapes=[pltpu.VMEM((2, TOP_K, tm, D // 2), jnp.uint32), pltpu.SemaphoreType.DMA((2,))],
        compiler_params=_cparams(("arbitrary",)),
        name="combine",
    )(pos_flat, pos_flat, tw, x1, gain.reshape(1, D), ys)


def _stream_row_tiles(n, in_copy, out_copy, compute):
    def body(r, carry):
        slot = r % 2

        @pl.when(r + 1 < n)
        def _():
            in_copy(r + 1, 1 - slot).start()

        in_copy(r, slot).wait()

        @pl.when(r >= 2)
        def _():
            out_copy(r - 2, slot).wait()

        compute(r, slot)
        out_copy(r, slot).start()
        return carry

    lax.fori_loop(0, n, body, 0)

    @pl.when(n >= 2)
    def _():
        out_copy(n - 2, n % 2).wait()

    out_copy(n - 1, (n - 1) % 2).wait()


def _gate_up_kernel(g0_ref, nt_ref, cnt_ref, w_ref, bg_ref, bu_ref, x_hbm, o_hbm,
                    wp_ref, xin_ref, xb_ref, obuf_ref, sin, sout, *, tn):
    j, e = pl.program_id(0), pl.program_id(1)
    tm = GROUP_TILE
    D = xb_ref.shape[1]
    half = MXU_DIM // 2
    n = nt_ref[e]
    row_base = g0_ref[e]
    cnt = cnt_ref[e]

    def rows(r):
        return pl.ds(pl.multiple_of(row_base + r * tm, tm), tm)

    def in_copy(r, slot):
        return pltpu.make_async_copy(x_hbm.at[rows(r), :], xin_ref.at[slot], sin.at[slot])

    def out_copy(r, slot):
        cols = pl.ds(pl.multiple_of(j * (tn // 2), LANES), tn // 2)
        return pltpu.make_async_copy(obuf_ref.at[slot], o_hbm.at[rows(r), cols], sout.at[slot])

    def compute(r, slot):
        valid = (r * tm + lax.broadcasted_iota(jnp.int32, (tm, 1), 0)) < cnt
        lo, hi = _unpack_bf16_pair(xin_ref[slot])
        xb_ref[:, :D // 2] = jnp.where(valid, lo, 0.0).astype(BF16)
        xb_ref[:, D // 2:] = jnp.where(valid, hi, 0.0).astype(BF16)
        for cb in range(tn // MXU_DIM):
            h = jnp.dot(xb_ref[...], wp_ref[:, cb * MXU_DIM:(cb + 1) * MXU_DIM], preferred_element_type=F32)
            hg = h[:, :half] + bg_ref[0, :, cb * half:(cb + 1) * half]
            hu = h[:, half:] + bu_ref[0, :, cb * half:(cb + 1) * half]
            gate = jnp.minimum(hg, SWIGLU_LIMIT)
            up = jnp.clip(hu, -SWIGLU_LIMIT, SWIGLU_LIMIT)
            act = gate * jax.nn.sigmoid(SWIGLU_ALPHA * gate) * (up + 1.0)
            obuf_ref[slot, :, cb * half:(cb + 1) * half] = act.astype(obuf_ref.dtype)

    @pl.when(n > 0)
    def _():
        in_copy(0, 0).start()
        r_i = lax.broadcasted_iota(jnp.int32, (MXU_DIM, MXU_DIM), 0)
        c_i = lax.broadcasted_iota(jnp.int32, (MXU_DIM, MXU_DIM), 1)
        perm = (((c_i < half) & (r_i == 2 * c_i)) | ((c_i >= half) & (r_i == 2 * (c_i - half) + 1))).astype(BF16)
        for cb in range(tn // MXU_DIM):
            for rb in range(D // 512):
                w = w_ref[0, rb * 512:(rb + 1) * 512, cb * MXU_DIM:(cb + 1) * MXU_DIM].astype(BF16)
                wp_ref[rb * 512:(rb + 1) * 512, cb * MXU_DIM:(cb + 1) * MXU_DIM] = jnp.dot(
                    w, perm, preferred_element_type=F32).astype(BF16)
        _stream_row_tiles(n, in_copy, out_copy, compute)


def _gate_up(g0, ntiles, cnt, xs, w_gate_up, bg, bu, tn):
    P = xs.shape[0]
    E, D, H2 = w_gate_up.shape
    tm = GROUP_TILE
    wmap = lambda j, e, *_: (e, 0, j)
    gs = pltpu.PrefetchScalarGridSpec(
        num_scalar_prefetch=3,
        grid=(H2 // tn, E),
        in_specs=[
            pl.BlockSpec((1, D, tn), wmap),
            pl.BlockSpec((1, 1, tn // 2), wmap),
            pl.BlockSpec((1, 1, tn // 2), wmap),
            pl.BlockSpec(memory_space=pl.ANY),
        ],
        out_specs=pl.BlockSpec(memory_space=pl.ANY),
        scratch_shapes=[
            pltpu.VMEM((D, tn), BF16),
            pltpu.VMEM((2, tm, D // 2), jnp.uint32),
            pltpu.VMEM((tm, D), BF16),
            pltpu.VMEM((2, tm, tn // 2), BF16),
            pltpu.SemaphoreType.DMA((2,)),
            pltpu.SemaphoreType.DMA((2,)),
        ],
    )
    return pl.pallas_call(
        functools.partial(_gate_up_kernel, tn=tn),
        grid_spec=gs,
        out_shape=jax.ShapeDtypeStruct((P, H2 // 2), BF16),
        compiler_params=_cparams(("arbitrary", "arbitrary")),
        name="moe_gate_up",
    )(g0, ntiles, cnt, w_gate_up, bg, bu, xs)


def _down_kernel(g0_ref, nt_ref, w_ref, b_ref, a_hbm, o_hbm, wb_ref, ain_ref, obuf_ref, sin, sout, *, tn):
    j, e = pl.program_id(0), pl.program_id(1)
    tm = GROUP_TILE
    H = wb_ref.shape[0]
    n = nt_ref[e]
    row_base = g0_ref[e]

    def rows(r):
        return pl.ds(pl.multiple_of(row_base + r * tm, tm), tm)

    def in_copy(r, slot):
        return pltpu.make_async_copy(a_hbm.at[rows(r), :], ain_ref.at[slot], sin.at[slot])

    def out_copy(r, slot):
        cols = pl.ds(pl.multiple_of(j * (tn // 2), LANES), tn // 2)
        return pltpu.make_async_copy(obuf_ref.at[slot], o_hbm.at[rows(r), cols], sout.at[slot])

    def compute(r, slot):
        a = ain_ref[slot]
        for c0 in range(0, tn // 2, MXU_DIM):
            c1 = tn // 2 + c0
            lo = jnp.dot(a, wb_ref[:, c0:c0 + MXU_DIM], preferred_element_type=F32) + b_ref[0, :, c0:c0 + MXU_DIM]
            hi = jnp.dot(a, wb_ref[:, c1:c1 + MXU_DIM], preferred_element_type=F32) + b_ref[0, :, c1:c1 + MXU_DIM]
            obuf_ref[slot, :, c0:c0 + MXU_DIM] = _pack_bf16_pair(lo, hi)

    @pl.when(n > 0)
    def _():
        in_copy(0, 0).start()
        for rb in range(H // 512):
            wb_ref[rb * 512:(rb + 1) * 512, :] = w_ref[0, rb * 512:(rb + 1) * 512, :].astype(BF16)
        _stream_row_tiles(n, in_copy, out_copy, compute)


def _down(g0, ntiles, act, wd, bd, tn):
    P, H = act.shape
    E, _, D = wd.shape
    tm = GROUP_TILE
    wmap = lambda j, e, *_: (e, 0, j)
    gs = pltpu.PrefetchScalarGridSpec(
        num_scalar_prefetch=2,
        grid=(D // tn, E),
        in_specs=[
            pl.BlockSpec((1, H, tn), wmap),
            pl.BlockSpec((1, 1, tn), wmap),
            pl.BlockSpec(memory_space=pl.ANY),
        ],
        out_specs=pl.BlockSpec(memory_space=pl.ANY),
        scratch_shapes=[
            pltpu.VMEM((H, tn), BF16),
            pltpu.VMEM((2, tm, H), BF16),
            pltpu.VMEM((2, tm, tn // 2), jnp.uint32),
            pltpu.SemaphoreType.DMA((2,)),
            pltpu.SemaphoreType.DMA((2,)),
        ],
    )
    return pl.pallas_call(
        functools.partial(_down_kernel, tn=tn),
        grid_spec=gs,
        out_shape=jax.ShapeDtypeStruct((P, D // 2), jnp.uint32),
        compiler_params=_cparams(("arbitrary", "arbitrary")),
        name="moe_down",
    )(g0, ntiles, wd, bd, act)


def _trunk(xp, xs, seq_shapes, norm_mix, w_in, w_gk_up_fwd, b_gk_fwd, w_gk_up_bwd, b_gk_bwd, gla_head_norm,
           w_fnet_out, w_gla_out, w_out, norm_ffn, w_router, b_router, w_gate_up, b_gate_up,
           w_down, b_down, norm_final):
    D = xp.shape[1]
    T = xp.shape[0] + xs.shape[0]
    fw = w_fnet_out.shape[0]
    dkk = w_gk_up_fwd.shape[1]
    dvv = w_gla_out.shape[0]
    dk, dv = dkk // GLA_HEADS, dvv // GLA_HEADS
    sizes = (fw, dkk, dkk, dvv, dvv, GATE_LOW_RANK, GATE_LOW_RANK, 2 * D)
    offs = np.concatenate([[0], np.cumsum(sizes)])
    sl = lambda n: slice(int(offs[n]), int(offs[n + 1]))
    w_u, w_q, w_k, w_v, w_og = (w_in[:, sl(n)] for n in range(5))
    w_lr = w_in[:, int(offs[5]):int(offs[7])]
    w_g = w_in[:, sl(7)]
    w_main = jnp.concatenate([w_og, w_g, w_v, w_u, w_q, w_k], axis=1).astype(BF16)
    og_blk, g0_blk, g1_blk = 0, dvv // D, dvv // D + 1
    v_off = dvv + 2 * D
    u_off = v_off + dvv
    q_off = u_off + fw
    k_off = q_off + dkk

    proj, lr = _inproj(xp, xs, norm_mix, w_main, _hi_lo(w_lr))
    lr_f, lr_b = lr[:, :GATE_LOW_RANK], lr[:, GATE_LOW_RANK:]

    gd = fw // FNET_GROUPS
    cc, sc = _dft_mats(gd, gd ** -0.5)
    cs = jnp.concatenate([cc, sc], axis=1).astype(BF16)
    zc, zs = _chan_dft(proj, u_off // fw, fw, cs)
    fft, o_gla = None, None
    row0 = 0
    for (B, S) in seq_shapes:
        cm, sm = _dft_mats(S, S ** -0.5)
        fft = _seq_dft(zc, zs, cm.astype(BF16), (-sm).astype(BF16), row0, B, S, prev=fft)
        o_gla = _gla(proj, lr_f, lr_b, w_gk_up_fwd, b_gk_fwd, w_gk_up_bwd, b_gk_bwd, row0, B, S,
                     q_off // dk, k_off // dk, v_off // dv, dk, dv, prev=o_gla)
        row0 += B * S

    merged = _merge(fft, o_gla, proj, og_blk, g0_blk, g1_blk, gla_head_norm,
                    w_fnet_out.astype(BF16), w_gla_out.astype(BF16))
    x1, xn2, idx, tw, rank, cnt = _outproj_router(merged, xp, xs, w_out.astype(BF16), norm_ffn,
                                                  _hi_lo(w_router), b_router)

    E = w_router.shape[1]
    cnt = cnt.reshape(E)
    gsz = ((cnt + GROUP_TILE - 1) // GROUP_TILE) * GROUP_TILE
    gend = jnp.cumsum(gsz)
    gstart = gend - gsz
    pos = (gstart[idx] + rank).reshape(-1).astype(jnp.int32)
    n_slots = T * TOP_K + E * GROUP_TILE
    g0 = gstart.astype(jnp.int32)
    ntiles = (gsz // GROUP_TILE).astype(jnp.int32)

    x_sorted = _dispatch(xn2, pos, n_slots)
    H = w_down.shape[1]
    bg = b_gate_up[:, 0::2].reshape(E, 1, H)
    bu = b_gate_up[:, 1::2].reshape(E, 1, H)
    act = _gate_up(g0, ntiles, cnt, x_sorted, w_gate_up, bg, bu, GATE_UP_TN)
    ys = _down(g0, ntiles, act, w_down, b_down.reshape(E, 1, D), DOWN_TN)
    return _combine(pos, tw, x1, norm_final, ys, xp.shape[0], DOWN_TN)


def kernel(x_prompt, x_sample, norm_mix, w_in, w_gk_up_fwd, b_gk_fwd, w_gk_up_bwd, b_gk_bwd, gla_head_norm,
           w_fnet_out, w_gla_out, w_out, norm_ffn, w_router, b_router, w_gate_up, b_gate_up, w_down,
           b_down, norm_final):
    D = x_prompt.shape[-1]
    shapes = (x_prompt.shape[:2], x_sample.shape[:2])
    yp, ys = _trunk(x_prompt.reshape(-1, D), x_sample.reshape(-1, D), shapes, norm_mix[0], w_in[0], w_gk_up_fwd[0], b_gk_fwd[0], w_gk_up_bwd[0], b_gk_bwd[0],
               gla_head_norm[0], w_fnet_out[0], w_gla_out[0], w_out[0], norm_ffn[0], w_router[0],
               b_router[0], w_gate_up[0], b_gate_up[0], w_down[0], b_down[0], norm_final)
    return (yp.reshape(x_prompt.shape), ys.reshape(x_sample.shape))
```

```python
import functools
import math

import numpy as np
import jax
import jax.numpy as jnp
from jax import lax
from jax.experimental import pallas as pl
from jax.experimental.pallas import tpu as pltpu

F32 = jnp.float32
BF16 = jnp.bfloat16
HIGHEST = lax.Precision.HIGHEST

EPS = 1e-5
FNET_GROUPS = 4
GLA_HEADS = 4
GATE_LOW_RANK = 16
GATE_LOGIT_NORMALIZER = 16.0
CHUNK = 64
TOP_K = 4
SWIGLU_LIMIT = 7.0
SWIGLU_ALPHA = 1.702

VMEM_LIMIT_BYTES = 56 * 1024 * 1024
MXU_DIM = 256
LANES = 128
GROUP_TILE = 256
GATE_UP_TN = 2048


def _cparams(sem):
    return pltpu.CompilerParams(dimension_semantics=sem, vmem_limit_bytes=VMEM_LIMIT_BYTES)


def _split3(x):
    hi = x.astype(BF16)
    r = x - hi.astype(F32)
    mid = r.astype(BF16)
    lo = (r - mid.astype(F32)).astype(BF16)
    return hi, mid, lo


def _hi_lo(w):
    hi = w.astype(BF16)
    lo = (w - hi.astype(F32)).astype(BF16)
    return jnp.concatenate([hi, lo], axis=1)


def _inproj_kernel(xp_ref, xs_ref, g_ref, w_ref, wlr_ref, o_ref, lr_ref, xn_ref, *, n0):
    @pl.when(pl.program_id(1) == 0)
    def _():
        x = jnp.where(pl.program_id(0) < n0, xp_ref[...], xs_ref[...])
        var = jnp.mean(x * x, axis=-1, keepdims=True)
        xn = (x * lax.rsqrt(var + EPS) * g_ref[...]).astype(BF16)
        xn_ref[...] = xn
        r = jnp.dot(xn, wlr_ref[...], preferred_element_type=F32)
        nlr = lr_ref.shape[1]
        lr_ref[...] = r[:, :nlr] + r[:, nlr:]

    o_ref[...] = jnp.dot(xn_ref[...], w_ref[...], preferred_element_type=F32).astype(o_ref.dtype)


def _inproj(xp, xs, gain, w_main, w_lr2, tm=512, tn=1024):
    D = xp.shape[1]
    T = xp.shape[0] + xs.shape[0]
    n0 = xp.shape[0] // tm
    N = w_main.shape[1]
    R = w_lr2.shape[1] // 2
    return pl.pallas_call(
        functools.partial(_inproj_kernel, n0=n0),
        grid=(T // tm, N // tn),
        in_specs=[
            pl.BlockSpec((tm, D), lambda i, j: (jnp.minimum(i, n0 - 1), 0)),
            pl.BlockSpec((tm, D), lambda i, j: (jnp.maximum(i - n0, 0), 0)),
            pl.BlockSpec((1, D), lambda i, j: (0, 0)),
            pl.BlockSpec((D, tn), lambda i, j: (0, j)),
            pl.BlockSpec((D, 2 * R), lambda i, j: (0, 0)),
        ],
        out_specs=[
            pl.BlockSpec((tm, tn), lambda i, j: (i, j)),
            pl.BlockSpec((tm, R), lambda i, j: (i, 0)),
        ],
        out_shape=[jax.ShapeDtypeStruct((T, N), BF16), jax.ShapeDtypeStruct((T, R), F32)],
        scratch_shapes=[pltpu.VMEM((tm, D), BF16)],
        compiler_params=_cparams(("parallel", "arbitrary")),
        name="inproj",
    )(xp, xs, gain.reshape(1, D), w_main, w_lr2)


def _chan_dft_kernel(u_ref, cs_ref, zc_ref, zs_ref, *, gd):
    for g in range(FNET_GROUPS):
        r = jnp.dot(u_ref[:, g * gd:(g + 1) * gd], cs_ref[...], preferred_element_type=F32)
        zc_ref[:, g * gd:(g + 1) * gd] = r[:, :gd].astype(BF16)
        zs_ref[:, g * gd:(g + 1) * gd] = r[:, gd:].astype(BF16)


def _chan_dft(proj, u_col_block, width, cs, tm=512):
    T = proj.shape[0]
    gd = width // FNET_GROUPS
    return pl.pallas_call(
        functools.partial(_chan_dft_kernel, gd=gd),
        grid=(T // tm,),
        in_specs=[
            pl.BlockSpec((tm, width), lambda i: (i, u_col_block)),
            pl.BlockSpec((gd, 2 * gd), lambda i: (0, 0)),
        ],
        out_specs=[pl.BlockSpec((tm, width), lambda i: (i, 0))] * 2,
        out_shape=[jax.ShapeDtypeStruct((T, width), BF16)] * 2,
        compiler_params=_cparams(("parallel",)),
        name="chan_dft",
    )(proj, cs)


def _seq_dft_kernel(c_ref, s_ref, zc_ref, zs_ref, *rest):
    o_ref, acc_ref = rest[-2:]
    k = pl.program_id(2)

    @pl.when(k == 0)
    def _():
        acc_ref[...] = jnp.zeros_like(acc_ref)

    acc_ref[...] += (jnp.dot(c_ref[...], zc_ref[...], preferred_element_type=F32)
                     + jnp.dot(s_ref[...], zs_ref[...], preferred_element_type=F32))

    @pl.when(k == pl.num_programs(2) - 1)
    def _():
        o_ref[...] = acc_ref[...].astype(o_ref.dtype)


def _seq_dft(zc, zs, cmat, nsmat, row0, B, S, prev=None, tm=512, tk=512):
    T, W = zc.shape
    tm, tk = min(tm, S), min(tk, S)
    nm, nk = S // tm, S // tk
    kb0, mb0 = row0 // tk, row0 // tm
    in_specs = [
        pl.BlockSpec((tm, tk), lambda b, i, k: (i, k)),
        pl.BlockSpec((tm, tk), lambda b, i, k: (i, k)),
        pl.BlockSpec((tk, W), lambda b, i, k: (kb0 + b * nk + k, 0)),
        pl.BlockSpec((tk, W), lambda b, i, k: (kb0 + b * nk + k, 0)),
    ]
    args = [cmat, nsmat, zc, zs]
    aliases = {}
    if prev is not None:
        in_specs.append(pl.BlockSpec(memory_space=pl.ANY))
        args.append(prev)
        aliases = {4: 0}
    return pl.pallas_call(
        _seq_dft_kernel,
        grid=(B, nm, nk),
        in_specs=in_specs,
        out_specs=pl.BlockSpec((tm, W), lambda b, i, k: (mb0 + b * nm + i, 0)),
        out_shape=jax.ShapeDtypeStruct((T, W), BF16),
        scratch_shapes=[pltpu.VMEM((tm, W), F32)],
        input_output_aliases=aliases,
        compiler_params=_cparams(("parallel", "parallel", "arbitrary")),
        name="seq_dft",
    )(*args)


def _dft_mats(n, scale, split=64):
    split = split if n % split == 0 else 1
    k = jnp.arange(n, dtype=jnp.int32)[None, :]
    j1 = jnp.arange(n // split, dtype=jnp.int32)[:, None]
    j2 = jnp.arange(split, dtype=jnp.int32)[:, None]
    w = 2.0 * math.pi / n
    ang_a = ((split * j1 * k) % n).astype(F32) * w
    ang_b = ((j2 * k) % n).astype(F32) * w
    ca, sa = jnp.cos(ang_a)[:, None, :], jnp.sin(ang_a)[:, None, :]
    cb, sb = (jnp.cos(ang_b) * scale)[None, :, :], (jnp.sin(ang_b) * scale)[None, :, :]
    c = (ca * cb - sa * sb).reshape(n, n)
    s = (sa * cb + ca * sb).reshape(n, n)
    return c, s


_NT = (((1,), (1,)), ((), ()))
_TN = (((0,), (0,)), ((), ()))


def _gla_block(q_ref, k_ref, v_ref, lr_ref, w3_ref, b_ref, st_ref, reverse, qscale, nchunk):
    R = nchunk * CHUNK
    lr = lr_ref[...]
    lr_hi = lr.astype(BF16)
    lr_lo = (lr - lr_hi.astype(F32)).astype(BF16)
    z = jnp.dot(jnp.concatenate([lr_hi, lr_lo, lr_hi], axis=1), w3_ref[...],
                preferred_element_type=F32) + b_ref[...]
    g = (jnp.minimum(z, 0.0) - jnp.log(1.0 + jnp.exp(-jnp.abs(z)))) * (1.0 / GATE_LOGIT_NORMALIZER)
    ri = lax.broadcasted_iota(jnp.int32, (R, R), 0)
    ci = lax.broadcasted_iota(jnp.int32, (R, R), 1)
    cum = ((ci >= ri) if reverse else (ci <= ri)).astype(BF16)
    g_hi = g.astype(BF16)
    g_lo = (g - g_hi.astype(F32)).astype(BF16)
    G = jnp.dot(cum, g_hi, preferred_element_type=F32) + jnp.dot(cum, g_lo, preferred_element_type=F32)

    dk = G.shape[1]
    zero_row = jnp.zeros((1, dk), F32)
    if reverse:
        starts = [G[(c + 1) * CHUNK:(c + 1) * CHUNK + 1, :] if c + 1 < nchunk else zero_row for c in range(nchunk)]
        ref_row, g_tot = CHUNK // 2, G[0:1, :]
    else:
        starts = [G[c * CHUNK - 1:c * CHUNK, :] if c > 0 else zero_row for c in range(nchunk)]
        ref_row, g_tot = CHUNK // 2 - 1, G[R - 1:R, :]
    bcast = lambda rows_: jnp.concatenate([jnp.broadcast_to(r_, (CHUNK, dk)) for r_ in rows_], axis=0)
    gc = G - bcast(starts)
    gref = bcast([gc[c * CHUNK + ref_row:c * CHUNK + ref_row + 1, :] for c in range(nchunk)])

    q = q_ref[...].astype(F32) * qscale
    k = k_ref[...].astype(F32)
    v = v_ref[...]
    q_in = (q * jnp.exp(gc - gref)).astype(BF16)
    k_in = (k * jnp.exp(gref - gc)).astype(BF16)
    q_it = (q * jnp.exp(gc)).astype(BF16)
    q_st = (q * jnp.exp(G)).astype(BF16)
    k_st = (k * jnp.exp(g_tot - G)).astype(BF16)

    s_diag = lax.dot_general(q_in, k_in, _NT, preferred_element_type=F32)
    same = (ri // CHUNK) == (ci // CHUNK)
    keep = same & ((ci > ri) if reverse else (ci <= ri))
    s_rows = []
    for c in range(nchunk):
        rows = slice(c * CHUNK, (c + 1) * CHUNK)
        s = jnp.where(keep[rows, :], s_diag[rows, :], 0.0)
        lo_, hi_ = ((c + 1) * CHUNK, R) if reverse else (0, c * CHUNK)
        if hi_ > lo_:
            kx = (k[lo_:hi_, :] * jnp.exp(starts[c] - G[lo_:hi_, :])).astype(BF16)
            pad = jnp.zeros((R - (hi_ - lo_), dk), BF16)
            kx = jnp.concatenate([pad, kx] if reverse else [kx, pad], axis=0)
            s = s + lax.dot_general(q_it[rows, :], kx, _NT, preferred_element_type=F32)
        s_rows.append(s.astype(BF16))
    scores = jnp.concatenate(s_rows, axis=0)

    st = st_ref[...]
    o = (jnp.dot(scores, v, preferred_element_type=F32)
         + lax.dot_general(q_st, st.astype(BF16), _NT, preferred_element_type=F32))
    st_ref[...] = st * jnp.exp(g_tot) + lax.dot_general(v, k_st, _TN, preferred_element_type=F32)
    return o


def _gla_kernel(qf_ref, kf_ref, vf_ref, qb_ref, kb_ref, vb_ref, lrf_ref, lrb_ref,
                wf_ref, bf_ref, wb_ref, bb_ref, *rest, nchunk, nsub, qscale):
    o_ref, stf_ref, stb_ref = rest[-3:]
    n = pl.program_id(2)
    nb = pl.num_programs(2)
    blk = nchunk * CHUNK
    rows = nsub * blk

    @pl.when(n == 0)
    def _():
        o_ref[...] = jnp.zeros_like(o_ref)
        stf_ref[...] = jnp.zeros_like(stf_ref)
        stb_ref[...] = jnp.zeros_like(stb_ref)

    for s_f in range(nsub):
        s_b = nsub - 1 - s_f
        sub_f, sub_b = pl.ds(s_f * blk, blk), pl.ds(s_b * blk, blk)
        o_f = _gla_block(qf_ref.at[sub_f, :], kf_ref.at[sub_f, :], vf_ref.at[sub_f, :], lrf_ref.at[sub_f, :],
                         wf_ref, bf_ref, stf_ref, False, qscale, nchunk)
        o_b = _gla_block(qb_ref.at[sub_b, :], kb_ref.at[sub_b, :], vb_ref.at[sub_b, :], lrb_ref.at[sub_b, :],
                         wb_ref, bb_ref, stb_ref, True, qscale, nchunk)
        o_ref[pl.ds(pl.multiple_of(n * rows + s_f * blk, blk), blk), :] += o_f
        o_ref[pl.ds(pl.multiple_of((nb - 1 - n) * rows + s_b * blk, blk), blk), :] += o_b


def _hi_hi_lo(w):
    hi = w.astype(BF16)
    lo = (w - hi.astype(F32)).astype(BF16)
    return jnp.concatenate([hi, hi, lo], axis=0)


def _gla(proj, lr_f, lr_b, wup_f, b_f, wup_b, b_b, row0, B, S, q_blk0, k_blk0, v_blk0, dk, dv,
         prev=None, blk=256, nsub=2):
    T = proj.shape[0]
    nsub = nsub if S % (nsub * blk) == 0 else 1
    rows = nsub * blk
    assert S % rows == 0 and row0 % S == 0
    nb = S // rows
    rb0, sb0 = row0 // rows, row0 // S
    fmap = lambda b, n: rb0 + b * nb + n
    bmap = lambda b, n: rb0 + b * nb + (nb - 1 - n)
    in_specs = [
        pl.BlockSpec((rows, dk), lambda b, h, n: (fmap(b, n), q_blk0 + h)),
        pl.BlockSpec((rows, dk), lambda b, h, n: (fmap(b, n), k_blk0 + h)),
        pl.BlockSpec((rows, dv), lambda b, h, n: (fmap(b, n), v_blk0 + h)),
        pl.BlockSpec((rows, dk), lambda b, h, n: (bmap(b, n), q_blk0 + h)),
        pl.BlockSpec((rows, dk), lambda b, h, n: (bmap(b, n), k_blk0 + h)),
        pl.BlockSpec((rows, dv), lambda b, h, n: (bmap(b, n), v_blk0 + h)),
        pl.BlockSpec((rows, GATE_LOW_RANK), lambda b, h, n: (fmap(b, n), 0)),
        pl.BlockSpec((rows, GATE_LOW_RANK), lambda b, h, n: (bmap(b, n), 0)),
        pl.BlockSpec((3 * GATE_LOW_RANK, dk), lambda b, h, n: (0, h)),
        pl.BlockSpec((1, dk), lambda b, h, n: (0, h)),
        pl.BlockSpec((3 * GATE_LOW_RANK, dk), lambda b, h, n: (0, h)),
        pl.BlockSpec((1, dk), lambda b, h, n: (0, h)),
    ]
    args = [proj, proj, proj, proj, proj, proj, lr_f, lr_b,
            _hi_hi_lo(wup_f), b_f.reshape(1, -1), _hi_hi_lo(wup_b), b_b.reshape(1, -1)]
    aliases = {}
    if prev is not None:
        in_specs.append(pl.BlockSpec(memory_space=pl.ANY))
        args.append(prev)
        aliases = {len(args) - 1: 0}
    return pl.pallas_call(
        functools.partial(_gla_kernel, nchunk=blk // CHUNK, nsub=nsub, qscale=dk ** -0.5),
        grid=(B, GLA_HEADS, nb),
        in_specs=in_specs,
        out_specs=pl.BlockSpec((S, dv), lambda b, h, n: (sb0 + b, h)),
        out_shape=jax.ShapeDtypeStruct((T, GLA_HEADS * dv), F32),
        scratch_shapes=[pltpu.VMEM((dv, dk), F32), pltpu.VMEM((dv, dk), F32)],
        input_output_aliases=aliases,
        compiler_params=_cparams(("parallel", "parallel", "arbitrary")),
        name="gla",
    )(*args)


def _merge_kernel(fft_ref, o_ref_in, og_ref, g0_ref, g1_ref, hn_ref, wf_ref, wg_ref, o_ref, a_ref, *, dv):
    ya = jnp.dot(fft_ref[...], wf_ref[...], preferred_element_type=F32)
    for h in range(GLA_HEADS):
        cs = slice(h * dv, (h + 1) * dv)
        o = o_ref_in[:, cs]
        var = jnp.mean(o * o, axis=-1, keepdims=True)
        on = o * lax.rsqrt(var + EPS) * hn_ref[...]
        og = og_ref[:, cs].astype(F32)
        a_ref[:, cs] = (on * (og * jax.nn.sigmoid(og))).astype(BF16)
    yb = jnp.dot(a_ref[...], wg_ref[...], preferred_element_type=F32)
    m = jax.nn.sigmoid(g0_ref[...].astype(F32)) * ya + jax.nn.sigmoid(g1_ref[...].astype(F32)) * yb
    o_ref[...] = m.astype(BF16)


def _merge(fft, o_gla, proj, og_blk, g0_blk, g1_blk, hn, wf, wg, tm=256):
    T, D = o_gla.shape
    FW = fft.shape[1]
    dv = D // GLA_HEADS
    const = dict(pipeline_mode=pl.Buffered(1))
    return pl.pallas_call(
        functools.partial(_merge_kernel, dv=dv),
        grid=(T // tm,),
        in_specs=[
            pl.BlockSpec((tm, FW), lambda i: (i, 0)),
            pl.BlockSpec((tm, D), lambda i: (i, 0)),
            pl.BlockSpec((tm, D), lambda i: (i, og_blk)),
            pl.BlockSpec((tm, D), lambda i: (i, g0_blk)),
            pl.BlockSpec((tm, D), lambda i: (i, g1_blk)),
            pl.BlockSpec((1, dv), lambda i: (0, 0)),
            pl.BlockSpec((FW, D), lambda i: (0, 0), **const),
            pl.BlockSpec((D, D), lambda i: (0, 0), **const),
        ],
        out_specs=pl.BlockSpec((tm, D), lambda i: (i, 0)),
        out_shape=jax.ShapeDtypeStruct((T, D), BF16),
        scratch_shapes=[pltpu.VMEM((tm, D), BF16)],
        compiler_params=_cparams(("parallel",)),
        name="merge",
    )(fft, o_gla, proj, proj, proj, hn.reshape(1, dv), wf, wg)


HI16 = 0xFFFF0000


def _pack_bf16_pair(lo, hi):
    lo_bits = lax.bitcast_convert_type(lo.astype(BF16).astype(F32), jnp.uint32)
    hi_bits = lax.bitcast_convert_type(hi.astype(BF16).astype(F32), jnp.uint32)
    return (hi_bits & jnp.uint32(HI16)) | (lo_bits >> 16)


def _unpack_bf16_pair(w):
    lo = lax.bitcast_convert_type(w << 16, F32)
    hi = lax.bitcast_convert_type(w & jnp.uint32(HI16), F32)
    return lo, hi


def _outproj_router_kernel(m_ref, xp_ref, xs_ref, wo_ref, g_ref, wr_ref, br_ref,
                           x1_ref, xn_ref, idx_ref, tw_ref, rank_ref, cnt_ref, run_ref, *, n_exp, n0):
    i = pl.program_id(0)

    @pl.when(i == 0)
    def _():
        run_ref[...] = jnp.zeros_like(run_ref)

    tm, D = m_ref.shape
    x = jnp.where(i < n0, xp_ref[...], xs_ref[...])
    x1 = x + jnp.dot(m_ref[...], wo_ref[...], preferred_element_type=F32)
    x1_ref[...] = x1
    var = jnp.mean(x1 * x1, axis=-1, keepdims=True)
    xn = x1 * lax.rsqrt(var + EPS) * g_ref[...]
    xn_ref[...] = _pack_bf16_pair(xn[:, :D // 2], xn[:, D // 2:])
    xh = xn.astype(BF16)
    xl = (xn - xh.astype(F32)).astype(BF16)
    r = jnp.dot(xh, wr_ref[...], preferred_element_type=F32)
    lg = (r[:, :n_exp] + r[:, n_exp:] + jnp.dot(xl, wr_ref[:, :n_exp], preferred_element_type=F32)
          + br_ref[...])

    lane = lax.broadcasted_iota(jnp.int32, (tm, n_exp), 1)
    vals, hots = [], []
    for _ in range(TOP_K):
        mx = jnp.max(lg, axis=-1, keepdims=True)
        ik = jnp.min(jnp.where(lg == mx, lane, n_exp), axis=-1, keepdims=True)
        hot = lane == ik
        vals.append(mx)
        hots.append(hot)
        lg = jnp.where(hot, -jnp.inf, lg)
    exps = [jnp.exp(v - vals[0]) for v in vals]
    denom = exps[0] + exps[1] + exps[2] + exps[3]

    sel = hots[0] | hots[1] | hots[2] | hots[3]
    sel_f = sel.astype(F32)
    r = lax.broadcasted_iota(jnp.int32, (tm, tm), 0)
    c = lax.broadcasted_iota(jnp.int32, (tm, tm), 1)
    strict = (c < r).astype(BF16)
    before = jnp.dot(strict, sel_f.astype(BF16), preferred_element_type=F32) + run_ref[...]
    run_ref[...] += jnp.sum(sel_f, axis=0, keepdims=True)
    cnt_ref[...] = run_ref[...].astype(jnp.int32)

    k4 = lax.broadcasted_iota(jnp.int32, (tm, TOP_K), 1)
    idx4 = jnp.zeros((tm, TOP_K), jnp.int32)
    w4 = jnp.zeros((tm, TOP_K), F32)
    rk4 = jnp.zeros((tm, TOP_K), jnp.int32)
    for k in range(TOP_K):
        ik = jnp.sum(jnp.where(hots[k], lane, 0), axis=-1, keepdims=True)
        rk = jnp.sum(jnp.where(hots[k], before, 0.0), axis=-1, keepdims=True).astype(jnp.int32)
        idx4 = jnp.where(k4 == k, ik, idx4)
        w4 = jnp.where(k4 == k, exps[k] / denom, w4)
        rk4 = jnp.where(k4 == k, rk, rk4)
    idx_ref[...] = idx4
    tw_ref[...] = w4
    rank_ref[...] = rk4


def _outproj_router(merged, xp, xs, wo, gain, wr2, br, tm=256):
    T, D = merged.shape
    E = wr2.shape[1] // 2
    n0 = xp.shape[0] // tm
    const = dict(pipeline_mode=pl.Buffered(1))
    row = lambda i: (i, 0)
    fix = lambda i: (0, 0)
    return pl.pallas_call(
        functools.partial(_outproj_router_kernel, n_exp=E, n0=n0),
        grid=(T // tm,),
        in_specs=[
            pl.BlockSpec((tm, D), row),
            pl.BlockSpec((tm, D), lambda i: (jnp.minimum(i, n0 - 1), 0)),
            pl.BlockSpec((tm, D), lambda i: (jnp.maximum(i - n0, 0), 0)),
            pl.BlockSpec((D, D), fix, **const),
            pl.BlockSpec((1, D), fix),
            pl.BlockSpec((D, 2 * E), fix),
            pl.BlockSpec((1, E), fix),
        ],
        out_specs=[
            pl.BlockSpec((tm, D), row),
            pl.BlockSpec((tm, D // 2), row),
            pl.BlockSpec((tm, TOP_K), row),
            pl.BlockSpec((tm, TOP_K), row),
            pl.BlockSpec((tm, TOP_K), row),
            pl.BlockSpec((1, E), fix),
        ],
        out_shape=[
            jax.ShapeDtypeStruct((T, D), F32),
            jax.ShapeDtypeStruct((T, D // 2), jnp.uint32),
            jax.ShapeDtypeStruct((T, TOP_K), jnp.int32),
            jax.ShapeDtypeStruct((T, TOP_K), F32),
            jax.ShapeDtypeStruct((T, TOP_K), jnp.int32),
            jax.ShapeDtypeStruct((1, E), jnp.int32),
        ],
        scratch_shapes=[pltpu.VMEM((1, E), F32)],
        compiler_params=_cparams(("arbitrary",)),
        name="outproj_router",
    )(merged, xp, xs, wo, gain.reshape(1, D), wr2, br.reshape(1, E))


def _combine_kernel(tw_ref, x1_ref, g_ref, *rest, n0):
    y_refs, (op_ref, os_ref) = rest[:TOP_K], rest[TOP_K:]
    i = pl.program_id(0)
    tw = tw_ref[...]
    lo_acc = hi_acc = None
    for k in range(TOP_K):
        wk = tw[:, k:k + 1]
        lo, hi = _unpack_bf16_pair(y_refs[k][...])
        lo_acc = wk * lo if lo_acc is None else lo_acc + wk * lo
        hi_acc = wk * hi if hi_acc is None else hi_acc + wk * hi
    x2 = x1_ref[...] + jnp.concatenate([lo_acc, hi_acc], axis=1)
    var = jnp.mean(x2 * x2, axis=-1, keepdims=True)
    y = x2 * lax.rsqrt(var + EPS) * g_ref[...]

    @pl.when(i < n0)
    def _():
        op_ref[...] = y

    @pl.when(i >= n0)
    def _():
        os_ref[...] = y


def _combine(tw, x1, gain, yk, t_prompt, tm=256):
    T, D = x1.shape
    n0 = t_prompt // tm
    nblk = T // tm
    y_spec = lambda k: pl.BlockSpec((tm, D // 2), lambda i: (k * nblk + i, 0))
    return pl.pallas_call(
        functools.partial(_combine_kernel, n0=n0),
        grid=(T // tm,),
        in_specs=[
            pl.BlockSpec((tm, TOP_K), lambda i: (i, 0)),
            pl.BlockSpec((tm, D), lambda i: (i, 0)),
            pl.BlockSpec((1, D), lambda i: (0, 0)),
        ] + [y_spec(k) for k in range(TOP_K)],
        out_specs=[
            pl.BlockSpec((tm, D), lambda i: (jnp.minimum(i, n0 - 1), 0)),
            pl.BlockSpec((tm, D), lambda i: (jnp.maximum(i - n0, 0), 0)),
        ],
        out_shape=[jax.ShapeDtypeStruct((t_prompt, D), F32), jax.ShapeDtypeStruct((T - t_prompt, D), F32)],
        compiler_params=_cparams(("arbitrary",)),
        name="combine",
    )(tw, x1, gain.reshape(1, D), *([yk] * TOP_K))


def _gate_up_kernel(g0_ref, nt_ref, cnt_ref, tok_ref, w_ref, bg_ref, bu_ref, x_hbm, o_hbm,
                    wp_ref, xin_ref, xb_ref, obuf_ref, sin, sout, *, tn):
    j, e = pl.program_id(0), pl.program_id(1)
    tm = GROUP_TILE
    D = xb_ref.shape[1]
    half = MXU_DIM // 2
    n = nt_ref[e]
    row_base = g0_ref[e]
    cnt = cnt_ref[e]

    def gather_tile(r, slot):
        base = row_base + r * tm
        for i in range(tm):
            tok = tok_ref[base + i]
            pltpu.make_async_copy(x_hbm.at[pl.ds(tok, 1), :], xin_ref.at[slot, pl.ds(i, 1), :], sin.at[slot]).start()

    def wait_tile(slot):
        pltpu.make_async_copy(x_hbm.at[pl.ds(0, tm), :], xin_ref.at[slot], sin.at[slot]).wait()

    def out_copy(r, slot):
        rows = pl.ds(pl.multiple_of(row_base + r * tm, tm), tm)
        cols = pl.ds(pl.multiple_of(j * (tn // 2), LANES), tn // 2)
        return pltpu.make_async_copy(obuf_ref.at[slot], o_hbm.at[rows, cols], sout.at[slot])

    def body(r, carry):
        slot = r % 2
        wait_tile(slot)

        @pl.when(r >= 2)
        def _():
            out_copy(r - 2, slot).wait()

        valid = (r * tm + lax.broadcasted_iota(jnp.int32, (tm, 1), 0)) < cnt
        lo, hi = _unpack_bf16_pair(xin_ref[slot])
        xb_ref[:, :D // 2] = jnp.where(valid, lo, 0.0).astype(BF16)
        xb_ref[:, D // 2:] = jnp.where(valid, hi, 0.0).astype(BF16)
        gather_tile(jnp.minimum(r + 1, n - 1), 1 - slot)
        for cb in range(tn // MXU_DIM):
            h = jnp.dot(xb_ref[...], wp_ref[:, cb * MXU_DIM:(cb + 1) * MXU_DIM], preferred_element_type=F32)
            hg = h[:, :half] + bg_ref[0, :, cb * half:(cb + 1) * half]
            hu = h[:, half:] + bu_ref[0, :, cb * half:(cb + 1) * half]
            gate = jnp.minimum(hg, SWIGLU_LIMIT)
            up = jnp.clip(hu, -SWIGLU_LIMIT, SWIGLU_LIMIT)
            act = gate * jax.nn.sigmoid(SWIGLU_ALPHA * gate) * (up + 1.0)
            obuf_ref[slot, :, cb * half:(cb + 1) * half] = act.astype(obuf_ref.dtype)
        out_copy(r, slot).start()
        return carry

    @pl.when(n > 0)
    def _():
        gather_tile(0, 0)
        r_i = lax.broadcasted_iota(jnp.int32, (MXU_DIM, MXU_DIM), 0)
        c_i = lax.broadcasted_iota(jnp.int32, (MXU_DIM, MXU_DIM), 1)
        perm = (((c_i < half) & (r_i == 2 * c_i)) | ((c_i >= half) & (r_i == 2 * (c_i - half) + 1))).astype(BF16)
        for cb in range(tn // MXU_DIM):
            for rb in range(D // 512):
                w = w_ref[0, rb * 512:(rb + 1) * 512, cb * MXU_DIM:(cb + 1) * MXU_DIM].astype(BF16)
                wp_ref[rb * 512:(rb + 1) * 512, cb * MXU_DIM:(cb + 1) * MXU_DIM] = jnp.dot(
                    w, perm, preferred_element_type=F32).astype(BF16)
        lax.fori_loop(0, n, body, 0)
        wait_tile(n % 2)

        @pl.when(n >= 2)
        def _():
            out_copy(n - 2, n % 2).wait()

        out_copy(n - 1, (n - 1) % 2).wait()


def _gate_up(g0, ntiles, cnt, slot_tok, xn, n_slots, w_gate_up, bg, bu, tn):
    P = n_slots
    E, D, H2 = w_gate_up.shape
    tm = GROUP_TILE
    wmap = lambda j, e, *_: (e, 0, j)
    gs = pltpu.PrefetchScalarGridSpec(
        num_scalar_prefetch=4,
        grid=(H2 // tn, E),
        in_specs=[
            pl.BlockSpec((1, D, tn), wmap),
            pl.BlockSpec((1, 1, tn // 2), wmap),
            pl.BlockSpec((1, 1, tn // 2), wmap),
            pl.BlockSpec(memory_space=pl.ANY),
        ],
        out_specs=pl.BlockSpec(memory_space=pl.ANY),
        scratch_shapes=[
            pltpu.VMEM((D, tn), BF16),
            pltpu.VMEM((2, tm, D // 2), jnp.uint32),
            pltpu.VMEM((tm, D), BF16),
            pltpu.VMEM((2, tm, tn // 2), BF16),
            pltpu.SemaphoreType.DMA((2,)),
            pltpu.SemaphoreType.DMA((2,)),
        ],
    )
    return pl.pallas_call(
        functools.partial(_gate_up_kernel, tn=tn),
        grid_spec=gs,
        out_shape=jax.ShapeDtypeStruct((P, H2 // 2), BF16),
        compiler_params=_cparams(("arbitrary", "arbitrary")),
        name="moe_gate_up",
    )(g0, ntiles, cnt, slot_tok, w_gate_up, bg, bu, xn)


def _down_kernel(g0_ref, nt_ref, dst_ref, w_ref, b_ref, a_hbm, y_hbm, wb_ref, ain_ref, obuf_ref, sin, sout):
    e = pl.program_id(0)
    tm = GROUP_TILE
    H, D = wb_ref.shape
    n = nt_ref[e]
    row_base = g0_ref[e]

    def in_copy(r, slot):
        rows = pl.ds(pl.multiple_of(row_base + r * tm, tm), tm)
        return pltpu.make_async_copy(a_hbm.at[rows, :], ain_ref.at[slot], sin.at[slot])

    nblk = (D // 2) // MXU_DIM

    def compute(slot, out_ref, before_block=None):
        a = ain_ref[slot]
        for bi, c0 in enumerate(range(0, D // 2, MXU_DIM)):
            if before_block is not None:
                before_block(bi)
            c1 = D // 2 + c0
            lo = jnp.dot(a, wb_ref[:, c0:c0 + MXU_DIM], preferred_element_type=F32) + b_ref[0, :, c0:c0 + MXU_DIM]
            hi = jnp.dot(a, wb_ref[:, c1:c1 + MXU_DIM], preferred_element_type=F32) + b_ref[0, :, c1:c1 + MXU_DIM]
            out_ref[:, c0:c0 + MXU_DIM] = _pack_bf16_pair(lo, hi)

    def scatter_rows(q, so, part):
        base = row_base + q * tm
        for i in range(part * (tm // nblk), (part + 1) * (tm // nblk)):
            d = dst_ref[base + i]
            pltpu.make_async_copy(obuf_ref.at[so, pl.ds(i, 1), :], y_hbm.at[pl.ds(d, 1), :], sout.at[so]).start()

    def scatter_tile(q, so):
        for part in range(nblk):
            scatter_rows(q, so, part)

    def wait_scatter(so):
        pltpu.make_async_copy(obuf_ref.at[so], y_hbm.at[pl.ds(0, tm), :], sout.at[so]).wait()

    def body(r, carry):
        slot = r % 2

        @pl.when(r + 1 < n)
        def _():
            in_copy(r + 1, 1 - slot).start()

        in_copy(r, slot).wait()

        @pl.when(r >= 2)
        def _():
            wait_scatter(slot)

        compute(slot, obuf_ref.at[slot], functools.partial(scatter_rows, r - 1, 1 - slot))
        return carry

    @pl.when(n > 0)
    def _():
        in_copy(0, 0).start()
        for rb in range(H // 512):
            wb_ref[rb * 512:(rb + 1) * 512, :] = w_ref[0, rb * 512:(rb + 1) * 512, :].astype(BF16)

        @pl.when(n > 1)
        def _():
            in_copy(1, 1).start()

        in_copy(0, 0).wait()
        compute(0, obuf_ref.at[0])
        lax.fori_loop(1, n, body, 0)
        so = (n - 1) % 2
        scatter_tile(n - 1, so)
        wait_scatter(so)

        @pl.when(n >= 2)
        def _():
            wait_scatter(1 - so)


def _down(g0, ntiles, slot_dst, n_rows, act, wd, bd):
    P, H = act.shape
    E, _, D = wd.shape
    tm = GROUP_TILE
    wmap = lambda e, *_: (e, 0, 0)
    gs = pltpu.PrefetchScalarGridSpec(
        num_scalar_prefetch=3,
        grid=(E,),
        in_specs=[
            pl.BlockSpec((1, H, D), wmap),
            pl.BlockSpec((1, 1, D), wmap),
            pl.BlockSpec(memory_space=pl.ANY),
        ],
        out_specs=pl.BlockSpec(memory_space=pl.ANY),
        scratch_shapes=[
            pltpu.VMEM((H, D), BF16),
            pltpu.VMEM((2, tm, H), BF16),
            pltpu.VMEM((2, tm, D // 2), jnp.uint32),
            pltpu.SemaphoreType.DMA((2,)),
            pltpu.SemaphoreType.DMA((2,)),
        ],
    )
    return pl.pallas_call(
        _down_kernel,
        grid_spec=gs,
        out_shape=jax.ShapeDtypeStruct((n_rows, D // 2), jnp.uint32),
        compiler_params=_cparams(("arbitrary",)),
        name="moe_down",
    )(g0, ntiles, slot_dst, wd, bd, act)


def _trunk(xp, xs, seq_shapes, norm_mix, w_in, w_gk_up_fwd, b_gk_fwd, w_gk_up_bwd, b_gk_bwd, gla_head_norm,
           w_fnet_out, w_gla_out, w_out, norm_ffn, w_router, b_router, w_gate_up, b_gate_up,
           w_down, b_down, norm_final):
    D = xp.shape[1]
    T = xp.shape[0] + xs.shape[0]
    fw = w_fnet_out.shape[0]
    dkk = w_gk_up_fwd.shape[1]
    dvv = w_gla_out.shape[0]
    dk, dv = dkk // GLA_HEADS, dvv // GLA_HEADS
    sizes = (fw, dkk, dkk, dvv, dvv, GATE_LOW_RANK, GATE_LOW_RANK, 2 * D)
    offs = np.concatenate([[0], np.cumsum(sizes)])
    sl = lambda n: slice(int(offs[n]), int(offs[n + 1]))
    w_u, w_q, w_k, w_v, w_og = (w_in[:, sl(n)] for n in range(5))
    w_lr = w_in[:, int(offs[5]):int(offs[7])]
    w_g = w_in[:, sl(7)]
    w_main = jnp.concatenate([w_og, w_g, w_v, w_u, w_q, w_k], axis=1).astype(BF16)
    og_blk, g0_blk, g1_blk = 0, dvv // D, dvv // D + 1
    v_off = dvv + 2 * D
    u_off = v_off + dvv
    q_off = u_off + fw
    k_off = q_off + dkk

    proj, lr = _inproj(xp, xs, norm_mix, w_main, _hi_lo(w_lr))
    lr_f, lr_b = lr[:, :GATE_LOW_RANK], lr[:, GATE_LOW_RANK:]

    gd = fw // FNET_GROUPS
    cc, sc = _dft_mats(gd, gd ** -0.5)
    cs = jnp.concatenate([cc, sc], axis=1).astype(BF16)
    zc, zs = _chan_dft(proj, u_off // fw, fw, cs)
    fft, o_gla = None, None
    row0 = 0
    for (B, S) in seq_shapes:
        cm, sm = _dft_mats(S, S ** -0.5)
        fft = _seq_dft(zc, zs, cm.astype(BF16), (-sm).astype(BF16), row0, B, S, prev=fft)
        o_gla = _gla(proj, lr_f, lr_b, w_gk_up_fwd, b_gk_fwd, w_gk_up_bwd, b_gk_bwd, row0, B, S,
                     q_off // dk, k_off // dk, v_off // dv, dk, dv, prev=o_gla)
        row0 += B * S

    merged = _merge(fft, o_gla, proj, og_blk, g0_blk, g1_blk, gla_head_norm,
                    w_fnet_out.astype(BF16), w_gla_out.astype(BF16))
    x1, xn2, idx, tw, rank, cnt = _outproj_router(merged, xp, xs, w_out.astype(BF16), norm_ffn,
                                                  _hi_lo(w_router), b_router)

    E = w_router.shape[1]
    cnt = cnt.reshape(E)
    gsz = ((cnt + GROUP_TILE - 1) // GROUP_TILE) * GROUP_TILE
    gend = jnp.cumsum(gsz)
    gstart = gend - gsz
    pos = (gstart[idx] + rank).reshape(-1).astype(jnp.int32)
    n_slots = T * TOP_K + E * GROUP_TILE
    g0 = gstart.astype(jnp.int32)
    ntiles = (gsz // GROUP_TILE).astype(jnp.int32)

    flat = jnp.arange(T * TOP_K, dtype=jnp.int32)
    slot_tok = jnp.zeros((n_slots,), jnp.int32).at[pos].set(flat // TOP_K, unique_indices=True)
    slot_dst = (T * TOP_K + jnp.arange(n_slots, dtype=jnp.int32)).at[pos].set(
        (flat % TOP_K) * T + flat // TOP_K, unique_indices=True)
    H = w_down.shape[1]
    bg = b_gate_up[:, 0::2].reshape(E, 1, H)
    bu = b_gate_up[:, 1::2].reshape(E, 1, H)
    act = _gate_up(g0, ntiles, cnt, slot_tok, xn2, n_slots, w_gate_up, bg, bu, GATE_UP_TN)
    yk = _down(g0, ntiles, slot_dst, T * TOP_K + n_slots, act, w_down, b_down.reshape(E, 1, D))
    return _combine(tw, x1, norm_final, yk, xp.shape[0])


def kernel(x_prompt, x_sample, norm_mix, w_in, w_gk_up_fwd, b_gk_fwd, w_gk_up_bwd, b_gk_bwd, gla_head_norm,
           w_fnet_out, w_gla_out, w_out, norm_ffn, w_router, b_router, w_gate_up, b_gate_up, w_down,
           b_down, norm_final):
    D = x_prompt.shape[-1]
    shapes = (x_prompt.shape[:2], x_sample.shape[:2])
    yp, ys = _trunk(x_prompt.reshape(-1, D), x_sample.reshape(-1, D), shapes, norm_mix[0], w_in[0], w_gk_up_fwd[0], b_gk_fwd[0], w_gk_up_bwd[0], b_gk_bwd[0],
               gla_head_norm[0], w_fnet_out[0], w_gla_out[0], w_out[0], norm_ffn[0], w_router[0],
               b_router[0], w_gate_up[0], b_gate_up[0], w_down[0], b_down[0], norm_final)
    return (yp.reshape(x_prompt.shape), ys.reshape(x_sample.shape))
```

```python
import functools
import math

import numpy as np
import jax
import jax.numpy as jnp
from jax import lax
from jax.experimental import pallas as pl
from jax.experimental.pallas import tpu as pltpu

F32 = jnp.float32
BF16 = jnp.bfloat16
HIGHEST = lax.Precision.HIGHEST

EPS = 1e-5
FNET_GROUPS = 4
GLA_HEADS = 4
GATE_LOW_RANK = 16
GATE_LOGIT_NORMALIZER = 16.0
CHUNK = 64
TOP_K = 4
SWIGLU_LIMIT = 7.0
SWIGLU_ALPHA = 1.702

VMEM_LIMIT_BYTES = 56 * 1024 * 1024
MXU_DIM = 256
LANES = 128
SUBLANES = 8
GROUP_TILE = 256
GATE_UP_TN = 2048


def _cparams(sem):
    return pltpu.CompilerParams(dimension_semantics=sem, vmem_limit_bytes=VMEM_LIMIT_BYTES)


def _split3(x):
    hi = x.astype(BF16)
    r = x - hi.astype(F32)
    mid = r.astype(BF16)
    lo = (r - mid.astype(F32)).astype(BF16)
    return hi, mid, lo


def _hi_lo(w):
    hi = w.astype(BF16)
    lo = (w - hi.astype(F32)).astype(BF16)
    return jnp.concatenate([hi, lo], axis=1)


def _inproj_kernel(xp_ref, xs_ref, g_ref, w_ref, wlr_ref, o_ref, lr_ref, xn_ref, *, n0):
    @pl.when(pl.program_id(1) == 0)
    def _():
        x = jnp.where(pl.program_id(0) < n0, xp_ref[...], xs_ref[...])
        var = jnp.mean(x * x, axis=-1, keepdims=True)
        xn = (x * lax.rsqrt(var + EPS) * g_ref[...]).astype(BF16)
        xn_ref[...] = xn
        r = jnp.dot(xn, wlr_ref[...], preferred_element_type=F32)
        nlr = lr_ref.shape[1]
        lr_ref[...] = r[:, :nlr] + r[:, nlr:]

    o_ref[...] = jnp.dot(xn_ref[...], w_ref[...], preferred_element_type=F32).astype(o_ref.dtype)


def _inproj(xp, xs, gain, w_main, w_lr2, tm=512, tn=1024):
    D = xp.shape[1]
    T = xp.shape[0] + xs.shape[0]
    n0 = xp.shape[0] // tm
    N = w_main.shape[1]
    R = w_lr2.shape[1] // 2
    return pl.pallas_call(
        functools.partial(_inproj_kernel, n0=n0),
        grid=(T // tm, N // tn),
        in_specs=[
            pl.BlockSpec((tm, D), lambda i, j: (jnp.minimum(i, n0 - 1), 0)),
            pl.BlockSpec((tm, D), lambda i, j: (jnp.maximum(i - n0, 0), 0)),
            pl.BlockSpec((1, D), lambda i, j: (0, 0)),
            pl.BlockSpec((D, tn), lambda i, j: (0, j)),
            pl.BlockSpec((D, 2 * R), lambda i, j: (0, 0)),
        ],
        out_specs=[
            pl.BlockSpec((tm, tn), lambda i, j: (i, j)),
            pl.BlockSpec((tm, R), lambda i, j: (i, 0)),
        ],
        out_shape=[jax.ShapeDtypeStruct((T, N), BF16), jax.ShapeDtypeStruct((T, R), F32)],
        scratch_shapes=[pltpu.VMEM((tm, D), BF16)],
        compiler_params=_cparams(("parallel", "arbitrary")),
        name="inproj",
    )(xp, xs, gain.reshape(1, D), w_main, w_lr2)


def _chan_dft_kernel(u_ref, cs_ref, zc_ref, zs_ref, *, gd):
    for g in range(FNET_GROUPS):
        r = jnp.dot(u_ref[:, g * gd:(g + 1) * gd], cs_ref[...], preferred_element_type=F32)
        zc_ref[:, g * gd:(g + 1) * gd] = r[:, :gd].astype(BF16)
        zs_ref[:, g * gd:(g + 1) * gd] = r[:, gd:].astype(BF16)


def _chan_dft(proj, u_col_block, width, cs, tm=512):
    T = proj.shape[0]
    gd = width // FNET_GROUPS
    return pl.pallas_call(
        functools.partial(_chan_dft_kernel, gd=gd),
        grid=(T // tm,),
        in_specs=[
            pl.BlockSpec((tm, width), lambda i: (i, u_col_block)),
            pl.BlockSpec((gd, 2 * gd), lambda i: (0, 0)),
        ],
        out_specs=[pl.BlockSpec((tm, width), lambda i: (i, 0))] * 2,
        out_shape=[jax.ShapeDtypeStruct((T, width), BF16)] * 2,
        compiler_params=_cparams(("parallel",)),
        name="chan_dft",
    )(proj, cs)


def _seq_dft_kernel(c_ref, s_ref, zc_ref, zs_ref, *rest):
    o_ref, acc_ref = rest[-2:]
    k = pl.program_id(2)

    @pl.when(k == 0)
    def _():
        acc_ref[...] = jnp.zeros_like(acc_ref)

    acc_ref[...] += (jnp.dot(c_ref[...], zc_ref[...], preferred_element_type=F32)
                     + jnp.dot(s_ref[...], zs_ref[...], preferred_element_type=F32))

    @pl.when(k == pl.num_programs(2) - 1)
    def _():
        o_ref[...] = acc_ref[...].astype(o_ref.dtype)


def _seq_dft(zc, zs, cmat, nsmat, row0, B, S, prev=None, tm=512, tk=512):
    T, W = zc.shape
    tm, tk = min(tm, S), min(tk, S)
    nm, nk = S // tm, S // tk
    kb0, mb0 = row0 // tk, row0 // tm
    in_specs = [
        pl.BlockSpec((tm, tk), lambda b, i, k: (i, k)),
        pl.BlockSpec((tm, tk), lambda b, i, k: (i, k)),
        pl.BlockSpec((tk, W), lambda b, i, k: (kb0 + b * nk + k, 0)),
        pl.BlockSpec((tk, W), lambda b, i, k: (kb0 + b * nk + k, 0)),
    ]
    args = [cmat, nsmat, zc, zs]
    aliases = {}
    if prev is not None:
        in_specs.append(pl.BlockSpec(memory_space=pl.ANY))
        args.append(prev)
        aliases = {4: 0}
    return pl.pallas_call(
        _seq_dft_kernel,
        grid=(B, nm, nk),
        in_specs=in_specs,
        out_specs=pl.BlockSpec((tm, W), lambda b, i, k: (mb0 + b * nm + i, 0)),
        out_shape=jax.ShapeDtypeStruct((T, W), BF16),
        scratch_shapes=[pltpu.VMEM((tm, W), F32)],
        input_output_aliases=aliases,
        compiler_params=_cparams(("parallel", "parallel", "arbitrary")),
        name="seq_dft",
    )(*args)


def _dft_mats(n, scale, split=64):
    split = split if n % split == 0 else 1
    k = jnp.arange(n, dtype=jnp.int32)[None, :]
    j1 = jnp.arange(n // split, dtype=jnp.int32)[:, None]
    j2 = jnp.arange(split, dtype=jnp.int32)[:, None]
    w = 2.0 * math.pi / n
    ang_a = ((split * j1 * k) % n).astype(F32) * w
    ang_b = ((j2 * k) % n).astype(F32) * w
    ca, sa = jnp.cos(ang_a)[:, None, :], jnp.sin(ang_a)[:, None, :]
    cb, sb = (jnp.cos(ang_b) * scale)[None, :, :], (jnp.sin(ang_b) * scale)[None, :, :]
    c = (ca * cb - sa * sb).reshape(n, n)
    s = (sa * cb + ca * sb).reshape(n, n)
    return c, s


_NT = (((1,), (1,)), ((), ()))
_TN = (((0,), (0,)), ((), ()))


def _gla_block(q_ref, k_ref, v_ref, lr_ref, w3_ref, b_ref, st_ref, reverse, qscale, nchunk):
    R = nchunk * CHUNK
    lr = lr_ref[...]
    lr_hi = lr.astype(BF16)
    lr_lo = (lr - lr_hi.astype(F32)).astype(BF16)
    z = jnp.dot(jnp.concatenate([lr_hi, lr_lo, lr_hi], axis=1), w3_ref[...],
                preferred_element_type=F32) + b_ref[...]
    g = (jnp.minimum(z, 0.0) - jnp.log(1.0 + jnp.exp(-jnp.abs(z)))) * (1.0 / GATE_LOGIT_NORMALIZER)
    ri = lax.broadcasted_iota(jnp.int32, (R, R), 0)
    ci = lax.broadcasted_iota(jnp.int32, (R, R), 1)
    cum = ((ci >= ri) if reverse else (ci <= ri)).astype(BF16)
    g_hi = g.astype(BF16)
    g_lo = (g - g_hi.astype(F32)).astype(BF16)
    G = jnp.dot(cum, g_hi, preferred_element_type=F32) + jnp.dot(cum, g_lo, preferred_element_type=F32)

    dk = G.shape[1]
    zero_row = jnp.zeros((1, dk), F32)
    if reverse:
        starts = [G[(c + 1) * CHUNK:(c + 1) * CHUNK + 1, :] if c + 1 < nchunk else zero_row for c in range(nchunk)]
        ref_row, g_tot = CHUNK // 2, G[0:1, :]
    else:
        starts = [G[c * CHUNK - 1:c * CHUNK, :] if c > 0 else zero_row for c in range(nchunk)]
        ref_row, g_tot = CHUNK // 2 - 1, G[R - 1:R, :]
    bcast = lambda rows_: jnp.concatenate([jnp.broadcast_to(r_, (CHUNK, dk)) for r_ in rows_], axis=0)
    gc = G - bcast(starts)
    gref = bcast([gc[c * CHUNK + ref_row:c * CHUNK + ref_row + 1, :] for c in range(nchunk)])

    q = q_ref[...].astype(F32) * qscale
    k = k_ref[...].astype(F32)
    v = v_ref[...]
    q_in = (q * jnp.exp(gc - gref)).astype(BF16)
    k_in = (k * jnp.exp(gref - gc)).astype(BF16)
    q_it = (q * jnp.exp(gc)).astype(BF16)
    q_st = (q * jnp.exp(G)).astype(BF16)
    k_st = (k * jnp.exp(g_tot - G)).astype(BF16)

    s_diag = lax.dot_general(q_in, k_in, _NT, preferred_element_type=F32)
    same = (ri // CHUNK) == (ci // CHUNK)
    keep = same & ((ci > ri) if reverse else (ci <= ri))
    s_rows = []
    for c in range(nchunk):
        rows = slice(c * CHUNK, (c + 1) * CHUNK)
        s = jnp.where(keep[rows, :], s_diag[rows, :], 0.0)
        lo_, hi_ = ((c + 1) * CHUNK, R) if reverse else (0, c * CHUNK)
        if hi_ > lo_:
            kx = (k[lo_:hi_, :] * jnp.exp(starts[c] - G[lo_:hi_, :])).astype(BF16)
            pad = jnp.zeros((R - (hi_ - lo_), dk), BF16)
            kx = jnp.concatenate([pad, kx] if reverse else [kx, pad], axis=0)
            s = s + lax.dot_general(q_it[rows, :], kx, _NT, preferred_element_type=F32)
        s_rows.append(s.astype(BF16))
    scores = jnp.concatenate(s_rows, axis=0)

    st = st_ref[...]
    o = (jnp.dot(scores, v, preferred_element_type=F32)
         + lax.dot_general(q_st, st.astype(BF16), _NT, preferred_element_type=F32))
    st_ref[...] = st * jnp.exp(g_tot) + lax.dot_general(v, k_st, _TN, preferred_element_type=F32)
    return o


def _gla_kernel(qf_ref, kf_ref, vf_ref, qb_ref, kb_ref, vb_ref, lrf_ref, lrb_ref,
                wf_ref, bf_ref, wb_ref, bb_ref, *rest, nchunk, nsub, qscale):
    o_ref, stf_ref, stb_ref = rest[-3:]
    n = pl.program_id(2)
    nb = pl.num_programs(2)
    blk = nchunk * CHUNK
    rows = nsub * blk

    @pl.when(n == 0)
    def _():
        o_ref[...] = jnp.zeros_like(o_ref)
        stf_ref[...] = jnp.zeros_like(stf_ref)
        stb_ref[...] = jnp.zeros_like(stb_ref)

    for s_f in range(nsub):
        s_b = nsub - 1 - s_f
        sub_f, sub_b = pl.ds(s_f * blk, blk), pl.ds(s_b * blk, blk)
        o_f = _gla_block(qf_ref.at[sub_f, :], kf_ref.at[sub_f, :], vf_ref.at[sub_f, :], lrf_ref.at[sub_f, :],
                         wf_ref, bf_ref, stf_ref, False, qscale, nchunk)
        o_b = _gla_block(qb_ref.at[sub_b, :], kb_ref.at[sub_b, :], vb_ref.at[sub_b, :], lrb_ref.at[sub_b, :],
                         wb_ref, bb_ref, stb_ref, True, qscale, nchunk)
        o_ref[pl.ds(pl.multiple_of(n * rows + s_f * blk, blk), blk), :] += o_f
        o_ref[pl.ds(pl.multiple_of((nb - 1 - n) * rows + s_b * blk, blk), blk), :] += o_b


def _hi_hi_lo(w):
    hi = w.astype(BF16)
    lo = (w - hi.astype(F32)).astype(BF16)
    return jnp.concatenate([hi, hi, lo], axis=0)


def _gla(proj, lr_f, lr_b, wup_f, b_f, wup_b, b_b, row0, B, S, q_blk0, k_blk0, v_blk0, dk, dv,
         prev=None, blk=256, nsub=2):
    T = proj.shape[0]
    nsub = nsub if S % (nsub * blk) == 0 else 1
    rows = nsub * blk
    assert S % rows == 0 and row0 % S == 0
    nb = S // rows
    rb0, sb0 = row0 // rows, row0 // S
    fmap = lambda b, n: rb0 + b * nb + n
    bmap = lambda b, n: rb0 + b * nb + (nb - 1 - n)
    in_specs = [
        pl.BlockSpec((rows, dk), lambda b, h, n: (fmap(b, n), q_blk0 + h)),
        pl.BlockSpec((rows, dk), lambda b, h, n: (fmap(b, n), k_blk0 + h)),
        pl.BlockSpec((rows, dv), lambda b, h, n: (fmap(b, n), v_blk0 + h)),
        pl.BlockSpec((rows, dk), lambda b, h, n: (bmap(b, n), q_blk0 + h)),
        pl.BlockSpec((rows, dk), lambda b, h, n: (bmap(b, n), k_blk0 + h)),
        pl.BlockSpec((rows, dv), lambda b, h, n: (bmap(b, n), v_blk0 + h)),
        pl.BlockSpec((rows, GATE_LOW_RANK), lambda b, h, n: (fmap(b, n), 0)),
        pl.BlockSpec((rows, GATE_LOW_RANK), lambda b, h, n: (bmap(b, n), 0)),
        pl.BlockSpec((3 * GATE_LOW_RANK, dk), lambda b, h, n: (0, h)),
        pl.BlockSpec((1, dk), lambda b, h, n: (0, h)),
        pl.BlockSpec((3 * GATE_LOW_RANK, dk), lambda b, h, n: (0, h)),
        pl.BlockSpec((1, dk), lambda b, h, n: (0, h)),
    ]
    args = [proj, proj, proj, proj, proj, proj, lr_f, lr_b,
            _hi_hi_lo(wup_f), b_f.reshape(1, -1), _hi_hi_lo(wup_b), b_b.reshape(1, -1)]
    aliases = {}
    if prev is not None:
        in_specs.append(pl.BlockSpec(memory_space=pl.ANY))
        args.append(prev)
        aliases = {len(args) - 1: 0}
    return pl.pallas_call(
        functools.partial(_gla_kernel, nchunk=blk // CHUNK, nsub=nsub, qscale=dk ** -0.5),
        grid=(B, GLA_HEADS, nb),
        in_specs=in_specs,
        out_specs=pl.BlockSpec((S, dv), lambda b, h, n: (sb0 + b, h)),
        out_shape=jax.ShapeDtypeStruct((T, GLA_HEADS * dv), F32),
        scratch_shapes=[pltpu.VMEM((dv, dk), F32), pltpu.VMEM((dv, dk), F32)],
        input_output_aliases=aliases,
        compiler_params=_cparams(("parallel", "parallel", "arbitrary")),
        name="gla",
    )(*args)


def _merge_kernel(fft_ref, o_ref_in, og_ref, g0_ref, g1_ref, hn_ref, wf_ref, wg_ref, o_ref, a_ref, *, dv):
    ya = jnp.dot(fft_ref[...], wf_ref[...], preferred_element_type=F32)
    for h in range(GLA_HEADS):
        cs = slice(h * dv, (h + 1) * dv)
        o = o_ref_in[:, cs]
        var = jnp.mean(o * o, axis=-1, keepdims=True)
        on = o * lax.rsqrt(var + EPS) * hn_ref[...]
        og = og_ref[:, cs].astype(F32)
        a_ref[:, cs] = (on * (og * jax.nn.sigmoid(og))).astype(BF16)
    yb = jnp.dot(a_ref[...], wg_ref[...], preferred_element_type=F32)
    m = jax.nn.sigmoid(g0_ref[...].astype(F32)) * ya + jax.nn.sigmoid(g1_ref[...].astype(F32)) * yb
    o_ref[...] = m.astype(BF16)


def _merge(fft, o_gla, proj, og_blk, g0_blk, g1_blk, hn, wf, wg, tm=256):
    T, D = o_gla.shape
    FW = fft.shape[1]
    dv = D // GLA_HEADS
    const = dict(pipeline_mode=pl.Buffered(1))
    return pl.pallas_call(
        functools.partial(_merge_kernel, dv=dv),
        grid=(T // tm,),
        in_specs=[
            pl.BlockSpec((tm, FW), lambda i: (i, 0)),
            pl.BlockSpec((tm, D), lambda i: (i, 0)),
            pl.BlockSpec((tm, D), lambda i: (i, og_blk)),
            pl.BlockSpec((tm, D), lambda i: (i, g0_blk)),
            pl.BlockSpec((tm, D), lambda i: (i, g1_blk)),
            pl.BlockSpec((1, dv), lambda i: (0, 0)),
            pl.BlockSpec((FW, D), lambda i: (0, 0), **const),
            pl.BlockSpec((D, D), lambda i: (0, 0), **const),
        ],
        out_specs=pl.BlockSpec((tm, D), lambda i: (i, 0)),
        out_shape=jax.ShapeDtypeStruct((T, D), BF16),
        scratch_shapes=[pltpu.VMEM((tm, D), BF16)],
        compiler_params=_cparams(("parallel",)),
        name="merge",
    )(fft, o_gla, proj, proj, proj, hn.reshape(1, dv), wf, wg)


HI16 = 0xFFFF0000


def _pack_bf16_pair(lo, hi):
    lo_bits = lax.bitcast_convert_type(lo.astype(BF16).astype(F32), jnp.uint32)
    hi_bits = lax.bitcast_convert_type(hi.astype(BF16).astype(F32), jnp.uint32)
    return (hi_bits & jnp.uint32(HI16)) | (lo_bits >> 16)


def _unpack_bf16_pair(w):
    lo = lax.bitcast_convert_type(w << 16, F32)
    hi = lax.bitcast_convert_type(w & jnp.uint32(HI16), F32)
    return lo, hi


def _store_tile_rows(ref, val):
    tm = val.shape[0]
    for s in range(SUBLANES):
        ref[pl.ds(s, tm, stride=SUBLANES), :] = val[:, s * LANES:(s + 1) * LANES]


def _load_tile_rows(ref, tm, s):
    return ref[pl.ds(s, tm, stride=SUBLANES), :]


def _outproj_router_kernel(m_ref, xp_ref, xs_ref, wo_ref, g_ref, wr_ref, br_ref,
                           x1_ref, xn_ref, idx_ref, tw_ref, rank_ref, cnt_ref, run_ref, *, n_exp, n0):
    i = pl.program_id(0)

    @pl.when(i == 0)
    def _():
        run_ref[...] = jnp.zeros_like(run_ref)

    tm, D = m_ref.shape
    x = jnp.where(i < n0, xp_ref[...], xs_ref[...])
    x1 = x + jnp.dot(m_ref[...], wo_ref[...], preferred_element_type=F32)
    x1_ref[...] = x1
    var = jnp.mean(x1 * x1, axis=-1, keepdims=True)
    xn = x1 * lax.rsqrt(var + EPS) * g_ref[...]
    _store_tile_rows(xn_ref, _pack_bf16_pair(xn[:, :D // 2], xn[:, D // 2:]))
    xh = xn.astype(BF16)
    xl = (xn - xh.astype(F32)).astype(BF16)
    r = jnp.dot(xh, wr_ref[...], preferred_element_type=F32)
    lg = (r[:, :n_exp] + r[:, n_exp:] + jnp.dot(xl, wr_ref[:, :n_exp], preferred_element_type=F32)
          + br_ref[...])

    lane = lax.broadcasted_iota(jnp.int32, (tm, n_exp), 1)
    vals, hots = [], []
    for _ in range(TOP_K):
        mx = jnp.max(lg, axis=-1, keepdims=True)
        ik = jnp.min(jnp.where(lg == mx, lane, n_exp), axis=-1, keepdims=True)
        hot = lane == ik
        vals.append(mx)
        hots.append(hot)
        lg = jnp.where(hot, -jnp.inf, lg)
    exps = [jnp.exp(v - vals[0]) for v in vals]
    denom = exps[0] + exps[1] + exps[2] + exps[3]

    sel = hots[0] | hots[1] | hots[2] | hots[3]
    sel_f = sel.astype(F32)
    r = lax.broadcasted_iota(jnp.int32, (tm, tm), 0)
    c = lax.broadcasted_iota(jnp.int32, (tm, tm), 1)
    strict = (c < r).astype(BF16)
    before = jnp.dot(strict, sel_f.astype(BF16), preferred_element_type=F32) + run_ref[...]
    run_ref[...] += jnp.sum(sel_f, axis=0, keepdims=True)
    cnt_ref[...] = run_ref[...].astype(jnp.int32)

    k4 = lax.broadcasted_iota(jnp.int32, (tm, TOP_K), 1)
    idx4 = jnp.zeros((tm, TOP_K), jnp.int32)
    w4 = jnp.zeros((tm, TOP_K), F32)
    rk4 = jnp.zeros((tm, TOP_K), jnp.int32)
    for k in range(TOP_K):
        ik = jnp.sum(jnp.where(hots[k], lane, 0), axis=-1, keepdims=True)
        rk = jnp.sum(jnp.where(hots[k], before, 0.0), axis=-1, keepdims=True).astype(jnp.int32)
        idx4 = jnp.where(k4 == k, ik, idx4)
        w4 = jnp.where(k4 == k, exps[k] / denom, w4)
        rk4 = jnp.where(k4 == k, rk, rk4)
    idx_ref[...] = idx4
    tw_ref[...] = w4
    rank_ref[...] = rk4


def _outproj_router(merged, xp, xs, wo, gain, wr2, br, tm=256):
    T, D = merged.shape
    E = wr2.shape[1] // 2
    n0 = xp.shape[0] // tm
    const = dict(pipeline_mode=pl.Buffered(1))
    row = lambda i: (i, 0)
    fix = lambda i: (0, 0)
    return pl.pallas_call(
        functools.partial(_outproj_router_kernel, n_exp=E, n0=n0),
        grid=(T // tm,),
        in_specs=[
            pl.BlockSpec((tm, D), row),
            pl.BlockSpec((tm, D), lambda i: (jnp.minimum(i, n0 - 1), 0)),
            pl.BlockSpec((tm, D), lambda i: (jnp.maximum(i - n0, 0), 0)),
            pl.BlockSpec((D, D), fix, **const),
            pl.BlockSpec((1, D), fix),
            pl.BlockSpec((D, 2 * E), fix),
            pl.BlockSpec((1, E), fix),
        ],
        out_specs=[
            pl.BlockSpec((tm, D), row),
            pl.BlockSpec((tm * SUBLANES, D // 2 // SUBLANES), row),
            pl.BlockSpec((tm, TOP_K), row),
            pl.BlockSpec((tm, TOP_K), row),
            pl.BlockSpec((tm, TOP_K), row),
            pl.BlockSpec((1, E), fix),
        ],
        out_shape=[
            jax.ShapeDtypeStruct((T, D), F32),
            jax.ShapeDtypeStruct((T * SUBLANES, D // 2 // SUBLANES), jnp.uint32),
            jax.ShapeDtypeStruct((T, TOP_K), jnp.int32),
            jax.ShapeDtypeStruct((T, TOP_K), F32),
            jax.ShapeDtypeStruct((T, TOP_K), jnp.int32),
            jax.ShapeDtypeStruct((1, E), jnp.int32),
        ],
        scratch_shapes=[pltpu.VMEM((1, E), F32)],
        compiler_params=_cparams(("arbitrary",)),
        name="outproj_router",
    )(merged, xp, xs, wo, gain.reshape(1, D), wr2, br.reshape(1, E))


def _dispatch_kernel(pos_ref, x_ref, xs_ref, dst_ref, sem, *, n_tok):
    i = pl.program_id(0)
    tm = x_ref.shape[0] // SUBLANES

    def body(t, carry):
        for k in range(TOP_K):
            p = pos_ref[t * TOP_K + k]
            dst_ref[p] = k * n_tok + i * tm + t
            pltpu.make_async_copy(x_ref.at[pl.ds(pl.multiple_of(t * SUBLANES, SUBLANES), SUBLANES), :],
                                  xs_ref.at[pl.ds(pl.multiple_of(p * SUBLANES, SUBLANES), SUBLANES), :], sem).start()
        return carry

    lax.fori_loop(0, tm, body, 0)
    for _ in range(TOP_K):
        pltpu.make_async_copy(x_ref, xs_ref.at[pl.ds(0, tm * SUBLANES), :], sem).wait()


def _dispatch(xn, pos_flat, n_slots, tm=256):
    T = xn.shape[0] // SUBLANES
    return pl.pallas_call(
        functools.partial(_dispatch_kernel, n_tok=T),
        grid=(T // tm,),
        in_specs=[
            pl.BlockSpec((tm * TOP_K,), lambda i: (i,), memory_space=pltpu.SMEM),
            pl.BlockSpec((tm * SUBLANES, LANES), lambda i: (i, 0)),
        ],
        out_specs=[
            pl.BlockSpec(memory_space=pl.ANY),
            pl.BlockSpec((n_slots,), lambda i: (0,), memory_space=pltpu.SMEM),
        ],
        out_shape=[jax.ShapeDtypeStruct((n_slots * SUBLANES, LANES), xn.dtype),
                   jax.ShapeDtypeStruct((n_slots,), jnp.int32)],
        scratch_shapes=[pltpu.SemaphoreType.DMA(())],
        compiler_params=_cparams(("arbitrary",)),
        name="dispatch",
    )(pos_flat, xn)


def _combine_kernel(tw_ref, x1_ref, g_ref, *rest, n0):
    y_refs, (op_ref, os_ref) = rest[:TOP_K], rest[TOP_K:]
    i = pl.program_id(0)
    tw = tw_ref[...]
    tm = x1_ref.shape[0]
    lo_acc = [None] * SUBLANES
    hi_acc = [None] * SUBLANES
    for k in range(TOP_K):
        wk = tw[:, k:k + 1]
        for s in range(SUBLANES):
            lo, hi = _unpack_bf16_pair(_load_tile_rows(y_refs[k], tm, s))
            lo_acc[s] = wk * lo if k == 0 else lo_acc[s] + wk * lo
            hi_acc[s] = wk * hi if k == 0 else hi_acc[s] + wk * hi
    x2 = x1_ref[...] + jnp.concatenate(lo_acc + hi_acc, axis=1)
    var = jnp.mean(x2 * x2, axis=-1, keepdims=True)
    y = x2 * lax.rsqrt(var + EPS) * g_ref[...]

    @pl.when(i < n0)
    def _():
        op_ref[...] = y

    @pl.when(i >= n0)
    def _():
        os_ref[...] = y


def _combine(tw, x1, gain, yk, t_prompt, tm=256):
    T, D = x1.shape
    n0 = t_prompt // tm
    nblk = T // tm
    y_spec = lambda k: pl.BlockSpec((tm * SUBLANES, LANES), lambda i: (k * nblk + i, 0))
    return pl.pallas_call(
        functools.partial(_combine_kernel, n0=n0),
        grid=(T // tm,),
        in_specs=[
            pl.BlockSpec((tm, TOP_K), lambda i: (i, 0)),
            pl.BlockSpec((tm, D), lambda i: (i, 0)),
            pl.BlockSpec((1, D), lambda i: (0, 0)),
        ] + [y_spec(k) for k in range(TOP_K)],
        out_specs=[
            pl.BlockSpec((tm, D), lambda i: (jnp.minimum(i, n0 - 1), 0)),
            pl.BlockSpec((tm, D), lambda i: (jnp.maximum(i - n0, 0), 0)),
        ],
        out_shape=[jax.ShapeDtypeStruct((t_prompt, D), F32), jax.ShapeDtypeStruct((T - t_prompt, D), F32)],
        compiler_params=_cparams(("arbitrary",)),
        name="combine",
    )(tw, x1, gain.reshape(1, D), *([yk] * TOP_K))


def _gate_up_kernel(g0_ref, nt_ref, cnt_ref, w_ref, bg_ref, bu_ref, x_hbm, o_hbm,
                    wp_ref, xin_ref, xb_ref, obuf_ref, sin, sout, *, tn):
    j, e = pl.program_id(0), pl.program_id(1)
    tm = GROUP_TILE
    D = xb_ref.shape[1]
    half = MXU_DIM // 2
    n = nt_ref[e]
    row_base = g0_ref[e]
    cnt = cnt_ref[e]

    def in_copy(r, slot):
        rows = pl.ds(pl.multiple_of((row_base + r * tm) * SUBLANES, tm * SUBLANES), tm * SUBLANES)
        return pltpu.make_async_copy(x_hbm.at[rows, :], xin_ref.at[slot], sin.at[slot])

    def out_copy(r, slot):
        rows = pl.ds(pl.multiple_of(row_base + r * tm, tm), tm)
        cols = pl.ds(pl.multiple_of(j * (tn // 2), LANES), tn // 2)
        return pltpu.make_async_copy(obuf_ref.at[slot], o_hbm.at[rows, cols], sout.at[slot])

    def body(r, carry):
        slot = r % 2

        @pl.when(r + 1 < n)
        def _():
            in_copy(r + 1, 1 - slot).start()

        in_copy(r, slot).wait()

        @pl.when(r >= 2)
        def _():
            out_copy(r - 2, slot).wait()

        valid = (r * tm + lax.broadcasted_iota(jnp.int32, (tm, 1), 0)) < cnt
        for s in range(SUBLANES):
            lo, hi = _unpack_bf16_pair(_load_tile_rows(xin_ref.at[slot], tm, s))
            xb_ref[:, s * LANES:(s + 1) * LANES] = jnp.where(valid, lo, 0.0).astype(BF16)
            xb_ref[:, D // 2 + s * LANES:D // 2 + (s + 1) * LANES] = jnp.where(valid, hi, 0.0).astype(BF16)
        for cb in range(tn // MXU_DIM):
            h = jnp.dot(xb_ref[...], wp_ref[:, cb * MXU_DIM:(cb + 1) * MXU_DIM], preferred_element_type=F32)
            hg = h[:, :half] + bg_ref[0, :, cb * half:(cb + 1) * half]
            hu = h[:, half:] + bu_ref[0, :, cb * half:(cb + 1) * half]
            gate = jnp.minimum(hg, SWIGLU_LIMIT)
            up = jnp.clip(hu, -SWIGLU_LIMIT, SWIGLU_LIMIT)
            act = gate * jax.nn.sigmoid(SWIGLU_ALPHA * gate) * (up + 1.0)
            obuf_ref[slot, :, cb * half:(cb + 1) * half] = act.astype(obuf_ref.dtype)
        out_copy(r, slot).start()
        return carry

    @pl.when(n > 0)
    def _():
        in_copy(0, 0).start()
        r_i = lax.broadcasted_iota(jnp.int32, (MXU_DIM, MXU_DIM), 0)
        c_i = lax.broadcasted_iota(jnp.int32, (MXU_DIM, MXU_DIM), 1)
        perm = (((c_i < half) & (r_i == 2 * c_i)) | ((c_i >= half) & (r_i == 2 * (c_i - half) + 1))).astype(BF16)
        for cb in range(tn // MXU_DIM):
            for rb in range(D // 512):
                w = w_ref[0, rb * 512:(rb + 1) * 512, cb * MXU_DIM:(cb + 1) * MXU_DIM].astype(BF16)
                wp_ref[rb * 512:(rb + 1) * 512, cb * MXU_DIM:(cb + 1) * MXU_DIM] = jnp.dot(
                    w, perm, preferred_element_type=F32).astype(BF16)
        lax.fori_loop(0, n, body, 0)

        @pl.when(n >= 2)
        def _():
            out_copy(n - 2, n % 2).wait()

        out_copy(n - 1, (n - 1) % 2).wait()


def _gate_up(g0, ntiles, cnt, xs, w_gate_up, bg, bu, tn):
    P = xs.shape[0] // SUBLANES
    E, D, H2 = w_gate_up.shape
    tm = GROUP_TILE
    wmap = lambda j, e, *_: (e, 0, j)
    gs = pltpu.PrefetchScalarGridSpec(
        num_scalar_prefetch=3,
        grid=(H2 // tn, E),
        in_specs=[
            pl.BlockSpec((1, D, tn), wmap),
            pl.BlockSpec((1, 1, tn // 2), wmap),
            pl.BlockSpec((1, 1, tn // 2), wmap),
            pl.BlockSpec(memory_space=pl.ANY),
        ],
        out_specs=pl.BlockSpec(memory_space=pl.ANY),
        scratch_shapes=[
            pltpu.VMEM((D, tn), BF16),
            pltpu.VMEM((2, tm * SUBLANES, LANES), jnp.uint32),
            pltpu.VMEM((tm, D), BF16),
            pltpu.VMEM((2, tm, tn // 2), BF16),
            pltpu.SemaphoreType.DMA((2,)),
            pltpu.SemaphoreType.DMA((2,)),
        ],
    )
    return pl.pallas_call(
        functools.partial(_gate_up_kernel, tn=tn),
        grid_spec=gs,
        out_shape=jax.ShapeDtypeStruct((P, H2 // 2), BF16),
        compiler_params=_cparams(("arbitrary", "arbitrary")),
        name="moe_gate_up",
    )(g0, ntiles, cnt, w_gate_up, bg, bu, xs)


def _down_kernel(g0_ref, nt_ref, cnt_ref, dst_ref, w_ref, b_ref, a_hbm, y_hbm, wb_ref, ain_ref, obuf_ref, sin, sout):
    e = pl.program_id(0)
    tm = GROUP_TILE
    H, D = wb_ref.shape
    n = nt_ref[e]
    row_base = g0_ref[e]
    cnt = cnt_ref[e]
    n_real = y_hbm.shape[0] // SUBLANES - dst_ref.shape[0]

    def in_copy(r, slot):
        rows = pl.ds(pl.multiple_of(row_base + r * tm, tm), tm)
        return pltpu.make_async_copy(a_hbm.at[rows, :], ain_ref.at[slot], sin.at[slot])

    nblk = (D // 2) // MXU_DIM

    def compute(slot, out_ref, before_block=None):
        a = ain_ref[slot]
        for bi, c0 in enumerate(range(0, D // 2, MXU_DIM)):
            if before_block is not None:
                before_block(bi)
            c1 = D // 2 + c0
            lo = jnp.dot(a, wb_ref[:, c0:c0 + MXU_DIM], preferred_element_type=F32) + b_ref[0, :, c0:c0 + MXU_DIM]
            hi = jnp.dot(a, wb_ref[:, c1:c1 + MXU_DIM], preferred_element_type=F32) + b_ref[0, :, c1:c1 + MXU_DIM]
            packed = _pack_bf16_pair(lo, hi)
            for u in range(MXU_DIM // LANES):
                out_ref[pl.ds(c0 // LANES + u, tm, stride=SUBLANES), :] = packed[:, u * LANES:(u + 1) * LANES]

    def scatter_rows(q, so, part):
        base = row_base + q * tm
        for i in range(part * (tm // nblk), (part + 1) * (tm // nblk)):
            d = jnp.where(q * tm + i < cnt, dst_ref[base + i], n_real + base + i)
            pltpu.make_async_copy(obuf_ref.at[so, pl.ds(i * SUBLANES, SUBLANES), :],
                                  y_hbm.at[pl.ds(pl.multiple_of(d * SUBLANES, SUBLANES), SUBLANES), :],
                                  sout.at[so]).start()

    def scatter_tile(q, so):
        for part in range(nblk):
            scatter_rows(q, so, part)

    def wait_scatter(so):
        pltpu.make_async_copy(obuf_ref.at[so], y_hbm.at[pl.ds(0, tm * SUBLANES), :], sout.at[so]).wait()

    def body(r, carry):
        slot = r % 2

        @pl.when(r + 1 < n)
        def _():
            in_copy(r + 1, 1 - slot).start()

        in_copy(r, slot).wait()

        @pl.when(r >= 2)
        def _():
            wait_scatter(slot)

        compute(slot, obuf_ref.at[slot], functools.partial(scatter_rows, r - 1, 1 - slot))
        return carry

    @pl.when(n > 0)
    def _():
        in_copy(0, 0).start()
        for rb in range(H // 512):
            wb_ref[rb * 512:(rb + 1) * 512, :] = w_ref[0, rb * 512:(rb + 1) * 512, :].astype(BF16)

        @pl.when(n > 1)
        def _():
            in_copy(1, 1).start()

        in_copy(0, 0).wait()
        compute(0, obuf_ref.at[0])
        lax.fori_loop(1, n, body, 0)
        so = (n - 1) % 2
        scatter_tile(n - 1, so)
        wait_scatter(so)

        @pl.when(n >= 2)
        def _():
            wait_scatter(1 - so)


def _down(g0, ntiles, cnt, slot_dst, n_rows, act, wd, bd):
    P, H = act.shape
    E, _, D = wd.shape
    tm = GROUP_TILE
    wmap = lambda e, *_: (e, 0, 0)
    gs = pltpu.PrefetchScalarGridSpec(
        num_scalar_prefetch=4,
        grid=(E,),
        in_specs=[
            pl.BlockSpec((1, H, D), wmap),
            pl.BlockSpec((1, 1, D), wmap),
            pl.BlockSpec(memory_space=pl.ANY),
        ],
        out_specs=pl.BlockSpec(memory_space=pl.ANY),
        scratch_shapes=[
            pltpu.VMEM((H, D), BF16),
            pltpu.VMEM((2, tm, H), BF16),
            pltpu.VMEM((2, tm * SUBLANES, LANES), jnp.uint32),
            pltpu.SemaphoreType.DMA((2,)),
            pltpu.SemaphoreType.DMA((2,)),
        ],
    )
    return pl.pallas_call(
        _down_kernel,
        grid_spec=gs,
        out_shape=jax.ShapeDtypeStruct((n_rows * SUBLANES, LANES), jnp.uint32),
        compiler_params=_cparams(("arbitrary",)),
        name="moe_down",
    )(g0, ntiles, cnt, slot_dst, wd, bd, act)


def _trunk(xp, xs, seq_shapes, norm_mix, w_in, w_gk_up_fwd, b_gk_fwd, w_gk_up_bwd, b_gk_bwd, gla_head_norm,
           w_fnet_out, w_gla_out, w_out, norm_ffn, w_router, b_router, w_gate_up, b_gate_up,
           w_down, b_down, norm_final):
    D = xp.shape[1]
    T = xp.shape[0] + xs.shape[0]
    fw = w_fnet_out.shape[0]
    dkk = w_gk_up_fwd.shape[1]
    dvv = w_gla_out.shape[0]
    dk, dv = dkk // GLA_HEADS, dvv // GLA_HEADS
    sizes = (fw, dkk, dkk, dvv, dvv, GATE_LOW_RANK, GATE_LOW_RANK, 2 * D)
    offs = np.concatenate([[0], np.cumsum(sizes)])
    sl = lambda n: slice(int(offs[n]), int(offs[n + 1]))
    w_u, w_q, w_k, w_v, w_og = (w_in[:, sl(n)] for n in range(5))
    w_lr = w_in[:, int(offs[5]):int(offs[7])]
    w_g = w_in[:, sl(7)]
    w_main = jnp.concatenate([w_og, w_g, w_v, w_u, w_q, w_k], axis=1).astype(BF16)
    og_blk, g0_blk, g1_blk = 0, dvv // D, dvv // D + 1
    v_off = dvv + 2 * D
    u_off = v_off + dvv
    q_off = u_off + fw
    k_off = q_off + dkk

    proj, lr = _inproj(xp, xs, norm_mix, w_main, _hi_lo(w_lr))
    lr_f, lr_b = lr[:, :GATE_LOW_RANK], lr[:, GATE_LOW_RANK:]

    gd = fw // FNET_GROUPS
    cc, sc = _dft_mats(gd, gd ** -0.5)
    cs = jnp.concatenate([cc, sc], axis=1).astype(BF16)
    zc, zs = _chan_dft(proj, u_off // fw, fw, cs)
    fft, o_gla = None, None
    row0 = 0
    for (B, S) in seq_shapes:
        cm, sm = _dft_mats(S, S ** -0.5)
        fft = _seq_dft(zc, zs, cm.astype(BF16), (-sm).astype(BF16), row0, B, S, prev=fft)
        o_gla = _gla(proj, lr_f, lr_b, w_gk_up_fwd, b_gk_fwd, w_gk_up_bwd, b_gk_bwd, row0, B, S,
                     q_off // dk, k_off // dk, v_off // dv, dk, dv, prev=o_gla)
        row0 += B * S

    merged = _merge(fft, o_gla, proj, og_blk, g0_blk, g1_blk, gla_head_norm,
                    w_fnet_out.astype(BF16), w_gla_out.astype(BF16))
    x1, xn2, idx, tw, rank, cnt = _outproj_router(merged, xp, xs, w_out.astype(BF16), norm_ffn,
                                                  _hi_lo(w_router), b_router)

    E = w_router.shape[1]
    cnt = cnt.reshape(E)
    gsz = ((cnt + GROUP_TILE - 1) // GROUP_TILE) * GROUP_TILE
    gend = jnp.cumsum(gsz)
    gstart = gend - gsz
    pos = (gstart[idx] + rank).reshape(-1).astype(jnp.int32)
    n_slots = T * TOP_K + E * GROUP_TILE
    g0 = gstart.astype(jnp.int32)
    ntiles = (gsz // GROUP_TILE).astype(jnp.int32)

    x_sorted, slot_dst = _dispatch(xn2, pos, n_slots)
    H = w_down.shape[1]
    bg = b_gate_up[:, 0::2].reshape(E, 1, H)
    bu = b_gate_up[:, 1::2].reshape(E, 1, H)
    act = _gate_up(g0, ntiles, cnt, x_sorted, w_gate_up, bg, bu, GATE_UP_TN)
    yk = _down(g0, ntiles, cnt, slot_dst, T * TOP_K + n_slots, act, w_down, b_down.reshape(E, 1, D))
    return _combine(tw, x1, norm_final, yk, xp.shape[0])


def kernel(x_prompt, x_sample, norm_mix, w_in, w_gk_up_fwd, b_gk_fwd, w_gk_up_bwd, b_gk_bwd, gla_head_norm,
           w_fnet_out, w_gla_out, w_out, norm_ffn, w_router, b_router, w_gate_up, b_gate_up, w_down,
           b_down, norm_final):
    D = x_prompt.shape[-1]
    shapes = (x_prompt.shape[:2], x_sample.shape[:2])
    yp, ys = _trunk(x_prompt.reshape(-1, D), x_sample.reshape(-1, D), shapes, norm_mix[0], w_in[0], w_gk_up_fwd[0], b_gk_fwd[0], w_gk_up_bwd[0], b_gk_bwd[0],
               gla_head_norm[0], w_fnet_out[0], w_gla_out[0], w_out[0], norm_ffn[0], w_router[0],
               b_router[0], w_gate_up[0], b_gate_up[0], w_down[0], b_down[0], norm_final)
    return (yp.reshape(x_prompt.shape), ys.reshape(x_sample.shape))
```

```python
import functools
import math

import numpy as np
import jax
import jax.numpy as jnp
from jax import lax
from jax.experimental import pallas as pl
from jax.experimental.pallas import tpu as pltpu

F32 = jnp.float32
BF16 = jnp.bfloat16
HIGHEST = lax.Precision.HIGHEST

EPS = 1e-5
FNET_GROUPS = 4
GLA_HEADS = 4
GATE_LOW_RANK = 16
GATE_LOGIT_NORMALIZER = 16.0
CHUNK = 64
TOP_K = 4
SWIGLU_LIMIT = 7.0
SWIGLU_ALPHA = 1.702

VMEM_LIMIT_BYTES = 56 * 1024 * 1024
MXU_DIM = 256
LANES = 128
SUBLANES = 8
GROUP_TILE = 256
GATE_UP_TN = 2048


def _cparams(sem):
    return pltpu.CompilerParams(dimension_semantics=sem, vmem_limit_bytes=VMEM_LIMIT_BYTES)


def _split3(x):
    hi = x.astype(BF16)
    r = x - hi.astype(F32)
    mid = r.astype(BF16)
    lo = (r - mid.astype(F32)).astype(BF16)
    return hi, mid, lo


def _hi_lo(w):
    hi = w.astype(BF16)
    lo = (w - hi.astype(F32)).astype(BF16)
    return jnp.concatenate([hi, lo], axis=1)


def _inproj_kernel(xp_ref, xs_ref, g_ref, w_ref, wlr_ref, o_ref, lr_ref, xn_ref, *, n0):
    @pl.when(pl.program_id(1) == 0)
    def _():
        x = jnp.where(pl.program_id(0) < n0, xp_ref[...], xs_ref[...])
        var = jnp.mean(x * x, axis=-1, keepdims=True)
        xn = (x * lax.rsqrt(var + EPS) * g_ref[...]).astype(BF16)
        xn_ref[...] = xn
        r = jnp.dot(xn, wlr_ref[...], preferred_element_type=F32)
        nlr = lr_ref.shape[1]
        lr_ref[...] = r[:, :nlr] + r[:, nlr:]

    o_ref[...] = jnp.dot(xn_ref[...], w_ref[...], preferred_element_type=F32).astype(o_ref.dtype)


def _inproj(xp, xs, gain, w_main, w_lr2, tm=512, tn=1024):
    D = xp.shape[1]
    T = xp.shape[0] + xs.shape[0]
    n0 = xp.shape[0] // tm
    N = w_main.shape[1]
    R = w_lr2.shape[1] // 2
    return pl.pallas_call(
        functools.partial(_inproj_kernel, n0=n0),
        grid=(T // tm, N // tn),
        in_specs=[
            pl.BlockSpec((tm, D), lambda i, j: (jnp.minimum(i, n0 - 1), 0)),
            pl.BlockSpec((tm, D), lambda i, j: (jnp.maximum(i - n0, 0), 0)),
            pl.BlockSpec((1, D), lambda i, j: (0, 0)),
            pl.BlockSpec((D, tn), lambda i, j: (0, j)),
            pl.BlockSpec((D, 2 * R), lambda i, j: (0, 0)),
        ],
        out_specs=[
            pl.BlockSpec((tm, tn), lambda i, j: (i, j)),
            pl.BlockSpec((tm, R), lambda i, j: (i, 0)),
        ],
        out_shape=[jax.ShapeDtypeStruct((T, N), BF16), jax.ShapeDtypeStruct((T, R), F32)],
        scratch_shapes=[pltpu.VMEM((tm, D), BF16)],
        compiler_params=_cparams(("parallel", "arbitrary")),
        name="inproj",
    )(xp, xs, gain.reshape(1, D), w_main, w_lr2)


def _chan_dft_kernel(u_ref, cs_ref, z_ref, *, gd):
    for g in range(FNET_GROUPS):
        r = jnp.dot(u_ref[:, g * gd:(g + 1) * gd], cs_ref[...], preferred_element_type=F32)
        z_ref[:, g * gd:(g + 1) * gd] = _pack_bf16_pair(r[:, :gd], r[:, gd:])


def _chan_dft(proj, u_col_block, width, cs, tm=512):
    T = proj.shape[0]
    gd = width // FNET_GROUPS
    return pl.pallas_call(
        functools.partial(_chan_dft_kernel, gd=gd),
        grid=(T // tm,),
        in_specs=[
            pl.BlockSpec((tm, width), lambda i: (i, u_col_block)),
            pl.BlockSpec((gd, 2 * gd), lambda i: (0, 0)),
        ],
        out_specs=pl.BlockSpec((tm, width), lambda i: (i, 0)),
        out_shape=jax.ShapeDtypeStruct((T, width), jnp.uint32),
        compiler_params=_cparams(("parallel",)),
        name="chan_dft",
    )(proj, cs)


_FFT_COLS = 2 * LANES


def _fft_stage1_kernel(z_ref, f1_ref, ct_ref, st_ref, a_ref, *, n1, n2):
    for m in range(n2):
        rows = pl.ds(m, n1, stride=n2)
        zc, zs = _unpack_bf16_pair(z_ref[rows, :])
        pc = jnp.dot(f1_ref[...], zc.astype(BF16), preferred_element_type=F32)
        ps = jnp.dot(f1_ref[...], zs.astype(BF16), preferred_element_type=F32)
        a_re = pc[:n1] - ps[n1:]
        a_im = -ps[:n1] - pc[n1:]
        ct = ct_ref[m][:, 0:1]
        st = st_ref[m][:, 0:1]
        a_ref[rows, :] = _pack_bf16_pair(a_re * ct + a_im * st, a_im * ct - a_re * st)


def _fft_stage2_kernel(a_ref, f2_ref, *rest, n1, n2):
    o_ref = rest[-1]
    half = _FFT_COLS // 2
    for k1 in range(n1):
        a_re, a_im = _unpack_bf16_pair(a_ref[pl.ds(k1 * n2, n2), :])
        rhs = jnp.concatenate([a_re.astype(BF16), a_im.astype(BF16)], axis=0)
        x = jnp.dot(f2_ref[...], rhs, preferred_element_type=F32)
        o_ref[pl.ds(k1, n2, stride=n1), :] = _pack_bf16_pair(x[:, :half], x[:, half:])


def _seq_dft(z, row0, B, S, prev=None):
    T, W = z.shape
    wc = _FFT_COLS
    n2 = 64 if S % (64 * SUBLANES) == 0 else S // SUBLANES
    n1 = S // n2
    assert n1 * n2 == S and row0 % S == 0 and W % wc == 0
    s0 = row0 // S
    i1 = jnp.arange(n1, dtype=jnp.int32)
    i2 = jnp.arange(n2, dtype=jnp.int32)
    ang1 = ((i1[:, None] * i1[None, :]) % n1).astype(F32) * (2.0 * math.pi / n1)
    f1 = (jnp.concatenate([jnp.cos(ang1), jnp.sin(ang1)], axis=0) * S ** -0.5).astype(BF16)
    ang2 = ((i2[:, None] * i2[None, :]) % n2).astype(F32) * (2.0 * math.pi / n2)
    f2 = jnp.concatenate([jnp.cos(ang2), jnp.sin(ang2)], axis=1).astype(BF16)
    angt = (i2[:, None] * i1[None, :]).astype(F32) * (2.0 * math.pi / S)
    ct = jnp.broadcast_to(jnp.cos(angt)[:, :, None], (n2, n1, LANES))
    st = jnp.broadcast_to(jnp.sin(angt)[:, :, None], (n2, n1, LANES))

    fix2 = lambda b, c: (0, 0)
    fix3 = lambda b, c: (0, 0, 0)

    a = pl.pallas_call(
        functools.partial(_fft_stage1_kernel, n1=n1, n2=n2),
        grid=(B, W // LANES),
        in_specs=[pl.BlockSpec((S, LANES), lambda b, c: (s0 + b, c)),
                  pl.BlockSpec((2 * n1, n1), fix2),
                  pl.BlockSpec((n2, n1, LANES), fix3),
                  pl.BlockSpec((n2, n1, LANES), fix3)],
        out_specs=pl.BlockSpec((S, LANES), lambda b, c: (b, c)),
        out_shape=jax.ShapeDtypeStruct((B * S, W), jnp.uint32),
        compiler_params=_cparams(("parallel", "parallel")),
        name="fft_stage1",
    )(z, f1, ct, st)

    in_specs = [pl.BlockSpec((S, wc), lambda b, c: (b, c)), pl.BlockSpec((n2, 2 * n2), fix2)]
    args = [a, f2]
    aliases = {}
    if prev is not None:
        in_specs.append(pl.BlockSpec(memory_space=pl.ANY))
        args.append(prev)
        aliases = {2: 0}
    return pl.pallas_call(
        functools.partial(_fft_stage2_kernel, n1=n1, n2=n2),
        grid=(B, W // wc),
        in_specs=in_specs,
        out_specs=pl.BlockSpec((S, wc // 2), lambda b, c: (s0 + b, c)),
        out_shape=jax.ShapeDtypeStruct((T, W // 2), jnp.uint32),
        input_output_aliases=aliases,
        compiler_params=_cparams(("parallel", "parallel")),
        name="fft_stage2",
    )(*args)


def _dft_mats(n, scale, split=64):
    split = split if n % split == 0 else 1
    k = jnp.arange(n, dtype=jnp.int32)[None, :]
    j1 = jnp.arange(n // split, dtype=jnp.int32)[:, None]
    j2 = jnp.arange(split, dtype=jnp.int32)[:, None]
    w = 2.0 * math.pi / n
    ang_a = ((split * j1 * k) % n).astype(F32) * w
    ang_b = ((j2 * k) % n).astype(F32) * w
    ca, sa = jnp.cos(ang_a)[:, None, :], jnp.sin(ang_a)[:, None, :]
    cb, sb = (jnp.cos(ang_b) * scale)[None, :, :], (jnp.sin(ang_b) * scale)[None, :, :]
    c = (ca * cb - sa * sb).reshape(n, n)
    s = (sa * cb + ca * sb).reshape(n, n)
    return c, s


_NT = (((1,), (1,)), ((), ()))
_TN = (((0,), (0,)), ((), ()))


def _gla_block(q_ref, k_ref, v_ref, lr_ref, w3_ref, b_ref, st_ref, reverse, qscale, nchunk):
    R = nchunk * CHUNK
    lr = lr_ref[...]
    lr_hi = lr.astype(BF16)
    lr_lo = (lr - lr_hi.astype(F32)).astype(BF16)
    z = jnp.dot(jnp.concatenate([lr_hi, lr_lo, lr_hi], axis=1), w3_ref[...],
                preferred_element_type=F32) + b_ref[...]
    g = (jnp.minimum(z, 0.0) - jnp.log(1.0 + jnp.exp(-jnp.abs(z)))) * (1.0 / GATE_LOGIT_NORMALIZER)
    ri = lax.broadcasted_iota(jnp.int32, (R, R), 0)
    ci = lax.broadcasted_iota(jnp.int32, (R, R), 1)
    cum = ((ci >= ri) if reverse else (ci <= ri)).astype(BF16)
    g_hi = g.astype(BF16)
    g_lo = (g - g_hi.astype(F32)).astype(BF16)
    G = jnp.dot(cum, g_hi, preferred_element_type=F32) + jnp.dot(cum, g_lo, preferred_element_type=F32)

    dk = G.shape[1]
    zero_row = jnp.zeros((1, dk), F32)
    if reverse:
        starts = [G[(c + 1) * CHUNK:(c + 1) * CHUNK + 1, :] if c + 1 < nchunk else zero_row for c in range(nchunk)]
        ref_row, g_tot = CHUNK // 2, G[0:1, :]
    else:
        starts = [G[c * CHUNK - 1:c * CHUNK, :] if c > 0 else zero_row for c in range(nchunk)]
        ref_row, g_tot = CHUNK // 2 - 1, G[R - 1:R, :]
    bcast = lambda rows_: jnp.concatenate([jnp.broadcast_to(r_, (CHUNK, dk)) for r_ in rows_], axis=0)
    gc = G - bcast(starts)
    gref = bcast([gc[c * CHUNK + ref_row:c * CHUNK + ref_row + 1, :] for c in range(nchunk)])

    q = q_ref[...].astype(F32) * qscale
    k = k_ref[...].astype(F32)
    v = v_ref[...]
    q_in = (q * jnp.exp(gc - gref)).astype(BF16)
    k_in = (k * jnp.exp(gref - gc)).astype(BF16)
    q_it = (q * jnp.exp(gc)).astype(BF16)
    q_st = (q * jnp.exp(G)).astype(BF16)
    k_st = (k * jnp.exp(g_tot - G)).astype(BF16)

    s_diag = lax.dot_general(q_in, k_in, _NT, preferred_element_type=F32)
    same = (ri // CHUNK) == (ci // CHUNK)
    keep = same & ((ci > ri) if reverse else (ci <= ri))
    s_rows = []
    for c in range(nchunk):
        rows = slice(c * CHUNK, (c + 1) * CHUNK)
        s = jnp.where(keep[rows, :], s_diag[rows, :], 0.0)
        lo_, hi_ = ((c + 1) * CHUNK, R) if reverse else (0, c * CHUNK)
        if hi_ > lo_:
            kx = (k[lo_:hi_, :] * jnp.exp(starts[c] - G[lo_:hi_, :])).astype(BF16)
            pad = jnp.zeros((R - (hi_ - lo_), dk), BF16)
            kx = jnp.concatenate([pad, kx] if reverse else [kx, pad], axis=0)
            s = s + lax.dot_general(q_it[rows, :], kx, _NT, preferred_element_type=F32)
        s_rows.append(s.astype(BF16))
    scores = jnp.concatenate(s_rows, axis=0)

    st = st_ref[...]
    o = (jnp.dot(scores, v, preferred_element_type=F32)
         + lax.dot_general(q_st, st.astype(BF16), _NT, preferred_element_type=F32))
    st_ref[...] = st * jnp.exp(g_tot) + lax.dot_general(v, k_st, _TN, preferred_element_type=F32)
    return o


def _gla_kernel(qf_ref, kf_ref, vf_ref, qb_ref, kb_ref, vb_ref, lrf_ref, lrb_ref,
                wf_ref, bf_ref, wb_ref, bb_ref, *rest, nchunk, nsub, qscale):
    o_ref, stf_ref, stb_ref = rest[-3:]
    n = pl.program_id(2)
    nb = pl.num_programs(2)
    blk = nchunk * CHUNK
    rows = nsub * blk

    @pl.when(n == 0)
    def _():
        o_ref[...] = jnp.zeros_like(o_ref)
        stf_ref[...] = jnp.zeros_like(stf_ref)
        stb_ref[...] = jnp.zeros_like(stb_ref)

    for s_f in range(nsub):
        s_b = nsub - 1 - s_f
        sub_f, sub_b = pl.ds(s_f * blk, blk), pl.ds(s_b * blk, blk)
        o_f = _gla_block(qf_ref.at[sub_f, :], kf_ref.at[sub_f, :], vf_ref.at[sub_f, :], lrf_ref.at[sub_f, :],
                         wf_ref, bf_ref, stf_ref, False, qscale, nchunk)
        o_b = _gla_block(qb_ref.at[sub_b, :], kb_ref.at[sub_b, :], vb_ref.at[sub_b, :], lrb_ref.at[sub_b, :],
                         wb_ref, bb_ref, stb_ref, True, qscale, nchunk)
        o_ref[pl.ds(pl.multiple_of(n * rows + s_f * blk, blk), blk), :] += o_f
        o_ref[pl.ds(pl.multiple_of((nb - 1 - n) * rows + s_b * blk, blk), blk), :] += o_b


def _hi_hi_lo(w):
    hi = w.astype(BF16)
    lo = (w - hi.astype(F32)).astype(BF16)
    return jnp.concatenate([hi, hi, lo], axis=0)


def _gla(proj, lr_f, lr_b, wup_f, b_f, wup_b, b_b, row0, B, S, q_blk0, k_blk0, v_blk0, dk, dv,
         prev=None, blk=256, nsub=2):
    T = proj.shape[0]
    nsub = nsub if S % (nsub * blk) == 0 else 1
    rows = nsub * blk
    assert S % rows == 0 and row0 % S == 0
    nb = S // rows
    rb0, sb0 = row0 // rows, row0 // S
    fmap = lambda b, n: rb0 + b * nb + n
    bmap = lambda b, n: rb0 + b * nb + (nb - 1 - n)
    in_specs = [
        pl.BlockSpec((rows, dk), lambda b, h, n: (fmap(b, n), q_blk0 + h)),
        pl.BlockSpec((rows, dk), lambda b, h, n: (fmap(b, n), k_blk0 + h)),
        pl.BlockSpec((rows, dv), lambda b, h, n: (fmap(b, n), v_blk0 + h)),
        pl.BlockSpec((rows, dk), lambda b, h, n: (bmap(b, n), q_blk0 + h)),
        pl.BlockSpec((rows, dk), lambda b, h, n: (bmap(b, n), k_blk0 + h)),
        pl.BlockSpec((rows, dv), lambda b, h, n: (bmap(b, n), v_blk0 + h)),
        pl.BlockSpec((rows, GATE_LOW_RANK), lambda b, h, n: (fmap(b, n), 0)),
        pl.BlockSpec((rows, GATE_LOW_RANK), lambda b, h, n: (bmap(b, n), 0)),
        pl.BlockSpec((3 * GATE_LOW_RANK, dk), lambda b, h, n: (0, h)),
        pl.BlockSpec((1, dk), lambda b, h, n: (0, h)),
        pl.BlockSpec((3 * GATE_LOW_RANK, dk), lambda b, h, n: (0, h)),
        pl.BlockSpec((1, dk), lambda b, h, n: (0, h)),
    ]
    args = [proj, proj, proj, proj, proj, proj, lr_f, lr_b,
            _hi_hi_lo(wup_f), b_f.reshape(1, -1), _hi_hi_lo(wup_b), b_b.reshape(1, -1)]
    aliases = {}
    if prev is not None:
        in_specs.append(pl.BlockSpec(memory_space=pl.ANY))
        args.append(prev)
        aliases = {len(args) - 1: 0}
    return pl.pallas_call(
        functools.partial(_gla_kernel, nchunk=blk // CHUNK, nsub=nsub, qscale=dk ** -0.5),
        grid=(B, GLA_HEADS, nb),
        in_specs=in_specs,
        out_specs=pl.BlockSpec((S, dv), lambda b, h, n: (sb0 + b, h)),
        out_shape=jax.ShapeDtypeStruct((T, GLA_HEADS * dv), F32),
        scratch_shapes=[pltpu.VMEM((dv, dk), F32), pltpu.VMEM((dv, dk), F32)],
        input_output_aliases=aliases,
        compiler_params=_cparams(("parallel", "parallel", "arbitrary")),
        name="gla",
    )(*args)


def _merge_kernel(fft_ref, o_ref_in, og_ref, g0_ref, g1_ref, hn_ref, wf_ref, wg_ref, o_ref, a_ref, *, dv):
    half = _FFT_COLS // 2
    pieces = []
    for cb in range(fft_ref.shape[1] // half):
        pieces.extend(_unpack_bf16_pair(fft_ref[:, cb * half:(cb + 1) * half]))
    ya = jnp.dot(jnp.concatenate(pieces, axis=1).astype(BF16), wf_ref[...], preferred_element_type=F32)
    for h in range(GLA_HEADS):
        cs = slice(h * dv, (h + 1) * dv)
        o = o_ref_in[:, cs]
        var = jnp.mean(o * o, axis=-1, keepdims=True)
        on = o * lax.rsqrt(var + EPS) * hn_ref[...]
        og = og_ref[:, cs].astype(F32)
        a_ref[:, cs] = (on * (og * jax.nn.sigmoid(og))).astype(BF16)
    yb = jnp.dot(a_ref[...], wg_ref[...], preferred_element_type=F32)
    m = jax.nn.sigmoid(g0_ref[...].astype(F32)) * ya + jax.nn.sigmoid(g1_ref[...].astype(F32)) * yb
    o_ref[...] = m.astype(BF16)


def _merge(fft, o_gla, proj, og_blk, g0_blk, g1_blk, hn, wf, wg, tm=256):
    T, D = o_gla.shape
    FW = wf.shape[0]
    dv = D // GLA_HEADS
    const = dict(pipeline_mode=pl.Buffered(1))
    return pl.pallas_call(
        functools.partial(_merge_kernel, dv=dv),
        grid=(T // tm,),
        in_specs=[
            pl.BlockSpec((tm, FW // 2), lambda i: (i, 0)),
            pl.BlockSpec((tm, D), lambda i: (i, 0)),
            pl.BlockSpec((tm, D), lambda i: (i, og_blk)),
            pl.BlockSpec((tm, D), lambda i: (i, g0_blk)),
            pl.BlockSpec((tm, D), lambda i: (i, g1_blk)),
            pl.BlockSpec((1, dv), lambda i: (0, 0)),
            pl.BlockSpec((FW, D), lambda i: (0, 0), **const),
            pl.BlockSpec((D, D), lambda i: (0, 0), **const),
        ],
        out_specs=pl.BlockSpec((tm, D), lambda i: (i, 0)),
        out_shape=jax.ShapeDtypeStruct((T, D), BF16),
        scratch_shapes=[pltpu.VMEM((tm, D), BF16)],
        compiler_params=_cparams(("parallel",)),
        name="merge",
    )(fft, o_gla, proj, proj, proj, hn.reshape(1, dv), wf, wg)


HI16 = 0xFFFF0000


def _pack_bf16_pair(lo, hi):
    lo_bits = lax.bitcast_convert_type(lo.astype(BF16).astype(F32), jnp.uint32)
    hi_bits = lax.bitcast_convert_type(hi.astype(BF16).astype(F32), jnp.uint32)
    return (hi_bits & jnp.uint32(HI16)) | (lo_bits >> 16)


def _unpack_bf16_pair(w):
    lo = lax.bitcast_convert_type(w << 16, F32)
    hi = lax.bitcast_convert_type(w & jnp.uint32(HI16), F32)
    return lo, hi


def _store_tile_rows(ref, val):
    tm = val.shape[0]
    for s in range(SUBLANES):
        ref[pl.ds(s, tm, stride=SUBLANES), :] = val[:, s * LANES:(s + 1) * LANES]


def _load_tile_rows(ref, tm, s):
    return ref[pl.ds(s, tm, stride=SUBLANES), :]


def _outproj_router_kernel(m_ref, xp_ref, xs_ref, wo_ref, g_ref, wr_ref, br_ref,
                           x1_ref, xn_ref, idx_ref, tw_ref, rank_ref, cnt_ref, run_ref, *, n_exp, n0):
    i = pl.program_id(0)

    @pl.when(i == 0)
    def _():
        run_ref[...] = jnp.zeros_like(run_ref)

    tm, D = m_ref.shape
    x = jnp.where(i < n0, xp_ref[...], xs_ref[...])
    x1 = x + jnp.dot(m_ref[...], wo_ref[...], preferred_element_type=F32)
    x1_ref[...] = x1
    var = jnp.mean(x1 * x1, axis=-1, keepdims=True)
    xn = x1 * lax.rsqrt(var + EPS) * g_ref[...]
    _store_tile_rows(xn_ref, _pack_bf16_pair(xn[:, :D // 2], xn[:, D // 2:]))
    xh = xn.astype(BF16)
    xl = (xn - xh.astype(F32)).astype(BF16)
    r = jnp.dot(xh, wr_ref[...], preferred_element_type=F32)
    lg = (r[:, :n_exp] + r[:, n_exp:] + jnp.dot(xl, wr_ref[:, :n_exp], preferred_element_type=F32)
          + br_ref[...])

    lane = lax.broadcasted_iota(jnp.int32, (tm, n_exp), 1)
    vals, hots = [], []
    for _ in range(TOP_K):
        mx = jnp.max(lg, axis=-1, keepdims=True)
        ik = jnp.min(jnp.where(lg == mx, lane, n_exp), axis=-1, keepdims=True)
        hot = lane == ik
        vals.append(mx)
        hots.append(hot)
        lg = jnp.where(hot, -jnp.inf, lg)
    exps = [jnp.exp(v - vals[0]) for v in vals]
    denom = exps[0] + exps[1] + exps[2] + exps[3]

    sel = hots[0] | hots[1] | hots[2] | hots[3]
    sel_f = sel.astype(F32)
    r = lax.broadcasted_iota(jnp.int32, (tm, tm), 0)
    c = lax.broadcasted_iota(jnp.int32, (tm, tm), 1)
    strict = (c < r).astype(BF16)
    before = jnp.dot(strict, sel_f.astype(BF16), preferred_element_type=F32) + run_ref[...]
    run_ref[...] += jnp.sum(sel_f, axis=0, keepdims=True)
    cnt_ref[...] = run_ref[...].astype(jnp.int32)

    k4 = lax.broadcasted_iota(jnp.int32, (tm, TOP_K), 1)
    idx4 = jnp.zeros((tm, TOP_K), jnp.int32)
    w4 = jnp.zeros((tm, TOP_K), F32)
    rk4 = jnp.zeros((tm, TOP_K), jnp.int32)
    for k in range(TOP_K):
        ik = jnp.sum(jnp.where(hots[k], lane, 0), axis=-1, keepdims=True)
        rk = jnp.sum(jnp.where(hots[k], before, 0.0), axis=-1, keepdims=True).astype(jnp.int32)
        idx4 = jnp.where(k4 == k, ik, idx4)
        w4 = jnp.where(k4 == k, exps[k] / denom, w4)
        rk4 = jnp.where(k4 == k, rk, rk4)
    idx_ref[...] = idx4
    tw_ref[...] = w4
    rank_ref[...] = rk4


def _outproj_router(merged, xp, xs, wo, gain, wr2, br, tm=256):
    T, D = merged.shape
    E = wr2.shape[1] // 2
    n0 = xp.shape[0] // tm
    const = dict(pipeline_mode=pl.Buffered(1))
    row = lambda i: (i, 0)
    fix = lambda i: (0, 0)
    return pl.pallas_call(
        functools.partial(_outproj_router_kernel, n_exp=E, n0=n0),
        grid=(T // tm,),
        in_specs=[
            pl.BlockSpec((tm, D), row),
            pl.BlockSpec((tm, D), lambda i: (jnp.minimum(i, n0 - 1), 0)),
            pl.BlockSpec((tm, D), lambda i: (jnp.maximum(i - n0, 0), 0)),
            pl.BlockSpec((D, D), fix, **const),
            pl.BlockSpec((1, D), fix),
            pl.BlockSpec((D, 2 * E), fix),
            pl.BlockSpec((1, E), fix),
        ],
        out_specs=[
            pl.BlockSpec((tm, D), row),
            pl.BlockSpec((tm * SUBLANES, D // 2 // SUBLANES), row),
            pl.BlockSpec((tm, TOP_K), row),
            pl.BlockSpec((tm, TOP_K), row),
            pl.BlockSpec((tm, TOP_K), row),
            pl.BlockSpec((1, E), fix),
        ],
        out_shape=[
            jax.ShapeDtypeStruct((T, D), F32),
            jax.ShapeDtypeStruct((T * SUBLANES, D // 2 // SUBLANES), jnp.uint32),
            jax.ShapeDtypeStruct((T, TOP_K), jnp.int32),
            jax.ShapeDtypeStruct((T, TOP_K), F32),
            jax.ShapeDtypeStruct((T, TOP_K), jnp.int32),
            jax.ShapeDtypeStruct((1, E), jnp.int32),
        ],
        scratch_shapes=[pltpu.VMEM((1, E), F32)],
        compiler_params=_cparams(("arbitrary",)),
        name="outproj_router",
    )(merged, xp, xs, wo, gain.reshape(1, D), wr2, br.reshape(1, E))


def _dispatch_kernel(pos_ref, x_ref, xs_ref, dst_ref, sem, *, n_tok):
    i = pl.program_id(0)
    tm = x_ref.shape[0] // SUBLANES

    def body(t, carry):
        for k in range(TOP_K):
            p = pos_ref[t * TOP_K + k]
            dst_ref[p] = k * n_tok + i * tm + t
            pltpu.make_async_copy(x_ref.at[pl.ds(pl.multiple_of(t * SUBLANES, SUBLANES), SUBLANES), :],
                                  xs_ref.at[pl.ds(pl.multiple_of(p * SUBLANES, SUBLANES), SUBLANES), :], sem).start()
        return carry

    lax.fori_loop(0, tm, body, 0)
    for _ in range(TOP_K):
        pltpu.make_async_copy(x_ref, xs_ref.at[pl.ds(0, tm * SUBLANES), :], sem).wait()


def _dispatch(xn, pos_flat, n_slots, tm=256):
    T = xn.shape[0] // SUBLANES
    return pl.pallas_call(
        functools.partial(_dispatch_kernel, n_tok=T),
        grid=(T // tm,),
        in_specs=[
            pl.BlockSpec((tm * TOP_K,), lambda i: (i,), memory_space=pltpu.SMEM),
            pl.BlockSpec((tm * SUBLANES, LANES), lambda i: (i, 0)),
        ],
        out_specs=[
            pl.BlockSpec(memory_space=pl.ANY),
            pl.BlockSpec((n_slots,), lambda i: (0,), memory_space=pltpu.SMEM),
        ],
        out_shape=[jax.ShapeDtypeStruct((n_slots * SUBLANES, LANES), xn.dtype),
                   jax.ShapeDtypeStruct((n_slots,), jnp.int32)],
        scratch_shapes=[pltpu.SemaphoreType.DMA(())],
        compiler_params=_cparams(("arbitrary",)),
        name="dispatch",
    )(pos_flat, xn)


def _combine_kernel(tw_ref, x1_ref, g_ref, *rest, n0):
    y_refs, (op_ref, os_ref) = rest[:TOP_K], rest[TOP_K:]
    i = pl.program_id(0)
    tw = tw_ref[...]
    tm = x1_ref.shape[0]
    lo_acc = [None] * SUBLANES
    hi_acc = [None] * SUBLANES
    for k in range(TOP_K):
        wk = tw[:, k:k + 1]
        for s in range(SUBLANES):
            lo, hi = _unpack_bf16_pair(_load_tile_rows(y_refs[k], tm, s))
            lo_acc[s] = wk * lo if k == 0 else lo_acc[s] + wk * lo
            hi_acc[s] = wk * hi if k == 0 else hi_acc[s] + wk * hi
    x2 = x1_ref[...] + jnp.concatenate(lo_acc + hi_acc, axis=1)
    var = jnp.mean(x2 * x2, axis=-1, keepdims=True)
    y = x2 * lax.rsqrt(var + EPS) * g_ref[...]

    @pl.when(i < n0)
    def _():
        op_ref[...] = y

    @pl.when(i >= n0)
    def _():
        os_ref[...] = y


def _combine(tw, x1, gain, yk, t_prompt, tm=256):
    T, D = x1.shape
    n0 = t_prompt // tm
    nblk = T // tm
    y_spec = lambda k: pl.BlockSpec((tm * SUBLANES, LANES), lambda i: (k * nblk + i, 0))
    return pl.pallas_call(
        functools.partial(_combine_kernel, n0=n0),
        grid=(T // tm,),
        in_specs=[
            pl.BlockSpec((tm, TOP_K), lambda i: (i, 0)),
            pl.BlockSpec((tm, D), lambda i: (i, 0)),
            pl.BlockSpec((1, D), lambda i: (0, 0)),
        ] + [y_spec(k) for k in range(TOP_K)],
        out_specs=[
            pl.BlockSpec((tm, D), lambda i: (jnp.minimum(i, n0 - 1), 0)),
            pl.BlockSpec((tm, D), lambda i: (jnp.maximum(i - n0, 0), 0)),
        ],
        out_shape=[jax.ShapeDtypeStruct((t_prompt, D), F32), jax.ShapeDtypeStruct((T - t_prompt, D), F32)],
        compiler_params=_cparams(("arbitrary",)),
        name="combine",
    )(tw, x1, gain.reshape(1, D), *([yk] * TOP_K))


def _gate_up_kernel(g0_ref, nt_ref, cnt_ref, w_ref, bg_ref, bu_ref, x_hbm, o_hbm,
                    wp_ref, xin_ref, xb_ref, obuf_ref, sin, sout, *, tn):
    j, e = pl.program_id(0), pl.program_id(1)
    tm = GROUP_TILE
    D = xb_ref.shape[1]
    half = MXU_DIM // 2
    n = nt_ref[e]
    row_base = g0_ref[e]
    cnt = cnt_ref[e]

    def in_copy(r, slot):
        rows = pl.ds(pl.multiple_of((row_base + r * tm) * SUBLANES, tm * SUBLANES), tm * SUBLANES)
        return pltpu.make_async_copy(x_hbm.at[rows, :], xin_ref.at[slot], sin.at[slot])

    def out_copy(r, slot):
        rows = pl.ds(pl.multiple_of(row_base + r * tm, tm), tm)
        cols = pl.ds(pl.multiple_of(j * (tn // 2), LANES), tn // 2)
        return pltpu.make_async_copy(obuf_ref.at[slot], o_hbm.at[rows, cols], sout.at[slot])

    def body(r, carry):
        slot = r % 2

        @pl.when(r + 1 < n)
        def _():
            in_copy(r + 1, 1 - slot).start()

        in_copy(r, slot).wait()

        @pl.when(r >= 2)
        def _():
            out_copy(r - 2, slot).wait()

        valid = (r * tm + lax.broadcasted_iota(jnp.int32, (tm, 1), 0)) < cnt
        for s in range(SUBLANES):
            lo, hi = _unpack_bf16_pair(_load_tile_rows(xin_ref.at[slot], tm, s))
            xb_ref[:, s * LANES:(s + 1) * LANES] = jnp.where(valid, lo, 0.0).astype(BF16)
            xb_ref[:, D // 2 + s * LANES:D // 2 + (s + 1) * LANES] = jnp.where(valid, hi, 0.0).astype(BF16)
        for cb in range(tn // MXU_DIM):
            h = jnp.dot(xb_ref[...], wp_ref[:, cb * MXU_DIM:(cb + 1) * MXU_DIM], preferred_element_type=F32)
            hg = h[:, :half] + bg_ref[0, :, cb * half:(cb + 1) * half]
            hu = h[:, half:] + bu_ref[0, :, cb * half:(cb + 1) * half]
            gate = jnp.minimum(hg, SWIGLU_LIMIT)
            up = jnp.clip(hu, -SWIGLU_LIMIT, SWIGLU_LIMIT)
            act = gate * jax.nn.sigmoid(SWIGLU_ALPHA * gate) * (up + 1.0)
            obuf_ref[slot, :, cb * half:(cb + 1) * half] = act.astype(obuf_ref.dtype)
        out_copy(r, slot).start()
        return carry

    @pl.when(n > 0)
    def _():
        in_copy(0, 0).start()
        r_i = lax.broadcasted_iota(jnp.int32, (MXU_DIM, MXU_DIM), 0)
        c_i = lax.broadcasted_iota(jnp.int32, (MXU_DIM, MXU_DIM), 1)
        perm = (((c_i < half) & (r_i == 2 * c_i)) | ((c_i >= half) & (r_i == 2 * (c_i - half) + 1))).astype(BF16)
        for cb in range(tn // MXU_DIM):
            for rb in range(D // 512):
                w = w_ref[0, rb * 512:(rb + 1) * 512, cb * MXU_DIM:(cb + 1) * MXU_DIM].astype(BF16)
                wp_ref[rb * 512:(rb + 1) * 512, cb * MXU_DIM:(cb + 1) * MXU_DIM] = jnp.dot(
                    w, perm, preferred_element_type=F32).astype(BF16)
        lax.fori_loop(0, n, body, 0)

        @pl.when(n >= 2)
        def _():
            out_copy(n - 2, n % 2).wait()

        out_copy(n - 1, (n - 1) % 2).wait()


def _gate_up(g0, ntiles, cnt, xs, w_gate_up, bg, bu, tn):
    P = xs.shape[0] // SUBLANES
    E, D, H2 = w_gate_up.shape
    tm = GROUP_TILE
    wmap = lambda j, e, *_: (e, 0, j)
    gs = pltpu.PrefetchScalarGridSpec(
        num_scalar_prefetch=3,
        grid=(H2 // tn, E),
        in_specs=[
            pl.BlockSpec((1, D, tn), wmap),
            pl.BlockSpec((1, 1, tn // 2), wmap),
            pl.BlockSpec((1, 1, tn // 2), wmap),
            pl.BlockSpec(memory_space=pl.ANY),
        ],
        out_specs=pl.BlockSpec(memory_space=pl.ANY),
        scratch_shapes=[
            pltpu.VMEM((D, tn), BF16),
            pltpu.VMEM((2, tm * SUBLANES, LANES), jnp.uint32),
            pltpu.VMEM((tm, D), BF16),
            pltpu.VMEM((2, tm, tn // 2), BF16),
            pltpu.SemaphoreType.DMA((2,)),
            pltpu.SemaphoreType.DMA((2,)),
        ],
    )
    return pl.pallas_call(
        functools.partial(_gate_up_kernel, tn=tn),
        grid_spec=gs,
        out_shape=jax.ShapeDtypeStruct((P, H2 // 2), BF16),
        compiler_params=_cparams(("arbitrary", "arbitrary")),
        name="moe_gate_up",
    )(g0, ntiles, cnt, w_gate_up, bg, bu, xs)


def _down_kernel(g0_ref, nt_ref, cnt_ref, dst_ref, w_ref, b_ref, a_hbm, y_hbm, wb_ref, ain_ref, obuf_ref, sin, sout):
    e = pl.program_id(0)
    tm = GROUP_TILE
    H, D = wb_ref.shape
    n = nt_ref[e]
    row_base = g0_ref[e]
    cnt = cnt_ref[e]
    n_real = y_hbm.shape[0] // SUBLANES - dst_ref.shape[0]

    def in_copy(r, slot):
        rows = pl.ds(pl.multiple_of(row_base + r * tm, tm), tm)
        return pltpu.make_async_copy(a_hbm.at[rows, :], ain_ref.at[slot], sin.at[slot])

    nblk = (D // 2) // MXU_DIM

    def compute(slot, out_ref, before_block=None):
        a = ain_ref[slot]
        for bi, c0 in enumerate(range(0, D // 2, MXU_DIM)):
            if before_block is not None:
                before_block(bi)
            c1 = D // 2 + c0
            lo = jnp.dot(a, wb_ref[:, c0:c0 + MXU_DIM], preferred_element_type=F32) + b_ref[0, :, c0:c0 + MXU_DIM]
            hi = jnp.dot(a, wb_ref[:, c1:c1 + MXU_DIM], preferred_element_type=F32) + b_ref[0, :, c1:c1 + MXU_DIM]
            packed = _pack_bf16_pair(lo, hi)
            for u in range(MXU_DIM // LANES):
                out_ref[pl.ds(c0 // LANES + u, tm, stride=SUBLANES), :] = packed[:, u * LANES:(u + 1) * LANES]

    def scatter_rows(q, so, part):
        base = row_base + q * tm
        for i in range(part * (tm // nblk), (part + 1) * (tm // nblk)):
            d = jnp.where(q * tm + i < cnt, dst_ref[base + i], n_real + base + i)
            pltpu.make_async_copy(obuf_ref.at[so, pl.ds(i * SUBLANES, SUBLANES), :],
                                  y_hbm.at[pl.ds(pl.multiple_of(d * SUBLANES, SUBLANES), SUBLANES), :],
                                  sout.at[so]).start()

    def scatter_tile(q, so):
        for part in range(nblk):
            scatter_rows(q, so, part)

    def wait_scatter(so):
        pltpu.make_async_copy(obuf_ref.at[so], y_hbm.at[pl.ds(0, tm * SUBLANES), :], sout.at[so]).wait()

    def body(r, carry):
        slot = r % 2

        @pl.when(r + 1 < n)
        def _():
            in_copy(r + 1, 1 - slot).start()

        in_copy(r, slot).wait()

        @pl.when(r >= 2)
        def _():
            wait_scatter(slot)

        compute(slot, obuf_ref.at[slot], functools.partial(scatter_rows, r - 1, 1 - slot))
        return carry

    @pl.when(n > 0)
    def _():
        in_copy(0, 0).start()
        for rb in range(H // 512):
            wb_ref[rb * 512:(rb + 1) * 512, :] = w_ref[0, rb * 512:(rb + 1) * 512, :].astype(BF16)

        @pl.when(n > 1)
        def _():
            in_copy(1, 1).start()

        in_copy(0, 0).wait()
        compute(0, obuf_ref.at[0])
        lax.fori_loop(1, n, body, 0)
        so = (n - 1) % 2
        scatter_tile(n - 1, so)
        wait_scatter(so)

        @pl.when(n >= 2)
        def _():
            wait_scatter(1 - so)


def _down(g0, ntiles, cnt, slot_dst, n_rows, act, wd, bd):
    P, H = act.shape
    E, _, D = wd.shape
    tm = GROUP_TILE
    wmap = lambda e, *_: (e, 0, 0)
    gs = pltpu.PrefetchScalarGridSpec(
        num_scalar_prefetch=4,
        grid=(E,),
        in_specs=[
            pl.BlockSpec((1, H, D), wmap),
            pl.BlockSpec((1, 1, D), wmap),
            pl.BlockSpec(memory_space=pl.ANY),
        ],
        out_specs=pl.BlockSpec(memory_space=pl.ANY),
        scratch_shapes=[
            pltpu.VMEM((H, D), BF16),
            pltpu.VMEM((2, tm, H), BF16),
            pltpu.VMEM((2, tm * SUBLANES, LANES), jnp.uint32),
            pltpu.SemaphoreType.DMA((2,)),
            pltpu.SemaphoreType.DMA((2,)),
        ],
    )
    return pl.pallas_call(
        _down_kernel,
        grid_spec=gs,
        out_shape=jax.ShapeDtypeStruct((n_rows * SUBLANES, LANES), jnp.uint32),
        compiler_params=_cparams(("arbitrary",)),
        name="moe_down",
    )(g0, ntiles, cnt, slot_dst, wd, bd, act)


def _trunk(xp, xs, seq_shapes, norm_mix, w_in, w_gk_up_fwd, b_gk_fwd, w_gk_up_bwd, b_gk_bwd, gla_head_norm,
           w_fnet_out, w_gla_out, w_out, norm_ffn, w_router, b_router, w_gate_up, b_gate_up,
           w_down, b_down, norm_final):
    D = xp.shape[1]
    T = xp.shape[0] + xs.shape[0]
    fw = w_fnet_out.shape[0]
    dkk = w_gk_up_fwd.shape[1]
    dvv = w_gla_out.shape[0]
    dk, dv = dkk // GLA_HEADS, dvv // GLA_HEADS
    sizes = (fw, dkk, dkk, dvv, dvv, GATE_LOW_RANK, GATE_LOW_RANK, 2 * D)
    offs = np.concatenate([[0], np.cumsum(sizes)])
    sl = lambda n: slice(int(offs[n]), int(offs[n + 1]))
    w_u, w_q, w_k, w_v, w_og = (w_in[:, sl(n)] for n in range(5))
    w_lr = w_in[:, int(offs[5]):int(offs[7])]
    w_g = w_in[:, sl(7)]
    w_main = jnp.concatenate([w_og, w_g, w_v, w_u, w_q, w_k], axis=1).astype(BF16)
    og_blk, g0_blk, g1_blk = 0, dvv // D, dvv // D + 1
    v_off = dvv + 2 * D
    u_off = v_off + dvv
    q_off = u_off + fw
    k_off = q_off + dkk

    proj, lr = _inproj(xp, xs, norm_mix, w_main, _hi_lo(w_lr))
    lr_f, lr_b = lr[:, :GATE_LOW_RANK], lr[:, GATE_LOW_RANK:]

    gd = fw // FNET_GROUPS
    cc, sc = _dft_mats(gd, gd ** -0.5)
    cs = jnp.concatenate([cc, sc], axis=1).astype(BF16)
    z = _chan_dft(proj, u_off // fw, fw, cs)
    fft, o_gla = None, None
    row0 = 0
    for (B, S) in seq_shapes:
        fft = _seq_dft(z, row0, B, S, prev=fft)
        o_gla = _gla(proj, lr_f, lr_b, w_gk_up_fwd, b_gk_fwd, w_gk_up_bwd, b_gk_bwd, row0, B, S,
                     q_off // dk, k_off // dk, v_off // dv, dk, dv, prev=o_gla)
        row0 += B * S

    merged = _merge(fft, o_gla, proj, og_blk, g0_blk, g1_blk, gla_head_norm,
                    w_fnet_out.astype(BF16), w_gla_out.astype(BF16))
    x1, xn2, idx, tw, rank, cnt = _outproj_router(merged, xp, xs, w_out.astype(BF16), norm_ffn,
                                                  _hi_lo(w_router), b_router)

    E = w_router.shape[1]
    cnt = cnt.reshape(E)
    gsz = ((cnt + GROUP_TILE - 1) // GROUP_TILE) * GROUP_TILE
    gend = jnp.cumsum(gsz)
    gstart = gend - gsz
    pos = (gstart[idx] + rank).reshape(-1).astype(jnp.int32)
    n_slots = T * TOP_K + E * GROUP_TILE
    g0 = gstart.astype(jnp.int32)
    ntiles = (gsz // GROUP_TILE).astype(jnp.int32)

    x_sorted, slot_dst = _dispatch(xn2, pos, n_slots)
    H = w_down.shape[1]
    bg = b_gate_up[:, 0::2].reshape(E, 1, H)
    bu = b_gate_up[:, 1::2].reshape(E, 1, H)
    act = _gate_up(g0, ntiles, cnt, x_sorted, w_gate_up, bg, bu, GATE_UP_TN)
    yk = _down(g0, ntiles, cnt, slot_dst, T * TOP_K + n_slots, act, w_down, b_down.reshape(E, 1, D))
    return _combine(tw, x1, norm_final, yk, xp.shape[0])


def kernel(x_prompt, x_sample, norm_mix, w_in, w_gk_up_fwd, b_gk_fwd, w_gk_up_bwd, b_gk_bwd, gla_head_norm,
           w_fnet_out, w_gla_out, w_out, norm_ffn, w_router, b_router, w_gate_up, b_gate_up, w_down,
           b_down, norm_final):
    D = x_prompt.shape[-1]
    shapes = (x_prompt.shape[:2], x_sample.shape[:2])
    yp, ys = _trunk(x_prompt.reshape(-1, D), x_sample.reshape(-1, D), shapes, norm_mix[0], w_in[0], w_gk_up_fwd[0], b_gk_fwd[0], w_gk_up_bwd[0], b_gk_bwd[0],
               gla_head_norm[0], w_fnet_out[0], w_gla_out[0], w_out[0], norm_ffn[0], w_router[0],
               b_router[0], w_gate_up[0], b_gate_up[0], w_down[0], b_down[0], norm_final)
    return (yp.reshape(x_prompt.shape), ys.reshape(x_sample.shape))
```

```python
import functools
import math

import numpy as np
import jax
import jax.numpy as jnp
from jax import lax
from jax.experimental import pallas as pl
from jax.experimental.pallas import tpu as pltpu

F32 = jnp.float32
BF16 = jnp.bfloat16
HIGHEST = lax.Precision.HIGHEST

EPS = 1e-5
FNET_GROUPS = 4
GLA_HEADS = 4
GATE_LOW_RANK = 16
GATE_LOGIT_NORMALIZER = 16.0
CHUNK = 64
TOP_K = 4
SWIGLU_LIMIT = 7.0
SWIGLU_ALPHA = 1.702

VMEM_LIMIT_BYTES = 56 * 1024 * 1024
MXU_DIM = 256
LANES = 128
SUBLANES = 8
GROUP_TILE = 256
GATE_UP_TN = 2048


def _cparams(sem):
    return pltpu.CompilerParams(dimension_semantics=sem, vmem_limit_bytes=VMEM_LIMIT_BYTES)


def _split3(x):
    hi = x.astype(BF16)
    r = x - hi.astype(F32)
    mid = r.astype(BF16)
    lo = (r - mid.astype(F32)).astype(BF16)
    return hi, mid, lo


def _hi_lo(w):
    hi = w.astype(BF16)
    lo = (w - hi.astype(F32)).astype(BF16)
    return jnp.concatenate([hi, lo], axis=1)


def _repack_kernel(w_ref, o_ref, lr_ref, *, pieces, lr_cols):
    c0 = 0
    for a, b in pieces:
        o_ref[:, c0:c0 + (b - a)] = w_ref[:, a:b].astype(BF16)
        c0 += b - a
    w = w_ref[:, lr_cols[0]:lr_cols[1]]
    hi = w.astype(BF16)
    n = lr_cols[1] - lr_cols[0]
    lr_ref[:, :n] = hi
    lr_ref[:, n:] = (w - hi.astype(F32)).astype(BF16)


def _repack_w_in(w_in, pieces, lr_cols, tr=256):
    K, N = w_in.shape
    n_main = sum(b - a for a, b in pieces)
    n_lr = lr_cols[1] - lr_cols[0]
    return pl.pallas_call(
        functools.partial(_repack_kernel, pieces=pieces, lr_cols=lr_cols),
        grid=(K // tr,),
        in_specs=[pl.BlockSpec((tr, N), lambda i: (i, 0))],
        out_specs=[pl.BlockSpec((tr, n_main), lambda i: (i, 0)), pl.BlockSpec((tr, 2 * n_lr), lambda i: (i, 0))],
        out_shape=[jax.ShapeDtypeStruct((K, n_main), BF16), jax.ShapeDtypeStruct((K, 2 * n_lr), BF16)],
        compiler_params=_cparams(("parallel",)),
        name="repack_w_in",
    )(w_in)


def _inproj_kernel(xp_ref, xs_ref, g_ref, w_ref, wlr_ref, o_ref, lr_ref, xn_ref, *, n0):
    @pl.when(pl.program_id(1) == 0)
    def _():
        x = jnp.where(pl.program_id(0) < n0, xp_ref[...], xs_ref[...])
        var = jnp.mean(x * x, axis=-1, keepdims=True)
        xn = (x * lax.rsqrt(var + EPS) * g_ref[...]).astype(BF16)
        xn_ref[...] = xn
        r = jnp.dot(xn, wlr_ref[...], preferred_element_type=F32)
        nlr = lr_ref.shape[1]
        lr_ref[...] = r[:, :nlr] + r[:, nlr:]

    o_ref[...] = jnp.dot(xn_ref[...], w_ref[...], preferred_element_type=F32).astype(o_ref.dtype)


def _inproj(xp, xs, gain, w_main, w_lr2, tm=512, tn=1024):
    D = xp.shape[1]
    T = xp.shape[0] + xs.shape[0]
    n0 = xp.shape[0] // tm
    N = w_main.shape[1]
    R = w_lr2.shape[1] // 2
    return pl.pallas_call(
        functools.partial(_inproj_kernel, n0=n0),
        grid=(T // tm, N // tn),
        in_specs=[
            pl.BlockSpec((tm, D), lambda i, j: (jnp.minimum(i, n0 - 1), 0)),
            pl.BlockSpec((tm, D), lambda i, j: (jnp.maximum(i - n0, 0), 0)),
            pl.BlockSpec((1, D), lambda i, j: (0, 0)),
            pl.BlockSpec((D, tn), lambda i, j: (0, j)),
            pl.BlockSpec((D, 2 * R), lambda i, j: (0, 0)),
        ],
        out_specs=[
            pl.BlockSpec((tm, tn), lambda i, j: (i, j)),
            pl.BlockSpec((tm, R), lambda i, j: (i, 0)),
        ],
        out_shape=[jax.ShapeDtypeStruct((T, N), BF16), jax.ShapeDtypeStruct((T, R), F32)],
        scratch_shapes=[pltpu.VMEM((tm, D), BF16)],
        compiler_params=_cparams(("parallel", "arbitrary")),
        name="inproj",
    )(xp, xs, gain.reshape(1, D), w_main, w_lr2)


def _chan_dft_kernel(u_ref, cs_ref, z_ref, *, gd):
    for g in range(FNET_GROUPS):
        r = jnp.dot(u_ref[:, g * gd:(g + 1) * gd], cs_ref[...], preferred_element_type=F32)
        z_ref[:, g * gd:(g + 1) * gd] = _pack_bf16_pair(r[:, :gd], r[:, gd:])


def _chan_dft(proj, u_col_block, width, cs, tm=512):
    T = proj.shape[0]
    gd = width // FNET_GROUPS
    return pl.pallas_call(
        functools.partial(_chan_dft_kernel, gd=gd),
        grid=(T // tm,),
        in_specs=[
            pl.BlockSpec((tm, width), lambda i: (i, u_col_block)),
            pl.BlockSpec((gd, 2 * gd), lambda i: (0, 0)),
        ],
        out_specs=pl.BlockSpec((tm, width), lambda i: (i, 0)),
        out_shape=jax.ShapeDtypeStruct((T, width), jnp.uint32),
        compiler_params=_cparams(("parallel",)),
        name="chan_dft",
    )(proj, cs)


_FFT_COLS = 2 * LANES


def _fft_stage1_kernel(z_ref, f1_ref, ct_ref, st_ref, a_ref, *, n1, n2):
    for m in range(n2):
        rows = pl.ds(m, n1, stride=n2)
        zc, zs = _unpack_bf16_pair(z_ref[rows, :])
        pc = jnp.dot(f1_ref[...], zc.astype(BF16), preferred_element_type=F32)
        ps = jnp.dot(f1_ref[...], zs.astype(BF16), preferred_element_type=F32)
        a_re = pc[:n1] - ps[n1:]
        a_im = -ps[:n1] - pc[n1:]
        ct = ct_ref[m][:, 0:1]
        st = st_ref[m][:, 0:1]
        a_ref[rows, :] = _pack_bf16_pair(a_re * ct + a_im * st, a_im * ct - a_re * st)


def _fft_stage2_kernel(a_ref, f2_ref, *rest, n1, n2):
    o_ref = rest[-1]
    half = _FFT_COLS // 2
    for k1 in range(n1):
        a_re, a_im = _unpack_bf16_pair(a_ref[pl.ds(k1 * n2, n2), :])
        rhs = jnp.concatenate([a_re.astype(BF16), a_im.astype(BF16)], axis=0)
        x = jnp.dot(f2_ref[...], rhs, preferred_element_type=F32)
        o_ref[pl.ds(k1, n2, stride=n1), :] = _pack_bf16_pair(x[:, :half], x[:, half:])


def _seq_dft(z, row0, B, S, prev=None):
    T, W = z.shape
    wc = _FFT_COLS
    n2 = 64 if S % (64 * SUBLANES) == 0 else S // SUBLANES
    n1 = S // n2
    assert n1 * n2 == S and row0 % S == 0 and W % wc == 0
    s0 = row0 // S
    i1 = jnp.arange(n1, dtype=jnp.int32)
    i2 = jnp.arange(n2, dtype=jnp.int32)
    ang1 = ((i1[:, None] * i1[None, :]) % n1).astype(F32) * (2.0 * math.pi / n1)
    f1 = (jnp.concatenate([jnp.cos(ang1), jnp.sin(ang1)], axis=0) * S ** -0.5).astype(BF16)
    ang2 = ((i2[:, None] * i2[None, :]) % n2).astype(F32) * (2.0 * math.pi / n2)
    f2 = jnp.concatenate([jnp.cos(ang2), jnp.sin(ang2)], axis=1).astype(BF16)
    angt = (i2[:, None] * i1[None, :]).astype(F32) * (2.0 * math.pi / S)
    ct = jnp.broadcast_to(jnp.cos(angt)[:, :, None], (n2, n1, LANES))
    st = jnp.broadcast_to(jnp.sin(angt)[:, :, None], (n2, n1, LANES))

    fix2 = lambda b, c: (0, 0)
    fix3 = lambda b, c: (0, 0, 0)

    a = pl.pallas_call(
        functools.partial(_fft_stage1_kernel, n1=n1, n2=n2),
        grid=(B, W // LANES),
        in_specs=[pl.BlockSpec((S, LANES), lambda b, c: (s0 + b, c)),
                  pl.BlockSpec((2 * n1, n1), fix2),
                  pl.BlockSpec((n2, n1, LANES), fix3),
                  pl.BlockSpec((n2, n1, LANES), fix3)],
        out_specs=pl.BlockSpec((S, LANES), lambda b, c: (b, c)),
        out_shape=jax.ShapeDtypeStruct((B * S, W), jnp.uint32),
        compiler_params=_cparams(("parallel", "parallel")),
        name="fft_stage1",
    )(z, f1, ct, st)

    in_specs = [pl.BlockSpec((S, wc), lambda b, c: (b, c)), pl.BlockSpec((n2, 2 * n2), fix2)]
    args = [a, f2]
    aliases = {}
    if prev is not None:
        in_specs.append(pl.BlockSpec(memory_space=pl.ANY))
        args.append(prev)
        aliases = {2: 0}
    return pl.pallas_call(
        functools.partial(_fft_stage2_kernel, n1=n1, n2=n2),
        grid=(B, W // wc),
        in_specs=in_specs,
        out_specs=pl.BlockSpec((S, wc // 2), lambda b, c: (s0 + b, c)),
        out_shape=jax.ShapeDtypeStruct((T, W // 2), jnp.uint32),
        input_output_aliases=aliases,
        compiler_params=_cparams(("parallel", "parallel")),
        name="fft_stage2",
    )(*args)


def _dft_mats(n, scale, split=64):
    split = split if n % split == 0 else 1
    k = jnp.arange(n, dtype=jnp.int32)[None, :]
    j1 = jnp.arange(n // split, dtype=jnp.int32)[:, None]
    j2 = jnp.arange(split, dtype=jnp.int32)[:, None]
    w = 2.0 * math.pi / n
    ang_a = ((split * j1 * k) % n).astype(F32) * w
    ang_b = ((j2 * k) % n).astype(F32) * w
    ca, sa = jnp.cos(ang_a)[:, None, :], jnp.sin(ang_a)[:, None, :]
    cb, sb = (jnp.cos(ang_b) * scale)[None, :, :], (jnp.sin(ang_b) * scale)[None, :, :]
    c = (ca * cb - sa * sb).reshape(n, n)
    s = (sa * cb + ca * sb).reshape(n, n)
    return c, s


_NT = (((1,), (1,)), ((), ()))
_TN = (((0,), (0,)), ((), ()))


def _gla_block(q_ref, k_ref, v_ref, lr_ref, w3_ref, b_ref, st_ref, reverse, qscale, nchunk):
    R = nchunk * CHUNK
    lr = lr_ref[...]
    lr_hi = lr.astype(BF16)
    lr_lo = (lr - lr_hi.astype(F32)).astype(BF16)
    z = jnp.dot(jnp.concatenate([lr_hi, lr_lo, lr_hi], axis=1), w3_ref[...],
                preferred_element_type=F32) + b_ref[...]
    g = (jnp.minimum(z, 0.0) - jnp.log(1.0 + jnp.exp(-jnp.abs(z)))) * (1.0 / GATE_LOGIT_NORMALIZER)
    ri = lax.broadcasted_iota(jnp.int32, (R, R), 0)
    ci = lax.broadcasted_iota(jnp.int32, (R, R), 1)
    cum = ((ci >= ri) if reverse else (ci <= ri)).astype(BF16)
    g_hi = g.astype(BF16)
    g_lo = (g - g_hi.astype(F32)).astype(BF16)
    G = jnp.dot(cum, g_hi, preferred_element_type=F32) + jnp.dot(cum, g_lo, preferred_element_type=F32)

    dk = G.shape[1]
    zero_row = jnp.zeros((1, dk), F32)
    if reverse:
        starts = [G[(c + 1) * CHUNK:(c + 1) * CHUNK + 1, :] if c + 1 < nchunk else zero_row for c in range(nchunk)]
        ref_row, g_tot = CHUNK // 2, G[0:1, :]
    else:
        starts = [G[c * CHUNK - 1:c * CHUNK, :] if c > 0 else zero_row for c in range(nchunk)]
        ref_row, g_tot = CHUNK // 2 - 1, G[R - 1:R, :]
    bcast = lambda rows_: jnp.concatenate([jnp.broadcast_to(r_, (CHUNK, dk)) for r_ in rows_], axis=0)
    gc = G - bcast(starts)
    gref = bcast([gc[c * CHUNK + ref_row:c * CHUNK + ref_row + 1, :] for c in range(nchunk)])

    q = q_ref[...].astype(F32) * qscale
    k = k_ref[...].astype(F32)
    v = v_ref[...]
    q_in = (q * jnp.exp(gc - gref)).astype(BF16)
    k_in = (k * jnp.exp(gref - gc)).astype(BF16)
    q_it = (q * jnp.exp(gc)).astype(BF16)
    q_st = (q * jnp.exp(G)).astype(BF16)
    k_st = (k * jnp.exp(g_tot - G)).astype(BF16)

    s_diag = lax.dot_general(q_in, k_in, _NT, preferred_element_type=F32)
    same = (ri // CHUNK) == (ci // CHUNK)
    keep = same & ((ci > ri) if reverse else (ci <= ri))
    s_rows = []
    for c in range(nchunk):
        rows = slice(c * CHUNK, (c + 1) * CHUNK)
        s = jnp.where(keep[rows, :], s_diag[rows, :], 0.0)
        lo_, hi_ = ((c + 1) * CHUNK, R) if reverse else (0, c * CHUNK)
        if hi_ > lo_:
            kx = (k[lo_:hi_, :] * jnp.exp(starts[c] - G[lo_:hi_, :])).astype(BF16)
            pad = jnp.zeros((R - (hi_ - lo_), dk), BF16)
            kx = jnp.concatenate([pad, kx] if reverse else [kx, pad], axis=0)
            s = s + lax.dot_general(q_it[rows, :], kx, _NT, preferred_element_type=F32)
        s_rows.append(s.astype(BF16))
    scores = jnp.concatenate(s_rows, axis=0)

    st = st_ref[...]
    o = (jnp.dot(scores, v, preferred_element_type=F32)
         + lax.dot_general(q_st, st.astype(BF16), _NT, preferred_element_type=F32))
    st_ref[...] = st * jnp.exp(g_tot) + lax.dot_general(v, k_st, _TN, preferred_element_type=F32)
    return o


def _gla_kernel(qf_ref, kf_ref, vf_ref, qb_ref, kb_ref, vb_ref, lrf_ref, lrb_ref,
                wf_ref, bf_ref, wb_ref, bb_ref, *rest, nchunk, nsub, qscale):
    o_ref, stf_ref, stb_ref = rest[-3:]
    n = pl.program_id(2)
    nb = pl.num_programs(2)
    blk = nchunk * CHUNK
    rows = nsub * blk

    @pl.when(n == 0)
    def _():
        o_ref[...] = jnp.zeros_like(o_ref)
        stf_ref[...] = jnp.zeros_like(stf_ref)
        stb_ref[...] = jnp.zeros_like(stb_ref)

    for s_f in range(nsub):
        s_b = nsub - 1 - s_f
        sub_f, sub_b = pl.ds(s_f * blk, blk), pl.ds(s_b * blk, blk)
        o_f = _gla_block(qf_ref.at[sub_f, :], kf_ref.at[sub_f, :], vf_ref.at[sub_f, :], lrf_ref.at[sub_f, :],
                         wf_ref, bf_ref, stf_ref, False, qscale, nchunk)
        o_b = _gla_block(qb_ref.at[sub_b, :], kb_ref.at[sub_b, :], vb_ref.at[sub_b, :], lrb_ref.at[sub_b, :],
                         wb_ref, bb_ref, stb_ref, True, qscale, nchunk)
        o_ref[pl.ds(pl.multiple_of(n * rows + s_f * blk, blk), blk), :] += o_f
        o_ref[pl.ds(pl.multiple_of((nb - 1 - n) * rows + s_b * blk, blk), blk), :] += o_b


def _hi_hi_lo(w):
    hi = w.astype(BF16)
    lo = (w - hi.astype(F32)).astype(BF16)
    return jnp.concatenate([hi, hi, lo], axis=0)


def _gla(proj, lr_f, lr_b, wup_f, b_f, wup_b, b_b, row0, B, S, q_blk0, k_blk0, v_blk0, dk, dv,
         prev=None, blk=256, nsub=4):
    T = proj.shape[0]
    nsub = nsub if S % (nsub * blk) == 0 else 1
    rows = nsub * blk
    assert S % rows == 0 and row0 % S == 0
    nb = S // rows
    rb0, sb0 = row0 // rows, row0 // S
    fmap = lambda b, n: rb0 + b * nb + n
    bmap = lambda b, n: rb0 + b * nb + (nb - 1 - n)
    in_specs = [
        pl.BlockSpec((rows, dk), lambda b, h, n: (fmap(b, n), q_blk0 + h)),
        pl.BlockSpec((rows, dk), lambda b, h, n: (fmap(b, n), k_blk0 + h)),
        pl.BlockSpec((rows, dv), lambda b, h, n: (fmap(b, n), v_blk0 + h)),
        pl.BlockSpec((rows, dk), lambda b, h, n: (bmap(b, n), q_blk0 + h)),
        pl.BlockSpec((rows, dk), lambda b, h, n: (bmap(b, n), k_blk0 + h)),
        pl.BlockSpec((rows, dv), lambda b, h, n: (bmap(b, n), v_blk0 + h)),
        pl.BlockSpec((rows, GATE_LOW_RANK), lambda b, h, n: (fmap(b, n), 0)),
        pl.BlockSpec((rows, GATE_LOW_RANK), lambda b, h, n: (bmap(b, n), 0)),
        pl.BlockSpec((3 * GATE_LOW_RANK, dk), lambda b, h, n: (0, h)),
        pl.BlockSpec((1, dk), lambda b, h, n: (0, h)),
        pl.BlockSpec((3 * GATE_LOW_RANK, dk), lambda b, h, n: (0, h)),
        pl.BlockSpec((1, dk), lambda b, h, n: (0, h)),
    ]
    args = [proj, proj, proj, proj, proj, proj, lr_f, lr_b,
            _hi_hi_lo(wup_f), b_f.reshape(1, -1), _hi_hi_lo(wup_b), b_b.reshape(1, -1)]
    aliases = {}
    if prev is not None:
        in_specs.append(pl.BlockSpec(memory_space=pl.ANY))
        args.append(prev)
        aliases = {len(args) - 1: 0}
    return pl.pallas_call(
        functools.partial(_gla_kernel, nchunk=blk // CHUNK, nsub=nsub, qscale=dk ** -0.5),
        grid=(B, GLA_HEADS, nb),
        in_specs=in_specs,
        out_specs=pl.BlockSpec((S, dv), lambda b, h, n: (sb0 + b, h)),
        out_shape=jax.ShapeDtypeStruct((T, GLA_HEADS * dv), F32),
        scratch_shapes=[pltpu.VMEM((dv, dk), F32), pltpu.VMEM((dv, dk), F32)],
        input_output_aliases=aliases,
        compiler_params=_cparams(("parallel", "parallel", "arbitrary")),
        name="gla",
    )(*args)


def _merge_kernel(fft_ref, o_ref_in, og_ref, g0_ref, g1_ref, hn_ref, wf_ref, wg_ref, o_ref, a_ref, *, dv):
    half = _FFT_COLS // 2
    pieces = []
    for cb in range(fft_ref.shape[1] // half):
        pieces.extend(_unpack_bf16_pair(fft_ref[:, cb * half:(cb + 1) * half]))
    ya = jnp.dot(jnp.concatenate(pieces, axis=1).astype(BF16), wf_ref[...], preferred_element_type=F32)
    for h in range(GLA_HEADS):
        cs = slice(h * dv, (h + 1) * dv)
        o = o_ref_in[:, cs]
        var = jnp.mean(o * o, axis=-1, keepdims=True)
        on = o * lax.rsqrt(var + EPS) * hn_ref[...]
        og = og_ref[:, cs].astype(F32)
        a_ref[:, cs] = (on * (og * jax.nn.sigmoid(og))).astype(BF16)
    yb = jnp.dot(a_ref[...], wg_ref[...], preferred_element_type=F32)
    m = jax.nn.sigmoid(g0_ref[...].astype(F32)) * ya + jax.nn.sigmoid(g1_ref[...].astype(F32)) * yb
    o_ref[...] = m.astype(BF16)


def _merge(fft, o_gla, proj, og_blk, g0_blk, g1_blk, hn, wf, wg, tm=256):
    T, D = o_gla.shape
    FW = wf.shape[0]
    dv = D // GLA_HEADS
    const = dict(pipeline_mode=pl.Buffered(1))
    return pl.pallas_call(
        functools.partial(_merge_kernel, dv=dv),
        grid=(T // tm,),
        in_specs=[
            pl.BlockSpec((tm, FW // 2), lambda i: (i, 0)),
            pl.BlockSpec((tm, D), lambda i: (i, 0)),
            pl.BlockSpec((tm, D), lambda i: (i, og_blk)),
            pl.BlockSpec((tm, D), lambda i: (i, g0_blk)),
            pl.BlockSpec((tm, D), lambda i: (i, g1_blk)),
            pl.BlockSpec((1, dv), lambda i: (0, 0)),
            pl.BlockSpec((FW, D), lambda i: (0, 0), **const),
            pl.BlockSpec((D, D), lambda i: (0, 0), **const),
        ],
        out_specs=pl.BlockSpec((tm, D), lambda i: (i, 0)),
        out_shape=jax.ShapeDtypeStruct((T, D), BF16),
        scratch_shapes=[pltpu.VMEM((tm, D), BF16)],
        compiler_params=_cparams(("parallel",)),
        name="merge",
    )(fft, o_gla, proj, proj, proj, hn.reshape(1, dv), wf, wg)


HI16 = 0xFFFF0000


def _pack_bf16_pair(lo, hi):
    lo_bits = lax.bitcast_convert_type(lo.astype(BF16).astype(F32), jnp.uint32)
    hi_bits = lax.bitcast_convert_type(hi.astype(BF16).astype(F32), jnp.uint32)
    return (hi_bits & jnp.uint32(HI16)) | (lo_bits >> 16)


def _unpack_bf16_pair(w):
    lo = lax.bitcast_convert_type(w << 16, F32)
    hi = lax.bitcast_convert_type(w & jnp.uint32(HI16), F32)
    return lo, hi


def _store_tile_rows(ref, val):
    tm = val.shape[0]
    for s in range(SUBLANES):
        ref[pl.ds(s, tm, stride=SUBLANES), :] = val[:, s * LANES:(s + 1) * LANES]


def _load_tile_rows(ref, tm, s):
    return ref[pl.ds(s, tm, stride=SUBLANES), :]


def _outproj_router_kernel(m_ref, xp_ref, xs_ref, wo_ref, g_ref, wr_ref, br_ref,
                           x1_ref, xn_ref, idx_ref, tw_ref, rank_ref, cnt_ref, run_ref, *, n_exp, n0):
    i = pl.program_id(0)

    @pl.when(i == 0)
    def _():
        run_ref[...] = jnp.zeros_like(run_ref)

    tm, D = m_ref.shape
    x = jnp.where(i < n0, xp_ref[...], xs_ref[...])
    x1 = x + jnp.dot(m_ref[...], wo_ref[...], preferred_element_type=F32)
    x1_ref[...] = x1
    var = jnp.mean(x1 * x1, axis=-1, keepdims=True)
    xn = x1 * lax.rsqrt(var + EPS) * g_ref[...]
    _store_tile_rows(xn_ref, _pack_bf16_pair(xn[:, :D // 2], xn[:, D // 2:]))
    xh = xn.astype(BF16)
    xl = (xn - xh.astype(F32)).astype(BF16)
    r = jnp.dot(xh, wr_ref[...], preferred_element_type=F32)
    lg = (r[:, :n_exp] + r[:, n_exp:] + jnp.dot(xl, wr_ref[:, :n_exp], preferred_element_type=F32)
          + br_ref[...])

    lane = lax.broadcasted_iota(jnp.int32, (tm, n_exp), 1)
    vals, hots = [], []
    for _ in range(TOP_K):
        mx = jnp.max(lg, axis=-1, keepdims=True)
        ik = jnp.min(jnp.where(lg == mx, lane, n_exp), axis=-1, keepdims=True)
        hot = lane == ik
        vals.append(mx)
        hots.append(hot)
        lg = jnp.where(hot, -jnp.inf, lg)
    exps = [jnp.exp(v - vals[0]) for v in vals]
    denom = exps[0] + exps[1] + exps[2] + exps[3]

    sel = hots[0] | hots[1] | hots[2] | hots[3]
    sel_f = sel.astype(F32)
    r = lax.broadcasted_iota(jnp.int32, (tm, tm), 0)
    c = lax.broadcasted_iota(jnp.int32, (tm, tm), 1)
    strict = (c < r).astype(BF16)
    before = jnp.dot(strict, sel_f.astype(BF16), preferred_element_type=F32) + run_ref[...]
    run_ref[...] += jnp.sum(sel_f, axis=0, keepdims=True)
    cnt_ref[...] = run_ref[...].astype(jnp.int32)

    k4 = lax.broadcasted_iota(jnp.int32, (tm, TOP_K), 1)
    idx4 = jnp.zeros((tm, TOP_K), jnp.int32)
    w4 = jnp.zeros((tm, TOP_K), F32)
    rk4 = jnp.zeros((tm, TOP_K), jnp.int32)
    for k in range(TOP_K):
        ik = jnp.sum(jnp.where(hots[k], lane, 0), axis=-1, keepdims=True)
        rk = jnp.sum(jnp.where(hots[k], before, 0.0), axis=-1, keepdims=True).astype(jnp.int32)
        idx4 = jnp.where(k4 == k, ik, idx4)
        w4 = jnp.where(k4 == k, exps[k] / denom, w4)
        rk4 = jnp.where(k4 == k, rk, rk4)
    idx_ref[...] = idx4
    tw_ref[...] = w4
    rank_ref[...] = rk4


def _outproj_router(merged, xp, xs, wo, gain, wr2, br, tm=256):
    T, D = merged.shape
    E = wr2.shape[1] // 2
    n0 = xp.shape[0] // tm
    const = dict(pipeline_mode=pl.Buffered(1))
    row = lambda i: (i, 0)
    fix = lambda i: (0, 0)
    return pl.pallas_call(
        functools.partial(_outproj_router_kernel, n_exp=E, n0=n0),
        grid=(T // tm,),
        in_specs=[
            pl.BlockSpec((tm, D), row),
            pl.BlockSpec((tm, D), lambda i: (jnp.minimum(i, n0 - 1), 0)),
            pl.BlockSpec((tm, D), lambda i: (jnp.maximum(i - n0, 0), 0)),
            pl.BlockSpec((D, D), fix, **const),
            pl.BlockSpec((1, D), fix),
            pl.BlockSpec((D, 2 * E), fix),
            pl.BlockSpec((1, E), fix),
        ],
        out_specs=[
            pl.BlockSpec((tm, D), row),
            pl.BlockSpec((tm * SUBLANES, D // 2 // SUBLANES), row),
            pl.BlockSpec((tm, TOP_K), row),
            pl.BlockSpec((tm, TOP_K), row),
            pl.BlockSpec((tm, TOP_K), row),
            pl.BlockSpec((1, E), fix),
        ],
        out_shape=[
            jax.ShapeDtypeStruct((T, D), F32),
            jax.ShapeDtypeStruct((T * SUBLANES, D // 2 // SUBLANES), jnp.uint32),
            jax.ShapeDtypeStruct((T, TOP_K), jnp.int32),
            jax.ShapeDtypeStruct((T, TOP_K), F32),
            jax.ShapeDtypeStruct((T, TOP_K), jnp.int32),
            jax.ShapeDtypeStruct((1, E), jnp.int32),
        ],
        scratch_shapes=[pltpu.VMEM((1, E), F32)],
        compiler_params=_cparams(("arbitrary",)),
        name="outproj_router",
    )(merged, xp, xs, wo, gain.reshape(1, D), wr2, br.reshape(1, E))


def _dispatch_kernel(pos_ref, x_ref, xs_ref, dst_ref, sem, *, n_tok):
    i = pl.program_id(0)
    tm = x_ref.shape[0] // SUBLANES

    def body(t, carry):
        for k in range(TOP_K):
            p = pos_ref[t * TOP_K + k]
            dst_ref[p] = k * n_tok + i * tm + t
            pltpu.make_async_copy(x_ref.at[pl.ds(pl.multiple_of(t * SUBLANES, SUBLANES), SUBLANES), :],
                                  xs_ref.at[pl.ds(pl.multiple_of(p * SUBLANES, SUBLANES), SUBLANES), :], sem).start()
        return carry

    lax.fori_loop(0, tm, body, 0)
    for _ in range(TOP_K):
        pltpu.make_async_copy(x_ref, xs_ref.at[pl.ds(0, tm * SUBLANES), :], sem).wait()


def _dispatch(xn, pos_flat, n_slots, tm=256):
    T = xn.shape[0] // SUBLANES
    return pl.pallas_call(
        functools.partial(_dispatch_kernel, n_tok=T),
        grid=(T // tm,),
        in_specs=[
            pl.BlockSpec((tm * TOP_K,), lambda i: (i,), memory_space=pltpu.SMEM),
            pl.BlockSpec((tm * SUBLANES, LANES), lambda i: (i, 0)),
        ],
        out_specs=[
            pl.BlockSpec(memory_space=pl.ANY),
            pl.BlockSpec((n_slots,), lambda i: (0,), memory_space=pltpu.SMEM),
        ],
        out_shape=[jax.ShapeDtypeStruct((n_slots * SUBLANES, LANES), xn.dtype),
                   jax.ShapeDtypeStruct((n_slots,), jnp.int32)],
        scratch_shapes=[pltpu.SemaphoreType.DMA(())],
        compiler_params=_cparams(("arbitrary",)),
        name="dispatch",
    )(pos_flat, xn)


def _combine_kernel(tw_ref, x1_ref, g_ref, *rest, n0):
    y_refs, (op_ref, os_ref) = rest[:TOP_K], rest[TOP_K:]
    i = pl.program_id(0)
    tw = tw_ref[...]
    tm = x1_ref.shape[0]
    lo_acc = [None] * SUBLANES
    hi_acc = [None] * SUBLANES
    for k in range(TOP_K):
        wk = tw[:, k:k + 1]
        for s in range(SUBLANES):
            lo, hi = _unpack_bf16_pair(_load_tile_rows(y_refs[k], tm, s))
            lo_acc[s] = wk * lo if k == 0 else lo_acc[s] + wk * lo
            hi_acc[s] = wk * hi if k == 0 else hi_acc[s] + wk * hi
    x2 = x1_ref[...] + jnp.concatenate(lo_acc + hi_acc, axis=1)
    var = jnp.mean(x2 * x2, axis=-1, keepdims=True)
    y = x2 * lax.rsqrt(var + EPS) * g_ref[...]

    @pl.when(i < n0)
    def _():
        op_ref[...] = y

    @pl.when(i >= n0)
    def _():
        os_ref[...] = y


def _combine(tw, x1, gain, yk, t_prompt, tm=256):
    T, D = x1.shape
    n0 = t_prompt // tm
    nblk = T // tm
    y_spec = lambda k: pl.BlockSpec((tm * SUBLANES, LANES), lambda i: (k * nblk + i, 0))
    return pl.pallas_call(
        functools.partial(_combine_kernel, n0=n0),
        grid=(T // tm,),
        in_specs=[
            pl.BlockSpec((tm, TOP_K), lambda i: (i, 0)),
            pl.BlockSpec((tm, D), lambda i: (i, 0)),
            pl.BlockSpec((1, D), lambda i: (0, 0)),
        ] + [y_spec(k) for k in range(TOP_K)],
        out_specs=[
            pl.BlockSpec((tm, D), lambda i: (jnp.minimum(i, n0 - 1), 0)),
            pl.BlockSpec((tm, D), lambda i: (jnp.maximum(i - n0, 0), 0)),
        ],
        out_shape=[jax.ShapeDtypeStruct((t_prompt, D), F32), jax.ShapeDtypeStruct((T - t_prompt, D), F32)],
        compiler_params=_cparams(("arbitrary",)),
        name="combine",
    )(tw, x1, gain.reshape(1, D), *([yk] * TOP_K))


def _gate_up_kernel(g0_ref, nt_ref, cnt_ref, w_ref, bg_ref, bu_ref, x_hbm, o_hbm,
                    wp_ref, xin_ref, xb_ref, obuf_ref, sin, sout, *, tn):
    j, e = pl.program_id(0), pl.program_id(1)
    tm = GROUP_TILE
    D = xb_ref.shape[1]
    half = MXU_DIM // 2
    n = nt_ref[e]
    row_base = g0_ref[e]
    cnt = cnt_ref[e]

    def in_copy(r, slot):
        rows = pl.ds(pl.multiple_of((row_base + r * tm) * SUBLANES, tm * SUBLANES), tm * SUBLANES)
        return pltpu.make_async_copy(x_hbm.at[rows, :], xin_ref.at[slot], sin.at[slot])

    def out_copy(r, slot):
        rows = pl.ds(pl.multiple_of(row_base + r * tm, tm), tm)
        cols = pl.ds(pl.multiple_of(j * (tn // 2), LANES), tn // 2)
        return pltpu.make_async_copy(obuf_ref.at[slot], o_hbm.at[rows, cols], sout.at[slot])

    def body(r, carry):
        slot = r % 2

        @pl.when(r + 1 < n)
        def _():
            in_copy(r + 1, 1 - slot).start()

        in_copy(r, slot).wait()

        @pl.when(r >= 2)
        def _():
            out_copy(r - 2, slot).wait()

        valid = (r * tm + lax.broadcasted_iota(jnp.int32, (tm, 1), 0)) < cnt
        for s in range(SUBLANES):
            lo, hi = _unpack_bf16_pair(_load_tile_rows(xin_ref.at[slot], tm, s))
            xb_ref[:, s * LANES:(s + 1) * LANES] = jnp.where(valid, lo, 0.0).astype(BF16)
            xb_ref[:, D // 2 + s * LANES:D // 2 + (s + 1) * LANES] = jnp.where(valid, hi, 0.0).astype(BF16)
        for cb in range(tn // MXU_DIM):
            h = jnp.dot(xb_ref[...], wp_ref[:, cb * MXU_DIM:(cb + 1) * MXU_DIM], preferred_element_type=F32)
            hg = h[:, :half] + bg_ref[0, :, cb * half:(cb + 1) * half]
            hu = h[:, half:] + bu_ref[0, :, cb * half:(cb + 1) * half]
            gate = jnp.minimum(hg, SWIGLU_LIMIT)
            up = jnp.clip(hu, -SWIGLU_LIMIT, SWIGLU_LIMIT)
            act = gate * jax.nn.sigmoid(SWIGLU_ALPHA * gate) * (up + 1.0)
            obuf_ref[slot, :, cb * half:(cb + 1) * half] = act.astype(obuf_ref.dtype)
        out_copy(r, slot).start()
        return carry

    @pl.when(n > 0)
    def _():
        in_copy(0, 0).start()
        r_i = lax.broadcasted_iota(jnp.int32, (MXU_DIM, MXU_DIM), 0)
        c_i = lax.broadcasted_iota(jnp.int32, (MXU_DIM, MXU_DIM), 1)
        perm = (((c_i < half) & (r_i == 2 * c_i)) | ((c_i >= half) & (r_i == 2 * (c_i - half) + 1))).astype(BF16)
        for cb in range(tn // MXU_DIM):
            for rb in range(D // 512):
                w = w_ref[0, rb * 512:(rb + 1) * 512, cb * MXU_DIM:(cb + 1) * MXU_DIM].astype(BF16)
                wp_ref[rb * 512:(rb + 1) * 512, cb * MXU_DIM:(cb + 1) * MXU_DIM] = jnp.dot(
                    w, perm, preferred_element_type=F32).astype(BF16)
        lax.fori_loop(0, n, body, 0)

        @pl.when(n >= 2)
        def _():
            out_copy(n - 2, n % 2).wait()

        out_copy(n - 1, (n - 1) % 2).wait()


def _gate_up(g0, ntiles, cnt, xs, w_gate_up, bg, bu, tn):
    P = xs.shape[0] // SUBLANES
    E, D, H2 = w_gate_up.shape
    tm = GROUP_TILE
    wmap = lambda j, e, *_: (e, 0, j)
    gs = pltpu.PrefetchScalarGridSpec(
        num_scalar_prefetch=3,
        grid=(H2 // tn, E),
        in_specs=[
            pl.BlockSpec((1, D, tn), wmap),
            pl.BlockSpec((1, 1, tn // 2), wmap),
            pl.BlockSpec((1, 1, tn // 2), wmap),
            pl.BlockSpec(memory_space=pl.ANY),
        ],
        out_specs=pl.BlockSpec(memory_space=pl.ANY),
        scratch_shapes=[
            pltpu.VMEM((D, tn), BF16),
            pltpu.VMEM((2, tm * SUBLANES, LANES), jnp.uint32),
            pltpu.VMEM((tm, D), BF16),
            pltpu.VMEM((2, tm, tn // 2), BF16),
            pltpu.SemaphoreType.DMA((2,)),
            pltpu.SemaphoreType.DMA((2,)),
        ],
    )
    return pl.pallas_call(
        functools.partial(_gate_up_kernel, tn=tn),
        grid_spec=gs,
        out_shape=jax.ShapeDtypeStruct((P, H2 // 2), BF16),
        compiler_params=_cparams(("arbitrary", "arbitrary")),
        name="moe_gate_up",
    )(g0, ntiles, cnt, w_gate_up, bg, bu, xs)


def _down_kernel(g0_ref, nt_ref, cnt_ref, dst_ref, w_ref, b_ref, a_hbm, y_hbm, wb_ref, ain_ref, obuf_ref, sin, sout):
    e = pl.program_id(0)
    tm = GROUP_TILE
    H, D = wb_ref.shape
    n = nt_ref[e]
    row_base = g0_ref[e]
    cnt = cnt_ref[e]
    n_real = y_hbm.shape[0] // SUBLANES - dst_ref.shape[0]

    def in_copy(r, slot):
        rows = pl.ds(pl.multiple_of(row_base + r * tm, tm), tm)
        return pltpu.make_async_copy(a_hbm.at[rows, :], ain_ref.at[slot], sin.at[slot])

    nblk = (D // 2) // MXU_DIM

    def compute(slot, out_ref, before_block=None):
        a = ain_ref[slot]
        for bi, c0 in enumerate(range(0, D // 2, MXU_DIM)):
            if before_block is not None:
                before_block(bi)
            c1 = D // 2 + c0
            lo = jnp.dot(a, wb_ref[:, c0:c0 + MXU_DIM], preferred_element_type=F32) + b_ref[0, :, c0:c0 + MXU_DIM]
            hi = jnp.dot(a, wb_ref[:, c1:c1 + MXU_DIM], preferred_element_type=F32) + b_ref[0, :, c1:c1 + MXU_DIM]
            packed = _pack_bf16_pair(lo, hi)
            for u in range(MXU_DIM // LANES):
                out_ref[pl.ds(c0 // LANES + u, tm, stride=SUBLANES), :] = packed[:, u * LANES:(u + 1) * LANES]

    def scatter_rows(q, so, part):
        base = row_base + q * tm
        for i in range(part * (tm // nblk), (part + 1) * (tm // nblk)):
            d = jnp.where(q * tm + i < cnt, dst_ref[base + i], n_real + base + i)
            pltpu.make_async_copy(obuf_ref.at[so, pl.ds(i * SUBLANES, SUBLANES), :],
                                  y_hbm.at[pl.ds(pl.multiple_of(d * SUBLANES, SUBLANES), SUBLANES), :],
                                  sout.at[so]).start()

    def scatter_tile(q, so):
        for part in range(nblk):
            scatter_rows(q, so, part)

    def wait_scatter(so):
        pltpu.make_async_copy(obuf_ref.at[so], y_hbm.at[pl.ds(0, tm * SUBLANES), :], sout.at[so]).wait()

    def body(r, carry):
        slot = r % 2

        @pl.when(r + 1 < n)
        def _():
            in_copy(r + 1, 1 - slot).start()

        in_copy(r, slot).wait()

        @pl.when(r >= 2)
        def _():
            wait_scatter(slot)

        compute(slot, obuf_ref.at[slot], functools.partial(scatter_rows, r - 1, 1 - slot))
        return carry

    @pl.when(n > 0)
    def _():
        in_copy(0, 0).start()
        for rb in range(H // 512):
            wb_ref[rb * 512:(rb + 1) * 512, :] = w_ref[0, rb * 512:(rb + 1) * 512, :].astype(BF16)

        @pl.when(n > 1)
        def _():
            in_copy(1, 1).start()

        in_copy(0, 0).wait()
        compute(0, obuf_ref.at[0])
        lax.fori_loop(1, n, body, 0)
        so = (n - 1) % 2
        scatter_tile(n - 1, so)
        wait_scatter(so)

        @pl.when(n >= 2)
        def _():
            wait_scatter(1 - so)


def _down(g0, ntiles, cnt, slot_dst, n_rows, act, wd, bd):
    P, H = act.shape
    E, _, D = wd.shape
    tm = GROUP_TILE
    wmap = lambda e, *_: (e, 0, 0)
    gs = pltpu.PrefetchScalarGridSpec(
        num_scalar_prefetch=4,
        grid=(E,),
        in_specs=[
            pl.BlockSpec((1, H, D), wmap),
            pl.BlockSpec((1, 1, D), wmap),
            pl.BlockSpec(memory_space=pl.ANY),
        ],
        out_specs=pl.BlockSpec(memory_space=pl.ANY),
        scratch_shapes=[
            pltpu.VMEM((H, D), BF16),
            pltpu.VMEM((2, tm, H), BF16),
            pltpu.VMEM((2, tm * SUBLANES, LANES), jnp.uint32),
            pltpu.SemaphoreType.DMA((2,)),
            pltpu.SemaphoreType.DMA((2,)),
        ],
    )
    return pl.pallas_call(
        _down_kernel,
        grid_spec=gs,
        out_shape=jax.ShapeDtypeStruct((n_rows * SUBLANES, LANES), jnp.uint32),
        compiler_params=_cparams(("arbitrary",)),
        name="moe_down",
    )(g0, ntiles, cnt, slot_dst, wd, bd, act)


def _trunk(xp, xs, seq_shapes, norm_mix, w_in, w_gk_up_fwd, b_gk_fwd, w_gk_up_bwd, b_gk_bwd, gla_head_norm,
           w_fnet_out, w_gla_out, w_out, norm_ffn, w_router, b_router, w_gate_up, b_gate_up,
           w_down, b_down, norm_final):
    D = xp.shape[1]
    T = xp.shape[0] + xs.shape[0]
    fw = w_fnet_out.shape[0]
    dkk = w_gk_up_fwd.shape[1]
    dvv = w_gla_out.shape[0]
    dk, dv = dkk // GLA_HEADS, dvv // GLA_HEADS
    sizes = (fw, dkk, dkk, dvv, dvv, GATE_LOW_RANK, GATE_LOW_RANK, 2 * D)
    offs = np.concatenate([[0], np.cumsum(sizes)])
    span = lambda n: (int(offs[n]), int(offs[n + 1]))
    w_main, w_lr2 = _repack_w_in(w_in, tuple(span(n) for n in (4, 7, 3, 0, 1, 2)), (int(offs[5]), int(offs[7])))
    og_blk, g0_blk, g1_blk = 0, dvv // D, dvv // D + 1
    v_off = dvv + 2 * D
    u_off = v_off + dvv
    q_off = u_off + fw
    k_off = q_off + dkk

    proj, lr = _inproj(xp, xs, norm_mix, w_main, w_lr2)
    lr_f, lr_b = lr[:, :GATE_LOW_RANK], lr[:, GATE_LOW_RANK:]

    gd = fw // FNET_GROUPS
    cc, sc = _dft_mats(gd, gd ** -0.5)
    cs = jnp.concatenate([cc, sc], axis=1).astype(BF16)
    z = _chan_dft(proj, u_off // fw, fw, cs)
    fft, o_gla = None, None
    row0 = 0
    for (B, S) in seq_shapes:
        fft = _seq_dft(z, row0, B, S, prev=fft)
        o_gla = _gla(proj, lr_f, lr_b, w_gk_up_fwd, b_gk_fwd, w_gk_up_bwd, b_gk_bwd, row0, B, S,
                     q_off // dk, k_off // dk, v_off // dv, dk, dv, prev=o_gla)
        row0 += B * S

    merged = _merge(fft, o_gla, proj, og_blk, g0_blk, g1_blk, gla_head_norm,
                    w_fnet_out.astype(BF16), w_gla_out.astype(BF16))
    x1, xn2, idx, tw, rank, cnt = _outproj_router(merged, xp, xs, w_out.astype(BF16), norm_ffn,
                                                  _hi_lo(w_router), b_router)

    E = w_router.shape[1]
    cnt = cnt.reshape(E)
    gsz = ((cnt + GROUP_TILE - 1) // GROUP_TILE) * GROUP_TILE
    gend = jnp.cumsum(gsz)
    gstart = gend - gsz
    pos = (gstart[idx] + rank).reshape(-1).astype(jnp.int32)
    n_slots = T * TOP_K + E * GROUP_TILE
    g0 = gstart.astype(jnp.int32)
    ntiles = (gsz // GROUP_TILE).astype(jnp.int32)

    x_sorted, slot_dst = _dispatch(xn2, pos, n_slots)
    H = w_down.shape[1]
    bg = b_gate_up[:, 0::2].reshape(E, 1, H)
    bu = b_gate_up[:, 1::2].reshape(E, 1, H)
    act = _gate_up(g0, ntiles, cnt, x_sorted, w_gate_up, bg, bu, GATE_UP_TN)
    yk = _down(g0, ntiles, cnt, slot_dst, T * TOP_K + n_slots, act, w_down, b_down.reshape(E, 1, D))
    return _combine(tw, x1, norm_final, yk, xp.shape[0])


def kernel(x_prompt, x_sample, norm_mix, w_in, w_gk_up_fwd, b_gk_fwd, w_gk_up_bwd, b_gk_bwd, gla_head_norm,
           w_fnet_out, w_gla_out, w_out, norm_ffn, w_router, b_router, w_gate_up, b_gate_up, w_down,
           b_down, norm_final):
    D = x_prompt.shape[-1]
    shapes = (x_prompt.shape[:2], x_sample.shape[:2])
    yp, ys = _trunk(x_prompt.reshape(-1, D), x_sample.reshape(-1, D), shapes, norm_mix[0], w_in[0], w_gk_up_fwd[0], b_gk_fwd[0], w_gk_up_bwd[0], b_gk_bwd[0],
               gla_head_norm[0], w_fnet_out[0], w_gla_out[0], w_out[0], norm_ffn[0], w_router[0],
               b_router[0], w_gate_up[0], b_gate_up[0], w_down[0], b_down[0], norm_final)
    return (yp.reshape(x_prompt.shape), ys.reshape(x_sample.shape))
```

```python
import functools
import math

import numpy as np
import jax
import jax.numpy as jnp
from jax import lax
from jax.experimental import pallas as pl
from jax.experimental.pallas import tpu as pltpu

F32 = jnp.float32
BF16 = jnp.bfloat16
HIGHEST = lax.Precision.HIGHEST

EPS = 1e-5
FNET_GROUPS = 4
GLA_HEADS = 4
GATE_LOW_RANK = 16
GATE_LOGIT_NORMALIZER = 16.0
CHUNK = 64
TOP_K = 4
SWIGLU_LIMIT = 7.0
SWIGLU_ALPHA = 1.702

VMEM_LIMIT_BYTES = 56 * 1024 * 1024
MXU_DIM = 256
LANES = 128
SUBLANES = 8
GROUP_TILE = 256
GATE_UP_TN = 2048


def _cparams(sem):
    return pltpu.CompilerParams(dimension_semantics=sem, vmem_limit_bytes=VMEM_LIMIT_BYTES)


def _split3(x):
    hi = x.astype(BF16)
    r = x - hi.astype(F32)
    mid = r.astype(BF16)
    lo = (r - mid.astype(F32)).astype(BF16)
    return hi, mid, lo


def _hi_lo(w):
    hi = w.astype(BF16)
    lo = (w - hi.astype(F32)).astype(BF16)
    return jnp.concatenate([hi, lo], axis=1)


def _repack_kernel(w_ref, o_ref, lr_ref, *, pieces, lr_cols):
    c0 = 0
    for a, b in pieces:
        o_ref[:, c0:c0 + (b - a)] = w_ref[:, a:b].astype(BF16)
        c0 += b - a
    w = w_ref[:, lr_cols[0]:lr_cols[1]]
    hi = w.astype(BF16)
    n = lr_cols[1] - lr_cols[0]
    lr_ref[:, :n] = hi
    lr_ref[:, n:] = (w - hi.astype(F32)).astype(BF16)


def _repack_w_in(w_in, pieces, lr_cols, tr=256):
    K, N = w_in.shape
    n_main = sum(b - a for a, b in pieces)
    n_lr = lr_cols[1] - lr_cols[0]
    return pl.pallas_call(
        functools.partial(_repack_kernel, pieces=pieces, lr_cols=lr_cols),
        grid=(K // tr,),
        in_specs=[pl.BlockSpec((tr, N), lambda i: (i, 0))],
        out_specs=[pl.BlockSpec((tr, n_main), lambda i: (i, 0)), pl.BlockSpec((tr, 2 * n_lr), lambda i: (i, 0))],
        out_shape=[jax.ShapeDtypeStruct((K, n_main), BF16), jax.ShapeDtypeStruct((K, 2 * n_lr), BF16)],
        compiler_params=_cparams(("parallel",)),
        name="repack_w_in",
    )(w_in)


def _inproj_kernel(xp_ref, xs_ref, g_ref, w_ref, wlr_ref, o_ref, lr_ref, xn_ref, *, n0):
    @pl.when(pl.program_id(1) == 0)
    def _():
        x = jnp.where(pl.program_id(0) < n0, xp_ref[...], xs_ref[...])
        var = jnp.mean(x * x, axis=-1, keepdims=True)
        xn = (x * lax.rsqrt(var + EPS) * g_ref[...]).astype(BF16)
        xn_ref[...] = xn
        r = jnp.dot(xn, wlr_ref[...], preferred_element_type=F32)
        nlr = lr_ref.shape[1]
        lr_ref[...] = r[:, :nlr] + r[:, nlr:]

    o_ref[...] = jnp.dot(xn_ref[...], w_ref[...], preferred_element_type=F32).astype(o_ref.dtype)


def _inproj(xp, xs, gain, w_main, w_lr2, tm=1024, tn=1024):
    D = xp.shape[1]
    T = xp.shape[0] + xs.shape[0]
    tm = tm if xp.shape[0] % tm == 0 and xs.shape[0] % tm == 0 else tm // 2
    n0 = xp.shape[0] // tm
    N = w_main.shape[1]
    R = w_lr2.shape[1] // 2
    return pl.pallas_call(
        functools.partial(_inproj_kernel, n0=n0),
        grid=(T // tm, N // tn),
        in_specs=[
            pl.BlockSpec((tm, D), lambda i, j: (jnp.minimum(i, n0 - 1), 0), pipeline_mode=pl.Buffered(1)),
            pl.BlockSpec((tm, D), lambda i, j: (jnp.maximum(i - n0, 0), 0), pipeline_mode=pl.Buffered(1)),
            pl.BlockSpec((1, D), lambda i, j: (0, 0)),
            pl.BlockSpec((D, tn), lambda i, j: (0, j)),
            pl.BlockSpec((D, 2 * R), lambda i, j: (0, 0)),
        ],
        out_specs=[
            pl.BlockSpec((tm, tn), lambda i, j: (i, j)),
            pl.BlockSpec((tm, R), lambda i, j: (i, 0)),
        ],
        out_shape=[jax.ShapeDtypeStruct((T, N), BF16), jax.ShapeDtypeStruct((T, R), F32)],
        scratch_shapes=[pltpu.VMEM((tm, D), BF16)],
        compiler_params=_cparams(("parallel", "arbitrary")),
        name="inproj",
    )(xp, xs, gain.reshape(1, D), w_main, w_lr2)


def _chan_dft_kernel(u_ref, cs_ref, z_ref, *, gd):
    for g in range(FNET_GROUPS):
        r = jnp.dot(u_ref[:, g * gd:(g + 1) * gd], cs_ref[...], preferred_element_type=F32)
        z_ref[:, g * gd:(g + 1) * gd] = _pack_bf16_pair(r[:, :gd], r[:, gd:])


def _chan_dft(proj, u_col_block, width, cs, tm=512):
    T = proj.shape[0]
    gd = width // FNET_GROUPS
    return pl.pallas_call(
        functools.partial(_chan_dft_kernel, gd=gd),
        grid=(T // tm,),
        in_specs=[
            pl.BlockSpec((tm, width), lambda i: (i, u_col_block)),
            pl.BlockSpec((gd, 2 * gd), lambda i: (0, 0)),
        ],
        out_specs=pl.BlockSpec((tm, width), lambda i: (i, 0)),
        out_shape=jax.ShapeDtypeStruct((T, width), jnp.uint32),
        compiler_params=_cparams(("parallel",)),
        name="chan_dft",
    )(proj, cs)


_FFT_COLS = 2 * LANES


def _fft_stage1_kernel(z_ref, f1_ref, ct_ref, st_ref, a_ref, *, n1, n2):
    for m in range(n2):
        rows = pl.ds(m, n1, stride=n2)
        zc, zs = _unpack_bf16_pair(z_ref[rows, :])
        pc = jnp.dot(f1_ref[...], zc.astype(BF16), preferred_element_type=F32)
        ps = jnp.dot(f1_ref[...], zs.astype(BF16), preferred_element_type=F32)
        a_re = pc[:n1] - ps[n1:]
        a_im = -ps[:n1] - pc[n1:]
        ct = ct_ref[m][:, 0:1]
        st = st_ref[m][:, 0:1]
        a_ref[rows, :] = _pack_bf16_pair(a_re * ct + a_im * st, a_im * ct - a_re * st)


def _fft_stage2_kernel(a_ref, f2_ref, *rest, n1, n2):
    o_ref = rest[-1]
    half = _FFT_COLS // 2
    for k1 in range(n1):
        a_re, a_im = _unpack_bf16_pair(a_ref[pl.ds(k1 * n2, n2), :])
        rhs = jnp.concatenate([a_re.astype(BF16), a_im.astype(BF16)], axis=0)
        x = jnp.dot(f2_ref[...], rhs, preferred_element_type=F32)
        o_ref[pl.ds(k1, n2, stride=n1), :] = _pack_bf16_pair(x[:, :half], x[:, half:])


def _seq_dft(z, row0, B, S, prev=None):
    T, W = z.shape
    wc = _FFT_COLS
    n2 = 64 if S % (64 * SUBLANES) == 0 else S // SUBLANES
    n1 = S // n2
    assert n1 * n2 == S and row0 % S == 0 and W % wc == 0
    s0 = row0 // S
    i1 = jnp.arange(n1, dtype=jnp.int32)
    i2 = jnp.arange(n2, dtype=jnp.int32)
    ang1 = ((i1[:, None] * i1[None, :]) % n1).astype(F32) * (2.0 * math.pi / n1)
    f1 = (jnp.concatenate([jnp.cos(ang1), jnp.sin(ang1)], axis=0) * S ** -0.5).astype(BF16)
    ang2 = ((i2[:, None] * i2[None, :]) % n2).astype(F32) * (2.0 * math.pi / n2)
    f2 = jnp.concatenate([jnp.cos(ang2), jnp.sin(ang2)], axis=1).astype(BF16)
    angt = (i2[:, None] * i1[None, :]).astype(F32) * (2.0 * math.pi / S)
    ct = jnp.broadcast_to(jnp.cos(angt)[:, :, None], (n2, n1, LANES))
    st = jnp.broadcast_to(jnp.sin(angt)[:, :, None], (n2, n1, LANES))

    fix2 = lambda b, c: (0, 0)
    fix3 = lambda b, c: (0, 0, 0)

    a = pl.pallas_call(
        functools.partial(_fft_stage1_kernel, n1=n1, n2=n2),
        grid=(B, W // LANES),
        in_specs=[pl.BlockSpec((S, LANES), lambda b, c: (s0 + b, c)),
                  pl.BlockSpec((2 * n1, n1), fix2),
                  pl.BlockSpec((n2, n1, LANES), fix3),
                  pl.BlockSpec((n2, n1, LANES), fix3)],
        out_specs=pl.BlockSpec((S, LANES), lambda b, c: (b, c)),
        out_shape=jax.ShapeDtypeStruct((B * S, W), jnp.uint32),
        compiler_params=_cparams(("parallel", "parallel")),
        name="fft_stage1",
    )(z, f1, ct, st)

    in_specs = [pl.BlockSpec((S, wc), lambda b, c: (b, c)), pl.BlockSpec((n2, 2 * n2), fix2)]
    args = [a, f2]
    aliases = {}
    if prev is not None:
        in_specs.append(pl.BlockSpec(memory_space=pl.ANY))
        args.append(prev)
        aliases = {2: 0}
    return pl.pallas_call(
        functools.partial(_fft_stage2_kernel, n1=n1, n2=n2),
        grid=(B, W // wc),
        in_specs=in_specs,
        out_specs=pl.BlockSpec((S, wc // 2), lambda b, c: (s0 + b, c)),
        out_shape=jax.ShapeDtypeStruct((T, W // 2), jnp.uint32),
        input_output_aliases=aliases,
        compiler_params=_cparams(("parallel", "parallel")),
        name="fft_stage2",
    )(*args)


def _dft_mats(n, scale, split=64):
    split = split if n % split == 0 else 1
    k = jnp.arange(n, dtype=jnp.int32)[None, :]
    j1 = jnp.arange(n // split, dtype=jnp.int32)[:, None]
    j2 = jnp.arange(split, dtype=jnp.int32)[:, None]
    w = 2.0 * math.pi / n
    ang_a = ((split * j1 * k) % n).astype(F32) * w
    ang_b = ((j2 * k) % n).astype(F32) * w
    ca, sa = jnp.cos(ang_a)[:, None, :], jnp.sin(ang_a)[:, None, :]
    cb, sb = (jnp.cos(ang_b) * scale)[None, :, :], (jnp.sin(ang_b) * scale)[None, :, :]
    c = (ca * cb - sa * sb).reshape(n, n)
    s = (sa * cb + ca * sb).reshape(n, n)
    return c, s


_NT = (((1,), (1,)), ((), ()))
_TN = (((0,), (0,)), ((), ()))


def _gla_block(q_ref, k_ref, v_ref, lr_ref, w3_ref, b_ref, st_ref, reverse, qscale, nchunk):
    R = nchunk * CHUNK
    lr = lr_ref[...]
    lr_hi = lr.astype(BF16)
    lr_lo = (lr - lr_hi.astype(F32)).astype(BF16)
    z = jnp.dot(jnp.concatenate([lr_hi, lr_lo, lr_hi], axis=1), w3_ref[...],
                preferred_element_type=F32) + b_ref[...]
    g = (jnp.minimum(z, 0.0) - jnp.log(1.0 + jnp.exp(-jnp.abs(z)))) * (1.0 / GATE_LOGIT_NORMALIZER)
    ri = lax.broadcasted_iota(jnp.int32, (R, R), 0)
    ci = lax.broadcasted_iota(jnp.int32, (R, R), 1)
    cum = ((ci >= ri) if reverse else (ci <= ri)).astype(BF16)
    g_hi = g.astype(BF16)
    g_lo = (g - g_hi.astype(F32)).astype(BF16)
    G = jnp.dot(cum, g_hi, preferred_element_type=F32) + jnp.dot(cum, g_lo, preferred_element_type=F32)

    dk = G.shape[1]
    zero_row = jnp.zeros((1, dk), F32)
    if reverse:
        starts = [G[(c + 1) * CHUNK:(c + 1) * CHUNK + 1, :] if c + 1 < nchunk else zero_row for c in range(nchunk)]
        ref_row, g_tot = CHUNK // 2, G[0:1, :]
    else:
        starts = [G[c * CHUNK - 1:c * CHUNK, :] if c > 0 else zero_row for c in range(nchunk)]
        ref_row, g_tot = CHUNK // 2 - 1, G[R - 1:R, :]
    bcast = lambda rows_: jnp.concatenate([jnp.broadcast_to(r_, (CHUNK, dk)) for r_ in rows_], axis=0)
    gc = G - bcast(starts)
    gref = bcast([gc[c * CHUNK + ref_row:c * CHUNK + ref_row + 1, :] for c in range(nchunk)])

    q = q_ref[...].astype(F32) * qscale
    k = k_ref[...].astype(F32)
    v = v_ref[...]
    q_in = (q * jnp.exp(gc - gref)).astype(BF16)
    k_in = (k * jnp.exp(gref - gc)).astype(BF16)
    q_it = (q * jnp.exp(gc)).astype(BF16)
    q_st = (q * jnp.exp(G)).astype(BF16)
    k_st = (k * jnp.exp(g_tot - G)).astype(BF16)

    s_diag = lax.dot_general(q_in, k_in, _NT, preferred_element_type=F32)
    same = (ri // CHUNK) == (ci // CHUNK)
    keep = same & ((ci > ri) if reverse else (ci <= ri))
    s_rows = []
    for c in range(nchunk):
        rows = slice(c * CHUNK, (c + 1) * CHUNK)
        s = jnp.where(keep[rows, :], s_diag[rows, :], 0.0)
        lo_, hi_ = ((c + 1) * CHUNK, R) if reverse else (0, c * CHUNK)
        if hi_ > lo_:
            kx = (k[lo_:hi_, :] * jnp.exp(starts[c] - G[lo_:hi_, :])).astype(BF16)
            pad = jnp.zeros((R - (hi_ - lo_), dk), BF16)
            kx = jnp.concatenate([pad, kx] if reverse else [kx, pad], axis=0)
            s = s + lax.dot_general(q_it[rows, :], kx, _NT, preferred_element_type=F32)
        s_rows.append(s.astype(BF16))
    scores = jnp.concatenate(s_rows, axis=0)

    st = st_ref[...]
    o = (jnp.dot(scores, v, preferred_element_type=F32)
         + lax.dot_general(q_st, st.astype(BF16), _NT, preferred_element_type=F32))
    st_ref[...] = st * jnp.exp(g_tot) + lax.dot_general(v, k_st, _TN, preferred_element_type=F32)
    return o


def _gla_kernel(qf_ref, kf_ref, vf_ref, qb_ref, kb_ref, vb_ref, lrf_ref, lrb_ref,
                wf_ref, bf_ref, wb_ref, bb_ref, *rest, nchunk, nsub, qscale):
    o_ref, acc_ref, stf_ref, stb_ref = rest[-4:]
    n = pl.program_id(2)
    nb = pl.num_programs(2)
    blk = nchunk * CHUNK
    rows = nsub * blk

    @pl.when(n == 0)
    def _():
        acc_ref[...] = jnp.zeros_like(acc_ref)
        stf_ref[...] = jnp.zeros_like(stf_ref)
        stb_ref[...] = jnp.zeros_like(stb_ref)

    for s_f in range(nsub):
        s_b = nsub - 1 - s_f
        sub_f, sub_b = pl.ds(s_f * blk, blk), pl.ds(s_b * blk, blk)
        o_f = _gla_block(qf_ref.at[sub_f, :], kf_ref.at[sub_f, :], vf_ref.at[sub_f, :], lrf_ref.at[sub_f, :],
                         wf_ref, bf_ref, stf_ref, False, qscale, nchunk)
        o_b = _gla_block(qb_ref.at[sub_b, :], kb_ref.at[sub_b, :], vb_ref.at[sub_b, :], lrb_ref.at[sub_b, :],
                         wb_ref, bb_ref, stb_ref, True, qscale, nchunk)
        acc_ref[pl.ds(pl.multiple_of(n * rows + s_f * blk, blk), blk), :] += o_f
        acc_ref[pl.ds(pl.multiple_of((nb - 1 - n) * rows + s_b * blk, blk), blk), :] += o_b

    @pl.when(n == nb - 1)
    def _():
        o_ref[...] = acc_ref[...].astype(o_ref.dtype)


def _hi_hi_lo(w):
    hi = w.astype(BF16)
    lo = (w - hi.astype(F32)).astype(BF16)
    return jnp.concatenate([hi, hi, lo], axis=0)


def _gla(proj, lr_f, lr_b, wup_f, b_f, wup_b, b_b, row0, B, S, q_blk0, k_blk0, v_blk0, dk, dv,
         prev=None, blk=256, nsub=4):
    T = proj.shape[0]
    nsub = nsub if S % (nsub * blk) == 0 else 1
    rows = nsub * blk
    assert S % rows == 0 and row0 % S == 0
    nb = S // rows
    rb0, sb0 = row0 // rows, row0 // S
    fmap = lambda b, n: rb0 + b * nb + n
    bmap = lambda b, n: rb0 + b * nb + (nb - 1 - n)
    in_specs = [
        pl.BlockSpec((rows, dk), lambda b, h, n: (fmap(b, n), q_blk0 + h)),
        pl.BlockSpec((rows, dk), lambda b, h, n: (fmap(b, n), k_blk0 + h)),
        pl.BlockSpec((rows, dv), lambda b, h, n: (fmap(b, n), v_blk0 + h)),
        pl.BlockSpec((rows, dk), lambda b, h, n: (bmap(b, n), q_blk0 + h)),
        pl.BlockSpec((rows, dk), lambda b, h, n: (bmap(b, n), k_blk0 + h)),
        pl.BlockSpec((rows, dv), lambda b, h, n: (bmap(b, n), v_blk0 + h)),
        pl.BlockSpec((rows, GATE_LOW_RANK), lambda b, h, n: (fmap(b, n), 0)),
        pl.BlockSpec((rows, GATE_LOW_RANK), lambda b, h, n: (bmap(b, n), 0)),
        pl.BlockSpec((3 * GATE_LOW_RANK, dk), lambda b, h, n: (0, h)),
        pl.BlockSpec((1, dk), lambda b, h, n: (0, h)),
        pl.BlockSpec((3 * GATE_LOW_RANK, dk), lambda b, h, n: (0, h)),
        pl.BlockSpec((1, dk), lambda b, h, n: (0, h)),
    ]
    args = [proj, proj, proj, proj, proj, proj, lr_f, lr_b,
            _hi_hi_lo(wup_f), b_f.reshape(1, -1), _hi_hi_lo(wup_b), b_b.reshape(1, -1)]
    aliases = {}
    if prev is not None:
        in_specs.append(pl.BlockSpec(memory_space=pl.ANY))
        args.append(prev)
        aliases = {len(args) - 1: 0}
    return pl.pallas_call(
        functools.partial(_gla_kernel, nchunk=blk // CHUNK, nsub=nsub, qscale=dk ** -0.5),
        grid=(B, GLA_HEADS, nb),
        in_specs=in_specs,
        out_specs=pl.BlockSpec((S, dv), lambda b, h, n: (sb0 + b, h)),
        out_shape=jax.ShapeDtypeStruct((T, GLA_HEADS * dv), BF16),
        scratch_shapes=[pltpu.VMEM((S, dv), F32), pltpu.VMEM((dv, dk), F32), pltpu.VMEM((dv, dk), F32)],
        input_output_aliases=aliases,
        compiler_params=_cparams(("parallel", "parallel", "arbitrary")),
        name="gla",
    )(*args)


def _merge_kernel(fft_ref, o_ref_in, og_ref, g0_ref, g1_ref, hn_ref, wf_ref, wg_ref, o_ref, a_ref, *, dv):
    half = _FFT_COLS // 2
    pieces = []
    for cb in range(fft_ref.shape[1] // half):
        pieces.extend(_unpack_bf16_pair(fft_ref[:, cb * half:(cb + 1) * half]))
    ya = jnp.dot(jnp.concatenate(pieces, axis=1).astype(BF16), wf_ref[...], preferred_element_type=F32)
    for h in range(GLA_HEADS):
        cs = slice(h * dv, (h + 1) * dv)
        o = o_ref_in[:, cs].astype(F32)
        var = jnp.mean(o * o, axis=-1, keepdims=True)
        on = o * lax.rsqrt(var + EPS) * hn_ref[...]
        og = og_ref[:, cs].astype(F32)
        a_ref[:, cs] = (on * (og * jax.nn.sigmoid(og))).astype(BF16)
    yb = jnp.dot(a_ref[...], wg_ref[...], preferred_element_type=F32)
    m = jax.nn.sigmoid(g0_ref[...].astype(F32)) * ya + jax.nn.sigmoid(g1_ref[...].astype(F32)) * yb
    o_ref[...] = m.astype(BF16)


def _merge(fft, o_gla, proj, og_blk, g0_blk, g1_blk, hn, wf, wg, tm=256):
    T, D = o_gla.shape
    FW = wf.shape[0]
    dv = D // GLA_HEADS
    const = dict(pipeline_mode=pl.Buffered(1))
    return pl.pallas_call(
        functools.partial(_merge_kernel, dv=dv),
        grid=(T // tm,),
        in_specs=[
            pl.BlockSpec((tm, FW // 2), lambda i: (i, 0)),
            pl.BlockSpec((tm, D), lambda i: (i, 0)),
            pl.BlockSpec((tm, D), lambda i: (i, og_blk)),
            pl.BlockSpec((tm, D), lambda i: (i, g0_blk)),
            pl.BlockSpec((tm, D), lambda i: (i, g1_blk)),
            pl.BlockSpec((1, dv), lambda i: (0, 0)),
            pl.BlockSpec((FW, D), lambda i: (0, 0), **const),
            pl.BlockSpec((D, D), lambda i: (0, 0), **const),
        ],
        out_specs=pl.BlockSpec((tm, D), lambda i: (i, 0)),
        out_shape=jax.ShapeDtypeStruct((T, D), BF16),
        scratch_shapes=[pltpu.VMEM((tm, D), BF16)],
        compiler_params=_cparams(("parallel",)),
        name="merge",
    )(fft, o_gla, proj, proj, proj, hn.reshape(1, dv), wf, wg)


HI16 = 0xFFFF0000


def _pack_bf16_pair(lo, hi):
    lo_bits = lax.bitcast_convert_type(lo.astype(BF16).astype(F32), jnp.uint32)
    hi_bits = lax.bitcast_convert_type(hi.astype(BF16).astype(F32), jnp.uint32)
    return (hi_bits & jnp.uint32(HI16)) | (lo_bits >> 16)


def _unpack_bf16_pair(w):
    lo = lax.bitcast_convert_type(w << 16, F32)
    hi = lax.bitcast_convert_type(w & jnp.uint32(HI16), F32)
    return lo, hi


def _store_tile_rows(ref, val):
    tm = val.shape[0]
    for s in range(SUBLANES):
        ref[pl.ds(s, tm, stride=SUBLANES), :] = val[:, s * LANES:(s + 1) * LANES]


def _load_tile_rows(ref, tm, s):
    return ref[pl.ds(s, tm, stride=SUBLANES), :]


def _outproj_router_kernel(m_ref, xp_ref, xs_ref, wo_ref, g_ref, wr_ref, br_ref,
                           x1_ref, xn_ref, idx_ref, tw_ref, rank_ref, cnt_ref, run_ref, *, n_exp, n0):
    i = pl.program_id(0)

    @pl.when(i == 0)
    def _():
        run_ref[...] = jnp.zeros_like(run_ref)

    tm, D = m_ref.shape
    x = jnp.where(i < n0, xp_ref[...], xs_ref[...])
    x1 = x + jnp.dot(m_ref[...], wo_ref[...], preferred_element_type=F32)
    x1_ref[...] = x1
    var = jnp.mean(x1 * x1, axis=-1, keepdims=True)
    xn = x1 * lax.rsqrt(var + EPS) * g_ref[...]
    _store_tile_rows(xn_ref, _pack_bf16_pair(xn[:, :D // 2], xn[:, D // 2:]))
    xh = xn.astype(BF16)
    xl = (xn - xh.astype(F32)).astype(BF16)
    r = jnp.dot(xh, wr_ref[...], preferred_element_type=F32)
    lg = (r[:, :n_exp] + r[:, n_exp:] + jnp.dot(xl, wr_ref[:, :n_exp], preferred_element_type=F32)
          + br_ref[...])

    lane = lax.broadcasted_iota(jnp.int32, (tm, n_exp), 1)
    vals, hots = [], []
    for _ in range(TOP_K):
        mx = jnp.max(lg, axis=-1, keepdims=True)
        ik = jnp.min(jnp.where(lg == mx, lane, n_exp), axis=-1, keepdims=True)
        hot = lane == ik
        vals.append(mx)
        hots.append(hot)
        lg = jnp.where(hot, -jnp.inf, lg)
    exps = [jnp.exp(v - vals[0]) for v in vals]
    denom = exps[0] + exps[1] + exps[2] + exps[3]

    sel = hots[0] | hots[1] | hots[2] | hots[3]
    sel_f = sel.astype(F32)
    r = lax.broadcasted_iota(jnp.int32, (tm, tm), 0)
    c = lax.broadcasted_iota(jnp.int32, (tm, tm), 1)
    strict = (c < r).astype(BF16)
    before = jnp.dot(strict, sel_f.astype(BF16), preferred_element_type=F32) + run_ref[...]
    run_ref[...] += jnp.sum(sel_f, axis=0, keepdims=True)
    cnt_ref[...] = run_ref[...].astype(jnp.int32)

    k4 = lax.broadcasted_iota(jnp.int32, (tm, TOP_K), 1)
    idx4 = jnp.zeros((tm, TOP_K), jnp.int32)
    w4 = jnp.zeros((tm, TOP_K), F32)
    rk4 = jnp.zeros((tm, TOP_K), jnp.int32)
    for k in range(TOP_K):
        ik = jnp.sum(jnp.where(hots[k], lane, 0), axis=-1, keepdims=True)
        rk = jnp.sum(jnp.where(hots[k], before, 0.0), axis=-1, keepdims=True).astype(jnp.int32)
        idx4 = jnp.where(k4 == k, ik, idx4)
        w4 = jnp.where(k4 == k, exps[k] / denom, w4)
        rk4 = jnp.where(k4 == k, rk, rk4)
    idx_ref[...] = idx4
    tw_ref[...] = w4
    rank_ref[...] = rk4


def _outproj_router(merged, xp, xs, wo, gain, wr2, br, tm=256):
    T, D = merged.shape
    E = wr2.shape[1] // 2
    n0 = xp.shape[0] // tm
    const = dict(pipeline_mode=pl.Buffered(1))
    row = lambda i: (i, 0)
    fix = lambda i: (0, 0)
    return pl.pallas_call(
        functools.partial(_outproj_router_kernel, n_exp=E, n0=n0),
        grid=(T // tm,),
        in_specs=[
            pl.BlockSpec((tm, D), row),
            pl.BlockSpec((tm, D), lambda i: (jnp.minimum(i, n0 - 1), 0)),
            pl.BlockSpec((tm, D), lambda i: (jnp.maximum(i - n0, 0), 0)),
            pl.BlockSpec((D, D), fix, **const),
            pl.BlockSpec((1, D), fix),
            pl.BlockSpec((D, 2 * E), fix),
            pl.BlockSpec((1, E), fix),
        ],
        out_specs=[
            pl.BlockSpec((tm, D), row),
            pl.BlockSpec((tm * SUBLANES, D // 2 // SUBLANES), row),
            pl.BlockSpec((tm, TOP_K), row),
            pl.BlockSpec((tm, TOP_K), row),
            pl.BlockSpec((tm, TOP_K), row),
            pl.BlockSpec((1, E), fix),
        ],
        out_shape=[
            jax.ShapeDtypeStruct((T, D), F32),
            jax.ShapeDtypeStruct((T * SUBLANES, D // 2 // SUBLANES), jnp.uint32),
            jax.ShapeDtypeStruct((T, TOP_K), jnp.int32),
            jax.ShapeDtypeStruct((T, TOP_K), F32),
            jax.ShapeDtypeStruct((T, TOP_K), jnp.int32),
            jax.ShapeDtypeStruct((1, E), jnp.int32),
        ],
        scratch_shapes=[pltpu.VMEM((1, E), F32)],
        compiler_params=_cparams(("arbitrary",)),
        name="outproj_router",
    )(merged, xp, xs, wo, gain.reshape(1, D), wr2, br.reshape(1, E))


def _dispatch_kernel(pos_ref, x_ref, xs_ref, dst_ref, sem, *, n_tok):
    i = pl.program_id(0)
    tm = x_ref.shape[0] // SUBLANES

    def body(t, carry):
        for k in range(TOP_K):
            p = pos_ref[t * TOP_K + k]
            dst_ref[p] = k * n_tok + i * tm + t
            pltpu.make_async_copy(x_ref.at[pl.ds(pl.multiple_of(t * SUBLANES, SUBLANES), SUBLANES), :],
                                  xs_ref.at[pl.ds(pl.multiple_of(p * SUBLANES, SUBLANES), SUBLANES), :], sem).start()
        return carry

    lax.fori_loop(0, tm, body, 0)
    for _ in range(TOP_K):
        pltpu.make_async_copy(x_ref, xs_ref.at[pl.ds(0, tm * SUBLANES), :], sem).wait()


def _dispatch(xn, pos_flat, n_slots, tm=256):
    T = xn.shape[0] // SUBLANES
    return pl.pallas_call(
        functools.partial(_dispatch_kernel, n_tok=T),
        grid=(T // tm,),
        in_specs=[
            pl.BlockSpec((tm * TOP_K,), lambda i: (i,), memory_space=pltpu.SMEM),
            pl.BlockSpec((tm * SUBLANES, LANES), lambda i: (i, 0)),
        ],
        out_specs=[
            pl.BlockSpec(memory_space=pl.ANY),
            pl.BlockSpec((n_slots,), lambda i: (0,), memory_space=pltpu.SMEM),
        ],
        out_shape=[jax.ShapeDtypeStruct((n_slots * SUBLANES, LANES), xn.dtype),
                   jax.ShapeDtypeStruct((n_slots,), jnp.int32)],
        scratch_shapes=[pltpu.SemaphoreType.DMA(())],
        compiler_params=_cparams(("arbitrary",)),
        name="dispatch",
    )(pos_flat, xn)


def _combine_kernel(tw_ref, x1_ref, g_ref, *rest, n0):
    y_refs, (op_ref, os_ref) = rest[:TOP_K], rest[TOP_K:]
    i = pl.program_id(0)
    tw = tw_ref[...]
    tm = x1_ref.shape[0]
    lo_acc = [None] * SUBLANES
    hi_acc = [None] * SUBLANES
    for k in range(TOP_K):
        wk = tw[:, k:k + 1]
        for s in range(SUBLANES):
            lo, hi = _unpack_bf16_pair(_load_tile_rows(y_refs[k], tm, s))
            lo_acc[s] = wk * lo if k == 0 else lo_acc[s] + wk * lo
            hi_acc[s] = wk * hi if k == 0 else hi_acc[s] + wk * hi
    x2 = x1_ref[...] + jnp.concatenate(lo_acc + hi_acc, axis=1)
    var = jnp.mean(x2 * x2, axis=-1, keepdims=True)
    y = x2 * lax.rsqrt(var + EPS) * g_ref[...]

    @pl.when(i < n0)
    def _():
        op_ref[...] = y

    @pl.when(i >= n0)
    def _():
        os_ref[...] = y


def _combine(tw, x1, gain, yk, t_prompt, tm=256):
    T, D = x1.shape
    n0 = t_prompt // tm
    nblk = T // tm
    y_spec = lambda k: pl.BlockSpec((tm * SUBLANES, LANES), lambda i: (k * nblk + i, 0))
    return pl.pallas_call(
        functools.partial(_combine_kernel, n0=n0),
        grid=(T // tm,),
        in_specs=[
            pl.BlockSpec((tm, TOP_K), lambda i: (i, 0)),
            pl.BlockSpec((tm, D), lambda i: (i, 0)),
            pl.BlockSpec((1, D), lambda i: (0, 0)),
        ] + [y_spec(k) for k in range(TOP_K)],
        out_specs=[
            pl.BlockSpec((tm, D), lambda i: (jnp.minimum(i, n0 - 1), 0)),
            pl.BlockSpec((tm, D), lambda i: (jnp.maximum(i - n0, 0), 0)),
        ],
        out_shape=[jax.ShapeDtypeStruct((t_prompt, D), F32), jax.ShapeDtypeStruct((T - t_prompt, D), F32)],
        compiler_params=_cparams(("arbitrary",)),
        name="combine",
    )(tw, x1, gain.reshape(1, D), *([yk] * TOP_K))


def _gate_up_kernel(g0_ref, nt_ref, cnt_ref, w_ref, bg_ref, bu_ref, x_hbm, o_hbm,
                    wp_ref, xin_ref, xb_ref, obuf_ref, sin, sout, *, tn):
    j, e = pl.program_id(0), pl.program_id(1)
    tm = GROUP_TILE
    D = xb_ref.shape[1]
    half = MXU_DIM // 2
    n = nt_ref[e]
    row_base = g0_ref[e]
    cnt = cnt_ref[e]

    def in_copy(r, slot):
        rows = pl.ds(pl.multiple_of((row_base + r * tm) * SUBLANES, tm * SUBLANES), tm * SUBLANES)
        return pltpu.make_async_copy(x_hbm.at[rows, :], xin_ref.at[slot], sin.at[slot])

    def out_copy(r, slot):
        rows = pl.ds(pl.multiple_of(row_base + r * tm, tm), tm)
        cols = pl.ds(pl.multiple_of(j * (tn // 2), LANES), tn // 2)
        return pltpu.make_async_copy(obuf_ref.at[slot], o_hbm.at[rows, cols], sout.at[slot])

    def body(r, carry):
        slot = r % 2

        @pl.when(r + 1 < n)
        def _():
            in_copy(r + 1, 1 - slot).start()

        in_copy(r, slot).wait()

        @pl.when(r >= 2)
        def _():
            out_copy(r - 2, slot).wait()

        valid = (r * tm + lax.broadcasted_iota(jnp.int32, (tm, 1), 0)) < cnt
        for s in range(SUBLANES):
            lo, hi = _unpack_bf16_pair(_load_tile_rows(xin_ref.at[slot], tm, s))
            xb_ref[:, s * LANES:(s + 1) * LANES] = jnp.where(valid, lo, 0.0).astype(BF16)
            xb_ref[:, D // 2 + s * LANES:D // 2 + (s + 1) * LANES] = jnp.where(valid, hi, 0.0).astype(BF16)
        for cb in range(tn // MXU_DIM):
            h = jnp.dot(xb_ref[...], wp_ref[:, cb * MXU_DIM:(cb + 1) * MXU_DIM], preferred_element_type=F32)
            hg = h[:, :half] + bg_ref[0, :, cb * half:(cb + 1) * half]
            hu = h[:, half:] + bu_ref[0, :, cb * half:(cb + 1) * half]
            gate = jnp.minimum(hg, SWIGLU_LIMIT)
            up = jnp.clip(hu, -SWIGLU_LIMIT, SWIGLU_LIMIT)
            act = gate * jax.nn.sigmoid(SWIGLU_ALPHA * gate) * (up + 1.0)
            obuf_ref[slot, :, cb * half:(cb + 1) * half] = act.astype(obuf_ref.dtype)
        out_copy(r, slot).start()
        return carry

    @pl.when(n > 0)
    def _():
        in_copy(0, 0).start()
        r_i = lax.broadcasted_iota(jnp.int32, (MXU_DIM, MXU_DIM), 0)
        c_i = lax.broadcasted_iota(jnp.int32, (MXU_DIM, MXU_DIM), 1)
        perm = (((c_i < half) & (r_i == 2 * c_i)) | ((c_i >= half) & (r_i == 2 * (c_i - half) + 1))).astype(BF16)
        for cb in range(tn // MXU_DIM):
            for rb in range(D // 512):
                w = w_ref[0, rb * 512:(rb + 1) * 512, cb * MXU_DIM:(cb + 1) * MXU_DIM].astype(BF16)
                wp_ref[rb * 512:(rb + 1) * 512, cb * MXU_DIM:(cb + 1) * MXU_DIM] = jnp.dot(
                    w, perm, preferred_element_type=F32).astype(BF16)
        lax.fori_loop(0, n, body, 0)

        @pl.when(n >= 2)
        def _():
            out_copy(n - 2, n % 2).wait()

        out_copy(n - 1, (n - 1) % 2).wait()


def _gate_up(g0, ntiles, cnt, xs, w_gate_up, bg, bu, tn):
    P = xs.shape[0] // SUBLANES
    E, D, H2 = w_gate_up.shape
    tm = GROUP_TILE
    wmap = lambda j, e, *_: (e, 0, j)
    gs = pltpu.PrefetchScalarGridSpec(
        num_scalar_prefetch=3,
        grid=(H2 // tn, E),
        in_specs=[
            pl.BlockSpec((1, D, tn), wmap),
            pl.BlockSpec((1, 1, tn // 2), wmap),
            pl.BlockSpec((1, 1, tn // 2), wmap),
            pl.BlockSpec(memory_space=pl.ANY),
        ],
        out_specs=pl.BlockSpec(memory_space=pl.ANY),
        scratch_shapes=[
            pltpu.VMEM((D, tn), BF16),
            pltpu.VMEM((2, tm * SUBLANES, LANES), jnp.uint32),
            pltpu.VMEM((tm, D), BF16),
            pltpu.VMEM((2, tm, tn // 2), BF16),
            pltpu.SemaphoreType.DMA((2,)),
            pltpu.SemaphoreType.DMA((2,)),
        ],
    )
    return pl.pallas_call(
        functools.partial(_gate_up_kernel, tn=tn),
        grid_spec=gs,
        out_shape=jax.ShapeDtypeStruct((P, H2 // 2), BF16),
        compiler_params=_cparams(("arbitrary", "arbitrary")),
        name="moe_gate_up",
    )(g0, ntiles, cnt, w_gate_up, bg, bu, xs)


def _down_kernel(g0_ref, nt_ref, cnt_ref, dst_ref, w_ref, b_ref, a_hbm, y_hbm, wb_ref, ain_ref, obuf_ref, sin, sout):
    e = pl.program_id(0)
    tm = GROUP_TILE
    H, D = wb_ref.shape
    n = nt_ref[e]
    row_base = g0_ref[e]
    cnt = cnt_ref[e]
    n_real = y_hbm.shape[0] // SUBLANES - dst_ref.shape[0]

    def in_copy(r, slot):
        rows = pl.ds(pl.multiple_of(row_base + r * tm, tm), tm)
        return pltpu.make_async_copy(a_hbm.at[rows, :], ain_ref.at[slot], sin.at[slot])

    nblk = (D // 2) // MXU_DIM

    def compute(slot, out_ref, before_block=None):
        a = ain_ref[slot]
        for bi, c0 in enumerate(range(0, D // 2, MXU_DIM)):
            if before_block is not None:
                before_block(bi)
            c1 = D // 2 + c0
            lo = jnp.dot(a, wb_ref[:, c0:c0 + MXU_DIM], preferred_element_type=F32) + b_ref[0, :, c0:c0 + MXU_DIM]
            hi = jnp.dot(a, wb_ref[:, c1:c1 + MXU_DIM], preferred_element_type=F32) + b_ref[0, :, c1:c1 + MXU_DIM]
            packed = _pack_bf16_pair(lo, hi)
            for u in range(MXU_DIM // LANES):
                out_ref[pl.ds(c0 // LANES + u, tm, stride=SUBLANES), :] = packed[:, u * LANES:(u + 1) * LANES]

    def scatter_rows(q, so, part):
        base = row_base + q * tm
        for i in range(part * (tm // nblk), (part + 1) * (tm // nblk)):
            d = jnp.where(q * tm + i < cnt, dst_ref[base + i], n_real + base + i)
            pltpu.make_async_copy(obuf_ref.at[so, pl.ds(i * SUBLANES, SUBLANES), :],
                                  y_hbm.at[pl.ds(pl.multiple_of(d * SUBLANES, SUBLANES), SUBLANES), :],
                                  sout.at[so]).start()

    def scatter_tile(q, so):
        for part in range(nblk):
            scatter_rows(q, so, part)

    def wait_scatter(so):
        pltpu.make_async_copy(obuf_ref.at[so], y_hbm.at[pl.ds(0, tm * SUBLANES), :], sout.at[so]).wait()

    def body(r, carry):
        slot = r % 2

        @pl.when(r + 1 < n)
        def _():
            in_copy(r + 1, 1 - slot).start()

        in_copy(r, slot).wait()

        @pl.when(r >= 2)
        def _():
            wait_scatter(slot)

        compute(slot, obuf_ref.at[slot], functools.partial(scatter_rows, r - 1, 1 - slot))
        return carry

    @pl.when(n > 0)
    def _():
        in_copy(0, 0).start()
        for rb in range(H // 512):
            wb_ref[rb * 512:(rb + 1) * 512, :] = w_ref[0, rb * 512:(rb + 1) * 512, :].astype(BF16)

        @pl.when(n > 1)
        def _():
            in_copy(1, 1).start()

        in_copy(0, 0).wait()
        compute(0, obuf_ref.at[0])
        lax.fori_loop(1, n, body, 0)
        so = (n - 1) % 2
        scatter_tile(n - 1, so)
        wait_scatter(so)

        @pl.when(n >= 2)
        def _():
            wait_scatter(1 - so)


def _down(g0, ntiles, cnt, slot_dst, n_rows, act, wd, bd):
    P, H = act.shape
    E, _, D = wd.shape
    tm = GROUP_TILE
    wmap = lambda e, *_: (e, 0, 0)
    gs = pltpu.PrefetchScalarGridSpec(
        num_scalar_prefetch=4,
        grid=(E,),
        in_specs=[
            pl.BlockSpec((1, H, D), wmap),
            pl.BlockSpec((1, 1, D), wmap),
            pl.BlockSpec(memory_space=pl.ANY),
        ],
        out_specs=pl.BlockSpec(memory_space=pl.ANY),
        scratch_shapes=[
            pltpu.VMEM((H, D), BF16),
            pltpu.VMEM((2, tm, H), BF16),
            pltpu.VMEM((2, tm * SUBLANES, LANES), jnp.uint32),
            pltpu.SemaphoreType.DMA((2,)),
            pltpu.SemaphoreType.DMA((2,)),
        ],
    )
    return pl.pallas_call(
        _down_kernel,
        grid_spec=gs,
        out_shape=jax.ShapeDtypeStruct((n_rows * SUBLANES, LANES), jnp.uint32),
        compiler_params=_cparams(("arbitrary",)),
        name="moe_down",
    )(g0, ntiles, cnt, slot_dst, wd, bd, act)


def _trunk(xp, xs, seq_shapes, norm_mix, w_in, w_gk_up_fwd, b_gk_fwd, w_gk_up_bwd, b_gk_bwd, gla_head_norm,
           w_fnet_out, w_gla_out, w_out, norm_ffn, w_router, b_router, w_gate_up, b_gate_up,
           w_down, b_down, norm_final):
    D = xp.shape[1]
    T = xp.shape[0] + xs.shape[0]
    fw = w_fnet_out.shape[0]
    dkk = w_gk_up_fwd.shape[1]
    dvv = w_gla_out.shape[0]
    dk, dv = dkk // GLA_HEADS, dvv // GLA_HEADS
    sizes = (fw, dkk, dkk, dvv, dvv, GATE_LOW_RANK, GATE_LOW_RANK, 2 * D)
    offs = np.concatenate([[0], np.cumsum(sizes)])
    span = lambda n: (int(offs[n]), int(offs[n + 1]))
    w_main, w_lr2 = _repack_w_in(w_in, tuple(span(n) for n in (4, 7, 3, 0, 1, 2)), (int(offs[5]), int(offs[7])))
    og_blk, g0_blk, g1_blk = 0, dvv // D, dvv // D + 1
    v_off = dvv + 2 * D
    u_off = v_off + dvv
    q_off = u_off + fw
    k_off = q_off + dkk

    proj, lr = _inproj(xp, xs, norm_mix, w_main, w_lr2)
    lr_f, lr_b = lr[:, :GATE_LOW_RANK], lr[:, GATE_LOW_RANK:]

    gd = fw // FNET_GROUPS
    cc, sc = _dft_mats(gd, gd ** -0.5)
    cs = jnp.concatenate([cc, sc], axis=1).astype(BF16)
    z = _chan_dft(proj, u_off // fw, fw, cs)
    fft, o_gla = None, None
    row0 = 0
    for (B, S) in seq_shapes:
        fft = _seq_dft(z, row0, B, S, prev=fft)
        o_gla = _gla(proj, lr_f, lr_b, w_gk_up_fwd, b_gk_fwd, w_gk_up_bwd, b_gk_bwd, row0, B, S,
                     q_off // dk, k_off // dk, v_off // dv, dk, dv, prev=o_gla)
        row0 += B * S

    merged = _merge(fft, o_gla, proj, og_blk, g0_blk, g1_blk, gla_head_norm,
                    w_fnet_out.astype(BF16), w_gla_out.astype(BF16))
    x1, xn2, idx, tw, rank, cnt = _outproj_router(merged, xp, xs, w_out.astype(BF16), norm_ffn,
                                                  _hi_lo(w_router), b_router)

    E = w_router.shape[1]
    cnt = cnt.reshape(E)
    gsz = ((cnt + GROUP_TILE - 1) // GROUP_TILE) * GROUP_TILE
    gend = jnp.cumsum(gsz)
    gstart = gend - gsz
    pos = (gstart[idx] + rank).reshape(-1).astype(jnp.int32)
    n_slots = T * TOP_K + E * GROUP_TILE
    g0 = gstart.astype(jnp.int32)
    ntiles = (gsz // GROUP_TILE).astype(jnp.int32)

    x_sorted, slot_dst = _dispatch(xn2, pos, n_slots)
    H = w_down.shape[1]
    bg = b_gate_up[:, 0::2].reshape(E, 1, H)
    bu = b_gate_up[:, 1::2].reshape(E, 1, H)
    act = _gate_up(g0, ntiles, cnt, x_sorted, w_gate_up, bg, bu, GATE_UP_TN)
    yk = _down(g0, ntiles, cnt, slot_dst, T * TOP_K + n_slots, act, w_down, b_down.reshape(E, 1, D))
    return _combine(tw, x1, norm_final, yk, xp.shape[0])


def kernel(x_prompt, x_sample, norm_mix, w_in, w_gk_up_fwd, b_gk_fwd, w_gk_up_bwd, b_gk_bwd, gla_head_norm,
           w_fnet_out, w_gla_out, w_out, norm_ffn, w_router, b_router, w_gate_up, b_gate_up, w_down,
           b_down, norm_final):
    D = x_prompt.shape[-1]
    shapes = (x_prompt.shape[:2], x_sample.shape[:2])
    yp, ys = _trunk(x_prompt.reshape(-1, D), x_sample.reshape(-1, D), shapes, norm_mix[0], w_in[0], w_gk_up_fwd[0], b_gk_fwd[0], w_gk_up_bwd[0], b_gk_bwd[0],
               gla_head_norm[0], w_fnet_out[0], w_gla_out[0], w_out[0], norm_ffn[0], w_router[0],
               b_router[0], w_gate_up[0], b_gate_up[0], w_down[0], b_down[0], norm_final)
    return (yp.reshape(x_prompt.shape), ys.reshape(x_sample.shape))
```

```python
import functools
import math

import numpy as np
import jax
import jax.numpy as jnp
from jax import lax
from jax.experimental import pallas as pl
from jax.experimental.pallas import tpu as pltpu

F32 = jnp.float32
BF16 = jnp.bfloat16
HIGHEST = lax.Precision.HIGHEST

EPS = 1e-5
FNET_GROUPS = 4
GLA_HEADS = 4
GATE_LOW_RANK = 16
GATE_LOGIT_NORMALIZER = 16.0
CHUNK = 64
TOP_K = 4
SWIGLU_LIMIT = 7.0
SWIGLU_ALPHA = 1.702

VMEM_LIMIT_BYTES = 56 * 1024 * 1024
MXU_DIM = 256
LANES = 128
SUBLANES = 8
GROUP_TILE = 512
GATE_UP_TN = 2048


def _cparams(sem):
    return pltpu.CompilerParams(dimension_semantics=sem, vmem_limit_bytes=VMEM_LIMIT_BYTES)


def _split3(x):
    hi = x.astype(BF16)
    r = x - hi.astype(F32)
    mid = r.astype(BF16)
    lo = (r - mid.astype(F32)).astype(BF16)
    return hi, mid, lo


def _hi_lo(w):
    hi = w.astype(BF16)
    lo = (w - hi.astype(F32)).astype(BF16)
    return jnp.concatenate([hi, lo], axis=1)


def _repack_kernel(w_ref, o_ref, lr_ref, *, pieces, lr_cols):
    c0 = 0
    for a, b in pieces:
        o_ref[:, c0:c0 + (b - a)] = w_ref[:, a:b].astype(BF16)
        c0 += b - a
    w = w_ref[:, lr_cols[0]:lr_cols[1]]
    hi = w.astype(BF16)
    n = lr_cols[1] - lr_cols[0]
    lr_ref[:, :n] = hi
    lr_ref[:, n:] = (w - hi.astype(F32)).astype(BF16)


def _repack_w_in(w_in, pieces, lr_cols, tr=256):
    K, N = w_in.shape
    n_main = sum(b - a for a, b in pieces)
    n_lr = lr_cols[1] - lr_cols[0]
    return pl.pallas_call(
        functools.partial(_repack_kernel, pieces=pieces, lr_cols=lr_cols),
        grid=(K // tr,),
        in_specs=[pl.BlockSpec((tr, N), lambda i: (i, 0))],
        out_specs=[pl.BlockSpec((tr, n_main), lambda i: (i, 0)), pl.BlockSpec((tr, 2 * n_lr), lambda i: (i, 0))],
        out_shape=[jax.ShapeDtypeStruct((K, n_main), BF16), jax.ShapeDtypeStruct((K, 2 * n_lr), BF16)],
        compiler_params=_cparams(("parallel",)),
        name="repack_w_in",
    )(w_in)


def _inproj_kernel(xp_ref, xs_ref, g_ref, w_ref, wlr_ref, o_ref, lr_ref, xn_ref, *, n0):
    @pl.when(pl.program_id(1) == 0)
    def _():
        x = jnp.where(pl.program_id(0) < n0, xp_ref[...], xs_ref[...])
        var = jnp.mean(x * x, axis=-1, keepdims=True)
        xn = (x * lax.rsqrt(var + EPS) * g_ref[...]).astype(BF16)
        xn_ref[...] = xn
        r = jnp.dot(xn, wlr_ref[...], preferred_element_type=F32)
        nlr = lr_ref.shape[1]
        lr_ref[...] = r[:, :nlr] + r[:, nlr:]

    o_ref[...] = jnp.dot(xn_ref[...], w_ref[...], preferred_element_type=F32).astype(o_ref.dtype)


def _inproj(xp, xs, gain, w_main, w_lr2, tm=1024, tn=1024):
    D = xp.shape[1]
    T = xp.shape[0] + xs.shape[0]
    tm = tm if xp.shape[0] % tm == 0 and xs.shape[0] % tm == 0 else tm // 2
    n0 = xp.shape[0] // tm
    N = w_main.shape[1]
    R = w_lr2.shape[1] // 2
    return pl.pallas_call(
        functools.partial(_inproj_kernel, n0=n0),
        grid=(T // tm, N // tn),
        in_specs=[
            pl.BlockSpec((tm, D), lambda i, j: (jnp.minimum(i, n0 - 1), 0), pipeline_mode=pl.Buffered(1)),
            pl.BlockSpec((tm, D), lambda i, j: (jnp.maximum(i - n0, 0), 0), pipeline_mode=pl.Buffered(1)),
            pl.BlockSpec((1, D), lambda i, j: (0, 0)),
            pl.BlockSpec((D, tn), lambda i, j: (0, j)),
            pl.BlockSpec((D, 2 * R), lambda i, j: (0, 0)),
        ],
        out_specs=[
            pl.BlockSpec((tm, tn), lambda i, j: (i, j)),
            pl.BlockSpec((tm, R), lambda i, j: (i, 0)),
        ],
        out_shape=[jax.ShapeDtypeStruct((T, N), BF16), jax.ShapeDtypeStruct((T, R), F32)],
        scratch_shapes=[pltpu.VMEM((tm, D), BF16)],
        compiler_params=_cparams(("parallel", "arbitrary")),
        name="inproj",
    )(xp, xs, gain.reshape(1, D), w_main, w_lr2)


def _chan_dft_kernel(u_ref, cs_ref, z_ref, *, gd):
    for g in range(FNET_GROUPS):
        r = jnp.dot(u_ref[:, g * gd:(g + 1) * gd], cs_ref[...], preferred_element_type=F32)
        z_ref[:, g * gd:(g + 1) * gd] = _pack_bf16_pair(r[:, :gd], r[:, gd:])


def _chan_dft(proj, u_col_block, width, cs, tm=512):
    T = proj.shape[0]
    gd = width // FNET_GROUPS
    return pl.pallas_call(
        functools.partial(_chan_dft_kernel, gd=gd),
        grid=(T // tm,),
        in_specs=[
            pl.BlockSpec((tm, width), lambda i: (i, u_col_block)),
            pl.BlockSpec((gd, 2 * gd), lambda i: (0, 0)),
        ],
        out_specs=pl.BlockSpec((tm, width), lambda i: (i, 0)),
        out_shape=jax.ShapeDtypeStruct((T, width), jnp.uint32),
        compiler_params=_cparams(("parallel",)),
        name="chan_dft",
    )(proj, cs)


_FFT_COLS = 2 * LANES


def _fft_stage1_kernel(z_ref, f1_ref, ct_ref, st_ref, a_ref, *, n1, n2):
    for m in range(n2):
        rows = pl.ds(m, n1, stride=n2)
        zc, zs = _unpack_bf16_pair(z_ref[rows, :])
        pc = jnp.dot(f1_ref[...], zc.astype(BF16), preferred_element_type=F32)
        ps = jnp.dot(f1_ref[...], zs.astype(BF16), preferred_element_type=F32)
        a_re = pc[:n1] - ps[n1:]
        a_im = -ps[:n1] - pc[n1:]
        ct = ct_ref[m][:, 0:1]
        st = st_ref[m][:, 0:1]
        a_ref[rows, :] = _pack_bf16_pair(a_re * ct + a_im * st, a_im * ct - a_re * st)


def _fft_stage2_kernel(a_ref, f2_ref, *rest, n1, n2):
    o_ref = rest[-1]
    half = _FFT_COLS // 2
    for k1 in range(n1):
        a_re, a_im = _unpack_bf16_pair(a_ref[pl.ds(k1 * n2, n2), :])
        rhs = jnp.concatenate([a_re.astype(BF16), a_im.astype(BF16)], axis=0)
        x = jnp.dot(f2_ref[...], rhs, preferred_element_type=F32)
        o_ref[pl.ds(k1, n2, stride=n1), :] = _pack_bf16_pair(x[:, :half], x[:, half:])


def _seq_dft(z, row0, B, S, prev=None):
    T, W = z.shape
    wc = _FFT_COLS
    n2 = 64 if S % (64 * SUBLANES) == 0 else S // SUBLANES
    n1 = S // n2
    assert n1 * n2 == S and row0 % S == 0 and W % wc == 0
    s0 = row0 // S
    i1 = jnp.arange(n1, dtype=jnp.int32)
    i2 = jnp.arange(n2, dtype=jnp.int32)
    ang1 = ((i1[:, None] * i1[None, :]) % n1).astype(F32) * (2.0 * math.pi / n1)
    f1 = (jnp.concatenate([jnp.cos(ang1), jnp.sin(ang1)], axis=0) * S ** -0.5).astype(BF16)
    ang2 = ((i2[:, None] * i2[None, :]) % n2).astype(F32) * (2.0 * math.pi / n2)
    f2 = jnp.concatenate([jnp.cos(ang2), jnp.sin(ang2)], axis=1).astype(BF16)
    angt = (i2[:, None] * i1[None, :]).astype(F32) * (2.0 * math.pi / S)
    ct = jnp.broadcast_to(jnp.cos(angt)[:, :, None], (n2, n1, LANES))
    st = jnp.broadcast_to(jnp.sin(angt)[:, :, None], (n2, n1, LANES))

    fix2 = lambda b, c: (0, 0)
    fix3 = lambda b, c: (0, 0, 0)

    a = pl.pallas_call(
        functools.partial(_fft_stage1_kernel, n1=n1, n2=n2),
        grid=(B, W // LANES),
        in_specs=[pl.BlockSpec((S, LANES), lambda b, c: (s0 + b, c)),
                  pl.BlockSpec((2 * n1, n1), fix2),
                  pl.BlockSpec((n2, n1, LANES), fix3),
                  pl.BlockSpec((n2, n1, LANES), fix3)],
        out_specs=pl.BlockSpec((S, LANES), lambda b, c: (b, c)),
        out_shape=jax.ShapeDtypeStruct((B * S, W), jnp.uint32),
        compiler_params=_cparams(("parallel", "parallel")),
        name="fft_stage1",
    )(z, f1, ct, st)

    in_specs = [pl.BlockSpec((S, wc), lambda b, c: (b, c)), pl.BlockSpec((n2, 2 * n2), fix2)]
    args = [a, f2]
    aliases = {}
    if prev is not None:
        in_specs.append(pl.BlockSpec(memory_space=pl.ANY))
        args.append(prev)
        aliases = {2: 0}
    return pl.pallas_call(
        functools.partial(_fft_stage2_kernel, n1=n1, n2=n2),
        grid=(B, W // wc),
        in_specs=in_specs,
        out_specs=pl.BlockSpec((S, wc // 2), lambda b, c: (s0 + b, c)),
        out_shape=jax.ShapeDtypeStruct((T, W // 2), jnp.uint32),
        input_output_aliases=aliases,
        compiler_params=_cparams(("parallel", "parallel")),
        name="fft_stage2",
    )(*args)


def _dft_mats(n, scale, split=64):
    split = split if n % split == 0 else 1
    k = jnp.arange(n, dtype=jnp.int32)[None, :]
    j1 = jnp.arange(n // split, dtype=jnp.int32)[:, None]
    j2 = jnp.arange(split, dtype=jnp.int32)[:, None]
    w = 2.0 * math.pi / n
    ang_a = ((split * j1 * k) % n).astype(F32) * w
    ang_b = ((j2 * k) % n).astype(F32) * w
    ca, sa = jnp.cos(ang_a)[:, None, :], jnp.sin(ang_a)[:, None, :]
    cb, sb = (jnp.cos(ang_b) * scale)[None, :, :], (jnp.sin(ang_b) * scale)[None, :, :]
    c = (ca * cb - sa * sb).reshape(n, n)
    s = (sa * cb + ca * sb).reshape(n, n)
    return c, s


_NT = (((1,), (1,)), ((), ()))
_TN = (((0,), (0,)), ((), ()))


def _gla_block(q_ref, k_ref, v_ref, lr_ref, w3_ref, b_ref, st_ref, reverse, qscale, nchunk):
    R = nchunk * CHUNK
    lr = lr_ref[...]
    lr_hi = lr.astype(BF16)
    lr_lo = (lr - lr_hi.astype(F32)).astype(BF16)
    z = jnp.dot(jnp.concatenate([lr_hi, lr_lo, lr_hi], axis=1), w3_ref[...],
                preferred_element_type=F32) + b_ref[...]
    g = (jnp.minimum(z, 0.0) - jnp.log(1.0 + jnp.exp(-jnp.abs(z)))) * (1.0 / GATE_LOGIT_NORMALIZER)
    ri = lax.broadcasted_iota(jnp.int32, (R, R), 0)
    ci = lax.broadcasted_iota(jnp.int32, (R, R), 1)
    cum = ((ci >= ri) if reverse else (ci <= ri)).astype(BF16)
    g_hi = g.astype(BF16)
    g_lo = (g - g_hi.astype(F32)).astype(BF16)
    G = jnp.dot(cum, g_hi, preferred_element_type=F32) + jnp.dot(cum, g_lo, preferred_element_type=F32)

    dk = G.shape[1]
    zero_row = jnp.zeros((1, dk), F32)
    if reverse:
        starts = [G[(c + 1) * CHUNK:(c + 1) * CHUNK + 1, :] if c + 1 < nchunk else zero_row for c in range(nchunk)]
        ref_row, g_tot = CHUNK // 2, G[0:1, :]
    else:
        starts = [G[c * CHUNK - 1:c * CHUNK, :] if c > 0 else zero_row for c in range(nchunk)]
        ref_row, g_tot = CHUNK // 2 - 1, G[R - 1:R, :]
    bcast = lambda rows_: jnp.concatenate([jnp.broadcast_to(r_, (CHUNK, dk)) for r_ in rows_], axis=0)
    gc = G - bcast(starts)
    gref = bcast([gc[c * CHUNK + ref_row:c * CHUNK + ref_row + 1, :] for c in range(nchunk)])

    q = q_ref[...].astype(F32) * qscale
    k = k_ref[...].astype(F32)
    v = v_ref[...]
    q_in = (q * jnp.exp(gc - gref)).astype(BF16)
    k_in = (k * jnp.exp(gref - gc)).astype(BF16)
    q_it = (q * jnp.exp(gc)).astype(BF16)
    q_st = (q * jnp.exp(G)).astype(BF16)
    k_st = (k * jnp.exp(g_tot - G)).astype(BF16)

    s_diag = lax.dot_general(q_in, k_in, _NT, preferred_element_type=F32)
    same = (ri // CHUNK) == (ci // CHUNK)
    keep = same & ((ci > ri) if reverse else (ci <= ri))
    s_rows = []
    for c in range(nchunk):
        rows = slice(c * CHUNK, (c + 1) * CHUNK)
        s = jnp.where(keep[rows, :], s_diag[rows, :], 0.0)
        lo_, hi_ = ((c + 1) * CHUNK, R) if reverse else (0, c * CHUNK)
        if hi_ > lo_:
            kx = (k[lo_:hi_, :] * jnp.exp(starts[c] - G[lo_:hi_, :])).astype(BF16)
            pad = jnp.zeros((R - (hi_ - lo_), dk), BF16)
            kx = jnp.concatenate([pad, kx] if reverse else [kx, pad], axis=0)
            s = s + lax.dot_general(q_it[rows, :], kx, _NT, preferred_element_type=F32)
        s_rows.append(s.astype(BF16))
    scores = jnp.concatenate(s_rows, axis=0)

    st = st_ref[...]
    o = (jnp.dot(scores, v, preferred_element_type=F32)
         + lax.dot_general(q_st, st.astype(BF16), _NT, preferred_element_type=F32))
    st_ref[...] = st * jnp.exp(g_tot) + lax.dot_general(v, k_st, _TN, preferred_element_type=F32)
    return o


def _gla_kernel(qf_ref, kf_ref, vf_ref, qb_ref, kb_ref, vb_ref, lrf_ref, lrb_ref,
                wf_ref, bf_ref, wb_ref, bb_ref, *rest, nchunk, nsub, qscale):
    o_ref, acc_ref, stf_ref, stb_ref = rest[-4:]
    n = pl.program_id(2)
    nb = pl.num_programs(2)
    blk = nchunk * CHUNK
    rows = nsub * blk

    @pl.when(n == 0)
    def _():
        acc_ref[...] = jnp.zeros_like(acc_ref)
        stf_ref[...] = jnp.zeros_like(stf_ref)
        stb_ref[...] = jnp.zeros_like(stb_ref)

    for s_f in range(nsub):
        s_b = nsub - 1 - s_f
        sub_f, sub_b = pl.ds(s_f * blk, blk), pl.ds(s_b * blk, blk)
        o_f = _gla_block(qf_ref.at[sub_f, :], kf_ref.at[sub_f, :], vf_ref.at[sub_f, :], lrf_ref.at[sub_f, :],
                         wf_ref, bf_ref, stf_ref, False, qscale, nchunk)
        o_b = _gla_block(qb_ref.at[sub_b, :], kb_ref.at[sub_b, :], vb_ref.at[sub_b, :], lrb_ref.at[sub_b, :],
                         wb_ref, bb_ref, stb_ref, True, qscale, nchunk)
        acc_ref[pl.ds(pl.multiple_of(n * rows + s_f * blk, blk), blk), :] += o_f
        acc_ref[pl.ds(pl.multiple_of((nb - 1 - n) * rows + s_b * blk, blk), blk), :] += o_b

    @pl.when(n == nb - 1)
    def _():
        o_ref[...] = acc_ref[...].astype(o_ref.dtype)


def _hi_hi_lo(w):
    hi = w.astype(BF16)
    lo = (w - hi.astype(F32)).astype(BF16)
    return jnp.concatenate([hi, hi, lo], axis=0)


def _gla(proj, lr_f, lr_b, wup_f, b_f, wup_b, b_b, row0, B, S, q_blk0, k_blk0, v_blk0, dk, dv,
         prev=None, blk=256, nsub=4):
    T = proj.shape[0]
    nsub = nsub if S % (nsub * blk) == 0 else 1
    rows = nsub * blk
    assert S % rows == 0 and row0 % S == 0
    nb = S // rows
    rb0, sb0 = row0 // rows, row0 // S
    fmap = lambda b, n: rb0 + b * nb + n
    bmap = lambda b, n: rb0 + b * nb + (nb - 1 - n)
    in_specs = [
        pl.BlockSpec((rows, dk), lambda b, h, n: (fmap(b, n), q_blk0 + h)),
        pl.BlockSpec((rows, dk), lambda b, h, n: (fmap(b, n), k_blk0 + h)),
        pl.BlockSpec((rows, dv), lambda b, h, n: (fmap(b, n), v_blk0 + h)),
        pl.BlockSpec((rows, dk), lambda b, h, n: (bmap(b, n), q_blk0 + h)),
        pl.BlockSpec((rows, dk), lambda b, h, n: (bmap(b, n), k_blk0 + h)),
        pl.BlockSpec((rows, dv), lambda b, h, n: (bmap(b, n), v_blk0 + h)),
        pl.BlockSpec((rows, GATE_LOW_RANK), lambda b, h, n: (fmap(b, n), 0)),
        pl.BlockSpec((rows, GATE_LOW_RANK), lambda b, h, n: (bmap(b, n), 0)),
        pl.BlockSpec((3 * GATE_LOW_RANK, dk), lambda b, h, n: (0, h)),
        pl.BlockSpec((1, dk), lambda b, h, n: (0, h)),
        pl.BlockSpec((3 * GATE_LOW_RANK, dk), lambda b, h, n: (0, h)),
        pl.BlockSpec((1, dk), lambda b, h, n: (0, h)),
    ]
    args = [proj, proj, proj, proj, proj, proj, lr_f, lr_b,
            _hi_hi_lo(wup_f), b_f.reshape(1, -1), _hi_hi_lo(wup_b), b_b.reshape(1, -1)]
    aliases = {}
    if prev is not None:
        in_specs.append(pl.BlockSpec(memory_space=pl.ANY))
        args.append(prev)
        aliases = {len(args) - 1: 0}
    return pl.pallas_call(
        functools.partial(_gla_kernel, nchunk=blk // CHUNK, nsub=nsub, qscale=dk ** -0.5),
        grid=(B, GLA_HEADS, nb),
        in_specs=in_specs,
        out_specs=pl.BlockSpec((S, dv), lambda b, h, n: (sb0 + b, h)),
        out_shape=jax.ShapeDtypeStruct((T, GLA_HEADS * dv), BF16),
        scratch_shapes=[pltpu.VMEM((S, dv), F32), pltpu.VMEM((dv, dk), F32), pltpu.VMEM((dv, dk), F32)],
        input_output_aliases=aliases,
        compiler_params=_cparams(("parallel", "parallel", "arbitrary")),
        name="gla",
    )(*args)


def _merge_kernel(fft_ref, o_ref_in, og_ref, g0_ref, g1_ref, hn_ref, wf_ref, wg_ref, o_ref, a_ref, *, dv):
    half = _FFT_COLS // 2
    pieces = []
    for cb in range(fft_ref.shape[1] // half):
        pieces.extend(_unpack_bf16_pair(fft_ref[:, cb * half:(cb + 1) * half]))
    ya = jnp.dot(jnp.concatenate(pieces, axis=1).astype(BF16), wf_ref[...], preferred_element_type=F32)
    for h in range(GLA_HEADS):
        cs = slice(h * dv, (h + 1) * dv)
        o = o_ref_in[:, cs].astype(F32)
        var = jnp.mean(o * o, axis=-1, keepdims=True)
        on = o * lax.rsqrt(var + EPS) * hn_ref[...]
        og = og_ref[:, cs].astype(F32)
        a_ref[:, cs] = (on * (og * jax.nn.sigmoid(og))).astype(BF16)
    yb = jnp.dot(a_ref[...], wg_ref[...], preferred_element_type=F32)
    m = jax.nn.sigmoid(g0_ref[...].astype(F32)) * ya + jax.nn.sigmoid(g1_ref[...].astype(F32)) * yb
    o_ref[...] = m.astype(BF16)


def _merge(fft, o_gla, proj, og_blk, g0_blk, g1_blk, hn, wf, wg, tm=256):
    T, D = o_gla.shape
    FW = wf.shape[0]
    dv = D // GLA_HEADS
    const = dict(pipeline_mode=pl.Buffered(1))
    return pl.pallas_call(
        functools.partial(_merge_kernel, dv=dv),
        grid=(T // tm,),
        in_specs=[
            pl.BlockSpec((tm, FW // 2), lambda i: (i, 0)),
            pl.BlockSpec((tm, D), lambda i: (i, 0)),
            pl.BlockSpec((tm, D), lambda i: (i, og_blk)),
            pl.BlockSpec((tm, D), lambda i: (i, g0_blk)),
            pl.BlockSpec((tm, D), lambda i: (i, g1_blk)),
            pl.BlockSpec((1, dv), lambda i: (0, 0)),
            pl.BlockSpec((FW, D), lambda i: (0, 0), **const),
            pl.BlockSpec((D, D), lambda i: (0, 0), **const),
        ],
        out_specs=pl.BlockSpec((tm, D), lambda i: (i, 0)),
        out_shape=jax.ShapeDtypeStruct((T, D), BF16),
        scratch_shapes=[pltpu.VMEM((tm, D), BF16)],
        compiler_params=_cparams(("parallel",)),
        name="merge",
    )(fft, o_gla, proj, proj, proj, hn.reshape(1, dv), wf, wg)


HI16 = 0xFFFF0000


def _pack_bf16_pair(lo, hi):
    lo_bits = lax.bitcast_convert_type(lo.astype(BF16).astype(F32), jnp.uint32)
    hi_bits = lax.bitcast_convert_type(hi.astype(BF16).astype(F32), jnp.uint32)
    return (hi_bits & jnp.uint32(HI16)) | (lo_bits >> 16)


def _unpack_bf16_pair(w):
    lo = lax.bitcast_convert_type(w << 16, F32)
    hi = lax.bitcast_convert_type(w & jnp.uint32(HI16), F32)
    return lo, hi


def _store_tile_rows(ref, val):
    tm = val.shape[0]
    for s in range(SUBLANES):
        ref[pl.ds(s, tm, stride=SUBLANES), :] = val[:, s * LANES:(s + 1) * LANES]


def _load_tile_rows(ref, tm, s):
    return ref[pl.ds(s, tm, stride=SUBLANES), :]


def _outproj_router_kernel(m_ref, xp_ref, xs_ref, wo_ref, g_ref, wr_ref, br_ref,
                           x1_ref, xn_ref, idx_ref, tw_ref, rank_ref, cnt_ref, run_ref, *, n_exp, n0):
    i = pl.program_id(0)

    @pl.when(i == 0)
    def _():
        run_ref[...] = jnp.zeros_like(run_ref)

    tm, D = m_ref.shape
    x = jnp.where(i < n0, xp_ref[...], xs_ref[...])
    x1 = x + jnp.dot(m_ref[...], wo_ref[...], preferred_element_type=F32)
    x1_ref[...] = x1
    var = jnp.mean(x1 * x1, axis=-1, keepdims=True)
    xn = x1 * lax.rsqrt(var + EPS) * g_ref[...]
    _store_tile_rows(xn_ref, _pack_bf16_pair(xn[:, :D // 2], xn[:, D // 2:]))
    xh = xn.astype(BF16)
    xl = (xn - xh.astype(F32)).astype(BF16)
    r = jnp.dot(xh, wr_ref[...], preferred_element_type=F32)
    lg = (r[:, :n_exp] + r[:, n_exp:] + jnp.dot(xl, wr_ref[:, :n_exp], preferred_element_type=F32)
          + br_ref[...])

    lane = lax.broadcasted_iota(jnp.int32, (tm, n_exp), 1)
    vals, hots = [], []
    for _ in range(TOP_K):
        mx = jnp.max(lg, axis=-1, keepdims=True)
        ik = jnp.min(jnp.where(lg == mx, lane, n_exp), axis=-1, keepdims=True)
        hot = lane == ik
        vals.append(mx)
        hots.append(hot)
        lg = jnp.where(hot, -jnp.inf, lg)
    exps = [jnp.exp(v - vals[0]) for v in vals]
    denom = exps[0] + exps[1] + exps[2] + exps[3]

    sel = hots[0] | hots[1] | hots[2] | hots[3]
    sel_f = sel.astype(F32)
    r = lax.broadcasted_iota(jnp.int32, (tm, tm), 0)
    c = lax.broadcasted_iota(jnp.int32, (tm, tm), 1)
    strict = (c < r).astype(BF16)
    before = jnp.dot(strict, sel_f.astype(BF16), preferred_element_type=F32) + run_ref[...]
    run_ref[...] += jnp.sum(sel_f, axis=0, keepdims=True)
    cnt_ref[...] = run_ref[...].astype(jnp.int32)

    k4 = lax.broadcasted_iota(jnp.int32, (tm, TOP_K), 1)
    idx4 = jnp.zeros((tm, TOP_K), jnp.int32)
    w4 = jnp.zeros((tm, TOP_K), F32)
    rk4 = jnp.zeros((tm, TOP_K), jnp.int32)
    for k in range(TOP_K):
        ik = jnp.sum(jnp.where(hots[k], lane, 0), axis=-1, keepdims=True)
        rk = jnp.sum(jnp.where(hots[k], before, 0.0), axis=-1, keepdims=True).astype(jnp.int32)
        idx4 = jnp.where(k4 == k, ik, idx4)
        w4 = jnp.where(k4 == k, exps[k] / denom, w4)
        rk4 = jnp.where(k4 == k, rk, rk4)
    idx_ref[...] = idx4
    tw_ref[...] = w4
    rank_ref[...] = rk4


def _outproj_router(merged, xp, xs, wo, gain, wr2, br, tm=256):
    T, D = merged.shape
    E = wr2.shape[1] // 2
    n0 = xp.shape[0] // tm
    const = dict(pipeline_mode=pl.Buffered(1))
    row = lambda i: (i, 0)
    fix = lambda i: (0, 0)
    return pl.pallas_call(
        functools.partial(_outproj_router_kernel, n_exp=E, n0=n0),
        grid=(T // tm,),
        in_specs=[
            pl.BlockSpec((tm, D), row),
            pl.BlockSpec((tm, D), lambda i: (jnp.minimum(i, n0 - 1), 0)),
            pl.BlockSpec((tm, D), lambda i: (jnp.maximum(i - n0, 0), 0)),
            pl.BlockSpec((D, D), fix, **const),
            pl.BlockSpec((1, D), fix),
            pl.BlockSpec((D, 2 * E), fix),
            pl.BlockSpec((1, E), fix),
        ],
        out_specs=[
            pl.BlockSpec((tm, D), row),
            pl.BlockSpec((tm * SUBLANES, D // 2 // SUBLANES), row),
            pl.BlockSpec((tm, TOP_K), row),
            pl.BlockSpec((tm, TOP_K), row),
            pl.BlockSpec((tm, TOP_K), row),
            pl.BlockSpec((1, E), fix),
        ],
        out_shape=[
            jax.ShapeDtypeStruct((T, D), F32),
            jax.ShapeDtypeStruct((T * SUBLANES, D // 2 // SUBLANES), jnp.uint32),
            jax.ShapeDtypeStruct((T, TOP_K), jnp.int32),
            jax.ShapeDtypeStruct((T, TOP_K), F32),
            jax.ShapeDtypeStruct((T, TOP_K), jnp.int32),
            jax.ShapeDtypeStruct((1, E), jnp.int32),
        ],
        scratch_shapes=[pltpu.VMEM((1, E), F32)],
        compiler_params=_cparams(("arbitrary",)),
        name="outproj_router",
    )(merged, xp, xs, wo, gain.reshape(1, D), wr2, br.reshape(1, E))


def _dispatch_kernel(pos_ref, x_ref, xs_ref, dst_ref, sem, *, n_tok):
    i = pl.program_id(0)
    tm = x_ref.shape[0] // SUBLANES

    def body(t, carry):
        for k in range(TOP_K):
            p = pos_ref[t * TOP_K + k]
            dst_ref[p] = k * n_tok + i * tm + t
            pltpu.make_async_copy(x_ref.at[pl.ds(pl.multiple_of(t * SUBLANES, SUBLANES), SUBLANES), :],
                                  xs_ref.at[pl.ds(pl.multiple_of(p * SUBLANES, SUBLANES), SUBLANES), :], sem).start()
        return carry

    lax.fori_loop(0, tm, body, 0)
    for _ in range(TOP_K):
        pltpu.make_async_copy(x_ref, xs_ref.at[pl.ds(0, tm * SUBLANES), :], sem).wait()


def _dispatch(xn, pos_flat, n_slots, tm=256):
    T = xn.shape[0] // SUBLANES
    return pl.pallas_call(
        functools.partial(_dispatch_kernel, n_tok=T),
        grid=(T // tm,),
        in_specs=[
            pl.BlockSpec((tm * TOP_K,), lambda i: (i,), memory_space=pltpu.SMEM),
            pl.BlockSpec((tm * SUBLANES, LANES), lambda i: (i, 0)),
        ],
        out_specs=[
            pl.BlockSpec(memory_space=pl.ANY),
            pl.BlockSpec((n_slots,), lambda i: (0,), memory_space=pltpu.SMEM),
        ],
        out_shape=[jax.ShapeDtypeStruct((n_slots * SUBLANES, LANES), xn.dtype),
                   jax.ShapeDtypeStruct((n_slots,), jnp.int32)],
        scratch_shapes=[pltpu.SemaphoreType.DMA(())],
        compiler_params=_cparams(("arbitrary",)),
        name="dispatch",
    )(pos_flat, xn)


def _combine_kernel(tw_ref, x1_ref, g_ref, *rest, n0):
    y_refs, (op_ref, os_ref) = rest[:TOP_K], rest[TOP_K:]
    i = pl.program_id(0)
    tw = tw_ref[...]
    tm = x1_ref.shape[0]
    lo_acc = [None] * SUBLANES
    hi_acc = [None] * SUBLANES
    for k in range(TOP_K):
        wk = tw[:, k:k + 1]
        for s in range(SUBLANES):
            lo, hi = _unpack_bf16_pair(_load_tile_rows(y_refs[k], tm, s))
            lo_acc[s] = wk * lo if k == 0 else lo_acc[s] + wk * lo
            hi_acc[s] = wk * hi if k == 0 else hi_acc[s] + wk * hi
    x2 = x1_ref[...] + jnp.concatenate(lo_acc + hi_acc, axis=1)
    var = jnp.mean(x2 * x2, axis=-1, keepdims=True)
    y = x2 * lax.rsqrt(var + EPS) * g_ref[...]

    @pl.when(i < n0)
    def _():
        op_ref[...] = y

    @pl.when(i >= n0)
    def _():
        os_ref[...] = y


def _combine(tw, x1, gain, yk, t_prompt, tm=256):
    T, D = x1.shape
    n0 = t_prompt // tm
    nblk = T // tm
    y_spec = lambda k: pl.BlockSpec((tm * SUBLANES, LANES), lambda i: (k * nblk + i, 0))
    return pl.pallas_call(
        functools.partial(_combine_kernel, n0=n0),
        grid=(T // tm,),
        in_specs=[
            pl.BlockSpec((tm, TOP_K), lambda i: (i, 0)),
            pl.BlockSpec((tm, D), lambda i: (i, 0)),
            pl.BlockSpec((1, D), lambda i: (0, 0)),
        ] + [y_spec(k) for k in range(TOP_K)],
        out_specs=[
            pl.BlockSpec((tm, D), lambda i: (jnp.minimum(i, n0 - 1), 0)),
            pl.BlockSpec((tm, D), lambda i: (jnp.maximum(i - n0, 0), 0)),
        ],
        out_shape=[jax.ShapeDtypeStruct((t_prompt, D), F32), jax.ShapeDtypeStruct((T - t_prompt, D), F32)],
        compiler_params=_cparams(("arbitrary",)),
        name="combine",
    )(tw, x1, gain.reshape(1, D), *([yk] * TOP_K))


def _gate_up_kernel(g0_ref, nt_ref, cnt_ref, w_ref, bg_ref, bu_ref, x_hbm, o_hbm,
                    wp_ref, xin_ref, xb_ref, obuf_ref, sin, sout, *, tn):
    j, e = pl.program_id(0), pl.program_id(1)
    tm = GROUP_TILE
    D = xb_ref.shape[1]
    half = MXU_DIM // 2
    n = nt_ref[e]
    row_base = g0_ref[e]
    cnt = cnt_ref[e]

    def in_copy(r, slot):
        rows = pl.ds(pl.multiple_of((row_base + r * tm) * SUBLANES, tm * SUBLANES), tm * SUBLANES)
        return pltpu.make_async_copy(x_hbm.at[rows, :], xin_ref.at[slot], sin.at[slot])

    def out_copy(r, slot):
        rows = pl.ds(pl.multiple_of(row_base + r * tm, tm), tm)
        cols = pl.ds(pl.multiple_of(j * (tn // 2), LANES), tn // 2)
        return pltpu.make_async_copy(obuf_ref.at[slot], o_hbm.at[rows, cols], sout.at[slot])

    def body(r, carry):
        slot = r % 2

        @pl.when(r + 1 < n)
        def _():
            in_copy(r + 1, 1 - slot).start()

        in_copy(r, slot).wait()

        @pl.when(r >= 2)
        def _():
            out_copy(r - 2, slot).wait()

        valid = (r * tm + lax.broadcasted_iota(jnp.int32, (tm, 1), 0)) < cnt
        for s in range(SUBLANES):
            lo, hi = _unpack_bf16_pair(_load_tile_rows(xin_ref.at[slot], tm, s))
            xb_ref[:, s * LANES:(s + 1) * LANES] = jnp.where(valid, lo, 0.0).astype(BF16)
            xb_ref[:, D // 2 + s * LANES:D // 2 + (s + 1) * LANES] = jnp.where(valid, hi, 0.0).astype(BF16)
        for cb in range(tn // MXU_DIM):
            h = jnp.dot(xb_ref[...], wp_ref[:, cb * MXU_DIM:(cb + 1) * MXU_DIM], preferred_element_type=F32)
            hg = h[:, :half] + bg_ref[0, :, cb * half:(cb + 1) * half]
            hu = h[:, half:] + bu_ref[0, :, cb * half:(cb + 1) * half]
            gate = jnp.minimum(hg, SWIGLU_LIMIT)
            up = jnp.clip(hu, -SWIGLU_LIMIT, SWIGLU_LIMIT)
            act = gate * jax.nn.sigmoid(SWIGLU_ALPHA * gate) * (up + 1.0)
            obuf_ref[slot, :, cb * half:(cb + 1) * half] = act.astype(obuf_ref.dtype)
        out_copy(r, slot).start()
        return carry

    @pl.when(n > 0)
    def _():
        in_copy(0, 0).start()
        r_i = lax.broadcasted_iota(jnp.int32, (MXU_DIM, MXU_DIM), 0)
        c_i = lax.broadcasted_iota(jnp.int32, (MXU_DIM, MXU_DIM), 1)
        perm = (((c_i < half) & (r_i == 2 * c_i)) | ((c_i >= half) & (r_i == 2 * (c_i - half) + 1))).astype(BF16)
        for cb in range(tn // MXU_DIM):
            for rb in range(D // 512):
                w = w_ref[0, rb * 512:(rb + 1) * 512, cb * MXU_DIM:(cb + 1) * MXU_DIM].astype(BF16)
                wp_ref[rb * 512:(rb + 1) * 512, cb * MXU_DIM:(cb + 1) * MXU_DIM] = jnp.dot(
                    w, perm, preferred_element_type=F32).astype(BF16)
        lax.fori_loop(0, n, body, 0)

        @pl.when(n >= 2)
        def _():
            out_copy(n - 2, n % 2).wait()

        out_copy(n - 1, (n - 1) % 2).wait()


def _gate_up(g0, ntiles, cnt, xs, w_gate_up, bg, bu, tn):
    P = xs.shape[0] // SUBLANES
    E, D, H2 = w_gate_up.shape
    tm = GROUP_TILE
    wmap = lambda j, e, *_: (e, 0, j)
    gs = pltpu.PrefetchScalarGridSpec(
        num_scalar_prefetch=3,
        grid=(H2 // tn, E),
        in_specs=[
            pl.BlockSpec((1, D, tn), wmap),
            pl.BlockSpec((1, 1, tn // 2), wmap),
            pl.BlockSpec((1, 1, tn // 2), wmap),
            pl.BlockSpec(memory_space=pl.ANY),
        ],
        out_specs=pl.BlockSpec(memory_space=pl.ANY),
        scratch_shapes=[
            pltpu.VMEM((D, tn), BF16),
            pltpu.VMEM((2, tm * SUBLANES, LANES), jnp.uint32),
            pltpu.VMEM((tm, D), BF16),
            pltpu.VMEM((2, tm, tn // 2), BF16),
            pltpu.SemaphoreType.DMA((2,)),
            pltpu.SemaphoreType.DMA((2,)),
        ],
    )
    return pl.pallas_call(
        functools.partial(_gate_up_kernel, tn=tn),
        grid_spec=gs,
        out_shape=jax.ShapeDtypeStruct((P, H2 // 2), BF16),
        compiler_params=_cparams(("arbitrary", "arbitrary")),
        name="moe_gate_up",
    )(g0, ntiles, cnt, w_gate_up, bg, bu, xs)


def _down_kernel(g0_ref, nt_ref, cnt_ref, dst_ref, w_ref, b_ref, a_hbm, y_hbm, wb_ref, ain_ref, obuf_ref, sin, sout):
    e = pl.program_id(0)
    tm = GROUP_TILE
    H, D = wb_ref.shape
    n = nt_ref[e]
    row_base = g0_ref[e]
    cnt = cnt_ref[e]
    n_real = y_hbm.shape[0] // SUBLANES - dst_ref.shape[0]

    def in_copy(r, slot):
        rows = pl.ds(pl.multiple_of(row_base + r * tm, tm), tm)
        return pltpu.make_async_copy(a_hbm.at[rows, :], ain_ref.at[slot], sin.at[slot])

    nblk = (D // 2) // MXU_DIM

    def compute(slot, out_ref, before_block=None):
        a = ain_ref[slot]
        for bi, c0 in enumerate(range(0, D // 2, MXU_DIM)):
            if before_block is not None:
                before_block(bi)
            c1 = D // 2 + c0
            lo = jnp.dot(a, wb_ref[:, c0:c0 + MXU_DIM], preferred_element_type=F32) + b_ref[0, :, c0:c0 + MXU_DIM]
            hi = jnp.dot(a, wb_ref[:, c1:c1 + MXU_DIM], preferred_element_type=F32) + b_ref[0, :, c1:c1 + MXU_DIM]
            packed = _pack_bf16_pair(lo, hi)
            for u in range(MXU_DIM // LANES):
                out_ref[pl.ds(c0 // LANES + u, tm, stride=SUBLANES), :] = packed[:, u * LANES:(u + 1) * LANES]

    def scatter_rows(q, so, part):
        base = row_base + q * tm
        for i in range(part * (tm // nblk), (part + 1) * (tm // nblk)):
            d = jnp.where(q * tm + i < cnt, dst_ref[base + i], n_real + base + i)
            pltpu.make_async_copy(obuf_ref.at[so, pl.ds(i * SUBLANES, SUBLANES), :],
                                  y_hbm.at[pl.ds(pl.multiple_of(d * SUBLANES, SUBLANES), SUBLANES), :],
                                  sout.at[so]).start()

    def scatter_tile(q, so):
        for part in range(nblk):
            scatter_rows(q, so, part)

    def wait_scatter(so):
        pltpu.make_async_copy(obuf_ref.at[so], y_hbm.at[pl.ds(0, tm * SUBLANES), :], sout.at[so]).wait()

    def body(r, carry):
        slot = r % 2

        @pl.when(r + 1 < n)
        def _():
            in_copy(r + 1, 1 - slot).start()

        in_copy(r, slot).wait()

        @pl.when(r >= 2)
        def _():
            wait_scatter(slot)

        compute(slot, obuf_ref.at[slot], functools.partial(scatter_rows, r - 1, 1 - slot))
        return carry

    @pl.when(n > 0)
    def _():
        in_copy(0, 0).start()
        for rb in range(H // 512):
            wb_ref[rb * 512:(rb + 1) * 512, :] = w_ref[0, rb * 512:(rb + 1) * 512, :].astype(BF16)

        @pl.when(n > 1)
        def _():
            in_copy(1, 1).start()

        in_copy(0, 0).wait()
        compute(0, obuf_ref.at[0])
        lax.fori_loop(1, n, body, 0)
        so = (n - 1) % 2
        scatter_tile(n - 1, so)
        wait_scatter(so)

        @pl.when(n >= 2)
        def _():
            wait_scatter(1 - so)


def _down(g0, ntiles, cnt, slot_dst, n_rows, act, wd, bd):
    P, H = act.shape
    E, _, D = wd.shape
    tm = GROUP_TILE
    wmap = lambda e, *_: (e, 0, 0)
    gs = pltpu.PrefetchScalarGridSpec(
        num_scalar_prefetch=4,
        grid=(E,),
        in_specs=[
            pl.BlockSpec((1, H, D), wmap),
            pl.BlockSpec((1, 1, D), wmap),
            pl.BlockSpec(memory_space=pl.ANY),
        ],
        out_specs=pl.BlockSpec(memory_space=pl.ANY),
        scratch_shapes=[
            pltpu.VMEM((H, D), BF16),
            pltpu.VMEM((2, tm, H), BF16),
            pltpu.VMEM((2, tm * SUBLANES, LANES), jnp.uint32),
            pltpu.SemaphoreType.DMA((2,)),
            pltpu.SemaphoreType.DMA((2,)),
        ],
    )
    return pl.pallas_call(
        _down_kernel,
        grid_spec=gs,
        out_shape=jax.ShapeDtypeStruct((n_rows * SUBLANES, LANES), jnp.uint32),
        compiler_params=_cparams(("arbitrary",)),
        name="moe_down",
    )(g0, ntiles, cnt, slot_dst, wd, bd, act)


def _trunk(xp, xs, seq_shapes, norm_mix, w_in, w_gk_up_fwd, b_gk_fwd, w_gk_up_bwd, b_gk_bwd, gla_head_norm,
           w_fnet_out, w_gla_out, w_out, norm_ffn, w_router, b_router, w_gate_up, b_gate_up,
           w_down, b_down, norm_final):
    D = xp.shape[1]
    T = xp.shape[0] + xs.shape[0]
    fw = w_fnet_out.shape[0]
    dkk = w_gk_up_fwd.shape[1]
    dvv = w_gla_out.shape[0]
    dk, dv = dkk // GLA_HEADS, dvv // GLA_HEADS
    sizes = (fw, dkk, dkk, dvv, dvv, GATE_LOW_RANK, GATE_LOW_RANK, 2 * D)
    offs = np.concatenate([[0], np.cumsum(sizes)])
    span = lambda n: (int(offs[n]), int(offs[n + 1]))
    w_main, w_lr2 = _repack_w_in(w_in, tuple(span(n) for n in (4, 7, 3, 0, 1, 2)), (int(offs[5]), int(offs[7])))
    og_blk, g0_blk, g1_blk = 0, dvv // D, dvv // D + 1
    v_off = dvv + 2 * D
    u_off = v_off + dvv
    q_off = u_off + fw
    k_off = q_off + dkk

    proj, lr = _inproj(xp, xs, norm_mix, w_main, w_lr2)
    lr_f, lr_b = lr[:, :GATE_LOW_RANK], lr[:, GATE_LOW_RANK:]

    gd = fw // FNET_GROUPS
    cc, sc = _dft_mats(gd, gd ** -0.5)
    cs = jnp.concatenate([cc, sc], axis=1).astype(BF16)
    z = _chan_dft(proj, u_off // fw, fw, cs)
    fft, o_gla = None, None
    row0 = 0
    for (B, S) in seq_shapes:
        fft = _seq_dft(z, row0, B, S, prev=fft)
        o_gla = _gla(proj, lr_f, lr_b, w_gk_up_fwd, b_gk_fwd, w_gk_up_bwd, b_gk_bwd, row0, B, S,
                     q_off // dk, k_off // dk, v_off // dv, dk, dv, prev=o_gla)
        row0 += B * S

    merged = _merge(fft, o_gla, proj, og_blk, g0_blk, g1_blk, gla_head_norm,
                    w_fnet_out.astype(BF16), w_gla_out.astype(BF16))
    x1, xn2, idx, tw, rank, cnt = _outproj_router(merged, xp, xs, w_out.astype(BF16), norm_ffn,
                                                  _hi_lo(w_router), b_router)

    E = w_router.shape[1]
    cnt = cnt.reshape(E)
    gsz = ((cnt + GROUP_TILE - 1) // GROUP_TILE) * GROUP_TILE
    gend = jnp.cumsum(gsz)
    gstart = gend - gsz
    pos = (gstart[idx] + rank).reshape(-1).astype(jnp.int32)
    n_slots = T * TOP_K + E * GROUP_TILE
    g0 = gstart.astype(jnp.int32)
    ntiles = (gsz // GROUP_TILE).astype(jnp.int32)

    x_sorted, slot_dst = _dispatch(xn2, pos, n_slots)
    H = w_down.shape[1]
    bg = b_gate_up[:, 0::2].reshape(E, 1, H)
    bu = b_gate_up[:, 1::2].reshape(E, 1, H)
    act = _gate_up(g0, ntiles, cnt, x_sorted, w_gate_up, bg, bu, GATE_UP_TN)
    yk = _down(g0, ntiles, cnt, slot_dst, T * TOP_K + n_slots, act, w_down, b_down.reshape(E, 1, D))
    return _combine(tw, x1, norm_final, yk, xp.shape[0])


def kernel(x_prompt, x_sample, norm_mix, w_in, w_gk_up_fwd, b_gk_fwd, w_gk_up_bwd, b_gk_bwd, gla_head_norm,
           w_fnet_out, w_gla_out, w_out, norm_ffn, w_router, b_router, w_gate_up, b_gate_up, w_down,
           b_down, norm_final):
    D = x_prompt.shape[-1]
    shapes = (x_prompt.shape[:2], x_sample.shape[:2])
    yp, ys = _trunk(x_prompt.reshape(-1, D), x_sample.reshape(-1, D), shapes, norm_mix[0], w_in[0], w_gk_up_fwd[0], b_gk_fwd[0], w_gk_up_bwd[0], b_gk_bwd[0],
               gla_head_norm[0], w_fnet_out[0], w_gla_out[0], w_out[0], norm_ffn[0], w_router[0],
               b_router[0], w_gate_up[0], b_gate_up[0], w_down[0], b_down[0], norm_final)
    return (yp.reshape(x_prompt.shape), ys.reshape(x_sample.shape))
```

```python
import functools
import math

import numpy as np
import jax
import jax.numpy as jnp
from jax import lax
from jax.experimental import pallas as pl
from jax.experimental.pallas import tpu as pltpu

F32 = jnp.float32
BF16 = jnp.bfloat16
HIGHEST = lax.Precision.HIGHEST

EPS = 1e-5
FNET_GROUPS = 4
GLA_HEADS = 4
GATE_LOW_RANK = 16
GATE_LOGIT_NORMALIZER = 16.0
CHUNK = 64
TOP_K = 4
SWIGLU_LIMIT = 7.0
SWIGLU_ALPHA = 1.702

VMEM_LIMIT_BYTES = 56 * 1024 * 1024
MXU_DIM = 256
LANES = 128
SUBLANES = 8
GROUP_TILE = 512
GATE_UP_TN = 2048


def _cparams(sem):
    return pltpu.CompilerParams(dimension_semantics=sem, vmem_limit_bytes=VMEM_LIMIT_BYTES)


def _split3(x):
    hi = x.astype(BF16)
    r = x - hi.astype(F32)
    mid = r.astype(BF16)
    lo = (r - mid.astype(F32)).astype(BF16)
    return hi, mid, lo


def _hi_lo(w):
    hi = w.astype(BF16)
    lo = (w - hi.astype(F32)).astype(BF16)
    return jnp.concatenate([hi, lo], axis=1)


def _repack_kernel(w_ref, o_ref, lr_ref, *, pieces, lr_cols):
    c0 = 0
    for a, b in pieces:
        o_ref[:, c0:c0 + (b - a)] = w_ref[:, a:b].astype(BF16)
        c0 += b - a
    w = w_ref[:, lr_cols[0]:lr_cols[1]]
    hi = w.astype(BF16)
    n = lr_cols[1] - lr_cols[0]
    lr_ref[:, :n] = hi
    lr_ref[:, n:] = (w - hi.astype(F32)).astype(BF16)


def _repack_w_in(w_in, pieces, lr_cols, tr=256):
    K, N = w_in.shape
    n_main = sum(b - a for a, b in pieces)
    n_lr = lr_cols[1] - lr_cols[0]
    return pl.pallas_call(
        functools.partial(_repack_kernel, pieces=pieces, lr_cols=lr_cols),
        grid=(K // tr,),
        in_specs=[pl.BlockSpec((tr, N), lambda i: (i, 0))],
        out_specs=[pl.BlockSpec((tr, n_main), lambda i: (i, 0)), pl.BlockSpec((tr, 2 * n_lr), lambda i: (i, 0))],
        out_shape=[jax.ShapeDtypeStruct((K, n_main), BF16), jax.ShapeDtypeStruct((K, 2 * n_lr), BF16)],
        compiler_params=_cparams(("parallel",)),
        name="repack_w_in",
    )(w_in)


def _inproj_kernel(xp_ref, xs_ref, g_ref, w_ref, wlr_ref, o_ref, lr_ref, xn_ref, *, n0):
    @pl.when(pl.program_id(1) == 0)
    def _():
        x = jnp.where(pl.program_id(0) < n0, xp_ref[...], xs_ref[...])
        var = jnp.mean(x * x, axis=-1, keepdims=True)
        xn = (x * lax.rsqrt(var + EPS) * g_ref[...]).astype(BF16)
        xn_ref[...] = xn
        r = jnp.dot(xn, wlr_ref[...], preferred_element_type=F32)
        nlr = lr_ref.shape[1]
        lr_ref[...] = r[:, :nlr] + r[:, nlr:]

    o_ref[...] = jnp.dot(xn_ref[...], w_ref[...], preferred_element_type=F32).astype(o_ref.dtype)


def _inproj(xp, xs, gain, w_main, w_lr2, tm=1024, tn=1024):
    D = xp.shape[1]
    T = xp.shape[0] + xs.shape[0]
    tm = tm if xp.shape[0] % tm == 0 and xs.shape[0] % tm == 0 else tm // 2
    n0 = xp.shape[0] // tm
    N = w_main.shape[1]
    R = w_lr2.shape[1] // 2
    return pl.pallas_call(
        functools.partial(_inproj_kernel, n0=n0),
        grid=(T // tm, N // tn),
        in_specs=[
            pl.BlockSpec((tm, D), lambda i, j: (jnp.minimum(i, n0 - 1), 0), pipeline_mode=pl.Buffered(1)),
            pl.BlockSpec((tm, D), lambda i, j: (jnp.maximum(i - n0, 0), 0), pipeline_mode=pl.Buffered(1)),
            pl.BlockSpec((1, D), lambda i, j: (0, 0)),
            pl.BlockSpec((D, tn), lambda i, j: (0, j)),
            pl.BlockSpec((D, 2 * R), lambda i, j: (0, 0)),
        ],
        out_specs=[
            pl.BlockSpec((tm, tn), lambda i, j: (i, j)),
            pl.BlockSpec((tm, R), lambda i, j: (i, 0)),
        ],
        out_shape=[jax.ShapeDtypeStruct((T, N), BF16), jax.ShapeDtypeStruct((T, R), F32)],
        scratch_shapes=[pltpu.VMEM((tm, D), BF16)],
        compiler_params=_cparams(("parallel", "arbitrary")),
        name="inproj",
    )(xp, xs, gain.reshape(1, D), w_main, w_lr2)


def _chan_dft_kernel(u_ref, cs_ref, z_ref, *, gd):
    for g in range(FNET_GROUPS):
        r = jnp.dot(u_ref[:, g * gd:(g + 1) * gd], cs_ref[...], preferred_element_type=F32)
        z_ref[:, g * gd:(g + 1) * gd] = _pack_bf16_pair(r[:, :gd], r[:, gd:])


def _chan_dft(proj, u_col_block, width, cs, tm=512):
    T = proj.shape[0]
    gd = width // FNET_GROUPS
    return pl.pallas_call(
        functools.partial(_chan_dft_kernel, gd=gd),
        grid=(T // tm,),
        in_specs=[
            pl.BlockSpec((tm, width), lambda i: (i, u_col_block)),
            pl.BlockSpec((gd, 2 * gd), lambda i: (0, 0)),
        ],
        out_specs=pl.BlockSpec((tm, width), lambda i: (i, 0)),
        out_shape=jax.ShapeDtypeStruct((T, width), jnp.uint32),
        compiler_params=_cparams(("parallel",)),
        name="chan_dft",
    )(proj, cs)


_FFT_COLS = 2 * LANES


def _fft_stage1_kernel(z_ref, f1_ref, ct_ref, st_ref, a_ref, *, n1, n2):
    for m in range(n2):
        rows = pl.ds(m, n1, stride=n2)
        zc, zs = _unpack_bf16_pair(z_ref[rows, :])
        pc = jnp.dot(f1_ref[...], zc.astype(BF16), preferred_element_type=F32)
        ps = jnp.dot(f1_ref[...], zs.astype(BF16), preferred_element_type=F32)
        a_re = pc[:n1] - ps[n1:]
        a_im = -ps[:n1] - pc[n1:]
        ct = ct_ref[m][:, 0:1]
        st = st_ref[m][:, 0:1]
        a_ref[rows, :] = _pack_bf16_pair(a_re * ct + a_im * st, a_im * ct - a_re * st)


def _fft_stage2_kernel(a_ref, f2_ref, *rest, n1, n2):
    o_ref = rest[-1]
    half = _FFT_COLS // 2
    for k1 in range(n1):
        a_re, a_im = _unpack_bf16_pair(a_ref[pl.ds(k1 * n2, n2), :])
        rhs = jnp.concatenate([a_re.astype(BF16), a_im.astype(BF16)], axis=0)
        x = jnp.dot(f2_ref[...], rhs, preferred_element_type=F32)
        o_ref[pl.ds(k1, n2, stride=n1), :] = _pack_bf16_pair(x[:, :half], x[:, half:])


def _seq_dft(z, row0, B, S, prev=None):
    T, W = z.shape
    wc = _FFT_COLS
    n2 = 64 if S % (64 * SUBLANES) == 0 else S // SUBLANES
    n1 = S // n2
    assert n1 * n2 == S and row0 % S == 0 and W % wc == 0
    s0 = row0 // S
    i1 = jnp.arange(n1, dtype=jnp.int32)
    i2 = jnp.arange(n2, dtype=jnp.int32)
    ang1 = ((i1[:, None] * i1[None, :]) % n1).astype(F32) * (2.0 * math.pi / n1)
    f1 = (jnp.concatenate([jnp.cos(ang1), jnp.sin(ang1)], axis=0) * S ** -0.5).astype(BF16)
    ang2 = ((i2[:, None] * i2[None, :]) % n2).astype(F32) * (2.0 * math.pi / n2)
    f2 = jnp.concatenate([jnp.cos(ang2), jnp.sin(ang2)], axis=1).astype(BF16)
    angt = (i2[:, None] * i1[None, :]).astype(F32) * (2.0 * math.pi / S)
    ct = jnp.broadcast_to(jnp.cos(angt)[:, :, None], (n2, n1, LANES))
    st = jnp.broadcast_to(jnp.sin(angt)[:, :, None], (n2, n1, LANES))

    fix2 = lambda b, c: (0, 0)
    fix3 = lambda b, c: (0, 0, 0)

    a = pl.pallas_call(
        functools.partial(_fft_stage1_kernel, n1=n1, n2=n2),
        grid=(B, W // LANES),
        in_specs=[pl.BlockSpec((S, LANES), lambda b, c: (s0 + b, c)),
                  pl.BlockSpec((2 * n1, n1), fix2),
                  pl.BlockSpec((n2, n1, LANES), fix3),
                  pl.BlockSpec((n2, n1, LANES), fix3)],
        out_specs=pl.BlockSpec((S, LANES), lambda b, c: (b, c)),
        out_shape=jax.ShapeDtypeStruct((B * S, W), jnp.uint32),
        compiler_params=_cparams(("parallel", "parallel")),
        name="fft_stage1",
    )(z, f1, ct, st)

    in_specs = [pl.BlockSpec((S, wc), lambda b, c: (b, c)), pl.BlockSpec((n2, 2 * n2), fix2)]
    args = [a, f2]
    aliases = {}
    if prev is not None:
        in_specs.append(pl.BlockSpec(memory_space=pl.ANY))
        args.append(prev)
        aliases = {2: 0}
    return pl.pallas_call(
        functools.partial(_fft_stage2_kernel, n1=n1, n2=n2),
        grid=(B, W // wc),
        in_specs=in_specs,
        out_specs=pl.BlockSpec((S, wc // 2), lambda b, c: (s0 + b, c)),
        out_shape=jax.ShapeDtypeStruct((T, W // 2), jnp.uint32),
        input_output_aliases=aliases,
        compiler_params=_cparams(("parallel", "parallel")),
        name="fft_stage2",
    )(*args)


def _dft_mats(n, scale, split=64):
    split = split if n % split == 0 else 1
    k = jnp.arange(n, dtype=jnp.int32)[None, :]
    j1 = jnp.arange(n // split, dtype=jnp.int32)[:, None]
    j2 = jnp.arange(split, dtype=jnp.int32)[:, None]
    w = 2.0 * math.pi / n
    ang_a = ((split * j1 * k) % n).astype(F32) * w
    ang_b = ((j2 * k) % n).astype(F32) * w
    ca, sa = jnp.cos(ang_a)[:, None, :], jnp.sin(ang_a)[:, None, :]
    cb, sb = (jnp.cos(ang_b) * scale)[None, :, :], (jnp.sin(ang_b) * scale)[None, :, :]
    c = (ca * cb - sa * sb).reshape(n, n)
    s = (sa * cb + ca * sb).reshape(n, n)
    return c, s


_NT = (((1,), (1,)), ((), ()))
_TN = (((0,), (0,)), ((), ()))


def _gla_block(q_ref, k_ref, v_ref, lr_ref, w3_ref, b_ref, st_ref, reverse, qscale, nchunk):
    R = nchunk * CHUNK
    lr = lr_ref[...]
    lr_hi = lr.astype(BF16)
    lr_lo = (lr - lr_hi.astype(F32)).astype(BF16)
    z = jnp.dot(jnp.concatenate([lr_hi, lr_lo, lr_hi], axis=1), w3_ref[...],
                preferred_element_type=F32) + b_ref[...]
    g = (jnp.minimum(z, 0.0) - jnp.log(1.0 + jnp.exp(-jnp.abs(z)))) * (1.0 / GATE_LOGIT_NORMALIZER)
    ri = lax.broadcasted_iota(jnp.int32, (R, R), 0)
    ci = lax.broadcasted_iota(jnp.int32, (R, R), 1)
    cum = ((ci >= ri) if reverse else (ci <= ri)).astype(BF16)
    g_hi = g.astype(BF16)
    g_lo = (g - g_hi.astype(F32)).astype(BF16)
    G = jnp.dot(cum, g_hi, preferred_element_type=F32) + jnp.dot(cum, g_lo, preferred_element_type=F32)

    dk = G.shape[1]
    zero_row = jnp.zeros((1, dk), F32)
    if reverse:
        starts = [G[(c + 1) * CHUNK:(c + 1) * CHUNK + 1, :] if c + 1 < nchunk else zero_row for c in range(nchunk)]
        ref_row, g_tot = CHUNK // 2, G[0:1, :]
    else:
        starts = [G[c * CHUNK - 1:c * CHUNK, :] if c > 0 else zero_row for c in range(nchunk)]
        ref_row, g_tot = CHUNK // 2 - 1, G[R - 1:R, :]
    bcast = lambda rows_: jnp.concatenate([jnp.broadcast_to(r_, (CHUNK, dk)) for r_ in rows_], axis=0)
    gc = G - bcast(starts)
    gref = bcast([gc[c * CHUNK + ref_row:c * CHUNK + ref_row + 1, :] for c in range(nchunk)])

    q = q_ref[...].astype(F32) * qscale
    k = k_ref[...].astype(F32)
    v = v_ref[...]
    q_in = (q * jnp.exp(gc - gref)).astype(BF16)
    k_in = (k * jnp.exp(gref - gc)).astype(BF16)
    q_it = (q * jnp.exp(gc)).astype(BF16)
    q_st = (q * jnp.exp(G)).astype(BF16)
    k_st = (k * jnp.exp(g_tot - G)).astype(BF16)

    s_diag = lax.dot_general(q_in, k_in, _NT, preferred_element_type=F32)
    same = (ri // CHUNK) == (ci // CHUNK)
    keep = same & ((ci > ri) if reverse else (ci <= ri))
    s_rows = []
    for c in range(nchunk):
        rows = slice(c * CHUNK, (c + 1) * CHUNK)
        s = jnp.where(keep[rows, :], s_diag[rows, :], 0.0)
        lo_, hi_ = ((c + 1) * CHUNK, R) if reverse else (0, c * CHUNK)
        if hi_ > lo_:
            kx = (k[lo_:hi_, :] * jnp.exp(starts[c] - G[lo_:hi_, :])).astype(BF16)
            pad = jnp.zeros((R - (hi_ - lo_), dk), BF16)
            kx = jnp.concatenate([pad, kx] if reverse else [kx, pad], axis=0)
            s = s + lax.dot_general(q_it[rows, :], kx, _NT, preferred_element_type=F32)
        s_rows.append(s.astype(BF16))
    scores = jnp.concatenate(s_rows, axis=0)

    st = st_ref[...]
    o = (jnp.dot(scores, v, preferred_element_type=F32)
         + lax.dot_general(q_st, st.astype(BF16), _NT, preferred_element_type=F32))
    st_ref[...] = st * jnp.exp(g_tot) + lax.dot_general(v, k_st, _TN, preferred_element_type=F32)
    return o


def _gla_kernel(qf_ref, kf_ref, vf_ref, qb_ref, kb_ref, vb_ref, lrf_ref, lrb_ref,
                wf_ref, bf_ref, wb_ref, bb_ref, *rest, nchunk, nsub, qscale):
    o_ref, acc_ref, stf_ref, stb_ref = rest[-4:]
    n = pl.program_id(2)
    nb = pl.num_programs(2)
    blk = nchunk * CHUNK
    rows = nsub * blk

    @pl.when(n == 0)
    def _():
        acc_ref[...] = jnp.zeros_like(acc_ref)
        stf_ref[...] = jnp.zeros_like(stf_ref)
        stb_ref[...] = jnp.zeros_like(stb_ref)

    for s_f in range(nsub):
        s_b = nsub - 1 - s_f
        sub_f, sub_b = pl.ds(s_f * blk, blk), pl.ds(s_b * blk, blk)
        o_f = _gla_block(qf_ref.at[sub_f, :], kf_ref.at[sub_f, :], vf_ref.at[sub_f, :], lrf_ref.at[sub_f, :],
                         wf_ref, bf_ref, stf_ref, False, qscale, nchunk)
        o_b = _gla_block(qb_ref.at[sub_b, :], kb_ref.at[sub_b, :], vb_ref.at[sub_b, :], lrb_ref.at[sub_b, :],
                         wb_ref, bb_ref, stb_ref, True, qscale, nchunk)
        acc_ref[pl.ds(pl.multiple_of(n * rows + s_f * blk, blk), blk), :] += o_f
        acc_ref[pl.ds(pl.multiple_of((nb - 1 - n) * rows + s_b * blk, blk), blk), :] += o_b

    @pl.when(n == nb - 1)
    def _():
        o_ref[...] = acc_ref[...].astype(o_ref.dtype)


def _hi_hi_lo(w):
    hi = w.astype(BF16)
    lo = (w - hi.astype(F32)).astype(BF16)
    return jnp.concatenate([hi, hi, lo], axis=0)


def _gla(proj, lr_f, lr_b, wup_f, b_f, wup_b, b_b, row0, B, S, q_blk0, k_blk0, v_blk0, dk, dv,
         prev=None, blk=256, nsub=4):
    T = proj.shape[0]
    nsub = nsub if S % (nsub * blk) == 0 else 1
    rows = nsub * blk
    assert S % rows == 0 and row0 % S == 0
    nb = S // rows
    rb0, sb0 = row0 // rows, row0 // S
    fmap = lambda b, n: rb0 + b * nb + n
    bmap = lambda b, n: rb0 + b * nb + (nb - 1 - n)
    in_specs = [
        pl.BlockSpec((rows, dk), lambda b, h, n: (fmap(b, n), q_blk0 + h)),
        pl.BlockSpec((rows, dk), lambda b, h, n: (fmap(b, n), k_blk0 + h)),
        pl.BlockSpec((rows, dv), lambda b, h, n: (fmap(b, n), v_blk0 + h)),
        pl.BlockSpec((rows, dk), lambda b, h, n: (bmap(b, n), q_blk0 + h)),
        pl.BlockSpec((rows, dk), lambda b, h, n: (bmap(b, n), k_blk0 + h)),
        pl.BlockSpec((rows, dv), lambda b, h, n: (bmap(b, n), v_blk0 + h)),
        pl.BlockSpec((rows, GATE_LOW_RANK), lambda b, h, n: (fmap(b, n), 0)),
        pl.BlockSpec((rows, GATE_LOW_RANK), lambda b, h, n: (bmap(b, n), 0)),
        pl.BlockSpec((3 * GATE_LOW_RANK, dk), lambda b, h, n: (0, h)),
        pl.BlockSpec((1, dk), lambda b, h, n: (0, h)),
        pl.BlockSpec((3 * GATE_LOW_RANK, dk), lambda b, h, n: (0, h)),
        pl.BlockSpec((1, dk), lambda b, h, n: (0, h)),
    ]
    args = [proj, proj, proj, proj, proj, proj, lr_f, lr_b,
            _hi_hi_lo(wup_f), b_f.reshape(1, -1), _hi_hi_lo(wup_b), b_b.reshape(1, -1)]
    aliases = {}
    if prev is not None:
        in_specs.append(pl.BlockSpec(memory_space=pl.ANY))
        args.append(prev)
        aliases = {len(args) - 1: 0}
    return pl.pallas_call(
        functools.partial(_gla_kernel, nchunk=blk // CHUNK, nsub=nsub, qscale=dk ** -0.5),
        grid=(B, GLA_HEADS, nb),
        in_specs=in_specs,
        out_specs=pl.BlockSpec((S, dv), lambda b, h, n: (sb0 + b, h)),
        out_shape=jax.ShapeDtypeStruct((T, GLA_HEADS * dv), BF16),
        scratch_shapes=[pltpu.VMEM((S, dv), F32), pltpu.VMEM((dv, dk), F32), pltpu.VMEM((dv, dk), F32)],
        input_output_aliases=aliases,
        compiler_params=_cparams(("parallel", "parallel", "arbitrary")),
        name="gla",
    )(*args)


def _merge_kernel(fft_ref, o_ref_in, og_ref, g0_ref, g1_ref, hn_ref, wf_ref, wg_ref, o_ref, a_ref, *, dv):
    half = _FFT_COLS // 2
    pieces = []
    for cb in range(fft_ref.shape[1] // half):
        pieces.extend(_unpack_bf16_pair(fft_ref[:, cb * half:(cb + 1) * half]))
    ya = jnp.dot(jnp.concatenate(pieces, axis=1).astype(BF16), wf_ref[...], preferred_element_type=F32)
    for h in range(GLA_HEADS):
        cs = slice(h * dv, (h + 1) * dv)
        o = o_ref_in[:, cs].astype(F32)
        var = jnp.mean(o * o, axis=-1, keepdims=True)
        on = o * lax.rsqrt(var + EPS) * hn_ref[...]
        og = og_ref[:, cs].astype(F32)
        a_ref[:, cs] = (on * (og * jax.nn.sigmoid(og))).astype(BF16)
    yb = jnp.dot(a_ref[...], wg_ref[...], preferred_element_type=F32)
    m = jax.nn.sigmoid(g0_ref[...].astype(F32)) * ya + jax.nn.sigmoid(g1_ref[...].astype(F32)) * yb
    o_ref[...] = m.astype(BF16)


def _merge(fft, o_gla, proj, og_blk, g0_blk, g1_blk, hn, wf, wg, tm=512):
    T, D = o_gla.shape
    FW = wf.shape[0]
    dv = D // GLA_HEADS
    const = dict(pipeline_mode=pl.Buffered(1))
    return pl.pallas_call(
        functools.partial(_merge_kernel, dv=dv),
        grid=(T // tm,),
        in_specs=[
            pl.BlockSpec((tm, FW // 2), lambda i: (i, 0)),
            pl.BlockSpec((tm, D), lambda i: (i, 0)),
            pl.BlockSpec((tm, D), lambda i: (i, og_blk)),
            pl.BlockSpec((tm, D), lambda i: (i, g0_blk)),
            pl.BlockSpec((tm, D), lambda i: (i, g1_blk)),
            pl.BlockSpec((1, dv), lambda i: (0, 0)),
            pl.BlockSpec((FW, D), lambda i: (0, 0), **const),
            pl.BlockSpec((D, D), lambda i: (0, 0), **const),
        ],
        out_specs=pl.BlockSpec((tm, D), lambda i: (i, 0)),
        out_shape=jax.ShapeDtypeStruct((T, D), BF16),
        scratch_shapes=[pltpu.VMEM((tm, D), BF16)],
        compiler_params=_cparams(("parallel",)),
        name="merge",
    )(fft, o_gla, proj, proj, proj, hn.reshape(1, dv), wf, wg)


HI16 = 0xFFFF0000


def _pack_bf16_pair(lo, hi):
    lo_bits = lax.bitcast_convert_type(lo.astype(BF16).astype(F32), jnp.uint32)
    hi_bits = lax.bitcast_convert_type(hi.astype(BF16).astype(F32), jnp.uint32)
    return (hi_bits & jnp.uint32(HI16)) | (lo_bits >> 16)


def _unpack_bf16_pair(w):
    lo = lax.bitcast_convert_type(w << 16, F32)
    hi = lax.bitcast_convert_type(w & jnp.uint32(HI16), F32)
    return lo, hi


def _store_tile_rows(ref, val):
    tm = val.shape[0]
    for s in range(SUBLANES):
        ref[pl.ds(s, tm, stride=SUBLANES), :] = val[:, s * LANES:(s + 1) * LANES]


def _load_tile_rows(ref, tm, s):
    return ref[pl.ds(s, tm, stride=SUBLANES), :]


def _outproj_router_kernel(m_ref, xp_ref, xs_ref, wo_ref, g_ref, wr_ref, br_ref,
                           x1_ref, xn_ref, idx_ref, tw_ref, rank_ref, cnt_ref, run_ref, *, n_exp, n0):
    i = pl.program_id(0)

    @pl.when(i == 0)
    def _():
        run_ref[...] = jnp.zeros_like(run_ref)

    tm, D = m_ref.shape
    x = jnp.where(i < n0, xp_ref[...], xs_ref[...])
    x1 = x + jnp.dot(m_ref[...], wo_ref[...], preferred_element_type=F32)
    x1_ref[...] = x1
    var = jnp.mean(x1 * x1, axis=-1, keepdims=True)
    xn = x1 * lax.rsqrt(var + EPS) * g_ref[...]
    _store_tile_rows(xn_ref, _pack_bf16_pair(xn[:, :D // 2], xn[:, D // 2:]))
    xh = xn.astype(BF16)
    xl = (xn - xh.astype(F32)).astype(BF16)
    r = jnp.dot(xh, wr_ref[...], preferred_element_type=F32)
    lg = (r[:, :n_exp] + r[:, n_exp:] + jnp.dot(xl, wr_ref[:, :n_exp], preferred_element_type=F32)
          + br_ref[...])

    lane = lax.broadcasted_iota(jnp.int32, (tm, n_exp), 1)
    vals, hots = [], []
    for _ in range(TOP_K):
        mx = jnp.max(lg, axis=-1, keepdims=True)
        ik = jnp.min(jnp.where(lg == mx, lane, n_exp), axis=-1, keepdims=True)
        hot = lane == ik
        vals.append(mx)
        hots.append(hot)
        lg = jnp.where(hot, -jnp.inf, lg)
    exps = [jnp.exp(v - vals[0]) for v in vals]
    denom = exps[0] + exps[1] + exps[2] + exps[3]

    sel = hots[0] | hots[1] | hots[2] | hots[3]
    sel_f = sel.astype(F32)
    r = lax.broadcasted_iota(jnp.int32, (tm, tm), 0)
    c = lax.broadcasted_iota(jnp.int32, (tm, tm), 1)
    strict = (c < r).astype(BF16)
    before = jnp.dot(strict, sel_f.astype(BF16), preferred_element_type=F32) + run_ref[...]
    run_ref[...] += jnp.sum(sel_f, axis=0, keepdims=True)
    cnt_ref[...] = run_ref[...].astype(jnp.int32)

    k4 = lax.broadcasted_iota(jnp.int32, (tm, TOP_K), 1)
    idx4 = jnp.zeros((tm, TOP_K), jnp.int32)
    w4 = jnp.zeros((tm, TOP_K), F32)
    rk4 = jnp.zeros((tm, TOP_K), jnp.int32)
    for k in range(TOP_K):
        ik = jnp.sum(jnp.where(hots[k], lane, 0), axis=-1, keepdims=True)
        rk = jnp.sum(jnp.where(hots[k], before, 0.0), axis=-1, keepdims=True).astype(jnp.int32)
        idx4 = jnp.where(k4 == k, ik, idx4)
        w4 = jnp.where(k4 == k, exps[k] / denom, w4)
        rk4 = jnp.where(k4 == k, rk, rk4)
    idx_ref[...] = idx4
    tw_ref[...] = w4
    rank_ref[...] = rk4


def _outproj_router(merged, xp, xs, wo, gain, wr2, br, tm=512):
    T, D = merged.shape
    E = wr2.shape[1] // 2
    n0 = xp.shape[0] // tm
    const = dict(pipeline_mode=pl.Buffered(1))
    row = lambda i: (i, 0)
    fix = lambda i: (0, 0)
    return pl.pallas_call(
        functools.partial(_outproj_router_kernel, n_exp=E, n0=n0),
        grid=(T // tm,),
        in_specs=[
            pl.BlockSpec((tm, D), row),
            pl.BlockSpec((tm, D), lambda i: (jnp.minimum(i, n0 - 1), 0)),
            pl.BlockSpec((tm, D), lambda i: (jnp.maximum(i - n0, 0), 0)),
            pl.BlockSpec((D, D), fix, **const),
            pl.BlockSpec((1, D), fix),
            pl.BlockSpec((D, 2 * E), fix),
            pl.BlockSpec((1, E), fix),
        ],
        out_specs=[
            pl.BlockSpec((tm, D), row),
            pl.BlockSpec((tm * SUBLANES, D // 2 // SUBLANES), row),
            pl.BlockSpec((tm, TOP_K), row),
            pl.BlockSpec((tm, TOP_K), row),
            pl.BlockSpec((tm, TOP_K), row),
            pl.BlockSpec((1, E), fix),
        ],
        out_shape=[
            jax.ShapeDtypeStruct((T, D), F32),
            jax.ShapeDtypeStruct((T * SUBLANES, D // 2 // SUBLANES), jnp.uint32),
            jax.ShapeDtypeStruct((T, TOP_K), jnp.int32),
            jax.ShapeDtypeStruct((T, TOP_K), F32),
            jax.ShapeDtypeStruct((T, TOP_K), jnp.int32),
            jax.ShapeDtypeStruct((1, E), jnp.int32),
        ],
        scratch_shapes=[pltpu.VMEM((1, E), F32)],
        compiler_params=_cparams(("arbitrary",)),
        name="outproj_router",
    )(merged, xp, xs, wo, gain.reshape(1, D), wr2, br.reshape(1, E))


def _dispatch_kernel(pos_ref, x_ref, xs_ref, dst_ref, sem, *, n_tok):
    i = pl.program_id(0)
    tm = x_ref.shape[0] // SUBLANES

    def body(t, carry):
        for k in range(TOP_K):
            p = pos_ref[t * TOP_K + k]
            dst_ref[p] = k * n_tok + i * tm + t
            pltpu.make_async_copy(x_ref.at[pl.ds(pl.multiple_of(t * SUBLANES, SUBLANES), SUBLANES), :],
                                  xs_ref.at[pl.ds(pl.multiple_of(p * SUBLANES, SUBLANES), SUBLANES), :], sem).start()
        return carry

    lax.fori_loop(0, tm, body, 0)
    for _ in range(TOP_K):
        pltpu.make_async_copy(x_ref, xs_ref.at[pl.ds(0, tm * SUBLANES), :], sem).wait()


def _dispatch(xn, pos_flat, n_slots, tm=256):
    T = xn.shape[0] // SUBLANES
    return pl.pallas_call(
        functools.partial(_dispatch_kernel, n_tok=T),
        grid=(T // tm,),
        in_specs=[
            pl.BlockSpec((tm * TOP_K,), lambda i: (i,), memory_space=pltpu.SMEM),
            pl.BlockSpec((tm * SUBLANES, LANES), lambda i: (i, 0)),
        ],
        out_specs=[
            pl.BlockSpec(memory_space=pl.ANY),
            pl.BlockSpec((n_slots,), lambda i: (0,), memory_space=pltpu.SMEM),
        ],
        out_shape=[jax.ShapeDtypeStruct((n_slots * SUBLANES, LANES), xn.dtype),
                   jax.ShapeDtypeStruct((n_slots,), jnp.int32)],
        scratch_shapes=[pltpu.SemaphoreType.DMA(())],
        compiler_params=_cparams(("arbitrary",)),
        name="dispatch",
    )(pos_flat, xn)


def _combine_kernel(tw_ref, x1_ref, g_ref, *rest, n0):
    y_refs, (op_ref, os_ref) = rest[:TOP_K], rest[TOP_K:]
    i = pl.program_id(0)
    tw = tw_ref[...]
    tm = x1_ref.shape[0]
    lo_acc = [None] * SUBLANES
    hi_acc = [None] * SUBLANES
    for k in range(TOP_K):
        wk = tw[:, k:k + 1]
        for s in range(SUBLANES):
            lo, hi = _unpack_bf16_pair(_load_tile_rows(y_refs[k], tm, s))
            lo_acc[s] = wk * lo if k == 0 else lo_acc[s] + wk * lo
            hi_acc[s] = wk * hi if k == 0 else hi_acc[s] + wk * hi
    x2 = x1_ref[...] + jnp.concatenate(lo_acc + hi_acc, axis=1)
    var = jnp.mean(x2 * x2, axis=-1, keepdims=True)
    y = x2 * lax.rsqrt(var + EPS) * g_ref[...]

    @pl.when(i < n0)
    def _():
        op_ref[...] = y

    @pl.when(i >= n0)
    def _():
        os_ref[...] = y


def _combine(tw, x1, gain, yk, t_prompt, tm=256):
    T, D = x1.shape
    n0 = t_prompt // tm
    nblk = T // tm
    y_spec = lambda k: pl.BlockSpec((tm * SUBLANES, LANES), lambda i: (k * nblk + i, 0))
    return pl.pallas_call(
        functools.partial(_combine_kernel, n0=n0),
        grid=(T // tm,),
        in_specs=[
            pl.BlockSpec((tm, TOP_K), lambda i: (i, 0)),
            pl.BlockSpec((tm, D), lambda i: (i, 0)),
            pl.BlockSpec((1, D), lambda i: (0, 0)),
        ] + [y_spec(k) for k in range(TOP_K)],
        out_specs=[
            pl.BlockSpec((tm, D), lambda i: (jnp.minimum(i, n0 - 1), 0)),
            pl.BlockSpec((tm, D), lambda i: (jnp.maximum(i - n0, 0), 0)),
        ],
        out_shape=[jax.ShapeDtypeStruct((t_prompt, D), F32), jax.ShapeDtypeStruct((T - t_prompt, D), F32)],
        compiler_params=_cparams(("arbitrary",)),
        name="combine",
    )(tw, x1, gain.reshape(1, D), *([yk] * TOP_K))


def _gate_up_kernel(g0_ref, nt_ref, cnt_ref, w_ref, bg_ref, bu_ref, x_hbm, o_hbm,
                    wp_ref, xin_ref, xb_ref, obuf_ref, sin, sout, *, tn):
    j, e = pl.program_id(0), pl.program_id(1)
    tm = GROUP_TILE
    D = xb_ref.shape[1]
    half = MXU_DIM // 2
    n = nt_ref[e]
    row_base = g0_ref[e]
    cnt = cnt_ref[e]

    def in_copy(r, slot):
        rows = pl.ds(pl.multiple_of((row_base + r * tm) * SUBLANES, tm * SUBLANES), tm * SUBLANES)
        return pltpu.make_async_copy(x_hbm.at[rows, :], xin_ref.at[slot], sin.at[slot])

    def out_copy(r, slot):
        rows = pl.ds(pl.multiple_of(row_base + r * tm, tm), tm)
        cols = pl.ds(pl.multiple_of(j * (tn // 2), LANES), tn // 2)
        return pltpu.make_async_copy(obuf_ref.at[slot], o_hbm.at[rows, cols], sout.at[slot])

    def body(r, carry):
        slot = r % 2

        @pl.when(r + 1 < n)
        def _():
            in_copy(r + 1, 1 - slot).start()

        in_copy(r, slot).wait()

        @pl.when(r >= 2)
        def _():
            out_copy(r - 2, slot).wait()

        valid = (r * tm + lax.broadcasted_iota(jnp.int32, (tm, 1), 0)) < cnt
        for s in range(SUBLANES):
            lo, hi = _unpack_bf16_pair(_load_tile_rows(xin_ref.at[slot], tm, s))
            xb_ref[:, s * LANES:(s + 1) * LANES] = jnp.where(valid, lo, 0.0).astype(BF16)
            xb_ref[:, D // 2 + s * LANES:D // 2 + (s + 1) * LANES] = jnp.where(valid, hi, 0.0).astype(BF16)
        for cb in range(tn // MXU_DIM):
            h = jnp.dot(xb_ref[...], wp_ref[:, cb * MXU_DIM:(cb + 1) * MXU_DIM], preferred_element_type=F32)
            hg = h[:, :half] + bg_ref[0, :, cb * half:(cb + 1) * half]
            hu = h[:, half:] + bu_ref[0, :, cb * half:(cb + 1) * half]
            gate = jnp.minimum(hg, SWIGLU_LIMIT)
            up = jnp.clip(hu, -SWIGLU_LIMIT, SWIGLU_LIMIT)
            act = gate * jax.nn.sigmoid(SWIGLU_ALPHA * gate) * (up + 1.0)
            obuf_ref[slot, :, cb * half:(cb + 1) * half] = act.astype(obuf_ref.dtype)
        out_copy(r, slot).start()
        return carry

    @pl.when(n > 0)
    def _():
        in_copy(0, 0).start()
        r_i = lax.broadcasted_iota(jnp.int32, (MXU_DIM, MXU_DIM), 0)
        c_i = lax.broadcasted_iota(jnp.int32, (MXU_DIM, MXU_DIM), 1)
        perm = (((c_i < half) & (r_i == 2 * c_i)) | ((c_i >= half) & (r_i == 2 * (c_i - half) + 1))).astype(BF16)
        for cb in range(tn // MXU_DIM):
            for rb in range(D // 512):
                w = w_ref[0, rb * 512:(rb + 1) * 512, cb * MXU_DIM:(cb + 1) * MXU_DIM].astype(BF16)
                wp_ref[rb * 512:(rb + 1) * 512, cb * MXU_DIM:(cb + 1) * MXU_DIM] = jnp.dot(
                    w, perm, preferred_element_type=F32).astype(BF16)
        lax.fori_loop(0, n, body, 0)

        @pl.when(n >= 2)
        def _():
            out_copy(n - 2, n % 2).wait()

        out_copy(n - 1, (n - 1) % 2).wait()


def _gate_up(g0, ntiles, cnt, xs, w_gate_up, bg, bu, tn):
    P = xs.shape[0] // SUBLANES
    E, D, H2 = w_gate_up.shape
    tm = GROUP_TILE
    wmap = lambda j, e, *_: (e, 0, j)
    gs = pltpu.PrefetchScalarGridSpec(
        num_scalar_prefetch=3,
        grid=(H2 // tn, E),
        in_specs=[
            pl.BlockSpec((1, D, tn), wmap),
            pl.BlockSpec((1, 1, tn // 2), wmap),
            pl.BlockSpec((1, 1, tn // 2), wmap),
            pl.BlockSpec(memory_space=pl.ANY),
        ],
        out_specs=pl.BlockSpec(memory_space=pl.ANY),
        scratch_shapes=[
            pltpu.VMEM((D, tn), BF16),
            pltpu.VMEM((2, tm * SUBLANES, LANES), jnp.uint32),
            pltpu.VMEM((tm, D), BF16),
            pltpu.VMEM((2, tm, tn // 2), BF16),
            pltpu.SemaphoreType.DMA((2,)),
            pltpu.SemaphoreType.DMA((2,)),
        ],
    )
    return pl.pallas_call(
        functools.partial(_gate_up_kernel, tn=tn),
        grid_spec=gs,
        out_shape=jax.ShapeDtypeStruct((P, H2 // 2), BF16),
        compiler_params=_cparams(("arbitrary", "arbitrary")),
        name="moe_gate_up",
    )(g0, ntiles, cnt, w_gate_up, bg, bu, xs)


def _down_kernel(g0_ref, nt_ref, cnt_ref, dst_ref, w_ref, b_ref, a_hbm, y_hbm, wb_ref, ain_ref, obuf_ref, sin, sout):
    e = pl.program_id(0)
    tm = GROUP_TILE
    H, D = wb_ref.shape
    n = nt_ref[e]
    row_base = g0_ref[e]
    cnt = cnt_ref[e]
    n_real = y_hbm.shape[0] // SUBLANES - dst_ref.shape[0]

    def in_copy(r, slot):
        rows = pl.ds(pl.multiple_of(row_base + r * tm, tm), tm)
        return pltpu.make_async_copy(a_hbm.at[rows, :], ain_ref.at[slot], sin.at[slot])

    nblk = (D // 2) // MXU_DIM

    def compute(slot, out_ref, before_block=None):
        a = ain_ref[slot]
        for bi, c0 in enumerate(range(0, D // 2, MXU_DIM)):
            if before_block is not None:
                before_block(bi)
            c1 = D // 2 + c0
            lo = jnp.dot(a, wb_ref[:, c0:c0 + MXU_DIM], preferred_element_type=F32) + b_ref[0, :, c0:c0 + MXU_DIM]
            hi = jnp.dot(a, wb_ref[:, c1:c1 + MXU_DIM], preferred_element_type=F32) + b_ref[0, :, c1:c1 + MXU_DIM]
            packed = _pack_bf16_pair(lo, hi)
            for u in range(MXU_DIM // LANES):
                out_ref[pl.ds(c0 // LANES + u, tm, stride=SUBLANES), :] = packed[:, u * LANES:(u + 1) * LANES]

    def scatter_rows(q, so, part):
        base = row_base + q * tm
        for i in range(part * (tm // nblk), (part + 1) * (tm // nblk)):
            d = jnp.where(q * tm + i < cnt, dst_ref[base + i], n_real + base + i)
            pltpu.make_async_copy(obuf_ref.at[so, pl.ds(i * SUBLANES, SUBLANES), :],
                                  y_hbm.at[pl.ds(pl.multiple_of(d * SUBLANES, SUBLANES), SUBLANES), :],
                                  sout.at[so]).start()

    def scatter_tile(q, so):
        for part in range(nblk):
            scatter_rows(q, so, part)

    def wait_scatter(so):
        pltpu.make_async_copy(obuf_ref.at[so], y_hbm.at[pl.ds(0, tm * SUBLANES), :], sout.at[so]).wait()

    def body(r, carry):
        slot = r % 2

        @pl.when(r + 1 < n)
        def _():
            in_copy(r + 1, 1 - slot).start()

        in_copy(r, slot).wait()

        @pl.when(r >= 2)
        def _():
            wait_scatter(slot)

        compute(slot, obuf_ref.at[slot], functools.partial(scatter_rows, r - 1, 1 - slot))
        return carry

    @pl.when(n > 0)
    def _():
        in_copy(0, 0).start()
        for rb in range(H // 512):
            wb_ref[rb * 512:(rb + 1) * 512, :] = w_ref[0, rb * 512:(rb + 1) * 512, :].astype(BF16)

        @pl.when(n > 1)
        def _():
            in_copy(1, 1).start()

        in_copy(0, 0).wait()
        compute(0, obuf_ref.at[0])
        lax.fori_loop(1, n, body, 0)
        so = (n - 1) % 2
        scatter_tile(n - 1, so)
        wait_scatter(so)

        @pl.when(n >= 2)
        def _():
            wait_scatter(1 - so)


def _down(g0, ntiles, cnt, slot_dst, n_rows, act, wd, bd):
    P, H = act.shape
    E, _, D = wd.shape
    tm = GROUP_TILE
    wmap = lambda e, *_: (e, 0, 0)
    gs = pltpu.PrefetchScalarGridSpec(
        num_scalar_prefetch=4,
        grid=(E,),
        in_specs=[
            pl.BlockSpec((1, H, D), wmap),
            pl.BlockSpec((1, 1, D), wmap),
            pl.BlockSpec(memory_space=pl.ANY),
        ],
        out_specs=pl.BlockSpec(memory_space=pl.ANY),
        scratch_shapes=[
            pltpu.VMEM((H, D), BF16),
            pltpu.VMEM((2, tm, H), BF16),
            pltpu.VMEM((2, tm * SUBLANES, LANES), jnp.uint32),
            pltpu.SemaphoreType.DMA((2,)),
            pltpu.SemaphoreType.DMA((2,)),
        ],
    )
    return pl.pallas_call(
        _down_kernel,
        grid_spec=gs,
        out_shape=jax.ShapeDtypeStruct((n_rows * SUBLANES, LANES), jnp.uint32),
        compiler_params=_cparams(("arbitrary",)),
        name="moe_down",
    )(g0, ntiles, cnt, slot_dst, wd, bd, act)


def _trunk(xp, xs, seq_shapes, norm_mix, w_in, w_gk_up_fwd, b_gk_fwd, w_gk_up_bwd, b_gk_bwd, gla_head_norm,
           w_fnet_out, w_gla_out, w_out, norm_ffn, w_router, b_router, w_gate_up, b_gate_up,
           w_down, b_down, norm_final):
    D = xp.shape[1]
    T = xp.shape[0] + xs.shape[0]
    fw = w_fnet_out.shape[0]
    dkk = w_gk_up_fwd.shape[1]
    dvv = w_gla_out.shape[0]
    dk, dv = dkk // GLA_HEADS, dvv // GLA_HEADS
    sizes = (fw, dkk, dkk, dvv, dvv, GATE_LOW_RANK, GATE_LOW_RANK, 2 * D)
    offs = np.concatenate([[0], np.cumsum(sizes)])
    span = lambda n: (int(offs[n]), int(offs[n + 1]))
    w_main, w_lr2 = _repack_w_in(w_in, tuple(span(n) for n in (4, 7, 3, 0, 1, 2)), (int(offs[5]), int(offs[7])))
    og_blk, g0_blk, g1_blk = 0, dvv // D, dvv // D + 1
    v_off = dvv + 2 * D
    u_off = v_off + dvv
    q_off = u_off + fw
    k_off = q_off + dkk

    proj, lr = _inproj(xp, xs, norm_mix, w_main, w_lr2)
    lr_f, lr_b = lr[:, :GATE_LOW_RANK], lr[:, GATE_LOW_RANK:]

    gd = fw // FNET_GROUPS
    cc, sc = _dft_mats(gd, gd ** -0.5)
    cs = jnp.concatenate([cc, sc], axis=1).astype(BF16)
    z = _chan_dft(proj, u_off // fw, fw, cs)
    fft, o_gla = None, None
    row0 = 0
    for (B, S) in seq_shapes:
        fft = _seq_dft(z, row0, B, S, prev=fft)
        o_gla = _gla(proj, lr_f, lr_b, w_gk_up_fwd, b_gk_fwd, w_gk_up_bwd, b_gk_bwd, row0, B, S,
                     q_off // dk, k_off // dk, v_off // dv, dk, dv, prev=o_gla)
        row0 += B * S

    merged = _merge(fft, o_gla, proj, og_blk, g0_blk, g1_blk, gla_head_norm,
                    w_fnet_out.astype(BF16), w_gla_out.astype(BF16))
    x1, xn2, idx, tw, rank, cnt = _outproj_router(merged, xp, xs, w_out.astype(BF16), norm_ffn,
                                                  _hi_lo(w_router), b_router)

    E = w_router.shape[1]
    cnt = cnt.reshape(E)
    gsz = ((cnt + GROUP_TILE - 1) // GROUP_TILE) * GROUP_TILE
    gend = jnp.cumsum(gsz)
    gstart = gend - gsz
    pos = (gstart[idx] + rank).reshape(-1).astype(jnp.int32)
    n_slots = T * TOP_K + E * GROUP_TILE
    g0 = gstart.astype(jnp.int32)
    ntiles = (gsz // GROUP_TILE).astype(jnp.int32)

    x_sorted, slot_dst = _dispatch(xn2, pos, n_slots)
    H = w_down.shape[1]
    bg = b_gate_up[:, 0::2].reshape(E, 1, H)
    bu = b_gate_up[:, 1::2].reshape(E, 1, H)
    act = _gate_up(g0, ntiles, cnt, x_sorted, w_gate_up, bg, bu, GATE_UP_TN)
    yk = _down(g0, ntiles, cnt, slot_dst, T * TOP_K + n_slots, act, w_down, b_down.reshape(E, 1, D))
    return _combine(tw, x1, norm_final, yk, xp.shape[0])


def kernel(x_prompt, x_sample, norm_mix, w_in, w_gk_up_fwd, b_gk_fwd, w_gk_up_bwd, b_gk_bwd, gla_head_norm,
           w_fnet_out, w_gla_out, w_out, norm_ffn, w_router, b_router, w_gate_up, b_gate_up, w_down,
           b_down, norm_final):
    D = x_prompt.shape[-1]
    shapes = (x_prompt.shape[:2], x_sample.shape[:2])
    yp, ys = _trunk(x_prompt.reshape(-1, D), x_sample.reshape(-1, D), shapes, norm_mix[0], w_in[0], w_gk_up_fwd[0], b_gk_fwd[0], w_gk_up_bwd[0], b_gk_bwd[0],
               gla_head_norm[0], w_fnet_out[0], w_gla_out[0], w_out[0], norm_ffn[0], w_router[0],
               b_router[0], w_gate_up[0], b_gate_up[0], w_down[0], b_down[0], norm_final)
    return (yp.reshape(x_prompt.shape), ys.reshape(x_sample.shape))
```

```python
import functools
import math

import numpy as np
import jax
import jax.numpy as jnp
from jax import lax
from jax.experimental import pallas as pl
from jax.experimental.pallas import tpu as pltpu

F32 = jnp.float32
BF16 = jnp.bfloat16
HIGHEST = lax.Precision.HIGHEST

EPS = 1e-5
FNET_GROUPS = 4
GLA_HEADS = 4
GATE_LOW_RANK = 16
GATE_LOGIT_NORMALIZER = 16.0
CHUNK = 64
TOP_K = 4
SWIGLU_LIMIT = 7.0
SWIGLU_ALPHA = 1.702

VMEM_LIMIT_BYTES = 56 * 1024 * 1024
MXU_DIM = 256
LANES = 128
SUBLANES = 8
GROUP_TILE = 512
GATE_UP_TN = 2048


def _cparams(sem):
    return pltpu.CompilerParams(dimension_semantics=sem, vmem_limit_bytes=VMEM_LIMIT_BYTES)


def _split3(x):
    hi = x.astype(BF16)
    r = x - hi.astype(F32)
    mid = r.astype(BF16)
    lo = (r - mid.astype(F32)).astype(BF16)
    return hi, mid, lo


def _hi_lo(w):
    hi = w.astype(BF16)
    lo = (w - hi.astype(F32)).astype(BF16)
    return jnp.concatenate([hi, lo], axis=1)


def _repack_kernel(w_ref, o_ref, lr_ref, *, pieces, lr_cols):
    c0 = 0
    for a, b in pieces:
        o_ref[:, c0:c0 + (b - a)] = w_ref[:, a:b].astype(BF16)
        c0 += b - a
    w = w_ref[:, lr_cols[0]:lr_cols[1]]
    hi = w.astype(BF16)
    n = lr_cols[1] - lr_cols[0]
    lr_ref[:, :n] = hi
    lr_ref[:, n:] = (w - hi.astype(F32)).astype(BF16)


def _repack_w_in(w_in, pieces, lr_cols, tr=256):
    K, N = w_in.shape
    n_main = sum(b - a for a, b in pieces)
    n_lr = lr_cols[1] - lr_cols[0]
    return pl.pallas_call(
        functools.partial(_repack_kernel, pieces=pieces, lr_cols=lr_cols),
        grid=(K // tr,),
        in_specs=[pl.BlockSpec((tr, N), lambda i: (i, 0))],
        out_specs=[pl.BlockSpec((tr, n_main), lambda i: (i, 0)), pl.BlockSpec((tr, 2 * n_lr), lambda i: (i, 0))],
        out_shape=[jax.ShapeDtypeStruct((K, n_main), BF16), jax.ShapeDtypeStruct((K, 2 * n_lr), BF16)],
        compiler_params=_cparams(("parallel",)),
        name="repack_w_in",
    )(w_in)


def _inproj_kernel(xp_ref, xs_ref, g_ref, w_ref, wlr_ref, o_ref, lr_ref, xn_ref, *, n0):
    @pl.when(pl.program_id(1) == 0)
    def _():
        x = jnp.where(pl.program_id(0) < n0, xp_ref[...], xs_ref[...])
        var = jnp.mean(x * x, axis=-1, keepdims=True)
        xn = (x * lax.rsqrt(var + EPS) * g_ref[...]).astype(BF16)
        xn_ref[...] = xn
        r = jnp.dot(xn, wlr_ref[...], preferred_element_type=F32)
        nlr = lr_ref.shape[1]
        lr_ref[...] = r[:, :nlr] + r[:, nlr:]

    o_ref[...] = jnp.dot(xn_ref[...], w_ref[...], preferred_element_type=F32).astype(o_ref.dtype)


def _inproj(xp, xs, gain, w_main, w_lr2, tm=1024, tn=1024):
    D = xp.shape[1]
    T = xp.shape[0] + xs.shape[0]
    tm = tm if xp.shape[0] % tm == 0 and xs.shape[0] % tm == 0 else tm // 2
    n0 = xp.shape[0] // tm
    N = w_main.shape[1]
    R = w_lr2.shape[1] // 2
    return pl.pallas_call(
        functools.partial(_inproj_kernel, n0=n0),
        grid=(T // tm, N // tn),
        in_specs=[
            pl.BlockSpec((tm, D), lambda i, j: (jnp.minimum(i, n0 - 1), 0), pipeline_mode=pl.Buffered(1)),
            pl.BlockSpec((tm, D), lambda i, j: (jnp.maximum(i - n0, 0), 0), pipeline_mode=pl.Buffered(1)),
            pl.BlockSpec((1, D), lambda i, j: (0, 0)),
            pl.BlockSpec((D, tn), lambda i, j: (0, j)),
            pl.BlockSpec((D, 2 * R), lambda i, j: (0, 0)),
        ],
        out_specs=[
            pl.BlockSpec((tm, tn), lambda i, j: (i, j)),
            pl.BlockSpec((tm, R), lambda i, j: (i, 0)),
        ],
        out_shape=[jax.ShapeDtypeStruct((T, N), BF16), jax.ShapeDtypeStruct((T, R), F32)],
        scratch_shapes=[pltpu.VMEM((tm, D), BF16)],
        compiler_params=_cparams(("parallel", "arbitrary")),
        name="inproj",
    )(xp, xs, gain.reshape(1, D), w_main, w_lr2)


def _chan_dft_kernel(u_ref, cs_ref, z_ref, *, gd):
    for g in range(FNET_GROUPS):
        r = jnp.dot(u_ref[:, g * gd:(g + 1) * gd], cs_ref[...], preferred_element_type=F32)
        z_ref[:, g * gd:(g + 1) * gd] = _pack_bf16_pair(r[:, :gd], r[:, gd:])


def _chan_dft(proj, u_col_block, width, cs, tm=512):
    T = proj.shape[0]
    gd = width // FNET_GROUPS
    return pl.pallas_call(
        functools.partial(_chan_dft_kernel, gd=gd),
        grid=(T // tm,),
        in_specs=[
            pl.BlockSpec((tm, width), lambda i: (i, u_col_block)),
            pl.BlockSpec((gd, 2 * gd), lambda i: (0, 0)),
        ],
        out_specs=pl.BlockSpec((tm, width), lambda i: (i, 0)),
        out_shape=jax.ShapeDtypeStruct((T, width), jnp.uint32),
        compiler_params=_cparams(("parallel",)),
        name="chan_dft",
    )(proj, cs)


_FFT_COLS = 2 * LANES


def _fft_stage1_kernel(z_ref, f1_ref, ct_ref, st_ref, a_ref, *, n1, n2):
    for m in range(n2):
        rows = pl.ds(m, n1, stride=n2)
        zc, zs = _unpack_bf16_pair(z_ref[rows, :])
        pc = jnp.dot(f1_ref[...], zc.astype(BF16), preferred_element_type=F32)
        ps = jnp.dot(f1_ref[...], zs.astype(BF16), preferred_element_type=F32)
        a_re = pc[:n1] - ps[n1:]
        a_im = -ps[:n1] - pc[n1:]
        ct = ct_ref[m][:, 0:1]
        st = st_ref[m][:, 0:1]
        a_ref[rows, :] = _pack_bf16_pair(a_re * ct + a_im * st, a_im * ct - a_re * st)


def _fft_stage2_kernel(a_ref, f2_ref, *rest, n1, n2):
    o_ref = rest[-1]
    half = _FFT_COLS // 2
    for k1 in range(n1):
        a_re, a_im = _unpack_bf16_pair(a_ref[pl.ds(k1 * n2, n2), :])
        rhs = jnp.concatenate([a_re.astype(BF16), a_im.astype(BF16)], axis=0)
        x = jnp.dot(f2_ref[...], rhs, preferred_element_type=F32)
        o_ref[pl.ds(k1, n2, stride=n1), :] = _pack_bf16_pair(x[:, :half], x[:, half:])


def _seq_dft(z, row0, B, S, prev=None):
    T, W = z.shape
    wc = _FFT_COLS
    n2 = 64 if S % (64 * SUBLANES) == 0 else S // SUBLANES
    n1 = S // n2
    assert n1 * n2 == S and row0 % S == 0 and W % wc == 0
    s0 = row0 // S
    i1 = jnp.arange(n1, dtype=jnp.int32)
    i2 = jnp.arange(n2, dtype=jnp.int32)
    ang1 = ((i1[:, None] * i1[None, :]) % n1).astype(F32) * (2.0 * math.pi / n1)
    f1 = (jnp.concatenate([jnp.cos(ang1), jnp.sin(ang1)], axis=0) * S ** -0.5).astype(BF16)
    ang2 = ((i2[:, None] * i2[None, :]) % n2).astype(F32) * (2.0 * math.pi / n2)
    f2 = jnp.concatenate([jnp.cos(ang2), jnp.sin(ang2)], axis=1).astype(BF16)
    angt = (i2[:, None] * i1[None, :]).astype(F32) * (2.0 * math.pi / S)
    ct = jnp.broadcast_to(jnp.cos(angt)[:, :, None], (n2, n1, LANES))
    st = jnp.broadcast_to(jnp.sin(angt)[:, :, None], (n2, n1, LANES))

    fix2 = lambda b, c: (0, 0)
    fix3 = lambda b, c: (0, 0, 0)

    a = pl.pallas_call(
        functools.partial(_fft_stage1_kernel, n1=n1, n2=n2),
        grid=(B, W // LANES),
        in_specs=[pl.BlockSpec((S, LANES), lambda b, c: (s0 + b, c)),
                  pl.BlockSpec((2 * n1, n1), fix2),
                  pl.BlockSpec((n2, n1, LANES), fix3),
                  pl.BlockSpec((n2, n1, LANES), fix3)],
        out_specs=pl.BlockSpec((S, LANES), lambda b, c: (b, c)),
        out_shape=jax.ShapeDtypeStruct((B * S, W), jnp.uint32),
        compiler_params=_cparams(("parallel", "parallel")),
        name="fft_stage1",
    )(z, f1, ct, st)

    in_specs = [pl.BlockSpec((S, wc), lambda b, c: (b, c)), pl.BlockSpec((n2, 2 * n2), fix2)]
    args = [a, f2]
    aliases = {}
    if prev is not None:
        in_specs.append(pl.BlockSpec(memory_space=pl.ANY))
        args.append(prev)
        aliases = {2: 0}
    return pl.pallas_call(
        functools.partial(_fft_stage2_kernel, n1=n1, n2=n2),
        grid=(B, W // wc),
        in_specs=in_specs,
        out_specs=pl.BlockSpec((S, wc // 2), lambda b, c: (s0 + b, c)),
        out_shape=jax.ShapeDtypeStruct((T, W // 2), jnp.uint32),
        input_output_aliases=aliases,
        compiler_params=_cparams(("parallel", "parallel")),
        name="fft_stage2",
    )(*args)


def _dft_mats(n, scale, split=64):
    split = split if n % split == 0 else 1
    k = jnp.arange(n, dtype=jnp.int32)[None, :]
    j1 = jnp.arange(n // split, dtype=jnp.int32)[:, None]
    j2 = jnp.arange(split, dtype=jnp.int32)[:, None]
    w = 2.0 * math.pi / n
    ang_a = ((split * j1 * k) % n).astype(F32) * w
    ang_b = ((j2 * k) % n).astype(F32) * w
    ca, sa = jnp.cos(ang_a)[:, None, :], jnp.sin(ang_a)[:, None, :]
    cb, sb = (jnp.cos(ang_b) * scale)[None, :, :], (jnp.sin(ang_b) * scale)[None, :, :]
    c = (ca * cb - sa * sb).reshape(n, n)
    s = (sa * cb + ca * sb).reshape(n, n)
    return c, s


_NT = (((1,), (1,)), ((), ()))
_TN = (((0,), (0,)), ((), ()))


def _gla_block(q_ref, k_ref, v_ref, lr_ref, w3_ref, b_ref, st_ref, reverse, qscale, nchunk):
    R = nchunk * CHUNK
    lr = lr_ref[...]
    lr_hi = lr.astype(BF16)
    lr_lo = (lr - lr_hi.astype(F32)).astype(BF16)
    z = jnp.dot(jnp.concatenate([lr_hi, lr_lo, lr_hi], axis=1), w3_ref[...],
                preferred_element_type=F32) + b_ref[...]
    g = (jnp.minimum(z, 0.0) - jnp.log(1.0 + jnp.exp(-jnp.abs(z)))) * (1.0 / GATE_LOGIT_NORMALIZER)
    ri = lax.broadcasted_iota(jnp.int32, (R, R), 0)
    ci = lax.broadcasted_iota(jnp.int32, (R, R), 1)
    cum = ((ci >= ri) if reverse else (ci <= ri)).astype(BF16)
    g_hi = g.astype(BF16)
    g_lo = (g - g_hi.astype(F32)).astype(BF16)
    G = jnp.dot(cum, g_hi, preferred_element_type=F32) + jnp.dot(cum, g_lo, preferred_element_type=F32)

    dk = G.shape[1]
    zero_row = jnp.zeros((1, dk), F32)
    if reverse:
        starts = [G[(c + 1) * CHUNK:(c + 1) * CHUNK + 1, :] if c + 1 < nchunk else zero_row for c in range(nchunk)]
        ref_row, g_tot = CHUNK // 2, G[0:1, :]
    else:
        starts = [G[c * CHUNK - 1:c * CHUNK, :] if c > 0 else zero_row for c in range(nchunk)]
        ref_row, g_tot = CHUNK // 2 - 1, G[R - 1:R, :]
    bcast = lambda rows_: jnp.concatenate([jnp.broadcast_to(r_, (CHUNK, dk)) for r_ in rows_], axis=0)
    gc = G - bcast(starts)
    gref = bcast([gc[c * CHUNK + ref_row:c * CHUNK + ref_row + 1, :] for c in range(nchunk)])

    q = q_ref[...].astype(F32) * qscale
    k = k_ref[...].astype(F32)
    v = v_ref[...]
    q_in = (q * jnp.exp(gc - gref)).astype(BF16)
    k_in = (k * jnp.exp(gref - gc)).astype(BF16)
    q_it = (q * jnp.exp(gc)).astype(BF16)
    q_st = (q * jnp.exp(G)).astype(BF16)
    k_st = (k * jnp.exp(g_tot - G)).astype(BF16)

    s_diag = lax.dot_general(q_in, k_in, _NT, preferred_element_type=F32)
    same = (ri // CHUNK) == (ci // CHUNK)
    keep = same & ((ci > ri) if reverse else (ci <= ri))
    s_rows = []
    for c in range(nchunk):
        rows = slice(c * CHUNK, (c + 1) * CHUNK)
        s = jnp.where(keep[rows, :], s_diag[rows, :], 0.0)
        lo_, hi_ = ((c + 1) * CHUNK, R) if reverse else (0, c * CHUNK)
        if hi_ > lo_:
            kx = (k[lo_:hi_, :] * jnp.exp(starts[c] - G[lo_:hi_, :])).astype(BF16)
            pad = jnp.zeros((R - (hi_ - lo_), dk), BF16)
            kx = jnp.concatenate([pad, kx] if reverse else [kx, pad], axis=0)
            s = s + lax.dot_general(q_it[rows, :], kx, _NT, preferred_element_type=F32)
        s_rows.append(s.astype(BF16))
    scores = jnp.concatenate(s_rows, axis=0)

    st = st_ref[...]
    o = (jnp.dot(scores, v, preferred_element_type=F32)
         + lax.dot_general(q_st, st.astype(BF16), _NT, preferred_element_type=F32))
    st_ref[...] = st * jnp.exp(g_tot) + lax.dot_general(v, k_st, _TN, preferred_element_type=F32)
    return o


def _gla_kernel(qf_ref, kf_ref, vf_ref, qb_ref, kb_ref, vb_ref, lrf_ref, lrb_ref,
                wf_ref, bf_ref, wb_ref, bb_ref, *rest, nchunk, nsub, qscale):
    o_ref, acc_ref, stf_ref, stb_ref = rest[-4:]
    n = pl.program_id(2)
    nb = pl.num_programs(2)
    blk = nchunk * CHUNK
    rows = nsub * blk

    @pl.when(n == 0)
    def _():
        acc_ref[...] = jnp.zeros_like(acc_ref)
        stf_ref[...] = jnp.zeros_like(stf_ref)
        stb_ref[...] = jnp.zeros_like(stb_ref)

    for s_f in range(nsub):
        s_b = nsub - 1 - s_f
        sub_f, sub_b = pl.ds(s_f * blk, blk), pl.ds(s_b * blk, blk)
        o_f = _gla_block(qf_ref.at[sub_f, :], kf_ref.at[sub_f, :], vf_ref.at[sub_f, :], lrf_ref.at[sub_f, :],
                         wf_ref, bf_ref, stf_ref, False, qscale, nchunk)
        o_b = _gla_block(qb_ref.at[sub_b, :], kb_ref.at[sub_b, :], vb_ref.at[sub_b, :], lrb_ref.at[sub_b, :],
                         wb_ref, bb_ref, stb_ref, True, qscale, nchunk)
        acc_ref[pl.ds(pl.multiple_of(n * rows + s_f * blk, blk), blk), :] += o_f
        acc_ref[pl.ds(pl.multiple_of((nb - 1 - n) * rows + s_b * blk, blk), blk), :] += o_b

    @pl.when(n == nb - 1)
    def _():
        o_ref[...] = acc_ref[...].astype(o_ref.dtype)


def _hi_hi_lo(w):
    hi = w.astype(BF16)
    lo = (w - hi.astype(F32)).astype(BF16)
    return jnp.concatenate([hi, hi, lo], axis=0)


def _gla(proj, lr_f, lr_b, wup_f, b_f, wup_b, b_b, row0, B, S, q_blk0, k_blk0, v_blk0, dk, dv,
         prev=None, blk=256, nsub=4):
    T = proj.shape[0]
    nsub = nsub if S % (nsub * blk) == 0 else 1
    rows = nsub * blk
    assert S % rows == 0 and row0 % S == 0
    nb = S // rows
    rb0, sb0 = row0 // rows, row0 // S
    fmap = lambda b, n: rb0 + b * nb + n
    bmap = lambda b, n: rb0 + b * nb + (nb - 1 - n)
    in_specs = [
        pl.BlockSpec((rows, dk), lambda b, h, n: (fmap(b, n), q_blk0 + h)),
        pl.BlockSpec((rows, dk), lambda b, h, n: (fmap(b, n), k_blk0 + h)),
        pl.BlockSpec((rows, dv), lambda b, h, n: (fmap(b, n), v_blk0 + h)),
        pl.BlockSpec((rows, dk), lambda b, h, n: (bmap(b, n), q_blk0 + h)),
        pl.BlockSpec((rows, dk), lambda b, h, n: (bmap(b, n), k_blk0 + h)),
        pl.BlockSpec((rows, dv), lambda b, h, n: (bmap(b, n), v_blk0 + h)),
        pl.BlockSpec((rows, GATE_LOW_RANK), lambda b, h, n: (fmap(b, n), 0)),
        pl.BlockSpec((rows, GATE_LOW_RANK), lambda b, h, n: (bmap(b, n), 0)),
        pl.BlockSpec((3 * GATE_LOW_RANK, dk), lambda b, h, n: (0, h)),
        pl.BlockSpec((1, dk), lambda b, h, n: (0, h)),
        pl.BlockSpec((3 * GATE_LOW_RANK, dk), lambda b, h, n: (0, h)),
        pl.BlockSpec((1, dk), lambda b, h, n: (0, h)),
    ]
    args = [proj, proj, proj, proj, proj, proj, lr_f, lr_b,
            _hi_hi_lo(wup_f), b_f.reshape(1, -1), _hi_hi_lo(wup_b), b_b.reshape(1, -1)]
    aliases = {}
    if prev is not None:
        in_specs.append(pl.BlockSpec(memory_space=pl.ANY))
        args.append(prev)
        aliases = {len(args) - 1: 0}
    return pl.pallas_call(
        functools.partial(_gla_kernel, nchunk=blk // CHUNK, nsub=nsub, qscale=dk ** -0.5),
        grid=(B, GLA_HEADS, nb),
        in_specs=in_specs,
        out_specs=pl.BlockSpec((S, dv), lambda b, h, n: (sb0 + b, h)),
        out_shape=jax.ShapeDtypeStruct((T, GLA_HEADS * dv), BF16),
        scratch_shapes=[pltpu.VMEM((S, dv), F32), pltpu.VMEM((dv, dk), F32), pltpu.VMEM((dv, dk), F32)],
        input_output_aliases=aliases,
        compiler_params=_cparams(("parallel", "parallel", "arbitrary")),
        name="gla",
    )(*args)


def _merge_kernel(fft_ref, o_ref_in, og_ref, g0_ref, g1_ref, hn_ref, wf_ref, wg_ref, o_ref, a_ref, *, dv):
    half = _FFT_COLS // 2
    pieces = []
    for cb in range(fft_ref.shape[1] // half):
        pieces.extend(_unpack_bf16_pair(fft_ref[:, cb * half:(cb + 1) * half]))
    ya = jnp.dot(jnp.concatenate(pieces, axis=1).astype(BF16), wf_ref[...], preferred_element_type=F32)
    for h in range(GLA_HEADS):
        cs = slice(h * dv, (h + 1) * dv)
        o = o_ref_in[:, cs].astype(F32)
        var = jnp.mean(o * o, axis=-1, keepdims=True)
        on = o * lax.rsqrt(var + EPS) * hn_ref[...]
        og = og_ref[:, cs].astype(F32)
        a_ref[:, cs] = (on * (og * jax.nn.sigmoid(og))).astype(BF16)
    yb = jnp.dot(a_ref[...], wg_ref[...], preferred_element_type=F32)
    m = jax.nn.sigmoid(g0_ref[...].astype(F32)) * ya + jax.nn.sigmoid(g1_ref[...].astype(F32)) * yb
    o_ref[...] = m.astype(BF16)


def _merge(fft, o_gla, proj, og_blk, g0_blk, g1_blk, hn, wf, wg, tm=512):
    T, D = o_gla.shape
    FW = wf.shape[0]
    dv = D // GLA_HEADS
    const = dict(pipeline_mode=pl.Buffered(1))
    return pl.pallas_call(
        functools.partial(_merge_kernel, dv=dv),
        grid=(T // tm,),
        in_specs=[
            pl.BlockSpec((tm, FW // 2), lambda i: (i, 0)),
            pl.BlockSpec((tm, D), lambda i: (i, 0)),
            pl.BlockSpec((tm, D), lambda i: (i, og_blk)),
            pl.BlockSpec((tm, D), lambda i: (i, g0_blk)),
            pl.BlockSpec((tm, D), lambda i: (i, g1_blk)),
            pl.BlockSpec((1, dv), lambda i: (0, 0)),
            pl.BlockSpec((FW, D), lambda i: (0, 0), **const),
            pl.BlockSpec((D, D), lambda i: (0, 0), **const),
        ],
        out_specs=pl.BlockSpec((tm, D), lambda i: (i, 0)),
        out_shape=jax.ShapeDtypeStruct((T, D), BF16),
        scratch_shapes=[pltpu.VMEM((tm, D), BF16)],
        compiler_params=_cparams(("parallel",)),
        name="merge",
    )(fft, o_gla, proj, proj, proj, hn.reshape(1, dv), wf, wg)


HI16 = 0xFFFF0000


def _pack_bf16_pair(lo, hi):
    lo_bits = lax.bitcast_convert_type(lo.astype(BF16).astype(F32), jnp.uint32)
    hi_bits = lax.bitcast_convert_type(hi.astype(BF16).astype(F32), jnp.uint32)
    return (hi_bits & jnp.uint32(HI16)) | (lo_bits >> 16)


def _unpack_bf16_pair(w):
    lo = lax.bitcast_convert_type(w << 16, F32)
    hi = lax.bitcast_convert_type(w & jnp.uint32(HI16), F32)
    return lo, hi


def _store_tile_rows(ref, val):
    tm = val.shape[0]
    for s in range(SUBLANES):
        ref[pl.ds(s, tm, stride=SUBLANES), :] = val[:, s * LANES:(s + 1) * LANES]


def _load_tile_rows(ref, tm, s):
    return ref[pl.ds(s, tm, stride=SUBLANES), :]


def _outproj_router_kernel(m_ref, xp_ref, xs_ref, wo_ref, g_ref, wr_ref, br_ref,
                           x1_ref, xn_ref, idx_ref, tw_ref, rank_ref, cnt_ref, run_ref, *, n_exp, n0):
    i = pl.program_id(0)

    @pl.when(i == 0)
    def _():
        run_ref[...] = jnp.zeros_like(run_ref)

    tm, D = m_ref.shape
    x = jnp.where(i < n0, xp_ref[...], xs_ref[...])
    x1 = x + jnp.dot(m_ref[...], wo_ref[...], preferred_element_type=F32)
    x1_ref[...] = x1
    var = jnp.mean(x1 * x1, axis=-1, keepdims=True)
    xn = x1 * lax.rsqrt(var + EPS) * g_ref[...]
    _store_tile_rows(xn_ref, _pack_bf16_pair(xn[:, :D // 2], xn[:, D // 2:]))
    xh = xn.astype(BF16)
    xl = (xn - xh.astype(F32)).astype(BF16)
    r = jnp.dot(xh, wr_ref[...], preferred_element_type=F32)
    lg = (r[:, :n_exp] + r[:, n_exp:] + jnp.dot(xl, wr_ref[:, :n_exp], preferred_element_type=F32)
          + br_ref[...])

    lane = lax.broadcasted_iota(jnp.int32, (tm, n_exp), 1)
    vals, hots = [], []
    for _ in range(TOP_K):
        mx = jnp.max(lg, axis=-1, keepdims=True)
        ik = jnp.min(jnp.where(lg == mx, lane, n_exp), axis=-1, keepdims=True)
        hot = lane == ik
        vals.append(mx)
        hots.append(hot)
        lg = jnp.where(hot, -jnp.inf, lg)
    exps = [jnp.exp(v - vals[0]) for v in vals]
    denom = exps[0] + exps[1] + exps[2] + exps[3]

    sel = hots[0] | hots[1] | hots[2] | hots[3]
    sel_f = sel.astype(F32)
    r = lax.broadcasted_iota(jnp.int32, (tm, tm), 0)
    c = lax.broadcasted_iota(jnp.int32, (tm, tm), 1)
    strict = (c < r).astype(BF16)
    before = jnp.dot(strict, sel_f.astype(BF16), preferred_element_type=F32) + run_ref[...]
    run_ref[...] += jnp.sum(sel_f, axis=0, keepdims=True)
    cnt_ref[...] = run_ref[...].astype(jnp.int32)

    k4 = lax.broadcasted_iota(jnp.int32, (tm, TOP_K), 1)
    idx4 = jnp.zeros((tm, TOP_K), jnp.int32)
    w4 = jnp.zeros((tm, TOP_K), F32)
    rk4 = jnp.zeros((tm, TOP_K), jnp.int32)
    for k in range(TOP_K):
        ik = jnp.sum(jnp.where(hots[k], lane, 0), axis=-1, keepdims=True)
        rk = jnp.sum(jnp.where(hots[k], before, 0.0), axis=-1, keepdims=True).astype(jnp.int32)
        idx4 = jnp.where(k4 == k, ik, idx4)
        w4 = jnp.where(k4 == k, exps[k] / denom, w4)
        rk4 = jnp.where(k4 == k, rk, rk4)
    idx_ref[...] = idx4
    tw_ref[...] = w4
    rank_ref[...] = rk4


def _outproj_router(merged, xp, xs, wo, gain, wr2, br, tm=512):
    T, D = merged.shape
    E = wr2.shape[1] // 2
    n0 = xp.shape[0] // tm
    const = dict(pipeline_mode=pl.Buffered(1))
    row = lambda i: (i, 0)
    fix = lambda i: (0, 0)
    return pl.pallas_call(
        functools.partial(_outproj_router_kernel, n_exp=E, n0=n0),
        grid=(T // tm,),
        in_specs=[
            pl.BlockSpec((tm, D), row),
            pl.BlockSpec((tm, D), lambda i: (jnp.minimum(i, n0 - 1), 0)),
            pl.BlockSpec((tm, D), lambda i: (jnp.maximum(i - n0, 0), 0)),
            pl.BlockSpec((D, D), fix, **const),
            pl.BlockSpec((1, D), fix),
            pl.BlockSpec((D, 2 * E), fix),
            pl.BlockSpec((1, E), fix),
        ],
        out_specs=[
            pl.BlockSpec((tm, D), row),
            pl.BlockSpec((tm * SUBLANES, D // 2 // SUBLANES), row),
            pl.BlockSpec((tm, TOP_K), row),
            pl.BlockSpec((tm, TOP_K), row),
            pl.BlockSpec((tm, TOP_K), row),
            pl.BlockSpec((1, E), fix),
        ],
        out_shape=[
            jax.ShapeDtypeStruct((T, D), F32),
            jax.ShapeDtypeStruct((T * SUBLANES, D // 2 // SUBLANES), jnp.uint32),
            jax.ShapeDtypeStruct((T, TOP_K), jnp.int32),
            jax.ShapeDtypeStruct((T, TOP_K), F32),
            jax.ShapeDtypeStruct((T, TOP_K), jnp.int32),
            jax.ShapeDtypeStruct((1, E), jnp.int32),
        ],
        scratch_shapes=[pltpu.VMEM((1, E), F32)],
        compiler_params=_cparams(("arbitrary",)),
        name="outproj_router",
    )(merged, xp, xs, wo, gain.reshape(1, D), wr2, br.reshape(1, E))


def _dispatch_kernel(pos_ref, x_ref, xs_ref, dst_ref, sem, *, n_tok):
    i = pl.program_id(0)
    tm = x_ref.shape[0] // SUBLANES

    def body(t, carry):
        for k in range(TOP_K):
            p = pos_ref[t * TOP_K + k]
            dst_ref[p] = k * n_tok + i * tm + t
            pltpu.make_async_copy(x_ref.at[pl.ds(pl.multiple_of(t * SUBLANES, SUBLANES), SUBLANES), :],
                                  xs_ref.at[pl.ds(pl.multiple_of(p * SUBLANES, SUBLANES), SUBLANES), :], sem).start()
        return carry

    lax.fori_loop(0, tm, body, 0)
    for _ in range(TOP_K):
        pltpu.make_async_copy(x_ref, xs_ref.at[pl.ds(0, tm * SUBLANES), :], sem).wait()


def _dispatch(xn, pos_flat, n_slots, tm=1024):
    T = xn.shape[0] // SUBLANES
    return pl.pallas_call(
        functools.partial(_dispatch_kernel, n_tok=T),
        grid=(T // tm,),
        in_specs=[
            pl.BlockSpec((tm * TOP_K,), lambda i: (i,), memory_space=pltpu.SMEM),
            pl.BlockSpec((tm * SUBLANES, LANES), lambda i: (i, 0)),
        ],
        out_specs=[
            pl.BlockSpec(memory_space=pl.ANY),
            pl.BlockSpec((n_slots,), lambda i: (0,), memory_space=pltpu.SMEM),
        ],
        out_shape=[jax.ShapeDtypeStruct((n_slots * SUBLANES, LANES), xn.dtype),
                   jax.ShapeDtypeStruct((n_slots,), jnp.int32)],
        scratch_shapes=[pltpu.SemaphoreType.DMA(())],
        compiler_params=_cparams(("arbitrary",)),
        name="dispatch",
    )(pos_flat, xn)


def _combine_kernel(tw_ref, x1_ref, g_ref, *rest, n0):
    y_refs, (op_ref, os_ref) = rest[:TOP_K], rest[TOP_K:]
    i = pl.program_id(0)
    tw = tw_ref[...]
    tm = x1_ref.shape[0]
    lo_acc = [None] * SUBLANES
    hi_acc = [None] * SUBLANES
    for k in range(TOP_K):
        wk = tw[:, k:k + 1]
        for s in range(SUBLANES):
            lo, hi = _unpack_bf16_pair(_load_tile_rows(y_refs[k], tm, s))
            lo_acc[s] = wk * lo if k == 0 else lo_acc[s] + wk * lo
            hi_acc[s] = wk * hi if k == 0 else hi_acc[s] + wk * hi
    x2 = x1_ref[...] + jnp.concatenate(lo_acc + hi_acc, axis=1)
    var = jnp.mean(x2 * x2, axis=-1, keepdims=True)
    y = x2 * lax.rsqrt(var + EPS) * g_ref[...]

    @pl.when(i < n0)
    def _():
        op_ref[...] = y

    @pl.when(i >= n0)
    def _():
        os_ref[...] = y


def _combine(tw, x1, gain, yk, t_prompt, tm=256):
    T, D = x1.shape
    n0 = t_prompt // tm
    nblk = T // tm
    y_spec = lambda k: pl.BlockSpec((tm * SUBLANES, LANES), lambda i: (k * nblk + i, 0))
    return pl.pallas_call(
        functools.partial(_combine_kernel, n0=n0),
        grid=(T // tm,),
        in_specs=[
            pl.BlockSpec((tm, TOP_K), lambda i: (i, 0)),
            pl.BlockSpec((tm, D), lambda i: (i, 0)),
            pl.BlockSpec((1, D), lambda i: (0, 0)),
        ] + [y_spec(k) for k in range(TOP_K)],
        out_specs=[
            pl.BlockSpec((tm, D), lambda i: (jnp.minimum(i, n0 - 1), 0)),
            pl.BlockSpec((tm, D), lambda i: (jnp.maximum(i - n0, 0), 0)),
        ],
        out_shape=[jax.ShapeDtypeStruct((t_prompt, D), F32), jax.ShapeDtypeStruct((T - t_prompt, D), F32)],
        compiler_params=_cparams(("arbitrary",)),
        name="combine",
    )(tw, x1, gain.reshape(1, D), *([yk] * TOP_K))


def _gate_up_kernel(g0_ref, nt_ref, cnt_ref, w_ref, bg_ref, bu_ref, x_hbm, o_hbm,
                    wp_ref, xin_ref, xb_ref, obuf_ref, sin, sout, *, tn):
    j, e = pl.program_id(0), pl.program_id(1)
    tm = GROUP_TILE
    D = xb_ref.shape[1]
    half = MXU_DIM // 2
    n = nt_ref[e]
    row_base = g0_ref[e]
    cnt = cnt_ref[e]

    def in_copy(r, slot):
        rows = pl.ds(pl.multiple_of((row_base + r * tm) * SUBLANES, tm * SUBLANES), tm * SUBLANES)
        return pltpu.make_async_copy(x_hbm.at[rows, :], xin_ref.at[slot], sin.at[slot])

    def out_copy(r, slot):
        rows = pl.ds(pl.multiple_of(row_base + r * tm, tm), tm)
        cols = pl.ds(pl.multiple_of(j * (tn // 2), LANES), tn // 2)
        return pltpu.make_async_copy(obuf_ref.at[slot], o_hbm.at[rows, cols], sout.at[slot])

    def body(r, carry):
        slot = r % 2

        @pl.when(r + 1 < n)
        def _():
            in_copy(r + 1, 1 - slot).start()

        in_copy(r, slot).wait()

        @pl.when(r >= 2)
        def _():
            out_copy(r - 2, slot).wait()

        valid = (r * tm + lax.broadcasted_iota(jnp.int32, (tm, 1), 0)) < cnt
        for s in range(SUBLANES):
            lo, hi = _unpack_bf16_pair(_load_tile_rows(xin_ref.at[slot], tm, s))
            xb_ref[:, s * LANES:(s + 1) * LANES] = jnp.where(valid, lo, 0.0).astype(BF16)
            xb_ref[:, D // 2 + s * LANES:D // 2 + (s + 1) * LANES] = jnp.where(valid, hi, 0.0).astype(BF16)
        for cb in range(tn // MXU_DIM):
            h = jnp.dot(xb_ref[...], wp_ref[:, cb * MXU_DIM:(cb + 1) * MXU_DIM], preferred_element_type=F32)
            hg = h[:, :half] + bg_ref[0, :, cb * half:(cb + 1) * half]
            hu = h[:, half:] + bu_ref[0, :, cb * half:(cb + 1) * half]
            gate = jnp.minimum(hg, SWIGLU_LIMIT)
            up = jnp.clip(hu, -SWIGLU_LIMIT, SWIGLU_LIMIT)
            act = gate * jax.nn.sigmoid(SWIGLU_ALPHA * gate) * (up + 1.0)
            obuf_ref[slot, :, cb * half:(cb + 1) * half] = act.astype(obuf_ref.dtype)
        out_copy(r, slot).start()
        return carry

    @pl.when(n > 0)
    def _():
        in_copy(0, 0).start()
        r_i = lax.broadcasted_iota(jnp.int32, (MXU_DIM, MXU_DIM), 0)
        c_i = lax.broadcasted_iota(jnp.int32, (MXU_DIM, MXU_DIM), 1)
        perm = (((c_i < half) & (r_i == 2 * c_i)) | ((c_i >= half) & (r_i == 2 * (c_i - half) + 1))).astype(BF16)
        for cb in range(tn // MXU_DIM):
            for rb in range(D // 512):
                w = w_ref[0, rb * 512:(rb + 1) * 512, cb * MXU_DIM:(cb + 1) * MXU_DIM].astype(BF16)
                wp_ref[rb * 512:(rb + 1) * 512, cb * MXU_DIM:(cb + 1) * MXU_DIM] = jnp.dot(
                    w, perm, preferred_element_type=F32).astype(BF16)
        lax.fori_loop(0, n, body, 0)

        @pl.when(n >= 2)
        def _():
            out_copy(n - 2, n % 2).wait()

        out_copy(n - 1, (n - 1) % 2).wait()


def _gate_up(g0, ntiles, cnt, xs, w_gate_up, bg, bu, tn):
    P = xs.shape[0] // SUBLANES
    E, D, H2 = w_gate_up.shape
    tm = GROUP_TILE
    wmap = lambda j, e, *_: (e, 0, j)
    gs = pltpu.PrefetchScalarGridSpec(
        num_scalar_prefetch=3,
        grid=(H2 // tn, E),
        in_specs=[
            pl.BlockSpec((1, D, tn), wmap),
            pl.BlockSpec((1, 1, tn // 2), wmap),
            pl.BlockSpec((1, 1, tn // 2), wmap),
            pl.BlockSpec(memory_space=pl.ANY),
        ],
        out_specs=pl.BlockSpec(memory_space=pl.ANY),
        scratch_shapes=[
            pltpu.VMEM((D, tn), BF16),
            pltpu.VMEM((2, tm * SUBLANES, LANES), jnp.uint32),
            pltpu.VMEM((tm, D), BF16),
            pltpu.VMEM((2, tm, tn // 2), BF16),
            pltpu.SemaphoreType.DMA((2,)),
            pltpu.SemaphoreType.DMA((2,)),
        ],
    )
    return pl.pallas_call(
        functools.partial(_gate_up_kernel, tn=tn),
        grid_spec=gs,
        out_shape=jax.ShapeDtypeStruct((P, H2 // 2), BF16),
        compiler_params=_cparams(("arbitrary", "arbitrary")),
        name="moe_gate_up",
    )(g0, ntiles, cnt, w_gate_up, bg, bu, xs)


def _down_kernel(g0_ref, nt_ref, cnt_ref, dst_ref, w_ref, b_ref, a_hbm, y_hbm, wb_ref, ain_ref, obuf_ref, sin, sout):
    e = pl.program_id(0)
    tm = GROUP_TILE
    H, D = wb_ref.shape
    n = nt_ref[e]
    row_base = g0_ref[e]
    cnt = cnt_ref[e]
    n_real = y_hbm.shape[0] // SUBLANES - dst_ref.shape[0]

    def in_copy(r, slot):
        rows = pl.ds(pl.multiple_of(row_base + r * tm, tm), tm)
        return pltpu.make_async_copy(a_hbm.at[rows, :], ain_ref.at[slot], sin.at[slot])

    nblk = (D // 2) // MXU_DIM

    def compute(slot, out_ref, before_block=None):
        a = ain_ref[slot]
        for bi, c0 in enumerate(range(0, D // 2, MXU_DIM)):
            if before_block is not None:
                before_block(bi)
            c1 = D // 2 + c0
            lo = jnp.dot(a, wb_ref[:, c0:c0 + MXU_DIM], preferred_element_type=F32) + b_ref[0, :, c0:c0 + MXU_DIM]
            hi = jnp.dot(a, wb_ref[:, c1:c1 + MXU_DIM], preferred_element_type=F32) + b_ref[0, :, c1:c1 + MXU_DIM]
            packed = _pack_bf16_pair(lo, hi)
            for u in range(MXU_DIM // LANES):
                out_ref[pl.ds(c0 // LANES + u, tm, stride=SUBLANES), :] = packed[:, u * LANES:(u + 1) * LANES]

    def scatter_rows(q, so, part):
        base = row_base + q * tm
        for i in range(part * (tm // nblk), (part + 1) * (tm // nblk)):
            d = jnp.where(q * tm + i < cnt, dst_ref[base + i], n_real + base + i)
            pltpu.make_async_copy(obuf_ref.at[so, pl.ds(i * SUBLANES, SUBLANES), :],
                                  y_hbm.at[pl.ds(pl.multiple_of(d * SUBLANES, SUBLANES), SUBLANES), :],
                                  sout.at[so]).start()

    def scatter_tile(q, so):
        for part in range(nblk):
            scatter_rows(q, so, part)

    def wait_scatter(so):
        pltpu.make_async_copy(obuf_ref.at[so], y_hbm.at[pl.ds(0, tm * SUBLANES), :], sout.at[so]).wait()

    def body(r, carry):
        slot = r % 2

        @pl.when(r + 1 < n)
        def _():
            in_copy(r + 1, 1 - slot).start()

        in_copy(r, slot).wait()

        @pl.when(r >= 2)
        def _():
            wait_scatter(slot)

        compute(slot, obuf_ref.at[slot], functools.partial(scatter_rows, r - 1, 1 - slot))
        return carry

    @pl.when(n > 0)
    def _():
        in_copy(0, 0).start()
        for rb in range(H // 512):
            wb_ref[rb * 512:(rb + 1) * 512, :] = w_ref[0, rb * 512:(rb + 1) * 512, :].astype(BF16)

        @pl.when(n > 1)
        def _():
            in_copy(1, 1).start()

        in_copy(0, 0).wait()
        compute(0, obuf_ref.at[0])
        lax.fori_loop(1, n, body, 0)
        so = (n - 1) % 2
        scatter_tile(n - 1, so)
        wait_scatter(so)

        @pl.when(n >= 2)
        def _():
            wait_scatter(1 - so)


def _down(g0, ntiles, cnt, slot_dst, n_rows, act, wd, bd):
    P, H = act.shape
    E, _, D = wd.shape
    tm = GROUP_TILE
    wmap = lambda e, *_: (e, 0, 0)
    gs = pltpu.PrefetchScalarGridSpec(
        num_scalar_prefetch=4,
        grid=(E,),
        in_specs=[
            pl.BlockSpec((1, H, D), wmap),
            pl.BlockSpec((1, 1, D), wmap),
            pl.BlockSpec(memory_space=pl.ANY),
        ],
        out_specs=pl.BlockSpec(memory_space=pl.ANY),
        scratch_shapes=[
            pltpu.VMEM((H, D), BF16),
            pltpu.VMEM((2, tm, H), BF16),
            pltpu.VMEM((2, tm * SUBLANES, LANES), jnp.uint32),
            pltpu.SemaphoreType.DMA((2,)),
            pltpu.SemaphoreType.DMA((2,)),
        ],
    )
    return pl.pallas_call(
        _down_kernel,
        grid_spec=gs,
        out_shape=jax.ShapeDtypeStruct((n_rows * SUBLANES, LANES), jnp.uint32),
        compiler_params=_cparams(("arbitrary",)),
        name="moe_down",
    )(g0, ntiles, cnt, slot_dst, wd, bd, act)


def _trunk(xp, xs, seq_shapes, norm_mix, w_in, w_gk_up_fwd, b_gk_fwd, w_gk_up_bwd, b_gk_bwd, gla_head_norm,
           w_fnet_out, w_gla_out, w_out, norm_ffn, w_router, b_router, w_gate_up, b_gate_up,
           w_down, b_down, norm_final):
    D = xp.shape[1]
    T = xp.shape[0] + xs.shape[0]
    fw = w_fnet_out.shape[0]
    dkk = w_gk_up_fwd.shape[1]
    dvv = w_gla_out.shape[0]
    dk, dv = dkk // GLA_HEADS, dvv // GLA_HEADS
    sizes = (fw, dkk, dkk, dvv, dvv, GATE_LOW_RANK, GATE_LOW_RANK, 2 * D)
    offs = np.concatenate([[0], np.cumsum(sizes)])
    span = lambda n: (int(offs[n]), int(offs[n + 1]))
    w_main, w_lr2 = _repack_w_in(w_in, tuple(span(n) for n in (4, 7, 3, 0, 1, 2)), (int(offs[5]), int(offs[7])))
    og_blk, g0_blk, g1_blk = 0, dvv // D, dvv // D + 1
    v_off = dvv + 2 * D
    u_off = v_off + dvv
    q_off = u_off + fw
    k_off = q_off + dkk

    proj, lr = _inproj(xp, xs, norm_mix, w_main, w_lr2)
    lr_f, lr_b = lr[:, :GATE_LOW_RANK], lr[:, GATE_LOW_RANK:]

    gd = fw // FNET_GROUPS
    cc, sc = _dft_mats(gd, gd ** -0.5)
    cs = jnp.concatenate([cc, sc], axis=1).astype(BF16)
    z = _chan_dft(proj, u_off // fw, fw, cs)
    fft, o_gla = None, None
    row0 = 0
    for (B, S) in seq_shapes:
        fft = _seq_dft(z, row0, B, S, prev=fft)
        o_gla = _gla(proj, lr_f, lr_b, w_gk_up_fwd, b_gk_fwd, w_gk_up_bwd, b_gk_bwd, row0, B, S,
                     q_off // dk, k_off // dk, v_off // dv, dk, dv, prev=o_gla)
        row0 += B * S

    merged = _merge(fft, o_gla, proj, og_blk, g0_blk, g1_blk, gla_head_norm,
                    w_fnet_out.astype(BF16), w_gla_out.astype(BF16))
    x1, xn2, idx, tw, rank, cnt = _outproj_router(merged, xp, xs, w_out.astype(BF16), norm_ffn,
                                                  _hi_lo(w_router), b_router)

    E = w_router.shape[1]
    cnt = cnt.reshape(E)
    gsz = ((cnt + GROUP_TILE - 1) // GROUP_TILE) * GROUP_TILE
    gend = jnp.cumsum(gsz)
    gstart = gend - gsz
    pos = (gstart[idx] + rank).reshape(-1).astype(jnp.int32)
    n_slots = T * TOP_K + E * GROUP_TILE
    g0 = gstart.astype(jnp.int32)
    ntiles = (gsz // GROUP_TILE).astype(jnp.int32)

    x_sorted, slot_dst = _dispatch(xn2, pos, n_slots)
    H = w_down.shape[1]
    bg = b_gate_up[:, 0::2].reshape(E, 1, H)
    bu = b_gate_up[:, 1::2].reshape(E, 1, H)
    act = _gate_up(g0, ntiles, cnt, x_sorted, w_gate_up, bg, bu, GATE_UP_TN)
    yk = _down(g0, ntiles, cnt, slot_dst, T * TOP_K + n_slots, act, w_down, b_down.reshape(E, 1, D))
    return _combine(tw, x1, norm_final, yk, xp.shape[0])


def kernel(x_prompt, x_sample, norm_mix, w_in, w_gk_up_fwd, b_gk_fwd, w_gk_up_bwd, b_gk_bwd, gla_head_norm,
           w_fnet_out, w_gla_out, w_out, norm_ffn, w_router, b_router, w_gate_up, b_gate_up, w_down,
           b_down, norm_final):
    D = x_prompt.shape[-1]
    shapes = (x_prompt.shape[:2], x_sample.shape[:2])
    yp, ys = _trunk(x_prompt.reshape(-1, D), x_sample.reshape(-1, D), shapes, norm_mix[0], w_in[0], w_gk_up_fwd[0], b_gk_fwd[0], w_gk_up_bwd[0], b_gk_bwd[0],
               gla_head_norm[0], w_fnet_out[0], w_gla_out[0], w_out[0], norm_ffn[0], w_router[0],
               b_router[0], w_gate_up[0], b_gate_up[0], w_down[0], b_down[0], norm_final)
    return (yp.reshape(x_prompt.shape), ys.reshape(x_sample.shape))
```

```python
import functools
import math

import numpy as np
import jax
import jax.numpy as jnp
from jax import lax
from jax.experimental import pallas as pl
from jax.experimental.pallas import tpu as pltpu

F32 = jnp.float32
BF16 = jnp.bfloat16
HIGHEST = lax.Precision.HIGHEST

EPS = 1e-5
FNET_GROUPS = 4
GLA_HEADS = 4
GATE_LOW_RANK = 16
GATE_LOGIT_NORMALIZER = 16.0
CHUNK = 64
TOP_K = 4
SWIGLU_LIMIT = 7.0
SWIGLU_ALPHA = 1.702

VMEM_LIMIT_BYTES = 56 * 1024 * 1024
MXU_DIM = 256
LANES = 128
SUBLANES = 8
GROUP_TILE = 512
GATE_UP_TN = 2048


def _cparams(sem):
    return pltpu.CompilerParams(dimension_semantics=sem, vmem_limit_bytes=VMEM_LIMIT_BYTES)


def _split3(x):
    hi = x.astype(BF16)
    r = x - hi.astype(F32)
    mid = r.astype(BF16)
    lo = (r - mid.astype(F32)).astype(BF16)
    return hi, mid, lo


def _hi_lo(w):
    hi = w.astype(BF16)
    lo = (w - hi.astype(F32)).astype(BF16)
    return jnp.concatenate([hi, lo], axis=1)


def _repack_kernel(w_ref, o_ref, lr_ref, *, pieces, lr_cols):
    c0 = 0
    for a, b in pieces:
        o_ref[:, c0:c0 + (b - a)] = w_ref[:, a:b].astype(BF16)
        c0 += b - a
    w = w_ref[:, lr_cols[0]:lr_cols[1]]
    hi = w.astype(BF16)
    n = lr_cols[1] - lr_cols[0]
    lr_ref[:, :n] = hi
    lr_ref[:, n:] = (w - hi.astype(F32)).astype(BF16)


def _repack_w_in(w_in, pieces, lr_cols, tr=256):
    K, N = w_in.shape
    n_main = sum(b - a for a, b in pieces)
    n_lr = lr_cols[1] - lr_cols[0]
    return pl.pallas_call(
        functools.partial(_repack_kernel, pieces=pieces, lr_cols=lr_cols),
        grid=(K // tr,),
        in_specs=[pl.BlockSpec((tr, N), lambda i: (i, 0))],
        out_specs=[pl.BlockSpec((tr, n_main), lambda i: (i, 0)), pl.BlockSpec((tr, 2 * n_lr), lambda i: (i, 0))],
        out_shape=[jax.ShapeDtypeStruct((K, n_main), BF16), jax.ShapeDtypeStruct((K, 2 * n_lr), BF16)],
        compiler_params=_cparams(("parallel",)),
        name="repack_w_in",
    )(w_in)


def _inproj_kernel(xp_ref, xs_ref, g_ref, w_ref, wlr_ref, o_ref, lr_ref, xn_ref, *, n0):
    @pl.when(pl.program_id(1) == 0)
    def _():
        x = jnp.where(pl.program_id(0) < n0, xp_ref[...], xs_ref[...])
        var = jnp.mean(x * x, axis=-1, keepdims=True)
        xn = (x * lax.rsqrt(var + EPS) * g_ref[...]).astype(BF16)
        xn_ref[...] = xn
        r = jnp.dot(xn, wlr_ref[...], preferred_element_type=F32)
        nlr = lr_ref.shape[1]
        lr_ref[...] = r[:, :nlr] + r[:, nlr:]

    o_ref[...] = jnp.dot(xn_ref[...], w_ref[...], preferred_element_type=F32).astype(o_ref.dtype)


def _inproj(xp, xs, gain, w_main, w_lr2, tm=1024, tn=1024):
    D = xp.shape[1]
    T = xp.shape[0] + xs.shape[0]
    tm = tm if xp.shape[0] % tm == 0 and xs.shape[0] % tm == 0 else tm // 2
    n0 = xp.shape[0] // tm
    N = w_main.shape[1]
    R = w_lr2.shape[1] // 2
    return pl.pallas_call(
        functools.partial(_inproj_kernel, n0=n0),
        grid=(T // tm, N // tn),
        in_specs=[
            pl.BlockSpec((tm, D), lambda i, j: (jnp.minimum(i, n0 - 1), 0), pipeline_mode=pl.Buffered(1)),
            pl.BlockSpec((tm, D), lambda i, j: (jnp.maximum(i - n0, 0), 0), pipeline_mode=pl.Buffered(1)),
            pl.BlockSpec((1, D), lambda i, j: (0, 0)),
            pl.BlockSpec((D, tn), lambda i, j: (0, j)),
            pl.BlockSpec((D, 2 * R), lambda i, j: (0, 0)),
        ],
        out_specs=[
            pl.BlockSpec((tm, tn), lambda i, j: (i, j)),
            pl.BlockSpec((tm, R), lambda i, j: (i, 0)),
        ],
        out_shape=[jax.ShapeDtypeStruct((T, N), BF16), jax.ShapeDtypeStruct((T, R), F32)],
        scratch_shapes=[pltpu.VMEM((tm, D), BF16)],
        compiler_params=_cparams(("parallel", "arbitrary")),
        name="inproj",
    )(xp, xs, gain.reshape(1, D), w_main, w_lr2)


def _chan_dft_kernel(u_ref, cs_ref, z_ref, *, gd):
    for g in range(FNET_GROUPS):
        r = jnp.dot(u_ref[:, g * gd:(g + 1) * gd], cs_ref[...], preferred_element_type=F32)
        z_ref[:, g * gd:(g + 1) * gd] = _pack_bf16_pair(r[:, :gd], r[:, gd:])


def _chan_dft(proj, u_col_block, width, cs, tm=512):
    T = proj.shape[0]
    gd = width // FNET_GROUPS
    return pl.pallas_call(
        functools.partial(_chan_dft_kernel, gd=gd),
        grid=(T // tm,),
        in_specs=[
            pl.BlockSpec((tm, width), lambda i: (i, u_col_block)),
            pl.BlockSpec((gd, 2 * gd), lambda i: (0, 0)),
        ],
        out_specs=pl.BlockSpec((tm, width), lambda i: (i, 0)),
        out_shape=jax.ShapeDtypeStruct((T, width), jnp.uint32),
        compiler_params=_cparams(("parallel",)),
        name="chan_dft",
    )(proj, cs)


_FFT_COLS = 2 * LANES


def _fft_stage1_kernel(z_ref, f1_ref, ct_ref, st_ref, a_ref, *, n1, n2):
    for m in range(n2):
        rows = pl.ds(m, n1, stride=n2)
        zc, zs = _unpack_bf16_pair(z_ref[rows, :])
        pc = jnp.dot(f1_ref[...], zc.astype(BF16), preferred_element_type=F32)
        ps = jnp.dot(f1_ref[...], zs.astype(BF16), preferred_element_type=F32)
        a_re = pc[:n1] - ps[n1:]
        a_im = -ps[:n1] - pc[n1:]
        ct = ct_ref[m][:, 0:1]
        st = st_ref[m][:, 0:1]
        a_ref[rows, :] = _pack_bf16_pair(a_re * ct + a_im * st, a_im * ct - a_re * st)


def _fft_stage2_kernel(a_ref, f2_ref, *rest, n1, n2):
    o_ref = rest[-1]
    half = _FFT_COLS // 2
    for k1 in range(n1):
        a_re, a_im = _unpack_bf16_pair(a_ref[pl.ds(k1 * n2, n2), :])
        rhs = jnp.concatenate([a_re.astype(BF16), a_im.astype(BF16)], axis=0)
        x = jnp.dot(f2_ref[...], rhs, preferred_element_type=F32)
        o_ref[pl.ds(k1, n2, stride=n1), :] = _pack_bf16_pair(x[:, :half], x[:, half:])


def _seq_dft(z, row0, B, S, prev=None):
    T, W = z.shape
    wc = _FFT_COLS
    n2 = 64 if S % (64 * SUBLANES) == 0 else S // SUBLANES
    n1 = S // n2
    assert n1 * n2 == S and row0 % S == 0 and W % wc == 0
    s0 = row0 // S
    i1 = jnp.arange(n1, dtype=jnp.int32)
    i2 = jnp.arange(n2, dtype=jnp.int32)
    ang1 = ((i1[:, None] * i1[None, :]) % n1).astype(F32) * (2.0 * math.pi / n1)
    f1 = (jnp.concatenate([jnp.cos(ang1), jnp.sin(ang1)], axis=0) * S ** -0.5).astype(BF16)
    ang2 = ((i2[:, None] * i2[None, :]) % n2).astype(F32) * (2.0 * math.pi / n2)
    f2 = jnp.concatenate([jnp.cos(ang2), jnp.sin(ang2)], axis=1).astype(BF16)
    angt = (i2[:, None] * i1[None, :]).astype(F32) * (2.0 * math.pi / S)
    ct = jnp.broadcast_to(jnp.cos(angt)[:, :, None], (n2, n1, LANES))
    st = jnp.broadcast_to(jnp.sin(angt)[:, :, None], (n2, n1, LANES))

    fix2 = lambda b, c: (0, 0)
    fix3 = lambda b, c: (0, 0, 0)

    a = pl.pallas_call(
        functools.partial(_fft_stage1_kernel, n1=n1, n2=n2),
        grid=(B, W // LANES),
        in_specs=[pl.BlockSpec((S, LANES), lambda b, c: (s0 + b, c)),
                  pl.BlockSpec((2 * n1, n1), fix2),
                  pl.BlockSpec((n2, n1, LANES), fix3),
                  pl.BlockSpec((n2, n1, LANES), fix3)],
        out_specs=pl.BlockSpec((S, LANES), lambda b, c: (b, c)),
        out_shape=jax.ShapeDtypeStruct((B * S, W), jnp.uint32),
        compiler_params=_cparams(("parallel", "parallel")),
        name="fft_stage1",
    )(z, f1, ct, st)

    in_specs = [pl.BlockSpec((S, wc), lambda b, c: (b, c)), pl.BlockSpec((n2, 2 * n2), fix2)]
    args = [a, f2]
    aliases = {}
    if prev is not None:
        in_specs.append(pl.BlockSpec(memory_space=pl.ANY))
        args.append(prev)
        aliases = {2: 0}
    return pl.pallas_call(
        functools.partial(_fft_stage2_kernel, n1=n1, n2=n2),
        grid=(B, W // wc),
        in_specs=in_specs,
        out_specs=pl.BlockSpec((S, wc // 2), lambda b, c: (s0 + b, c)),
        out_shape=jax.ShapeDtypeStruct((T, W // 2), jnp.uint32),
        input_output_aliases=aliases,
        compiler_params=_cparams(("parallel", "parallel")),
        name="fft_stage2",
    )(*args)


def _dft_mats(n, scale, split=64):
    split = split if n % split == 0 else 1
    k = jnp.arange(n, dtype=jnp.int32)[None, :]
    j1 = jnp.arange(n // split, dtype=jnp.int32)[:, None]
    j2 = jnp.arange(split, dtype=jnp.int32)[:, None]
    w = 2.0 * math.pi / n
    ang_a = ((split * j1 * k) % n).astype(F32) * w
    ang_b = ((j2 * k) % n).astype(F32) * w
    ca, sa = jnp.cos(ang_a)[:, None, :], jnp.sin(ang_a)[:, None, :]
    cb, sb = (jnp.cos(ang_b) * scale)[None, :, :], (jnp.sin(ang_b) * scale)[None, :, :]
    c = (ca * cb - sa * sb).reshape(n, n)
    s = (sa * cb + ca * sb).reshape(n, n)
    return c, s


_NT = (((1,), (1,)), ((), ()))
_TN = (((0,), (0,)), ((), ()))


def _gla_block(q_ref, k_ref, v_ref, lr_ref, w3_ref, b_ref, st_ref, reverse, qscale, nchunk):
    R = nchunk * CHUNK
    lr = lr_ref[...]
    lr_hi = lr.astype(BF16)
    lr_lo = (lr - lr_hi.astype(F32)).astype(BF16)
    z = jnp.dot(jnp.concatenate([lr_hi, lr_lo, lr_hi], axis=1), w3_ref[...],
                preferred_element_type=F32) + b_ref[...]
    g = (jnp.minimum(z, 0.0) - jnp.log(1.0 + jnp.exp(-jnp.abs(z)))) * (1.0 / GATE_LOGIT_NORMALIZER)
    ri = lax.broadcasted_iota(jnp.int32, (R, R), 0)
    ci = lax.broadcasted_iota(jnp.int32, (R, R), 1)
    cum = ((ci >= ri) if reverse else (ci <= ri)).astype(BF16)
    g_hi = g.astype(BF16)
    g_lo = (g - g_hi.astype(F32)).astype(BF16)
    G = jnp.dot(cum, g_hi, preferred_element_type=F32) + jnp.dot(cum, g_lo, preferred_element_type=F32)

    dk = G.shape[1]
    zero_row = jnp.zeros((1, dk), F32)
    if reverse:
        starts = [G[(c + 1) * CHUNK:(c + 1) * CHUNK + 1, :] if c + 1 < nchunk else zero_row for c in range(nchunk)]
        ref_row, g_tot = CHUNK // 2, G[0:1, :]
    else:
        starts = [G[c * CHUNK - 1:c * CHUNK, :] if c > 0 else zero_row for c in range(nchunk)]
        ref_row, g_tot = CHUNK // 2 - 1, G[R - 1:R, :]
    bcast = lambda rows_: jnp.concatenate([jnp.broadcast_to(r_, (CHUNK, dk)) for r_ in rows_], axis=0)
    gc = G - bcast(starts)
    gref = bcast([gc[c * CHUNK + ref_row:c * CHUNK + ref_row + 1, :] for c in range(nchunk)])

    q = q_ref[...].astype(F32) * qscale
    k = k_ref[...].astype(F32)
    v = v_ref[...]
    q_in = (q * jnp.exp(gc - gref)).astype(BF16)
    k_in = (k * jnp.exp(gref - gc)).astype(BF16)
    q_it = (q * jnp.exp(gc)).astype(BF16)
    q_st = (q * jnp.exp(G)).astype(BF16)
    k_st = (k * jnp.exp(g_tot - G)).astype(BF16)

    s_diag = lax.dot_general(q_in, k_in, _NT, preferred_element_type=F32)
    same = (ri // CHUNK) == (ci // CHUNK)
    keep = same & ((ci > ri) if reverse else (ci <= ri))
    s_rows = []
    for c in range(nchunk):
        rows = slice(c * CHUNK, (c + 1) * CHUNK)
        s = jnp.where(keep[rows, :], s_diag[rows, :], 0.0)
        lo_, hi_ = ((c + 1) * CHUNK, R) if reverse else (0, c * CHUNK)
        if hi_ > lo_:
            kx = (k[lo_:hi_, :] * jnp.exp(starts[c] - G[lo_:hi_, :])).astype(BF16)
            pad = jnp.zeros((R - (hi_ - lo_), dk), BF16)
            kx = jnp.concatenate([pad, kx] if reverse else [kx, pad], axis=0)
            s = s + lax.dot_general(q_it[rows, :], kx, _NT, preferred_element_type=F32)
        s_rows.append(s.astype(BF16))
    scores = jnp.concatenate(s_rows, axis=0)

    st = st_ref[...]
    o = (jnp.dot(scores, v, preferred_element_type=F32)
         + lax.dot_general(q_st, st.astype(BF16), _NT, preferred_element_type=F32))
    st_ref[...] = st * jnp.exp(g_tot) + lax.dot_general(v, k_st, _TN, preferred_element_type=F32)
    return o


def _gla_kernel(qf_ref, kf_ref, vf_ref, qb_ref, kb_ref, vb_ref, lrf_ref, lrb_ref,
                wf_ref, bf_ref, wb_ref, bb_ref, *rest, nchunk, nsub, qscale):
    o_ref, acc_ref, stf_ref, stb_ref = rest[-4:]
    n = pl.program_id(2)
    nb = pl.num_programs(2)
    blk = nchunk * CHUNK
    rows = nsub * blk

    @pl.when(n == 0)
    def _():
        acc_ref[...] = jnp.zeros_like(acc_ref)
        stf_ref[...] = jnp.zeros_like(stf_ref)
        stb_ref[...] = jnp.zeros_like(stb_ref)

    for s_f in range(nsub):
        s_b = nsub - 1 - s_f
        sub_f, sub_b = pl.ds(s_f * blk, blk), pl.ds(s_b * blk, blk)
        o_f = _gla_block(qf_ref.at[sub_f, :], kf_ref.at[sub_f, :], vf_ref.at[sub_f, :], lrf_ref.at[sub_f, :],
                         wf_ref, bf_ref, stf_ref, False, qscale, nchunk)
        o_b = _gla_block(qb_ref.at[sub_b, :], kb_ref.at[sub_b, :], vb_ref.at[sub_b, :], lrb_ref.at[sub_b, :],
                         wb_ref, bb_ref, stb_ref, True, qscale, nchunk)
        acc_ref[pl.ds(pl.multiple_of(n * rows + s_f * blk, blk), blk), :] += o_f
        acc_ref[pl.ds(pl.multiple_of((nb - 1 - n) * rows + s_b * blk, blk), blk), :] += o_b

    @pl.when(n == nb - 1)
    def _():
        o_ref[...] = acc_ref[...].astype(o_ref.dtype)


def _hi_hi_lo(w):
    hi = w.astype(BF16)
    lo = (w - hi.astype(F32)).astype(BF16)
    return jnp.concatenate([hi, hi, lo], axis=0)


def _gla(proj, lr_f, lr_b, wup_f, b_f, wup_b, b_b, row0, B, S, q_blk0, k_blk0, v_blk0, dk, dv,
         prev=None, blk=256, nsub=4):
    T = proj.shape[0]
    nsub = nsub if S % (nsub * blk) == 0 else 1
    rows = nsub * blk
    assert S % rows == 0 and row0 % S == 0
    nb = S // rows
    rb0, sb0 = row0 // rows, row0 // S
    fmap = lambda b, n: rb0 + b * nb + n
    bmap = lambda b, n: rb0 + b * nb + (nb - 1 - n)
    in_specs = [
        pl.BlockSpec((rows, dk), lambda b, h, n: (fmap(b, n), q_blk0 + h)),
        pl.BlockSpec((rows, dk), lambda b, h, n: (fmap(b, n), k_blk0 + h)),
        pl.BlockSpec((rows, dv), lambda b, h, n: (fmap(b, n), v_blk0 + h)),
        pl.BlockSpec((rows, dk), lambda b, h, n: (bmap(b, n), q_blk0 + h)),
        pl.BlockSpec((rows, dk), lambda b, h, n: (bmap(b, n), k_blk0 + h)),
        pl.BlockSpec((rows, dv), lambda b, h, n: (bmap(b, n), v_blk0 + h)),
        pl.BlockSpec((rows, GATE_LOW_RANK), lambda b, h, n: (fmap(b, n), 0)),
        pl.BlockSpec((rows, GATE_LOW_RANK), lambda b, h, n: (bmap(b, n), 0)),
        pl.BlockSpec((3 * GATE_LOW_RANK, dk), lambda b, h, n: (0, h)),
        pl.BlockSpec((1, dk), lambda b, h, n: (0, h)),
        pl.BlockSpec((3 * GATE_LOW_RANK, dk), lambda b, h, n: (0, h)),
        pl.BlockSpec((1, dk), lambda b, h, n: (0, h)),
    ]
    args = [proj, proj, proj, proj, proj, proj, lr_f, lr_b,
            _hi_hi_lo(wup_f), b_f.reshape(1, -1), _hi_hi_lo(wup_b), b_b.reshape(1, -1)]
    aliases = {}
    if prev is not None:
        in_specs.append(pl.BlockSpec(memory_space=pl.ANY))
        args.append(prev)
        aliases = {len(args) - 1: 0}
    return pl.pallas_call(
        functools.partial(_gla_kernel, nchunk=blk // CHUNK, nsub=nsub, qscale=dk ** -0.5),
        grid=(B, GLA_HEADS, nb),
        in_specs=in_specs,
        out_specs=pl.BlockSpec((S, dv), lambda b, h, n: (sb0 + b, h)),
        out_shape=jax.ShapeDtypeStruct((T, GLA_HEADS * dv), BF16),
        scratch_shapes=[pltpu.VMEM((S, dv), F32), pltpu.VMEM((dv, dk), F32), pltpu.VMEM((dv, dk), F32)],
        input_output_aliases=aliases,
        compiler_params=_cparams(("parallel", "parallel", "arbitrary")),
        name="gla",
    )(*args)


def _merge_kernel(fft_ref, o_ref_in, og_ref, g0_ref, g1_ref, hn_ref, wf_ref, wg_ref, o_ref, a_ref, *, dv):
    half = _FFT_COLS // 2
    pieces = []
    for cb in range(fft_ref.shape[1] // half):
        pieces.extend(_unpack_bf16_pair(fft_ref[:, cb * half:(cb + 1) * half]))
    ya = jnp.dot(jnp.concatenate(pieces, axis=1).astype(BF16), wf_ref[...], preferred_element_type=F32)
    for h in range(GLA_HEADS):
        cs = slice(h * dv, (h + 1) * dv)
        o = o_ref_in[:, cs].astype(F32)
        var = jnp.mean(o * o, axis=-1, keepdims=True)
        on = o * lax.rsqrt(var + EPS) * hn_ref[...]
        og = og_ref[:, cs].astype(F32)
        a_ref[:, cs] = (on * (og * jax.nn.sigmoid(og))).astype(BF16)
    yb = jnp.dot(a_ref[...], wg_ref[...], preferred_element_type=F32)
    m = jax.nn.sigmoid(g0_ref[...].astype(F32)) * ya + jax.nn.sigmoid(g1_ref[...].astype(F32)) * yb
    o_ref[...] = m.astype(BF16)


def _merge(fft, o_gla, proj, og_blk, g0_blk, g1_blk, hn, wf, wg, tm=512):
    T, D = o_gla.shape
    FW = wf.shape[0]
    dv = D // GLA_HEADS
    const = dict(pipeline_mode=pl.Buffered(1))
    return pl.pallas_call(
        functools.partial(_merge_kernel, dv=dv),
        grid=(T // tm,),
        in_specs=[
            pl.BlockSpec((tm, FW // 2), lambda i: (i, 0)),
            pl.BlockSpec((tm, D), lambda i: (i, 0)),
            pl.BlockSpec((tm, D), lambda i: (i, og_blk)),
            pl.BlockSpec((tm, D), lambda i: (i, g0_blk)),
            pl.BlockSpec((tm, D), lambda i: (i, g1_blk)),
            pl.BlockSpec((1, dv), lambda i: (0, 0)),
            pl.BlockSpec((FW, D), lambda i: (0, 0), **const),
            pl.BlockSpec((D, D), lambda i: (0, 0), **const),
        ],
        out_specs=pl.BlockSpec((tm, D), lambda i: (i, 0)),
        out_shape=jax.ShapeDtypeStruct((T, D), BF16),
        scratch_shapes=[pltpu.VMEM((tm, D), BF16)],
        compiler_params=_cparams(("parallel",)),
        name="merge",
    )(fft, o_gla, proj, proj, proj, hn.reshape(1, dv), wf, wg)


HI16 = 0xFFFF0000


def _pack_bf16_pair(lo, hi):
    lo_bits = lax.bitcast_convert_type(lo.astype(BF16).astype(F32), jnp.uint32)
    hi_bits = lax.bitcast_convert_type(hi.astype(BF16).astype(F32), jnp.uint32)
    return (hi_bits & jnp.uint32(HI16)) | (lo_bits >> 16)


def _unpack_bf16_pair(w):
    lo = lax.bitcast_convert_type(w << 16, F32)
    hi = lax.bitcast_convert_type(w & jnp.uint32(HI16), F32)
    return lo, hi


def _store_tile_rows(ref, val):
    tm = val.shape[0]
    for s in range(SUBLANES):
        ref[pl.ds(s, tm, stride=SUBLANES), :] = val[:, s * LANES:(s + 1) * LANES]


def _load_tile_rows(ref, tm, s):
    return ref[pl.ds(s, tm, stride=SUBLANES), :]


def _outproj_router_kernel(m_ref, xp_ref, xs_ref, wo_ref, g_ref, wr_ref, br_ref,
                           x1_ref, xn_ref, idx_ref, tw_ref, rank_ref, cnt_ref, run_ref, *, n_exp, n0):
    i = pl.program_id(0)

    @pl.when(i == 0)
    def _():
        run_ref[...] = jnp.zeros_like(run_ref)

    tm, D = m_ref.shape
    x = jnp.where(i < n0, xp_ref[...], xs_ref[...])
    x1 = x + jnp.dot(m_ref[...], wo_ref[...], preferred_element_type=F32)
    x1_ref[...] = x1
    var = jnp.mean(x1 * x1, axis=-1, keepdims=True)
    xn = x1 * lax.rsqrt(var + EPS) * g_ref[...]
    _store_tile_rows(xn_ref, _pack_bf16_pair(xn[:, :D // 2], xn[:, D // 2:]))
    xh = xn.astype(BF16)
    xl = (xn - xh.astype(F32)).astype(BF16)
    r = jnp.dot(xh, wr_ref[...], preferred_element_type=F32)
    lg = (r[:, :n_exp] + r[:, n_exp:] + jnp.dot(xl, wr_ref[:, :n_exp], preferred_element_type=F32)
          + br_ref[...])

    lane = lax.broadcasted_iota(jnp.int32, (tm, n_exp), 1)
    vals, hots = [], []
    for _ in range(TOP_K):
        mx = jnp.max(lg, axis=-1, keepdims=True)
        ik = jnp.min(jnp.where(lg == mx, lane, n_exp), axis=-1, keepdims=True)
        hot = lane == ik
        vals.append(mx)
        hots.append(hot)
        lg = jnp.where(hot, -jnp.inf, lg)
    exps = [jnp.exp(v - vals[0]) for v in vals]
    denom = exps[0] + exps[1] + exps[2] + exps[3]

    sel = hots[0] | hots[1] | hots[2] | hots[3]
    sel_f = sel.astype(F32)
    r = lax.broadcasted_iota(jnp.int32, (tm, tm), 0)
    c = lax.broadcasted_iota(jnp.int32, (tm, tm), 1)
    strict = (c < r).astype(BF16)
    before = jnp.dot(strict, sel_f.astype(BF16), preferred_element_type=F32) + run_ref[...]
    run_ref[...] += jnp.sum(sel_f, axis=0, keepdims=True)
    cnt_ref[...] = run_ref[...].astype(jnp.int32)

    k4 = lax.broadcasted_iota(jnp.int32, (tm, TOP_K), 1)
    idx4 = jnp.zeros((tm, TOP_K), jnp.int32)
    w4 = jnp.zeros((tm, TOP_K), F32)
    rk4 = jnp.zeros((tm, TOP_K), jnp.int32)
    for k in range(TOP_K):
        ik = jnp.sum(jnp.where(hots[k], lane, 0), axis=-1, keepdims=True)
        rk = jnp.sum(jnp.where(hots[k], before, 0.0), axis=-1, keepdims=True).astype(jnp.int32)
        idx4 = jnp.where(k4 == k, ik, idx4)
        w4 = jnp.where(k4 == k, exps[k] / denom, w4)
        rk4 = jnp.where(k4 == k, rk, rk4)
    idx_ref[...] = idx4
    tw_ref[...] = w4
    rank_ref[...] = rk4


def _outproj_router(merged, xp, xs, wo, gain, wr2, br, tm=512):
    T, D = merged.shape
    E = wr2.shape[1] // 2
    n0 = xp.shape[0] // tm
    const = dict(pipeline_mode=pl.Buffered(1))
    row = lambda i: (i, 0)
    fix = lambda i: (0, 0)
    return pl.pallas_call(
        functools.partial(_outproj_router_kernel, n_exp=E, n0=n0),
        grid=(T // tm,),
        in_specs=[
            pl.BlockSpec((tm, D), row),
            pl.BlockSpec((tm, D), lambda i: (jnp.minimum(i, n0 - 1), 0)),
            pl.BlockSpec((tm, D), lambda i: (jnp.maximum(i - n0, 0), 0)),
            pl.BlockSpec((D, D), fix, **const),
            pl.BlockSpec((1, D), fix),
            pl.BlockSpec((D, 2 * E), fix),
            pl.BlockSpec((1, E), fix),
        ],
        out_specs=[
            pl.BlockSpec((tm, D), row),
            pl.BlockSpec((tm * SUBLANES, D // 2 // SUBLANES), row),
            pl.BlockSpec((tm, TOP_K), row),
            pl.BlockSpec((tm, TOP_K), row),
            pl.BlockSpec((tm, TOP_K), row),
            pl.BlockSpec((1, E), fix),
        ],
        out_shape=[
            jax.ShapeDtypeStruct((T, D), F32),
            jax.ShapeDtypeStruct((T * SUBLANES, D // 2 // SUBLANES), jnp.uint32),
            jax.ShapeDtypeStruct((T, TOP_K), jnp.int32),
            jax.ShapeDtypeStruct((T, TOP_K), F32),
            jax.ShapeDtypeStruct((T, TOP_K), jnp.int32),
            jax.ShapeDtypeStruct((1, E), jnp.int32),
        ],
        scratch_shapes=[pltpu.VMEM((1, E), F32)],
        compiler_params=_cparams(("arbitrary",)),
        name="outproj_router",
    )(merged, xp, xs, wo, gain.reshape(1, D), wr2, br.reshape(1, E))


def _dispatch_kernel(pos_ref, x_ref, xs_ref, dst_ref, sem, *, n_tok):
    i = pl.program_id(0)
    tm = x_ref.shape[0] // SUBLANES

    def body(t, carry):
        for k in range(TOP_K):
            p = pos_ref[t * TOP_K + k]
            dst_ref[p] = k * n_tok + i * tm + t
            pltpu.make_async_copy(x_ref.at[pl.ds(pl.multiple_of(t * SUBLANES, SUBLANES), SUBLANES), :],
                                  xs_ref.at[pl.ds(pl.multiple_of(p * SUBLANES, SUBLANES), SUBLANES), :], sem).start()
        return carry

    lax.fori_loop(0, tm, body, 0)
    for _ in range(TOP_K):
        pltpu.make_async_copy(x_ref, xs_ref.at[pl.ds(0, tm * SUBLANES), :], sem).wait()


def _dispatch(xn, pos_flat, n_slots, tm=1024):
    T = xn.shape[0] // SUBLANES
    return pl.pallas_call(
        functools.partial(_dispatch_kernel, n_tok=T),
        grid=(T // tm,),
        in_specs=[
            pl.BlockSpec((tm * TOP_K,), lambda i: (i,), memory_space=pltpu.SMEM),
            pl.BlockSpec((tm * SUBLANES, LANES), lambda i: (i, 0)),
        ],
        out_specs=[
            pl.BlockSpec(memory_space=pl.ANY),
            pl.BlockSpec((n_slots,), lambda i: (0,), memory_space=pltpu.SMEM),
        ],
        out_shape=[jax.ShapeDtypeStruct((n_slots * SUBLANES, LANES), xn.dtype),
                   jax.ShapeDtypeStruct((n_slots,), jnp.int32)],
        scratch_shapes=[pltpu.SemaphoreType.DMA(())],
        compiler_params=_cparams(("arbitrary",)),
        name="dispatch",
    )(pos_flat, xn)


def _combine_kernel(tw_ref, x1_ref, g_ref, *rest, n0):
    y_refs, (op_ref, os_ref) = rest[:TOP_K], rest[TOP_K:]
    i = pl.program_id(0)
    tw = tw_ref[...]
    tm = x1_ref.shape[0]
    lo_acc = [None] * SUBLANES
    hi_acc = [None] * SUBLANES
    for k in range(TOP_K):
        wk = tw[:, k:k + 1]
        for s in range(SUBLANES):
            lo, hi = _unpack_bf16_pair(_load_tile_rows(y_refs[k], tm, s))
            lo_acc[s] = wk * lo if k == 0 else lo_acc[s] + wk * lo
            hi_acc[s] = wk * hi if k == 0 else hi_acc[s] + wk * hi
    x2 = x1_ref[...] + jnp.concatenate(lo_acc + hi_acc, axis=1)
    var = jnp.mean(x2 * x2, axis=-1, keepdims=True)
    y = x2 * lax.rsqrt(var + EPS) * g_ref[...]

    @pl.when(i < n0)
    def _():
        op_ref[...] = y

    @pl.when(i >= n0)
    def _():
        os_ref[...] = y


def _combine(tw, x1, gain, yk, t_prompt, tm=512):
    T, D = x1.shape
    n0 = t_prompt // tm
    nblk = T // tm
    y_spec = lambda k: pl.BlockSpec((tm * SUBLANES, LANES), lambda i: (k * nblk + i, 0))
    return pl.pallas_call(
        functools.partial(_combine_kernel, n0=n0),
        grid=(T // tm,),
        in_specs=[
            pl.BlockSpec((tm, TOP_K), lambda i: (i, 0)),
            pl.BlockSpec((tm, D), lambda i: (i, 0)),
            pl.BlockSpec((1, D), lambda i: (0, 0)),
        ] + [y_spec(k) for k in range(TOP_K)],
        out_specs=[
            pl.BlockSpec((tm, D), lambda i: (jnp.minimum(i, n0 - 1), 0)),
            pl.BlockSpec((tm, D), lambda i: (jnp.maximum(i - n0, 0), 0)),
        ],
        out_shape=[jax.ShapeDtypeStruct((t_prompt, D), F32), jax.ShapeDtypeStruct((T - t_prompt, D), F32)],
        compiler_params=_cparams(("arbitrary",)),
        name="combine",
    )(tw, x1, gain.reshape(1, D), *([yk] * TOP_K))


def _gate_up_kernel(g0_ref, nt_ref, cnt_ref, w_ref, bg_ref, bu_ref, x_hbm, o_hbm,
                    wp_ref, xin_ref, xb_ref, obuf_ref, sin, sout, *, tn):
    j, e = pl.program_id(0), pl.program_id(1)
    tm = GROUP_TILE
    D = xb_ref.shape[1]
    half = MXU_DIM // 2
    n = nt_ref[e]
    row_base = g0_ref[e]
    cnt = cnt_ref[e]

    def in_copy(r, slot):
        rows = pl.ds(pl.multiple_of((row_base + r * tm) * SUBLANES, tm * SUBLANES), tm * SUBLANES)
        return pltpu.make_async_copy(x_hbm.at[rows, :], xin_ref.at[slot], sin.at[slot])

    def out_copy(r, slot):
        rows = pl.ds(pl.multiple_of(row_base + r * tm, tm), tm)
        cols = pl.ds(pl.multiple_of(j * (tn // 2), LANES), tn // 2)
        return pltpu.make_async_copy(obuf_ref.at[slot], o_hbm.at[rows, cols], sout.at[slot])

    def body(r, carry):
        slot = r % 2

        @pl.when(r + 1 < n)
        def _():
            in_copy(r + 1, 1 - slot).start()

        in_copy(r, slot).wait()

        @pl.when(r >= 2)
        def _():
            out_copy(r - 2, slot).wait()

        valid = (r * tm + lax.broadcasted_iota(jnp.int32, (tm, 1), 0)) < cnt
        for s in range(SUBLANES):
            lo, hi = _unpack_bf16_pair(_load_tile_rows(xin_ref.at[slot], tm, s))
            xb_ref[:, s * LANES:(s + 1) * LANES] = jnp.where(valid, lo, 0.0).astype(BF16)
            xb_ref[:, D // 2 + s * LANES:D // 2 + (s + 1) * LANES] = jnp.where(valid, hi, 0.0).astype(BF16)
        for cb in range(tn // MXU_DIM):
            h = jnp.dot(xb_ref[...], wp_ref[:, cb * MXU_DIM:(cb + 1) * MXU_DIM], preferred_element_type=F32)
            hg = h[:, :half] + bg_ref[0, :, cb * half:(cb + 1) * half]
            hu = h[:, half:] + bu_ref[0, :, cb * half:(cb + 1) * half]
            gate = jnp.minimum(hg, SWIGLU_LIMIT)
            up = jnp.clip(hu, -SWIGLU_LIMIT, SWIGLU_LIMIT)
            act = gate * jax.nn.sigmoid(SWIGLU_ALPHA * gate) * (up + 1.0)
            obuf_ref[slot, :, cb * half:(cb + 1) * half] = act.astype(obuf_ref.dtype)
        out_copy(r, slot).start()
        return carry

    @pl.when(n > 0)
    def _():
        in_copy(0, 0).start()
        r_i = lax.broadcasted_iota(jnp.int32, (MXU_DIM, MXU_DIM), 0)
        c_i = lax.broadcasted_iota(jnp.int32, (MXU_DIM, MXU_DIM), 1)
        perm = (((c_i < half) & (r_i == 2 * c_i)) | ((c_i >= half) & (r_i == 2 * (c_i - half) + 1))).astype(BF16)
        for cb in range(tn // MXU_DIM):
            for rb in range(D // 512):
                w = w_ref[0, rb * 512:(rb + 1) * 512, cb * MXU_DIM:(cb + 1) * MXU_DIM].astype(BF16)
                wp_ref[rb * 512:(rb + 1) * 512, cb * MXU_DIM:(cb + 1) * MXU_DIM] = jnp.dot(
                    w, perm, preferred_element_type=F32).astype(BF16)
        lax.fori_loop(0, n, body, 0)

        @pl.when(n >= 2)
        def _():
            out_copy(n - 2, n % 2).wait()

        out_copy(n - 1, (n - 1) % 2).wait()


def _gate_up(g0, ntiles, cnt, xs, w_gate_up, bg, bu, tn):
    P = xs.shape[0] // SUBLANES
    E, D, H2 = w_gate_up.shape
    tm = GROUP_TILE
    wmap = lambda j, e, *_: (e, 0, j)
    gs = pltpu.PrefetchScalarGridSpec(
        num_scalar_prefetch=3,
        grid=(H2 // tn, E),
        in_specs=[
            pl.BlockSpec((1, D, tn), wmap),
            pl.BlockSpec((1, 1, tn // 2), wmap),
            pl.BlockSpec((1, 1, tn // 2), wmap),
            pl.BlockSpec(memory_space=pl.ANY),
        ],
        out_specs=pl.BlockSpec(memory_space=pl.ANY),
        scratch_shapes=[
            pltpu.VMEM((D, tn), BF16),
            pltpu.VMEM((2, tm * SUBLANES, LANES), jnp.uint32),
            pltpu.VMEM((tm, D), BF16),
            pltpu.VMEM((2, tm, tn // 2), BF16),
            pltpu.SemaphoreType.DMA((2,)),
            pltpu.SemaphoreType.DMA((2,)),
        ],
    )
    return pl.pallas_call(
        functools.partial(_gate_up_kernel, tn=tn),
        grid_spec=gs,
        out_shape=jax.ShapeDtypeStruct((P, H2 // 2), BF16),
        compiler_params=_cparams(("arbitrary", "arbitrary")),
        name="moe_gate_up",
    )(g0, ntiles, cnt, w_gate_up, bg, bu, xs)


def _down_kernel(g0_ref, nt_ref, cnt_ref, dst_ref, w_ref, b_ref, a_hbm, y_hbm, wb_ref, ain_ref, obuf_ref, sin, sout):
    e = pl.program_id(0)
    tm = GROUP_TILE
    H, D = wb_ref.shape
    n = nt_ref[e]
    row_base = g0_ref[e]
    cnt = cnt_ref[e]
    n_real = y_hbm.shape[0] // SUBLANES - dst_ref.shape[0]

    def in_copy(r, slot):
        rows = pl.ds(pl.multiple_of(row_base + r * tm, tm), tm)
        return pltpu.make_async_copy(a_hbm.at[rows, :], ain_ref.at[slot], sin.at[slot])

    nblk = (D // 2) // MXU_DIM

    def compute(slot, out_ref, before_block=None):
        a = ain_ref[slot]
        for bi, c0 in enumerate(range(0, D // 2, MXU_DIM)):
            if before_block is not None:
                before_block(bi)
            c1 = D // 2 + c0
            lo = jnp.dot(a, wb_ref[:, c0:c0 + MXU_DIM], preferred_element_type=F32) + b_ref[0, :, c0:c0 + MXU_DIM]
            hi = jnp.dot(a, wb_ref[:, c1:c1 + MXU_DIM], preferred_element_type=F32) + b_ref[0, :, c1:c1 + MXU_DIM]
            packed = _pack_bf16_pair(lo, hi)
            for u in range(MXU_DIM // LANES):
                out_ref[pl.ds(c0 // LANES + u, tm, stride=SUBLANES), :] = packed[:, u * LANES:(u + 1) * LANES]

    def scatter_rows(q, so, part):
        base = row_base + q * tm
        for i in range(part * (tm // nblk), (part + 1) * (tm // nblk)):
            d = jnp.where(q * tm + i < cnt, dst_ref[base + i], n_real + base + i)
            pltpu.make_async_copy(obuf_ref.at[so, pl.ds(i * SUBLANES, SUBLANES), :],
                                  y_hbm.at[pl.ds(pl.multiple_of(d * SUBLANES, SUBLANES), SUBLANES), :],
                                  sout.at[so]).start()

    def scatter_tile(q, so):
        for part in range(nblk):
            scatter_rows(q, so, part)

    def wait_scatter(so):
        pltpu.make_async_copy(obuf_ref.at[so], y_hbm.at[pl.ds(0, tm * SUBLANES), :], sout.at[so]).wait()

    def body(r, carry):
        slot = r % 2

        @pl.when(r + 1 < n)
        def _():
            in_copy(r + 1, 1 - slot).start()

        in_copy(r, slot).wait()

        @pl.when(r >= 2)
        def _():
            wait_scatter(slot)

        compute(slot, obuf_ref.at[slot], functools.partial(scatter_rows, r - 1, 1 - slot))
        return carry

    @pl.when(n > 0)
    def _():
        in_copy(0, 0).start()
        for rb in range(H // 512):
            wb_ref[rb * 512:(rb + 1) * 512, :] = w_ref[0, rb * 512:(rb + 1) * 512, :].astype(BF16)

        @pl.when(n > 1)
        def _():
            in_copy(1, 1).start()

        in_copy(0, 0).wait()
        compute(0, obuf_ref.at[0])
        lax.fori_loop(1, n, body, 0)
        so = (n - 1) % 2
        scatter_tile(n - 1, so)
        wait_scatter(so)

        @pl.when(n >= 2)
        def _():
            wait_scatter(1 - so)


def _down(g0, ntiles, cnt, slot_dst, n_rows, act, wd, bd):
    P, H = act.shape
    E, _, D = wd.shape
    tm = GROUP_TILE
    wmap = lambda e, *_: (e, 0, 0)
    gs = pltpu.PrefetchScalarGridSpec(
        num_scalar_prefetch=4,
        grid=(E,),
        in_specs=[
            pl.BlockSpec((1, H, D), wmap),
            pl.BlockSpec((1, 1, D), wmap),
            pl.BlockSpec(memory_space=pl.ANY),
        ],
        out_specs=pl.BlockSpec(memory_space=pl.ANY),
        scratch_shapes=[
            pltpu.VMEM((H, D), BF16),
            pltpu.VMEM((2, tm, H), BF16),
            pltpu.VMEM((2, tm * SUBLANES, LANES), jnp.uint32),
            pltpu.SemaphoreType.DMA((2,)),
            pltpu.SemaphoreType.DMA((2,)),
        ],
    )
    return pl.pallas_call(
        _down_kernel,
        grid_spec=gs,
        out_shape=jax.ShapeDtypeStruct((n_rows * SUBLANES, LANES), jnp.uint32),
        compiler_params=_cparams(("arbitrary",)),
        name="moe_down",
    )(g0, ntiles, cnt, slot_dst, wd, bd, act)


def _trunk(xp, xs, seq_shapes, norm_mix, w_in, w_gk_up_fwd, b_gk_fwd, w_gk_up_bwd, b_gk_bwd, gla_head_norm,
           w_fnet_out, w_gla_out, w_out, norm_ffn, w_router, b_router, w_gate_up, b_gate_up,
           w_down, b_down, norm_final):
    D = xp.shape[1]
    T = xp.shape[0] + xs.shape[0]
    fw = w_fnet_out.shape[0]
    dkk = w_gk_up_fwd.shape[1]
    dvv = w_gla_out.shape[0]
    dk, dv = dkk // GLA_HEADS, dvv // GLA_HEADS
    sizes = (fw, dkk, dkk, dvv, dvv, GATE_LOW_RANK, GATE_LOW_RANK, 2 * D)
    offs = np.concatenate([[0], np.cumsum(sizes)])
    span = lambda n: (int(offs[n]), int(offs[n + 1]))
    w_main, w_lr2 = _repack_w_in(w_in, tuple(span(n) for n in (4, 7, 3, 0, 1, 2)), (int(offs[5]), int(offs[7])))
    og_blk, g0_blk, g1_blk = 0, dvv // D, dvv // D + 1
    v_off = dvv + 2 * D
    u_off = v_off + dvv
    q_off = u_off + fw
    k_off = q_off + dkk

    proj, lr = _inproj(xp, xs, norm_mix, w_main, w_lr2)
    lr_f, lr_b = lr[:, :GATE_LOW_RANK], lr[:, GATE_LOW_RANK:]

    gd = fw // FNET_GROUPS
    cc, sc = _dft_mats(gd, gd ** -0.5)
    cs = jnp.concatenate([cc, sc], axis=1).astype(BF16)
    z = _chan_dft(proj, u_off // fw, fw, cs)
    fft, o_gla = None, None
    row0 = 0
    for (B, S) in seq_shapes:
        fft = _seq_dft(z, row0, B, S, prev=fft)
        o_gla = _gla(proj, lr_f, lr_b, w_gk_up_fwd, b_gk_fwd, w_gk_up_bwd, b_gk_bwd, row0, B, S,
                     q_off // dk, k_off // dk, v_off // dv, dk, dv, prev=o_gla)
        row0 += B * S

    merged = _merge(fft, o_gla, proj, og_blk, g0_blk, g1_blk, gla_head_norm,
                    w_fnet_out.astype(BF16), w_gla_out.astype(BF16))
    x1, xn2, idx, tw, rank, cnt = _outproj_router(merged, xp, xs, w_out.astype(BF16), norm_ffn,
                                                  _hi_lo(w_router), b_router)

    E = w_router.shape[1]
    cnt = cnt.reshape(E)
    gsz = ((cnt + GROUP_TILE - 1) // GROUP_TILE) * GROUP_TILE
    gend = jnp.cumsum(gsz)
    gstart = gend - gsz
    pos = (gstart[idx] + rank).reshape(-1).astype(jnp.int32)
    n_slots = T * TOP_K + E * GROUP_TILE
    g0 = gstart.astype(jnp.int32)
    ntiles = (gsz // GROUP_TILE).astype(jnp.int32)

    x_sorted, slot_dst = _dispatch(xn2, pos, n_slots)
    H = w_down.shape[1]
    bg = b_gate_up[:, 0::2].reshape(E, 1, H)
    bu = b_gate_up[:, 1::2].reshape(E, 1, H)
    act = _gate_up(g0, ntiles, cnt, x_sorted, w_gate_up, bg, bu, GATE_UP_TN)
    yk = _down(g0, ntiles, cnt, slot_dst, T * TOP_K + n_slots, act, w_down, b_down.reshape(E, 1, D))
    return _combine(tw, x1, norm_final, yk, xp.shape[0])


def kernel(x_prompt, x_sample, norm_mix, w_in, w_gk_up_fwd, b_gk_fwd, w_gk_up_bwd, b_gk_bwd, gla_head_norm,
           w_fnet_out, w_gla_out, w_out, norm_ffn, w_router, b_router, w_gate_up, b_gate_up, w_down,
           b_down, norm_final):
    D = x_prompt.shape[-1]
    shapes = (x_prompt.shape[:2], x_sample.shape[:2])
    yp, ys = _trunk(x_prompt.reshape(-1, D), x_sample.reshape(-1, D), shapes, norm_mix[0], w_in[0], w_gk_up_fwd[0], b_gk_fwd[0], w_gk_up_bwd[0], b_gk_bwd[0],
               gla_head_norm[0], w_fnet_out[0], w_gla_out[0], w_out[0], norm_ffn[0], w_router[0],
               b_router[0], w_gate_up[0], b_gate_up[0], w_down[0], b_down[0], norm_final)
    return (yp.reshape(x_prompt.shape), ys.reshape(x_sample.shape))
```

```python
import functools
import math

import numpy as np
import jax
import jax.numpy as jnp
from jax import lax
from jax.experimental import pallas as pl
from jax.experimental.pallas import tpu as pltpu

F32 = jnp.float32
BF16 = jnp.bfloat16
HIGHEST = lax.Precision.HIGHEST

EPS = 1e-5
FNET_GROUPS = 4
GLA_HEADS = 4
GATE_LOW_RANK = 16
GATE_LOGIT_NORMALIZER = 16.0
CHUNK = 64
TOP_K = 4
SWIGLU_LIMIT = 7.0
SWIGLU_ALPHA = 1.702

VMEM_LIMIT_BYTES = 56 * 1024 * 1024
MXU_DIM = 256
LANES = 128
SUBLANES = 8
GROUP_TILE = 512
GATE_UP_TN = 2048


def _cparams(sem):
    return pltpu.CompilerParams(dimension_semantics=sem, vmem_limit_bytes=VMEM_LIMIT_BYTES)


def _split3(x):
    hi = x.astype(BF16)
    r = x - hi.astype(F32)
    mid = r.astype(BF16)
    lo = (r - mid.astype(F32)).astype(BF16)
    return hi, mid, lo


def _hi_lo(w):
    hi = w.astype(BF16)
    lo = (w - hi.astype(F32)).astype(BF16)
    return jnp.concatenate([hi, lo], axis=1)


def _repack_kernel(w_ref, o_ref, lr_ref, *, pieces, lr_cols):
    c0 = 0
    for a, b in pieces:
        o_ref[:, c0:c0 + (b - a)] = w_ref[:, a:b].astype(BF16)
        c0 += b - a
    w = w_ref[:, lr_cols[0]:lr_cols[1]]
    hi = w.astype(BF16)
    n = lr_cols[1] - lr_cols[0]
    lr_ref[:, :n] = hi
    lr_ref[:, n:] = (w - hi.astype(F32)).astype(BF16)


def _repack_w_in(w_in, pieces, lr_cols, tr=256):
    K, N = w_in.shape
    n_main = sum(b - a for a, b in pieces)
    n_lr = lr_cols[1] - lr_cols[0]
    return pl.pallas_call(
        functools.partial(_repack_kernel, pieces=pieces, lr_cols=lr_cols),
        grid=(K // tr,),
        in_specs=[pl.BlockSpec((tr, N), lambda i: (i, 0))],
        out_specs=[pl.BlockSpec((tr, n_main), lambda i: (i, 0)), pl.BlockSpec((tr, 2 * n_lr), lambda i: (i, 0))],
        out_shape=[jax.ShapeDtypeStruct((K, n_main), BF16), jax.ShapeDtypeStruct((K, 2 * n_lr), BF16)],
        compiler_params=_cparams(("parallel",)),
        name="repack_w_in",
    )(w_in)


def _inproj_kernel(xp_ref, xs_ref, g_ref, w_ref, wlr_ref, o_ref, lr_ref, xn_ref, *, n0):
    @pl.when(pl.program_id(1) == 0)
    def _():
        x = jnp.where(pl.program_id(0) < n0, xp_ref[...], xs_ref[...])
        var = jnp.mean(x * x, axis=-1, keepdims=True)
        xn = (x * lax.rsqrt(var + EPS) * g_ref[...]).astype(BF16)
        xn_ref[...] = xn
        r = jnp.dot(xn, wlr_ref[...], preferred_element_type=F32)
        nlr = lr_ref.shape[1]
        lr_ref[...] = r[:, :nlr] + r[:, nlr:]

    o_ref[...] = jnp.dot(xn_ref[...], w_ref[...], preferred_element_type=F32).astype(o_ref.dtype)


def _inproj(xp, xs, gain, w_main, w_lr2, tm=1024, tn=1024):
    D = xp.shape[1]
    T = xp.shape[0] + xs.shape[0]
    tm = tm if xp.shape[0] % tm == 0 and xs.shape[0] % tm == 0 else tm // 2
    n0 = xp.shape[0] // tm
    N = w_main.shape[1]
    R = w_lr2.shape[1] // 2
    return pl.pallas_call(
        functools.partial(_inproj_kernel, n0=n0),
        grid=(T // tm, N // tn),
        in_specs=[
            pl.BlockSpec((tm, D), lambda i, j: (jnp.minimum(i, n0 - 1), 0), pipeline_mode=pl.Buffered(1)),
            pl.BlockSpec((tm, D), lambda i, j: (jnp.maximum(i - n0, 0), 0), pipeline_mode=pl.Buffered(1)),
            pl.BlockSpec((1, D), lambda i, j: (0, 0)),
            pl.BlockSpec((D, tn), lambda i, j: (0, j)),
            pl.BlockSpec((D, 2 * R), lambda i, j: (0, 0)),
        ],
        out_specs=[
            pl.BlockSpec((tm, tn), lambda i, j: (i, j)),
            pl.BlockSpec((tm, R), lambda i, j: (i, 0)),
        ],
        out_shape=[jax.ShapeDtypeStruct((T, N), BF16), jax.ShapeDtypeStruct((T, R), F32)],
        scratch_shapes=[pltpu.VMEM((tm, D), BF16)],
        compiler_params=_cparams(("parallel", "arbitrary")),
        name="inproj",
    )(xp, xs, gain.reshape(1, D), w_main, w_lr2)


def _chan_dft_kernel(u_ref, cs_ref, z_ref, *, gd):
    for g in range(FNET_GROUPS):
        r = jnp.dot(u_ref[:, g * gd:(g + 1) * gd], cs_ref[...], preferred_element_type=F32)
        z_ref[:, g * gd:(g + 1) * gd] = _pack_bf16_pair(r[:, :gd], r[:, gd:])


def _chan_dft(proj, u_col_block, width, cs, tm=512):
    T = proj.shape[0]
    gd = width // FNET_GROUPS
    return pl.pallas_call(
        functools.partial(_chan_dft_kernel, gd=gd),
        grid=(T // tm,),
        in_specs=[
            pl.BlockSpec((tm, width), lambda i: (i, u_col_block)),
            pl.BlockSpec((gd, 2 * gd), lambda i: (0, 0)),
        ],
        out_specs=pl.BlockSpec((tm, width), lambda i: (i, 0)),
        out_shape=jax.ShapeDtypeStruct((T, width), jnp.uint32),
        compiler_params=_cparams(("parallel",)),
        name="chan_dft",
    )(proj, cs)


_FFT_COLS = 2 * LANES


def _fft_stage1_kernel(z_ref, f1_ref, ct_ref, st_ref, a_ref, *, n1, n2):
    for m in range(n2):
        rows = pl.ds(m, n1, stride=n2)
        zc, zs = _unpack_bf16_pair(z_ref[rows, :])
        pc = jnp.dot(f1_ref[...], zc.astype(BF16), preferred_element_type=F32)
        ps = jnp.dot(f1_ref[...], zs.astype(BF16), preferred_element_type=F32)
        a_re = pc[:n1] - ps[n1:]
        a_im = -ps[:n1] - pc[n1:]
        ct = ct_ref[m][:, 0:1]
        st = st_ref[m][:, 0:1]
        a_ref[rows, :] = _pack_bf16_pair(a_re * ct + a_im * st, a_im * ct - a_re * st)


def _fft_stage2_kernel(a_ref, f2_ref, *rest, n1, n2):
    o_ref = rest[-1]
    half = _FFT_COLS // 2
    for k1 in range(n1):
        a_re, a_im = _unpack_bf16_pair(a_ref[pl.ds(k1 * n2, n2), :])
        rhs = jnp.concatenate([a_re.astype(BF16), a_im.astype(BF16)], axis=0)
        x = jnp.dot(f2_ref[...], rhs, preferred_element_type=F32)
        o_ref[pl.ds(k1, n2, stride=n1), :] = _pack_bf16_pair(x[:, :half], x[:, half:])


def _seq_dft(z, row0, B, S, prev=None):
    T, W = z.shape
    wc = _FFT_COLS
    n2 = 64 if S % (64 * SUBLANES) == 0 else S // SUBLANES
    n1 = S // n2
    assert n1 * n2 == S and row0 % S == 0 and W % wc == 0
    s0 = row0 // S
    i1 = jnp.arange(n1, dtype=jnp.int32)
    i2 = jnp.arange(n2, dtype=jnp.int32)
    ang1 = ((i1[:, None] * i1[None, :]) % n1).astype(F32) * (2.0 * math.pi / n1)
    f1 = (jnp.concatenate([jnp.cos(ang1), jnp.sin(ang1)], axis=0) * S ** -0.5).astype(BF16)
    ang2 = ((i2[:, None] * i2[None, :]) % n2).astype(F32) * (2.0 * math.pi / n2)
    f2 = jnp.concatenate([jnp.cos(ang2), jnp.sin(ang2)], axis=1).astype(BF16)
    angt = (i2[:, None] * i1[None, :]).astype(F32) * (2.0 * math.pi / S)
    ct = jnp.broadcast_to(jnp.cos(angt)[:, :, None], (n2, n1, LANES))
    st = jnp.broadcast_to(jnp.sin(angt)[:, :, None], (n2, n1, LANES))

    fix2 = lambda b, c: (0, 0)
    fix3 = lambda b, c: (0, 0, 0)

    a = pl.pallas_call(
        functools.partial(_fft_stage1_kernel, n1=n1, n2=n2),
        grid=(B, W // LANES),
        in_specs=[pl.BlockSpec((S, LANES), lambda b, c: (s0 + b, c)),
                  pl.BlockSpec((2 * n1, n1), fix2),
                  pl.BlockSpec((n2, n1, LANES), fix3),
                  pl.BlockSpec((n2, n1, LANES), fix3)],
        out_specs=pl.BlockSpec((S, LANES), lambda b, c: (b, c)),
        out_shape=jax.ShapeDtypeStruct((B * S, W), jnp.uint32),
        compiler_params=_cparams(("parallel", "parallel")),
        name="fft_stage1",
    )(z, f1, ct, st)

    in_specs = [pl.BlockSpec((S, wc), lambda b, c: (b, c)), pl.BlockSpec((n2, 2 * n2), fix2)]
    args = [a, f2]
    aliases = {}
    if prev is not None:
        in_specs.append(pl.BlockSpec(memory_space=pl.ANY))
        args.append(prev)
        aliases = {2: 0}
    return pl.pallas_call(
        functools.partial(_fft_stage2_kernel, n1=n1, n2=n2),
        grid=(B, W // wc),
        in_specs=in_specs,
        out_specs=pl.BlockSpec((S, wc // 2), lambda b, c: (s0 + b, c)),
        out_shape=jax.ShapeDtypeStruct((T, W // 2), jnp.uint32),
        input_output_aliases=aliases,
        compiler_params=_cparams(("parallel", "parallel")),
        name="fft_stage2",
    )(*args)


def _dft_mats(n, scale, split=64):
    split = split if n % split == 0 else 1
    k = jnp.arange(n, dtype=jnp.int32)[None, :]
    j1 = jnp.arange(n // split, dtype=jnp.int32)[:, None]
    j2 = jnp.arange(split, dtype=jnp.int32)[:, None]
    w = 2.0 * math.pi / n
    ang_a = ((split * j1 * k) % n).astype(F32) * w
    ang_b = ((j2 * k) % n).astype(F32) * w
    ca, sa = jnp.cos(ang_a)[:, None, :], jnp.sin(ang_a)[:, None, :]
    cb, sb = (jnp.cos(ang_b) * scale)[None, :, :], (jnp.sin(ang_b) * scale)[None, :, :]
    c = (ca * cb - sa * sb).reshape(n, n)
    s = (sa * cb + ca * sb).reshape(n, n)
    return c, s


_NT = (((1,), (1,)), ((), ()))
_TN = (((0,), (0,)), ((), ()))


def _gla_block(q_ref, k_ref, v_ref, lr_ref, w3_ref, b_ref, st_ref, reverse, qscale, nchunk):
    R = nchunk * CHUNK
    lr = lr_ref[...]
    lr_hi = lr.astype(BF16)
    lr_lo = (lr - lr_hi.astype(F32)).astype(BF16)
    z = jnp.dot(jnp.concatenate([lr_hi, lr_lo, lr_hi], axis=1), w3_ref[...],
                preferred_element_type=F32) + b_ref[...]
    g = (jnp.minimum(z, 0.0) - jnp.log(1.0 + jnp.exp(-jnp.abs(z)))) * (1.0 / GATE_LOGIT_NORMALIZER)
    ri = lax.broadcasted_iota(jnp.int32, (R, R), 0)
    ci = lax.broadcasted_iota(jnp.int32, (R, R), 1)
    cum = ((ci >= ri) if reverse else (ci <= ri)).astype(BF16)
    g_hi = g.astype(BF16)
    g_lo = (g - g_hi.astype(F32)).astype(BF16)
    G = jnp.dot(cum, g_hi, preferred_element_type=F32) + jnp.dot(cum, g_lo, preferred_element_type=F32)

    dk = G.shape[1]
    zero_row = jnp.zeros((1, dk), F32)
    if reverse:
        starts = [G[(c + 1) * CHUNK:(c + 1) * CHUNK + 1, :] if c + 1 < nchunk else zero_row for c in range(nchunk)]
        ref_row, g_tot = CHUNK // 2, G[0:1, :]
    else:
        starts = [G[c * CHUNK - 1:c * CHUNK, :] if c > 0 else zero_row for c in range(nchunk)]
        ref_row, g_tot = CHUNK // 2 - 1, G[R - 1:R, :]
    bcast = lambda rows_: jnp.concatenate([jnp.broadcast_to(r_, (CHUNK, dk)) for r_ in rows_], axis=0)
    gc = G - bcast(starts)
    gref = bcast([gc[c * CHUNK + ref_row:c * CHUNK + ref_row + 1, :] for c in range(nchunk)])

    q = q_ref[...].astype(F32) * qscale
    k = k_ref[...].astype(F32)
    v = v_ref[...]
    q_in = (q * jnp.exp(gc - gref)).astype(BF16)
    k_in = (k * jnp.exp(gref - gc)).astype(BF16)
    q_it = (q * jnp.exp(gc)).astype(BF16)
    q_st = (q * jnp.exp(G)).astype(BF16)
    k_st = (k * jnp.exp(g_tot - G)).astype(BF16)

    s_diag = lax.dot_general(q_in, k_in, _NT, preferred_element_type=F32)
    same = (ri // CHUNK) == (ci // CHUNK)
    keep = same & ((ci > ri) if reverse else (ci <= ri))
    s_rows = []
    for c in range(nchunk):
        rows = slice(c * CHUNK, (c + 1) * CHUNK)
        s = jnp.where(keep[rows, :], s_diag[rows, :], 0.0)
        lo_, hi_ = ((c + 1) * CHUNK, R) if reverse else (0, c * CHUNK)
        if hi_ > lo_:
            kx = (k[lo_:hi_, :] * jnp.exp(starts[c] - G[lo_:hi_, :])).astype(BF16)
            pad = jnp.zeros((R - (hi_ - lo_), dk), BF16)
            kx = jnp.concatenate([pad, kx] if reverse else [kx, pad], axis=0)
            s = s + lax.dot_general(q_it[rows, :], kx, _NT, preferred_element_type=F32)
        s_rows.append(s.astype(BF16))
    scores = jnp.concatenate(s_rows, axis=0)

    st = st_ref[...]
    o = (jnp.dot(scores, v, preferred_element_type=F32)
         + lax.dot_general(q_st, st.astype(BF16), _NT, preferred_element_type=F32))
    st_ref[...] = st * jnp.exp(g_tot) + lax.dot_general(v, k_st, _TN, preferred_element_type=F32)
    return o


def _gla_kernel(qf_ref, kf_ref, vf_ref, qb_ref, kb_ref, vb_ref, lrf_ref, lrb_ref,
                wf_ref, bf_ref, wb_ref, bb_ref, *rest, nchunk, nsub, qscale):
    o_ref, acc_ref, stf_ref, stb_ref = rest[-4:]
    n = pl.program_id(2)
    nb = pl.num_programs(2)
    blk = nchunk * CHUNK
    rows = nsub * blk

    @pl.when(n == 0)
    def _():
        acc_ref[...] = jnp.zeros_like(acc_ref)
        stf_ref[...] = jnp.zeros_like(stf_ref)
        stb_ref[...] = jnp.zeros_like(stb_ref)

    for s_f in range(nsub):
        s_b = nsub - 1 - s_f
        sub_f, sub_b = pl.ds(s_f * blk, blk), pl.ds(s_b * blk, blk)
        o_f = _gla_block(qf_ref.at[sub_f, :], kf_ref.at[sub_f, :], vf_ref.at[sub_f, :], lrf_ref.at[sub_f, :],
                         wf_ref, bf_ref, stf_ref, False, qscale, nchunk)
        o_b = _gla_block(qb_ref.at[sub_b, :], kb_ref.at[sub_b, :], vb_ref.at[sub_b, :], lrb_ref.at[sub_b, :],
                         wb_ref, bb_ref, stb_ref, True, qscale, nchunk)
        acc_ref[pl.ds(pl.multiple_of(n * rows + s_f * blk, blk), blk), :] += o_f
        acc_ref[pl.ds(pl.multiple_of((nb - 1 - n) * rows + s_b * blk, blk), blk), :] += o_b

    @pl.when(n == nb - 1)
    def _():
        o_ref[...] = acc_ref[...].astype(o_ref.dtype)


def _hi_hi_lo(w):
    hi = w.astype(BF16)
    lo = (w - hi.astype(F32)).astype(BF16)
    return jnp.concatenate([hi, hi, lo], axis=0)


def _gla(proj, lr_f, lr_b, wup_f, b_f, wup_b, b_b, row0, B, S, q_blk0, k_blk0, v_blk0, dk, dv,
         prev=None, blk=256, nsub=4):
    T = proj.shape[0]
    nsub = nsub if S % (nsub * blk) == 0 else 1
    rows = nsub * blk
    assert S % rows == 0 and row0 % S == 0
    nb = S // rows
    rb0, sb0 = row0 // rows, row0 // S
    fmap = lambda b, n: rb0 + b * nb + n
    bmap = lambda b, n: rb0 + b * nb + (nb - 1 - n)
    in_specs = [
        pl.BlockSpec((rows, dk), lambda b, h, n: (fmap(b, n), q_blk0 + h)),
        pl.BlockSpec((rows, dk), lambda b, h, n: (fmap(b, n), k_blk0 + h)),
        pl.BlockSpec((rows, dv), lambda b, h, n: (fmap(b, n), v_blk0 + h)),
        pl.BlockSpec((rows, dk), lambda b, h, n: (bmap(b, n), q_blk0 + h)),
        pl.BlockSpec((rows, dk), lambda b, h, n: (bmap(b, n), k_blk0 + h)),
        pl.BlockSpec((rows, dv), lambda b, h, n: (bmap(b, n), v_blk0 + h)),
        pl.BlockSpec((rows, GATE_LOW_RANK), lambda b, h, n: (fmap(b, n), 0)),
        pl.BlockSpec((rows, GATE_LOW_RANK), lambda b, h, n: (bmap(b, n), 0)),
        pl.BlockSpec((3 * GATE_LOW_RANK, dk), lambda b, h, n: (0, h)),
        pl.BlockSpec((1, dk), lambda b, h, n: (0, h)),
        pl.BlockSpec((3 * GATE_LOW_RANK, dk), lambda b, h, n: (0, h)),
        pl.BlockSpec((1, dk), lambda b, h, n: (0, h)),
    ]
    args = [proj, proj, proj, proj, proj, proj, lr_f, lr_b,
            _hi_hi_lo(wup_f), b_f.reshape(1, -1), _hi_hi_lo(wup_b), b_b.reshape(1, -1)]
    aliases = {}
    if prev is not None:
        in_specs.append(pl.BlockSpec(memory_space=pl.ANY))
        args.append(prev)
        aliases = {len(args) - 1: 0}
    return pl.pallas_call(
        functools.partial(_gla_kernel, nchunk=blk // CHUNK, nsub=nsub, qscale=dk ** -0.5),
        grid=(B, GLA_HEADS, nb),
        in_specs=in_specs,
        out_specs=pl.BlockSpec((S, dv), lambda b, h, n: (sb0 + b, h)),
        out_shape=jax.ShapeDtypeStruct((T, GLA_HEADS * dv), BF16),
        scratch_shapes=[pltpu.VMEM((S, dv), F32), pltpu.VMEM((dv, dk), F32), pltpu.VMEM((dv, dk), F32)],
        input_output_aliases=aliases,
        compiler_params=_cparams(("parallel", "parallel", "arbitrary")),
        name="gla",
    )(*args)


def _merge_kernel(fft_ref, o_ref_in, og_ref, g0_ref, g1_ref, hn_ref, wf_ref, wg_ref, o_ref, a_ref, *, dv):
    half = _FFT_COLS // 2
    pieces = []
    for cb in range(fft_ref.shape[1] // half):
        pieces.extend(_unpack_bf16_pair(fft_ref[:, cb * half:(cb + 1) * half]))
    ya = jnp.dot(jnp.concatenate(pieces, axis=1).astype(BF16), wf_ref[...], preferred_element_type=F32)
    for h in range(GLA_HEADS):
        cs = slice(h * dv, (h + 1) * dv)
        o = o_ref_in[:, cs].astype(F32)
        var = jnp.mean(o * o, axis=-1, keepdims=True)
        on = o * lax.rsqrt(var + EPS) * hn_ref[...]
        og = og_ref[:, cs].astype(F32)
        a_ref[:, cs] = (on * (og * jax.nn.sigmoid(og))).astype(BF16)
    yb = jnp.dot(a_ref[...], wg_ref[...], preferred_element_type=F32)
    m = jax.nn.sigmoid(g0_ref[...].astype(F32)) * ya + jax.nn.sigmoid(g1_ref[...].astype(F32)) * yb
    o_ref[...] = m.astype(BF16)


def _merge(fft, o_gla, proj, og_blk, g0_blk, g1_blk, hn, wf, wg, tm=512):
    T, D = o_gla.shape
    FW = wf.shape[0]
    dv = D // GLA_HEADS
    const = dict(pipeline_mode=pl.Buffered(1))
    return pl.pallas_call(
        functools.partial(_merge_kernel, dv=dv),
        grid=(T // tm,),
        in_specs=[
            pl.BlockSpec((tm, FW // 2), lambda i: (i, 0)),
            pl.BlockSpec((tm, D), lambda i: (i, 0)),
            pl.BlockSpec((tm, D), lambda i: (i, og_blk)),
            pl.BlockSpec((tm, D), lambda i: (i, g0_blk)),
            pl.BlockSpec((tm, D), lambda i: (i, g1_blk)),
            pl.BlockSpec((1, dv), lambda i: (0, 0)),
            pl.BlockSpec((FW, D), lambda i: (0, 0), **const),
            pl.BlockSpec((D, D), lambda i: (0, 0), **const),
        ],
        out_specs=pl.BlockSpec((tm, D), lambda i: (i, 0)),
        out_shape=jax.ShapeDtypeStruct((T, D), BF16),
        scratch_shapes=[pltpu.VMEM((tm, D), BF16)],
        compiler_params=_cparams(("parallel",)),
        name="merge",
    )(fft, o_gla, proj, proj, proj, hn.reshape(1, dv), wf, wg)


HI16 = 0xFFFF0000


def _pack_bf16_pair(lo, hi):
    lo_bits = lax.bitcast_convert_type(lo.astype(BF16).astype(F32), jnp.uint32)
    hi_bits = lax.bitcast_convert_type(hi.astype(BF16).astype(F32), jnp.uint32)
    return (hi_bits & jnp.uint32(HI16)) | (lo_bits >> 16)


def _unpack_bf16_pair(w):
    lo = lax.bitcast_convert_type(w << 16, F32)
    hi = lax.bitcast_convert_type(w & jnp.uint32(HI16), F32)
    return lo, hi


def _store_tile_rows(ref, val):
    tm = val.shape[0]
    for s in range(SUBLANES):
        ref[pl.ds(s, tm, stride=SUBLANES), :] = val[:, s * LANES:(s + 1) * LANES]


def _load_tile_rows(ref, tm, s):
    return ref[pl.ds(s, tm, stride=SUBLANES), :]


def _outproj_router_kernel(m_ref, xp_ref, xs_ref, wo_ref, g_ref, wr_ref, br_ref,
                           x1_ref, xn_ref, idx_ref, tw_ref, rank_ref, cnt_ref, run_ref, *, n_exp, n0):
    i = pl.program_id(0)

    @pl.when(i == 0)
    def _():
        run_ref[...] = jnp.zeros_like(run_ref)

    tm, D = m_ref.shape
    x = jnp.where(i < n0, xp_ref[...], xs_ref[...])
    x1 = x + jnp.dot(m_ref[...], wo_ref[...], preferred_element_type=F32)
    x1_ref[...] = x1
    var = jnp.mean(x1 * x1, axis=-1, keepdims=True)
    xn = x1 * lax.rsqrt(var + EPS) * g_ref[...]
    _store_tile_rows(xn_ref, _pack_bf16_pair(xn[:, :D // 2], xn[:, D // 2:]))
    xh = xn.astype(BF16)
    xl = (xn - xh.astype(F32)).astype(BF16)
    r = jnp.dot(xh, wr_ref[...], preferred_element_type=F32)
    lg = (r[:, :n_exp] + r[:, n_exp:] + jnp.dot(xl, wr_ref[:, :n_exp], preferred_element_type=F32)
          + br_ref[...])

    lane = lax.broadcasted_iota(jnp.int32, (tm, n_exp), 1)
    vals, hots = [], []
    for _ in range(TOP_K):
        mx = jnp.max(lg, axis=-1, keepdims=True)
        ik = jnp.min(jnp.where(lg == mx, lane, n_exp), axis=-1, keepdims=True)
        hot = lane == ik
        vals.append(mx)
        hots.append(hot)
        lg = jnp.where(hot, -jnp.inf, lg)
    exps = [jnp.exp(v - vals[0]) for v in vals]
    denom = exps[0] + exps[1] + exps[2] + exps[3]

    sel = hots[0] | hots[1] | hots[2] | hots[3]
    sel_f = sel.astype(F32)
    r = lax.broadcasted_iota(jnp.int32, (tm, tm), 0)
    c = lax.broadcasted_iota(jnp.int32, (tm, tm), 1)
    strict = (c < r).astype(BF16)
    before = jnp.dot(strict, sel_f.astype(BF16), preferred_element_type=F32) + run_ref[...]
    run_ref[...] += jnp.sum(sel_f, axis=0, keepdims=True)
    cnt_ref[...] = run_ref[...].astype(jnp.int32)

    k4 = lax.broadcasted_iota(jnp.int32, (tm, TOP_K), 1)
    idx4 = jnp.zeros((tm, TOP_K), jnp.int32)
    w4 = jnp.zeros((tm, TOP_K), F32)
    rk4 = jnp.zeros((tm, TOP_K), jnp.int32)
    for k in range(TOP_K):
        ik = jnp.sum(jnp.where(hots[k], lane, 0), axis=-1, keepdims=True)
        rk = jnp.sum(jnp.where(hots[k], before, 0.0), axis=-1, keepdims=True).astype(jnp.int32)
        idx4 = jnp.where(k4 == k, ik, idx4)
        w4 = jnp.where(k4 == k, exps[k] / denom, w4)
        rk4 = jnp.where(k4 == k, rk, rk4)
    idx_ref[...] = idx4
    tw_ref[...] = w4
    rank_ref[...] = rk4


def _outproj_router(merged, xp, xs, wo, gain, wr2, br, tm=512):
    T, D = merged.shape
    E = wr2.shape[1] // 2
    n0 = xp.shape[0] // tm
    const = dict(pipeline_mode=pl.Buffered(1))
    row = lambda i: (i, 0)
    fix = lambda i: (0, 0)
    return pl.pallas_call(
        functools.partial(_outproj_router_kernel, n_exp=E, n0=n0),
        grid=(T // tm,),
        in_specs=[
            pl.BlockSpec((tm, D), row),
            pl.BlockSpec((tm, D), lambda i: (jnp.minimum(i, n0 - 1), 0)),
            pl.BlockSpec((tm, D), lambda i: (jnp.maximum(i - n0, 0), 0)),
            pl.BlockSpec((D, D), fix, **const),
            pl.BlockSpec((1, D), fix),
            pl.BlockSpec((D, 2 * E), fix),
            pl.BlockSpec((1, E), fix),
        ],
        out_specs=[
            pl.BlockSpec((tm, D), row),
            pl.BlockSpec((tm * SUBLANES, D // 2 // SUBLANES), row),
            pl.BlockSpec((tm, TOP_K), row),
            pl.BlockSpec((tm, TOP_K), row),
            pl.BlockSpec((tm, TOP_K), row),
            pl.BlockSpec((1, E), fix),
        ],
        out_shape=[
            jax.ShapeDtypeStruct((T, D), F32),
            jax.ShapeDtypeStruct((T * SUBLANES, D // 2 // SUBLANES), jnp.uint32),
            jax.ShapeDtypeStruct((T, TOP_K), jnp.int32),
            jax.ShapeDtypeStruct((T, TOP_K), F32),
            jax.ShapeDtypeStruct((T, TOP_K), jnp.int32),
            jax.ShapeDtypeStruct((1, E), jnp.int32),
        ],
        scratch_shapes=[pltpu.VMEM((1, E), F32)],
        compiler_params=_cparams(("arbitrary",)),
        name="outproj_router",
    )(merged, xp, xs, wo, gain.reshape(1, D), wr2, br.reshape(1, E))


def _dispatch_kernel(pos_ref, x_ref, xs_ref, dst_ref, sem, *, n_tok):
    i = pl.program_id(0)
    tm = x_ref.shape[0] // SUBLANES

    def body(t, carry):
        for k in range(TOP_K):
            p = pos_ref[t * TOP_K + k]
            dst_ref[p] = k * n_tok + i * tm + t
            pltpu.make_async_copy(x_ref.at[pl.ds(pl.multiple_of(t * SUBLANES, SUBLANES), SUBLANES), :],
                                  xs_ref.at[pl.ds(pl.multiple_of(p * SUBLANES, SUBLANES), SUBLANES), :],
                                  sem).start(priority=k % 2)
        return carry

    lax.fori_loop(0, tm, body, 0)
    for _ in range(TOP_K):
        pltpu.make_async_copy(x_ref, xs_ref.at[pl.ds(0, tm * SUBLANES), :], sem).wait()


def _dispatch(xn, pos_flat, n_slots, tm=1024):
    T = xn.shape[0] // SUBLANES
    return pl.pallas_call(
        functools.partial(_dispatch_kernel, n_tok=T),
        grid=(T // tm,),
        in_specs=[
            pl.BlockSpec((tm * TOP_K,), lambda i: (i,), memory_space=pltpu.SMEM),
            pl.BlockSpec((tm * SUBLANES, LANES), lambda i: (i, 0)),
        ],
        out_specs=[
            pl.BlockSpec(memory_space=pl.ANY),
            pl.BlockSpec((n_slots,), lambda i: (0,), memory_space=pltpu.SMEM),
        ],
        out_shape=[jax.ShapeDtypeStruct((n_slots * SUBLANES, LANES), xn.dtype),
                   jax.ShapeDtypeStruct((n_slots,), jnp.int32)],
        scratch_shapes=[pltpu.SemaphoreType.DMA(())],
        compiler_params=_cparams(("arbitrary",)),
        name="dispatch",
    )(pos_flat, xn)


def _combine_kernel(tw_ref, x1_ref, g_ref, *rest, n0):
    y_refs, (op_ref, os_ref) = rest[:TOP_K], rest[TOP_K:]
    i = pl.program_id(0)
    tw = tw_ref[...]
    tm = x1_ref.shape[0]
    lo_acc = [None] * SUBLANES
    hi_acc = [None] * SUBLANES
    for k in range(TOP_K):
        wk = tw[:, k:k + 1]
        for s in range(SUBLANES):
            lo, hi = _unpack_bf16_pair(_load_tile_rows(y_refs[k], tm, s))
            lo_acc[s] = wk * lo if k == 0 else lo_acc[s] + wk * lo
            hi_acc[s] = wk * hi if k == 0 else hi_acc[s] + wk * hi
    x2 = x1_ref[...] + jnp.concatenate(lo_acc + hi_acc, axis=1)
    var = jnp.mean(x2 * x2, axis=-1, keepdims=True)
    y = x2 * lax.rsqrt(var + EPS) * g_ref[...]

    @pl.when(i < n0)
    def _():
        op_ref[...] = y

    @pl.when(i >= n0)
    def _():
        os_ref[...] = y


def _combine(tw, x1, gain, yk, t_prompt, tm=256):
    T, D = x1.shape
    n0 = t_prompt // tm
    nblk = T // tm
    y_spec = lambda k: pl.BlockSpec((tm * SUBLANES, LANES), lambda i: (k * nblk + i, 0))
    return pl.pallas_call(
        functools.partial(_combine_kernel, n0=n0),
        grid=(T // tm,),
        in_specs=[
            pl.BlockSpec((tm, TOP_K), lambda i: (i, 0)),
            pl.BlockSpec((tm, D), lambda i: (i, 0)),
            pl.BlockSpec((1, D), lambda i: (0, 0)),
        ] + [y_spec(k) for k in range(TOP_K)],
        out_specs=[
            pl.BlockSpec((tm, D), lambda i: (jnp.minimum(i, n0 - 1), 0)),
            pl.BlockSpec((tm, D), lambda i: (jnp.maximum(i - n0, 0), 0)),
        ],
        out_shape=[jax.ShapeDtypeStruct((t_prompt, D), F32), jax.ShapeDtypeStruct((T - t_prompt, D), F32)],
        compiler_params=_cparams(("arbitrary",)),
        name="combine",
    )(tw, x1, gain.reshape(1, D), *([yk] * TOP_K))


def _gate_up_kernel(g0_ref, nt_ref, cnt_ref, w_ref, bg_ref, bu_ref, x_hbm, o_hbm,
                    wp_ref, xin_ref, xb_ref, obuf_ref, sin, sout, *, tn):
    j, e = pl.program_id(0), pl.program_id(1)
    tm = GROUP_TILE
    D = xb_ref.shape[1]
    half = MXU_DIM // 2
    n = nt_ref[e]
    row_base = g0_ref[e]
    cnt = cnt_ref[e]

    def in_copy(r, slot):
        rows = pl.ds(pl.multiple_of((row_base + r * tm) * SUBLANES, tm * SUBLANES), tm * SUBLANES)
        return pltpu.make_async_copy(x_hbm.at[rows, :], xin_ref.at[slot], sin.at[slot])

    def out_copy(r, slot):
        rows = pl.ds(pl.multiple_of(row_base + r * tm, tm), tm)
        cols = pl.ds(pl.multiple_of(j * (tn // 2), LANES), tn // 2)
        return pltpu.make_async_copy(obuf_ref.at[slot], o_hbm.at[rows, cols], sout.at[slot])

    def body(r, carry):
        slot = r % 2

        @pl.when(r + 1 < n)
        def _():
            in_copy(r + 1, 1 - slot).start()

        in_copy(r, slot).wait()

        @pl.when(r >= 2)
        def _():
            out_copy(r - 2, slot).wait()

        valid = (r * tm + lax.broadcasted_iota(jnp.int32, (tm, 1), 0)) < cnt
        for s in range(SUBLANES):
            lo, hi = _unpack_bf16_pair(_load_tile_rows(xin_ref.at[slot], tm, s))
            xb_ref[:, s * LANES:(s + 1) * LANES] = jnp.where(valid, lo, 0.0).astype(BF16)
            xb_ref[:, D // 2 + s * LANES:D // 2 + (s + 1) * LANES] = jnp.where(valid, hi, 0.0).astype(BF16)
        for cb in range(tn // MXU_DIM):
            h = jnp.dot(xb_ref[...], wp_ref[:, cb * MXU_DIM:(cb + 1) * MXU_DIM], preferred_element_type=F32)
            hg = h[:, :half] + bg_ref[0, :, cb * half:(cb + 1) * half]
            hu = h[:, half:] + bu_ref[0, :, cb * half:(cb + 1) * half]
            gate = jnp.minimum(hg, SWIGLU_LIMIT)
            up = jnp.clip(hu, -SWIGLU_LIMIT, SWIGLU_LIMIT)
            act = gate * jax.nn.sigmoid(SWIGLU_ALPHA * gate) * (up + 1.0)
            obuf_ref[slot, :, cb * half:(cb + 1) * half] = act.astype(obuf_ref.dtype)
        out_copy(r, slot).start()
        return carry

    @pl.when(n > 0)
    def _():
        in_copy(0, 0).start()
        r_i = lax.broadcasted_iota(jnp.int32, (MXU_DIM, MXU_DIM), 0)
        c_i = lax.broadcasted_iota(jnp.int32, (MXU_DIM, MXU_DIM), 1)
        perm = (((c_i < half) & (r_i == 2 * c_i)) | ((c_i >= half) & (r_i == 2 * (c_i - half) + 1))).astype(BF16)
        for cb in range(tn // MXU_DIM):
            for rb in range(D // 512):
                w = w_ref[0, rb * 512:(rb + 1) * 512, cb * MXU_DIM:(cb + 1) * MXU_DIM].astype(BF16)
                wp_ref[rb * 512:(rb + 1) * 512, cb * MXU_DIM:(cb + 1) * MXU_DIM] = jnp.dot(
                    w, perm, preferred_element_type=F32).astype(BF16)
        lax.fori_loop(0, n, body, 0)

        @pl.when(n >= 2)
        def _():
            out_copy(n - 2, n % 2).wait()

        out_copy(n - 1, (n - 1) % 2).wait()


def _gate_up(g0, ntiles, cnt, xs, w_gate_up, bg, bu, tn):
    P = xs.shape[0] // SUBLANES
    E, D, H2 = w_gate_up.shape
    tm = GROUP_TILE
    wmap = lambda j, e, *_: (e, 0, j)
    gs = pltpu.PrefetchScalarGridSpec(
        num_scalar_prefetch=3,
        grid=(H2 // tn, E),
        in_specs=[
            pl.BlockSpec((1, D, tn), wmap),
            pl.BlockSpec((1, 1, tn // 2), wmap),
            pl.BlockSpec((1, 1, tn // 2), wmap),
            pl.BlockSpec(memory_space=pl.ANY),
        ],
        out_specs=pl.BlockSpec(memory_space=pl.ANY),
        scratch_shapes=[
            pltpu.VMEM((D, tn), BF16),
            pltpu.VMEM((2, tm * SUBLANES, LANES), jnp.uint32),
            pltpu.VMEM((tm, D), BF16),
            pltpu.VMEM((2, tm, tn // 2), BF16),
            pltpu.SemaphoreType.DMA((2,)),
            pltpu.SemaphoreType.DMA((2,)),
        ],
    )
    return pl.pallas_call(
        functools.partial(_gate_up_kernel, tn=tn),
        grid_spec=gs,
        out_shape=jax.ShapeDtypeStruct((P, H2 // 2), BF16),
        compiler_params=_cparams(("arbitrary", "arbitrary")),
        name="moe_gate_up",
    )(g0, ntiles, cnt, w_gate_up, bg, bu, xs)


def _down_kernel(g0_ref, nt_ref, cnt_ref, dst_ref, w_ref, b_ref, a_hbm, y_hbm, wb_ref, ain_ref, obuf_ref, sin, sout):
    e = pl.program_id(0)
    tm = GROUP_TILE
    H, D = wb_ref.shape
    n = nt_ref[e]
    row_base = g0_ref[e]
    cnt = cnt_ref[e]
    n_real = y_hbm.shape[0] // SUBLANES - dst_ref.shape[0]

    def in_copy(r, slot):
        rows = pl.ds(pl.multiple_of(row_base + r * tm, tm), tm)
        return pltpu.make_async_copy(a_hbm.at[rows, :], ain_ref.at[slot], sin.at[slot])

    nblk = (D // 2) // MXU_DIM

    def compute(slot, out_ref, before_block=None):
        a = ain_ref[slot]
        for bi, c0 in enumerate(range(0, D // 2, MXU_DIM)):
            if before_block is not None:
                before_block(bi)
            c1 = D // 2 + c0
            lo = jnp.dot(a, wb_ref[:, c0:c0 + MXU_DIM], preferred_element_type=F32) + b_ref[0, :, c0:c0 + MXU_DIM]
            hi = jnp.dot(a, wb_ref[:, c1:c1 + MXU_DIM], preferred_element_type=F32) + b_ref[0, :, c1:c1 + MXU_DIM]
            packed = _pack_bf16_pair(lo, hi)
            for u in range(MXU_DIM // LANES):
                out_ref[pl.ds(c0 // LANES + u, tm, stride=SUBLANES), :] = packed[:, u * LANES:(u + 1) * LANES]

    def scatter_rows(q, so, part):
        base = row_base + q * tm
        for i in range(part * (tm // nblk), (part + 1) * (tm // nblk)):
            d = jnp.where(q * tm + i < cnt, dst_ref[base + i], n_real + base + i)
            pltpu.make_async_copy(obuf_ref.at[so, pl.ds(i * SUBLANES, SUBLANES), :],
                                  y_hbm.at[pl.ds(pl.multiple_of(d * SUBLANES, SUBLANES), SUBLANES), :],
                                  sout.at[so]).start(priority=i % 2)

    def scatter_tile(q, so):
        for part in range(nblk):
            scatter_rows(q, so, part)

    def wait_scatter(so):
        pltpu.make_async_copy(obuf_ref.at[so], y_hbm.at[pl.ds(0, tm * SUBLANES), :], sout.at[so]).wait()

    def body(r, carry):
        slot = r % 2

        @pl.when(r + 1 < n)
        def _():
            in_copy(r + 1, 1 - slot).start()

        in_copy(r, slot).wait()

        @pl.when(r >= 2)
        def _():
            wait_scatter(slot)

        compute(slot, obuf_ref.at[slot], functools.partial(scatter_rows, r - 1, 1 - slot))
        return carry

    @pl.when(n > 0)
    def _():
        in_copy(0, 0).start()
        for rb in range(H // 512):
            wb_ref[rb * 512:(rb + 1) * 512, :] = w_ref[0, rb * 512:(rb + 1) * 512, :].astype(BF16)

        @pl.when(n > 1)
        def _():
            in_copy(1, 1).start()

        in_copy(0, 0).wait()
        compute(0, obuf_ref.at[0])
        lax.fori_loop(1, n, body, 0)
        so = (n - 1) % 2
        scatter_tile(n - 1, so)
        wait_scatter(so)

        @pl.when(n >= 2)
        def _():
            wait_scatter(1 - so)


def _down(g0, ntiles, cnt, slot_dst, n_rows, act, wd, bd):
    P, H = act.shape
    E, _, D = wd.shape
    tm = GROUP_TILE
    wmap = lambda e, *_: (e, 0, 0)
    gs = pltpu.PrefetchScalarGridSpec(
        num_scalar_prefetch=4,
        grid=(E,),
        in_specs=[
            pl.BlockSpec((1, H, D), wmap),
            pl.BlockSpec((1, 1, D), wmap),
            pl.BlockSpec(memory_space=pl.ANY),
        ],
        out_specs=pl.BlockSpec(memory_space=pl.ANY),
        scratch_shapes=[
            pltpu.VMEM((H, D), BF16),
            pltpu.VMEM((2, tm, H), BF16),
            pltpu.VMEM((2, tm * SUBLANES, LANES), jnp.uint32),
            pltpu.SemaphoreType.DMA((2,)),
            pltpu.SemaphoreType.DMA((2,)),
        ],
    )
    return pl.pallas_call(
        _down_kernel,
        grid_spec=gs,
        out_shape=jax.ShapeDtypeStruct((n_rows * SUBLANES, LANES), jnp.uint32),
        compiler_params=_cparams(("arbitrary",)),
        name="moe_down",
    )(g0, ntiles, cnt, slot_dst, wd, bd, act)


def _trunk(xp, xs, seq_shapes, norm_mix, w_in, w_gk_up_fwd, b_gk_fwd, w_gk_up_bwd, b_gk_bwd, gla_head_norm,
           w_fnet_out, w_gla_out, w_out, norm_ffn, w_router, b_router, w_gate_up, b_gate_up,
           w_down, b_down, norm_final):
    D = xp.shape[1]
    T = xp.shape[0] + xs.shape[0]
    fw = w_fnet_out.shape[0]
    dkk = w_gk_up_fwd.shape[1]
    dvv = w_gla_out.shape[0]
    dk, dv = dkk // GLA_HEADS, dvv // GLA_HEADS
    sizes = (fw, dkk, dkk, dvv, dvv, GATE_LOW_RANK, GATE_LOW_RANK, 2 * D)
    offs = np.concatenate([[0], np.cumsum(sizes)])
    span = lambda n: (int(offs[n]), int(offs[n + 1]))
    w_main, w_lr2 = _repack_w_in(w_in, tuple(span(n) for n in (4, 7, 3, 0, 1, 2)), (int(offs[5]), int(offs[7])))
    og_blk, g0_blk, g1_blk = 0, dvv // D, dvv // D + 1
    v_off = dvv + 2 * D
    u_off = v_off + dvv
    q_off = u_off + fw
    k_off = q_off + dkk

    proj, lr = _inproj(xp, xs, norm_mix, w_main, w_lr2)
    lr_f, lr_b = lr[:, :GATE_LOW_RANK], lr[:, GATE_LOW_RANK:]

    gd = fw // FNET_GROUPS
    cc, sc = _dft_mats(gd, gd ** -0.5)
    cs = jnp.concatenate([cc, sc], axis=1).astype(BF16)
    z = _chan_dft(proj, u_off // fw, fw, cs)
    fft, o_gla = None, None
    row0 = 0
    for (B, S) in seq_shapes:
        fft = _seq_dft(z, row0, B, S, prev=fft)
        o_gla = _gla(proj, lr_f, lr_b, w_gk_up_fwd, b_gk_fwd, w_gk_up_bwd, b_gk_bwd, row0, B, S,
                     q_off // dk, k_off // dk, v_off // dv, dk, dv, prev=o_gla)
        row0 += B * S

    merged = _merge(fft, o_gla, proj, og_blk, g0_blk, g1_blk, gla_head_norm,
                    w_fnet_out.astype(BF16), w_gla_out.astype(BF16))
    x1, xn2, idx, tw, rank, cnt = _outproj_router(merged, xp, xs, w_out.astype(BF16), norm_ffn,
                                                  _hi_lo(w_router), b_router)

    E = w_router.shape[1]
    cnt = cnt.reshape(E)
    gsz = ((cnt + GROUP_TILE - 1) // GROUP_TILE) * GROUP_TILE
    gend = jnp.cumsum(gsz)
    gstart = gend - gsz
    pos = (gstart[idx] + rank).reshape(-1).astype(jnp.int32)
    n_slots = T * TOP_K + E * GROUP_TILE
    g0 = gstart.astype(jnp.int32)
    ntiles = (gsz // GROUP_TILE).astype(jnp.int32)

    x_sorted, slot_dst = _dispatch(xn2, pos, n_slots)
    H = w_down.shape[1]
    bg = b_gate_up[:, 0::2].reshape(E, 1, H)
    bu = b_gate_up[:, 1::2].reshape(E, 1, H)
    act = _gate_up(g0, ntiles, cnt, x_sorted, w_gate_up, bg, bu, GATE_UP_TN)
    yk = _down(g0, ntiles, cnt, slot_dst, T * TOP_K + n_slots, act, w_down, b_down.reshape(E, 1, D))
    return _combine(tw, x1, norm_final, yk, xp.shape[0])


def kernel(x_prompt, x_sample, norm_mix, w_in, w_gk_up_fwd, b_gk_fwd, w_gk_up_bwd, b_gk_bwd, gla_head_norm,
           w_fnet_out, w_gla_out, w_out, norm_ffn, w_router, b_router, w_gate_up, b_gate_up, w_down,
           b_down, norm_final):
    D = x_prompt.shape[-1]
    shapes = (x_prompt.shape[:2], x_sample.shape[:2])
    yp, ys = _trunk(x_prompt.reshape(-1, D), x_sample.reshape(-1, D), shapes, norm_mix[0], w_in[0], w_gk_up_fwd[0], b_gk_fwd[0], w_gk_up_bwd[0], b_gk_bwd[0],
               gla_head_norm[0], w_fnet_out[0], w_gla_out[0], w_out[0], norm_ffn[0], w_router[0],
               b_router[0], w_gate_up[0], b_gate_up[0], w_down[0], b_down[0], norm_final)
    return (yp.reshape(x_prompt.shape), ys.reshape(x_sample.shape))
```

```python
import functools
import math

import numpy as np
import jax
import jax.numpy as jnp
from jax import lax
from jax.experimental import pallas as pl
from jax.experimental.pallas import tpu as pltpu

F32 = jnp.float32
BF16 = jnp.bfloat16
HIGHEST = lax.Precision.HIGHEST

EPS = 1e-5
FNET_GROUPS = 4
GLA_HEADS = 4
GATE_LOW_RANK = 16
GATE_LOGIT_NORMALIZER = 16.0
CHUNK = 64
TOP_K = 4
SWIGLU_LIMIT = 7.0
SWIGLU_ALPHA = 1.702

VMEM_LIMIT_BYTES = 56 * 1024 * 1024
MXU_DIM = 256
LANES = 128
SUBLANES = 8
GROUP_TILE = 512
GATE_UP_TN = 2048


def _cparams(sem):
    return pltpu.CompilerParams(dimension_semantics=sem, vmem_limit_bytes=VMEM_LIMIT_BYTES)


def _split3(x):
    hi = x.astype(BF16)
    r = x - hi.astype(F32)
    mid = r.astype(BF16)
    lo = (r - mid.astype(F32)).astype(BF16)
    return hi, mid, lo


def _hi_lo(w):
    hi = w.astype(BF16)
    lo = (w - hi.astype(F32)).astype(BF16)
    return jnp.concatenate([hi, lo], axis=1)


def _repack_kernel(w_ref, o_ref, lr_ref, *, pieces, lr_cols):
    c0 = 0
    for a, b in pieces:
        o_ref[:, c0:c0 + (b - a)] = w_ref[:, a:b].astype(BF16)
        c0 += b - a
    w = w_ref[:, lr_cols[0]:lr_cols[1]]
    hi = w.astype(BF16)
    n = lr_cols[1] - lr_cols[0]
    lr_ref[:, :n] = hi
    lr_ref[:, n:] = (w - hi.astype(F32)).astype(BF16)


def _repack_w_in(w_in, pieces, lr_cols, tr=256):
    K, N = w_in.shape
    n_main = sum(b - a for a, b in pieces)
    n_lr = lr_cols[1] - lr_cols[0]
    return pl.pallas_call(
        functools.partial(_repack_kernel, pieces=pieces, lr_cols=lr_cols),
        grid=(K // tr,),
        in_specs=[pl.BlockSpec((tr, N), lambda i: (i, 0))],
        out_specs=[pl.BlockSpec((tr, n_main), lambda i: (i, 0)), pl.BlockSpec((tr, 2 * n_lr), lambda i: (i, 0))],
        out_shape=[jax.ShapeDtypeStruct((K, n_main), BF16), jax.ShapeDtypeStruct((K, 2 * n_lr), BF16)],
        compiler_params=_cparams(("parallel",)),
        name="repack_w_in",
    )(w_in)


def _inproj_kernel(xp_ref, xs_ref, g_ref, w_ref, wlr_ref, o_ref, lr_ref, xn_ref, *, n0):
    @pl.when(pl.program_id(1) == 0)
    def _():
        x = jnp.where(pl.program_id(0) < n0, xp_ref[...], xs_ref[...])
        var = jnp.mean(x * x, axis=-1, keepdims=True)
        xn = (x * lax.rsqrt(var + EPS) * g_ref[...]).astype(BF16)
        xn_ref[...] = xn
        r = jnp.dot(xn, wlr_ref[...], preferred_element_type=F32)
        nlr = lr_ref.shape[1]
        lr_ref[...] = r[:, :nlr] + r[:, nlr:]

    o_ref[...] = jnp.dot(xn_ref[...], w_ref[...], preferred_element_type=F32).astype(o_ref.dtype)


def _inproj(xp, xs, gain, w_main, w_lr2, tm=1024, tn=1024):
    D = xp.shape[1]
    T = xp.shape[0] + xs.shape[0]
    tm = tm if xp.shape[0] % tm == 0 and xs.shape[0] % tm == 0 else tm // 2
    n0 = xp.shape[0] // tm
    N = w_main.shape[1]
    R = w_lr2.shape[1] // 2
    return pl.pallas_call(
        functools.partial(_inproj_kernel, n0=n0),
        grid=(T // tm, N // tn),
        in_specs=[
            pl.BlockSpec((tm, D), lambda i, j: (jnp.minimum(i, n0 - 1), 0), pipeline_mode=pl.Buffered(1)),
            pl.BlockSpec((tm, D), lambda i, j: (jnp.maximum(i - n0, 0), 0), pipeline_mode=pl.Buffered(1)),
            pl.BlockSpec((1, D), lambda i, j: (0, 0)),
            pl.BlockSpec((D, tn), lambda i, j: (0, j)),
            pl.BlockSpec((D, 2 * R), lambda i, j: (0, 0)),
        ],
        out_specs=[
            pl.BlockSpec((tm, tn), lambda i, j: (i, j)),
            pl.BlockSpec((tm, R), lambda i, j: (i, 0)),
        ],
        out_shape=[jax.ShapeDtypeStruct((T, N), BF16), jax.ShapeDtypeStruct((T, R), F32)],
        scratch_shapes=[pltpu.VMEM((tm, D), BF16)],
        compiler_params=_cparams(("parallel", "arbitrary")),
        name="inproj",
    )(xp, xs, gain.reshape(1, D), w_main, w_lr2)


def _chan_dft_kernel(u_ref, cs_ref, z_ref, *, gd):
    for g in range(FNET_GROUPS):
        r = jnp.dot(u_ref[:, g * gd:(g + 1) * gd], cs_ref[...], preferred_element_type=F32)
        z_ref[:, g * gd:(g + 1) * gd] = _pack_bf16_pair(r[:, :gd], r[:, gd:])


def _chan_dft(proj, u_col_block, width, cs, tm=512):
    T = proj.shape[0]
    gd = width // FNET_GROUPS
    return pl.pallas_call(
        functools.partial(_chan_dft_kernel, gd=gd),
        grid=(T // tm,),
        in_specs=[
            pl.BlockSpec((tm, width), lambda i: (i, u_col_block)),
            pl.BlockSpec((gd, 2 * gd), lambda i: (0, 0)),
        ],
        out_specs=pl.BlockSpec((tm, width), lambda i: (i, 0)),
        out_shape=jax.ShapeDtypeStruct((T, width), jnp.uint32),
        compiler_params=_cparams(("parallel",)),
        name="chan_dft",
    )(proj, cs)


_FFT_COLS = 2 * LANES


def _fft_stage1_kernel(z_ref, f1_ref, ct_ref, st_ref, a_ref, *, n1, n2):
    for m in range(n2):
        rows = pl.ds(m, n1, stride=n2)
        zc, zs = _unpack_bf16_pair(z_ref[rows, :])
        pc = jnp.dot(f1_ref[...], zc.astype(BF16), preferred_element_type=F32)
        ps = jnp.dot(f1_ref[...], zs.astype(BF16), preferred_element_type=F32)
        a_re = pc[:n1] - ps[n1:]
        a_im = -ps[:n1] - pc[n1:]
        ct = ct_ref[m][:, 0:1]
        st = st_ref[m][:, 0:1]
        a_ref[rows, :] = _pack_bf16_pair(a_re * ct + a_im * st, a_im * ct - a_re * st)


def _fft_stage2_kernel(a_ref, f2_ref, *rest, n1, n2):
    o_ref = rest[-1]
    half = _FFT_COLS // 2
    for k1 in range(n1):
        a_re, a_im = _unpack_bf16_pair(a_ref[pl.ds(k1 * n2, n2), :])
        rhs = jnp.concatenate([a_re.astype(BF16), a_im.astype(BF16)], axis=0)
        x = jnp.dot(f2_ref[...], rhs, preferred_element_type=F32)
        o_ref[pl.ds(k1, n2, stride=n1), :] = _pack_bf16_pair(x[:, :half], x[:, half:])


def _seq_dft(z, row0, B, S, prev=None):
    T, W = z.shape
    wc = _FFT_COLS
    n2 = 64 if S % (64 * SUBLANES) == 0 else S // SUBLANES
    n1 = S // n2
    assert n1 * n2 == S and row0 % S == 0 and W % wc == 0
    s0 = row0 // S
    i1 = jnp.arange(n1, dtype=jnp.int32)
    i2 = jnp.arange(n2, dtype=jnp.int32)
    ang1 = ((i1[:, None] * i1[None, :]) % n1).astype(F32) * (2.0 * math.pi / n1)
    f1 = (jnp.concatenate([jnp.cos(ang1), jnp.sin(ang1)], axis=0) * S ** -0.5).astype(BF16)
    ang2 = ((i2[:, None] * i2[None, :]) % n2).astype(F32) * (2.0 * math.pi / n2)
    f2 = jnp.concatenate([jnp.cos(ang2), jnp.sin(ang2)], axis=1).astype(BF16)
    angt = (i2[:, None] * i1[None, :]).astype(F32) * (2.0 * math.pi / S)
    ct = jnp.broadcast_to(jnp.cos(angt)[:, :, None], (n2, n1, LANES))
    st = jnp.broadcast_to(jnp.sin(angt)[:, :, None], (n2, n1, LANES))

    fix2 = lambda b, c: (0, 0)
    fix3 = lambda b, c: (0, 0, 0)

    a = pl.pallas_call(
        functools.partial(_fft_stage1_kernel, n1=n1, n2=n2),
        grid=(B, W // LANES),
        in_specs=[pl.BlockSpec((S, LANES), lambda b, c: (s0 + b, c)),
                  pl.BlockSpec((2 * n1, n1), fix2),
                  pl.BlockSpec((n2, n1, LANES), fix3),
                  pl.BlockSpec((n2, n1, LANES), fix3)],
        out_specs=pl.BlockSpec((S, LANES), lambda b, c: (b, c)),
        out_shape=jax.ShapeDtypeStruct((B * S, W), jnp.uint32),
        compiler_params=_cparams(("parallel", "parallel")),
        name="fft_stage1",
    )(z, f1, ct, st)

    in_specs = [pl.BlockSpec((S, wc), lambda b, c: (b, c)), pl.BlockSpec((n2, 2 * n2), fix2)]
    args = [a, f2]
    aliases = {}
    if prev is not None:
        in_specs.append(pl.BlockSpec(memory_space=pl.ANY))
        args.append(prev)
        aliases = {2: 0}
    return pl.pallas_call(
        functools.partial(_fft_stage2_kernel, n1=n1, n2=n2),
        grid=(B, W // wc),
        in_specs=in_specs,
        out_specs=pl.BlockSpec((S, wc // 2), lambda b, c: (s0 + b, c)),
        out_shape=jax.ShapeDtypeStruct((T, W // 2), jnp.uint32),
        input_output_aliases=aliases,
        compiler_params=_cparams(("parallel", "parallel")),
        name="fft_stage2",
    )(*args)


def _dft_mats(n, scale, split=64):
    split = split if n % split == 0 else 1
    k = jnp.arange(n, dtype=jnp.int32)[None, :]
    j1 = jnp.arange(n // split, dtype=jnp.int32)[:, None]
    j2 = jnp.arange(split, dtype=jnp.int32)[:, None]
    w = 2.0 * math.pi / n
    ang_a = ((split * j1 * k) % n).astype(F32) * w
    ang_b = ((j2 * k) % n).astype(F32) * w
    ca, sa = jnp.cos(ang_a)[:, None, :], jnp.sin(ang_a)[:, None, :]
    cb, sb = (jnp.cos(ang_b) * scale)[None, :, :], (jnp.sin(ang_b) * scale)[None, :, :]
    c = (ca * cb - sa * sb).reshape(n, n)
    s = (sa * cb + ca * sb).reshape(n, n)
    return c, s


_NT = (((1,), (1,)), ((), ()))
_TN = (((0,), (0,)), ((), ()))


def _gla_block(q_ref, k_ref, v_ref, lr_ref, w3_ref, b_ref, st_ref, reverse, qscale, nchunk):
    R = nchunk * CHUNK
    lr = lr_ref[...]
    lr_hi = lr.astype(BF16)
    lr_lo = (lr - lr_hi.astype(F32)).astype(BF16)
    z = jnp.dot(jnp.concatenate([lr_hi, lr_lo, lr_hi], axis=1), w3_ref[...],
                preferred_element_type=F32) + b_ref[...]
    g = (jnp.minimum(z, 0.0) - jnp.log(1.0 + jnp.exp(-jnp.abs(z)))) * (1.0 / GATE_LOGIT_NORMALIZER)
    ri = lax.broadcasted_iota(jnp.int32, (R, R), 0)
    ci = lax.broadcasted_iota(jnp.int32, (R, R), 1)
    cum = ((ci >= ri) if reverse else (ci <= ri)).astype(BF16)
    g_hi = g.astype(BF16)
    g_lo = (g - g_hi.astype(F32)).astype(BF16)
    G = jnp.dot(cum, g_hi, preferred_element_type=F32) + jnp.dot(cum, g_lo, preferred_element_type=F32)

    dk = G.shape[1]
    zero_row = jnp.zeros((1, dk), F32)
    if reverse:
        starts = [G[(c + 1) * CHUNK:(c + 1) * CHUNK + 1, :] if c + 1 < nchunk else zero_row for c in range(nchunk)]
        ref_row, g_tot = CHUNK // 2, G[0:1, :]
    else:
        starts = [G[c * CHUNK - 1:c * CHUNK, :] if c > 0 else zero_row for c in range(nchunk)]
        ref_row, g_tot = CHUNK // 2 - 1, G[R - 1:R, :]
    bcast = lambda rows_: jnp.concatenate([jnp.broadcast_to(r_, (CHUNK, dk)) for r_ in rows_], axis=0)
    gc = G - bcast(starts)
    gref = bcast([gc[c * CHUNK + ref_row:c * CHUNK + ref_row + 1, :] for c in range(nchunk)])

    q = q_ref[...].astype(F32) * qscale
    k = k_ref[...].astype(F32)
    v = v_ref[...]
    q_in = (q * jnp.exp(gc - gref)).astype(BF16)
    k_in = (k * jnp.exp(gref - gc)).astype(BF16)
    q_it = (q * jnp.exp(gc)).astype(BF16)
    q_st = (q * jnp.exp(G)).astype(BF16)
    k_st = (k * jnp.exp(g_tot - G)).astype(BF16)

    s_diag = lax.dot_general(q_in, k_in, _NT, preferred_element_type=F32)
    same = (ri // CHUNK) == (ci // CHUNK)
    keep = same & ((ci > ri) if reverse else (ci <= ri))
    s_rows = []
    for c in range(nchunk):
        rows = slice(c * CHUNK, (c + 1) * CHUNK)
        s = jnp.where(keep[rows, :], s_diag[rows, :], 0.0)
        lo_, hi_ = ((c + 1) * CHUNK, R) if reverse else (0, c * CHUNK)
        if hi_ > lo_:
            kx = (k[lo_:hi_, :] * jnp.exp(starts[c] - G[lo_:hi_, :])).astype(BF16)
            pad = jnp.zeros((R - (hi_ - lo_), dk), BF16)
            kx = jnp.concatenate([pad, kx] if reverse else [kx, pad], axis=0)
            s = s + lax.dot_general(q_it[rows, :], kx, _NT, preferred_element_type=F32)
        s_rows.append(s.astype(BF16))
    scores = jnp.concatenate(s_rows, axis=0)

    st = st_ref[...]
    o = (jnp.dot(scores, v, preferred_element_type=F32)
         + lax.dot_general(q_st, st.astype(BF16), _NT, preferred_element_type=F32))
    st_ref[...] = st * jnp.exp(g_tot) + lax.dot_general(v, k_st, _TN, preferred_element_type=F32)
    return o


def _gla_kernel(qf_ref, kf_ref, vf_ref, qb_ref, kb_ref, vb_ref, lrf_ref, lrb_ref,
                wf_ref, bf_ref, wb_ref, bb_ref, *rest, nchunk, nsub, qscale):
    o_ref, acc_ref, stf_ref, stb_ref = rest[-4:]
    n = pl.program_id(2)
    nb = pl.num_programs(2)
    blk = nchunk * CHUNK
    rows = nsub * blk

    @pl.when(n == 0)
    def _():
        acc_ref[...] = jnp.zeros_like(acc_ref)
        stf_ref[...] = jnp.zeros_like(stf_ref)
        stb_ref[...] = jnp.zeros_like(stb_ref)

    for s_f in range(nsub):
        s_b = nsub - 1 - s_f
        sub_f, sub_b = pl.ds(s_f * blk, blk), pl.ds(s_b * blk, blk)
        o_f = _gla_block(qf_ref.at[sub_f, :], kf_ref.at[sub_f, :], vf_ref.at[sub_f, :], lrf_ref.at[sub_f, :],
                         wf_ref, bf_ref, stf_ref, False, qscale, nchunk)
        o_b = _gla_block(qb_ref.at[sub_b, :], kb_ref.at[sub_b, :], vb_ref.at[sub_b, :], lrb_ref.at[sub_b, :],
                         wb_ref, bb_ref, stb_ref, True, qscale, nchunk)
        acc_ref[pl.ds(pl.multiple_of(n * rows + s_f * blk, blk), blk), :] += o_f
        acc_ref[pl.ds(pl.multiple_of((nb - 1 - n) * rows + s_b * blk, blk), blk), :] += o_b

    @pl.when(n == nb - 1)
    def _():
        o_ref[...] = acc_ref[...].astype(o_ref.dtype)


def _hi_hi_lo(w):
    hi = w.astype(BF16)
    lo = (w - hi.astype(F32)).astype(BF16)
    return jnp.concatenate([hi, hi, lo], axis=0)


def _gla(proj, lr_f, lr_b, wup_f, b_f, wup_b, b_b, row0, B, S, q_blk0, k_blk0, v_blk0, dk, dv,
         prev=None, blk=256, nsub=4):
    T = proj.shape[0]
    nsub = nsub if S % (nsub * blk) == 0 else 1
    rows = nsub * blk
    assert S % rows == 0 and row0 % S == 0
    nb = S // rows
    rb0, sb0 = row0 // rows, row0 // S
    fmap = lambda b, n: rb0 + b * nb + n
    bmap = lambda b, n: rb0 + b * nb + (nb - 1 - n)
    in_specs = [
        pl.BlockSpec((rows, dk), lambda b, h, n: (fmap(b, n), q_blk0 + h)),
        pl.BlockSpec((rows, dk), lambda b, h, n: (fmap(b, n), k_blk0 + h)),
        pl.BlockSpec((rows, dv), lambda b, h, n: (fmap(b, n), v_blk0 + h)),
        pl.BlockSpec((rows, dk), lambda b, h, n: (bmap(b, n), q_blk0 + h)),
        pl.BlockSpec((rows, dk), lambda b, h, n: (bmap(b, n), k_blk0 + h)),
        pl.BlockSpec((rows, dv), lambda b, h, n: (bmap(b, n), v_blk0 + h)),
        pl.BlockSpec((rows, GATE_LOW_RANK), lambda b, h, n: (fmap(b, n), 0)),
        pl.BlockSpec((rows, GATE_LOW_RANK), lambda b, h, n: (bmap(b, n), 0)),
        pl.BlockSpec((3 * GATE_LOW_RANK, dk), lambda b, h, n: (0, h)),
        pl.BlockSpec((1, dk), lambda b, h, n: (0, h)),
        pl.BlockSpec((3 * GATE_LOW_RANK, dk), lambda b, h, n: (0, h)),
        pl.BlockSpec((1, dk), lambda b, h, n: (0, h)),
    ]
    args = [proj, proj, proj, proj, proj, proj, lr_f, lr_b,
            _hi_hi_lo(wup_f), b_f.reshape(1, -1), _hi_hi_lo(wup_b), b_b.reshape(1, -1)]
    aliases = {}
    if prev is not None:
        in_specs.append(pl.BlockSpec(memory_space=pl.ANY))
        args.append(prev)
        aliases = {len(args) - 1: 0}
    return pl.pallas_call(
        functools.partial(_gla_kernel, nchunk=blk // CHUNK, nsub=nsub, qscale=dk ** -0.5),
        grid=(B, GLA_HEADS, nb),
        in_specs=in_specs,
        out_specs=pl.BlockSpec((S, dv), lambda b, h, n: (sb0 + b, h)),
        out_shape=jax.ShapeDtypeStruct((T, GLA_HEADS * dv), BF16),
        scratch_shapes=[pltpu.VMEM((S, dv), F32), pltpu.VMEM((dv, dk), F32), pltpu.VMEM((dv, dk), F32)],
        input_output_aliases=aliases,
        compiler_params=_cparams(("parallel", "parallel", "arbitrary")),
        name="gla",
    )(*args)


def _merge_kernel(fft_ref, o_ref_in, og_ref, g0_ref, g1_ref, hn_ref, wf_ref, wg_ref, o_ref, a_ref, *, dv):
    half = _FFT_COLS // 2
    pieces = []
    for cb in range(fft_ref.shape[1] // half):
        pieces.extend(_unpack_bf16_pair(fft_ref[:, cb * half:(cb + 1) * half]))
    ya = jnp.dot(jnp.concatenate(pieces, axis=1).astype(BF16), wf_ref[...], preferred_element_type=F32)
    for h in range(GLA_HEADS):
        cs = slice(h * dv, (h + 1) * dv)
        o = o_ref_in[:, cs].astype(F32)
        var = jnp.mean(o * o, axis=-1, keepdims=True)
        on = o * lax.rsqrt(var + EPS) * hn_ref[...]
        og = og_ref[:, cs].astype(F32)
        a_ref[:, cs] = (on * (og * jax.nn.sigmoid(og))).astype(BF16)
    yb = jnp.dot(a_ref[...], wg_ref[...], preferred_element_type=F32)
    m = jax.nn.sigmoid(g0_ref[...].astype(F32)) * ya + jax.nn.sigmoid(g1_ref[...].astype(F32)) * yb
    o_ref[...] = m.astype(BF16)


def _merge(fft, o_gla, proj, og_blk, g0_blk, g1_blk, hn, wf, wg, tm=512):
    T, D = o_gla.shape
    FW = wf.shape[0]
    dv = D // GLA_HEADS
    const = dict(pipeline_mode=pl.Buffered(1))
    return pl.pallas_call(
        functools.partial(_merge_kernel, dv=dv),
        grid=(T // tm,),
        in_specs=[
            pl.BlockSpec((tm, FW // 2), lambda i: (i, 0)),
            pl.BlockSpec((tm, D), lambda i: (i, 0)),
            pl.BlockSpec((tm, D), lambda i: (i, og_blk)),
            pl.BlockSpec((tm, D), lambda i: (i, g0_blk)),
            pl.BlockSpec((tm, D), lambda i: (i, g1_blk)),
            pl.BlockSpec((1, dv), lambda i: (0, 0)),
            pl.BlockSpec((FW, D), lambda i: (0, 0), **const),
            pl.BlockSpec((D, D), lambda i: (0, 0), **const),
        ],
        out_specs=pl.BlockSpec((tm, D), lambda i: (i, 0)),
        out_shape=jax.ShapeDtypeStruct((T, D), BF16),
        scratch_shapes=[pltpu.VMEM((tm, D), BF16)],
        compiler_params=_cparams(("parallel",)),
        name="merge",
    )(fft, o_gla, proj, proj, proj, hn.reshape(1, dv), wf, wg)


HI16 = 0xFFFF0000


def _pack_bf16_pair(lo, hi):
    lo_bits = lax.bitcast_convert_type(lo.astype(BF16).astype(F32), jnp.uint32)
    hi_bits = lax.bitcast_convert_type(hi.astype(BF16).astype(F32), jnp.uint32)
    return (hi_bits & jnp.uint32(HI16)) | (lo_bits >> 16)


def _unpack_bf16_pair(w):
    lo = lax.bitcast_convert_type(w << 16, F32)
    hi = lax.bitcast_convert_type(w & jnp.uint32(HI16), F32)
    return lo, hi


def _store_tile_rows(ref, val):
    tm = val.shape[0]
    for s in range(SUBLANES):
        ref[pl.ds(s, tm, stride=SUBLANES), :] = val[:, s * LANES:(s + 1) * LANES]


def _load_tile_rows(ref, tm, s):
    return ref[pl.ds(s, tm, stride=SUBLANES), :]


def _outproj_router_kernel(m_ref, xp_ref, xs_ref, wo_ref, g_ref, wr_ref, br_ref,
                           x1_ref, xn_ref, idx_ref, tw_ref, rank_ref, cnt_ref, run_ref, *, n_exp, n0):
    i = pl.program_id(0)

    @pl.when(i == 0)
    def _():
        run_ref[...] = jnp.zeros_like(run_ref)

    tm, D = m_ref.shape
    x = jnp.where(i < n0, xp_ref[...], xs_ref[...])
    x1 = x + jnp.dot(m_ref[...], wo_ref[...], preferred_element_type=F32)
    x1_ref[...] = x1
    var = jnp.mean(x1 * x1, axis=-1, keepdims=True)
    xn = x1 * lax.rsqrt(var + EPS) * g_ref[...]
    _store_tile_rows(xn_ref, _pack_bf16_pair(xn[:, :D // 2], xn[:, D // 2:]))
    xh = xn.astype(BF16)
    xl = (xn - xh.astype(F32)).astype(BF16)
    r = jnp.dot(xh, wr_ref[...], preferred_element_type=F32)
    lg = (r[:, :n_exp] + r[:, n_exp:] + jnp.dot(xl, wr_ref[:, :n_exp], preferred_element_type=F32)
          + br_ref[...])

    lane = lax.broadcasted_iota(jnp.int32, (tm, n_exp), 1)
    vals, hots = [], []
    for _ in range(TOP_K):
        mx = jnp.max(lg, axis=-1, keepdims=True)
        ik = jnp.min(jnp.where(lg == mx, lane, n_exp), axis=-1, keepdims=True)
        hot = lane == ik
        vals.append(mx)
        hots.append(hot)
        lg = jnp.where(hot, -jnp.inf, lg)
    exps = [jnp.exp(v - vals[0]) for v in vals]
    denom = exps[0] + exps[1] + exps[2] + exps[3]

    sel = hots[0] | hots[1] | hots[2] | hots[3]
    sel_f = sel.astype(F32)
    r = lax.broadcasted_iota(jnp.int32, (tm, tm), 0)
    c = lax.broadcasted_iota(jnp.int32, (tm, tm), 1)
    strict = (c < r).astype(BF16)
    before = jnp.dot(strict, sel_f.astype(BF16), preferred_element_type=F32) + run_ref[...]
    run_ref[...] += jnp.sum(sel_f, axis=0, keepdims=True)
    cnt_ref[...] = run_ref[...].astype(jnp.int32)

    k4 = lax.broadcasted_iota(jnp.int32, (tm, TOP_K), 1)
    idx4 = jnp.zeros((tm, TOP_K), jnp.int32)
    w4 = jnp.zeros((tm, TOP_K), F32)
    rk4 = jnp.zeros((tm, TOP_K), jnp.int32)
    for k in range(TOP_K):
        ik = jnp.sum(jnp.where(hots[k], lane, 0), axis=-1, keepdims=True)
        rk = jnp.sum(jnp.where(hots[k], before, 0.0), axis=-1, keepdims=True).astype(jnp.int32)
        idx4 = jnp.where(k4 == k, ik, idx4)
        w4 = jnp.where(k4 == k, exps[k] / denom, w4)
        rk4 = jnp.where(k4 == k, rk, rk4)
    idx_ref[...] = idx4
    tw_ref[...] = w4
    rank_ref[...] = rk4


def _outproj_router(merged, xp, xs, wo, gain, wr2, br, tm=512):
    T, D = merged.shape
    E = wr2.shape[1] // 2
    n0 = xp.shape[0] // tm
    const = dict(pipeline_mode=pl.Buffered(1))
    row = lambda i: (i, 0)
    fix = lambda i: (0, 0)
    return pl.pallas_call(
        functools.partial(_outproj_router_kernel, n_exp=E, n0=n0),
        grid=(T // tm,),
        in_specs=[
            pl.BlockSpec((tm, D), row),
            pl.BlockSpec((tm, D), lambda i: (jnp.minimum(i, n0 - 1), 0)),
            pl.BlockSpec((tm, D), lambda i: (jnp.maximum(i - n0, 0), 0)),
            pl.BlockSpec((D, D), fix, **const),
            pl.BlockSpec((1, D), fix),
            pl.BlockSpec((D, 2 * E), fix),
            pl.BlockSpec((1, E), fix),
        ],
        out_specs=[
            pl.BlockSpec((tm, D), row),
            pl.BlockSpec((tm * SUBLANES, D // 2 // SUBLANES), row),
            pl.BlockSpec((tm, TOP_K), row),
            pl.BlockSpec((tm, TOP_K), row),
            pl.BlockSpec((tm, TOP_K), row),
            pl.BlockSpec((1, E), fix),
        ],
        out_shape=[
            jax.ShapeDtypeStruct((T, D), F32),
            jax.ShapeDtypeStruct((T * SUBLANES, D // 2 // SUBLANES), jnp.uint32),
            jax.ShapeDtypeStruct((T, TOP_K), jnp.int32),
            jax.ShapeDtypeStruct((T, TOP_K), F32),
            jax.ShapeDtypeStruct((T, TOP_K), jnp.int32),
            jax.ShapeDtypeStruct((1, E), jnp.int32),
        ],
        scratch_shapes=[pltpu.VMEM((1, E), F32)],
        compiler_params=_cparams(("arbitrary",)),
        name="outproj_router",
    )(merged, xp, xs, wo, gain.reshape(1, D), wr2, br.reshape(1, E))


def _dispatch_kernel(pos_ref, x_ref, xs_ref, dst_ref, sem, *, n_tok):
    i = pl.program_id(0)
    tm = x_ref.shape[0] // SUBLANES

    def body(t, carry):
        for k in range(TOP_K):
            p = pos_ref[t * TOP_K + k]
            dst_ref[p] = k * n_tok + i * tm + t
            pltpu.make_async_copy(x_ref.at[pl.ds(pl.multiple_of(t * SUBLANES, SUBLANES), SUBLANES), :],
                                  xs_ref.at[pl.ds(pl.multiple_of(p * SUBLANES, SUBLANES), SUBLANES), :],
                                  sem).start(priority=k % 2)
        return carry

    lax.fori_loop(0, tm, body, 0)
    for _ in range(TOP_K):
        pltpu.make_async_copy(x_ref, xs_ref.at[pl.ds(0, tm * SUBLANES), :], sem).wait()


def _dispatch(xn, pos_flat, n_slots, tm=1024):
    T = xn.shape[0] // SUBLANES
    return pl.pallas_call(
        functools.partial(_dispatch_kernel, n_tok=T),
        grid=(T // tm,),
        in_specs=[
            pl.BlockSpec((tm * TOP_K,), lambda i: (i,), memory_space=pltpu.SMEM),
            pl.BlockSpec((tm * SUBLANES, LANES), lambda i: (i, 0)),
        ],
        out_specs=[
            pl.BlockSpec(memory_space=pl.ANY),
            pl.BlockSpec((n_slots,), lambda i: (0,), memory_space=pltpu.SMEM),
        ],
        out_shape=[jax.ShapeDtypeStruct((n_slots * SUBLANES, LANES), xn.dtype),
                   jax.ShapeDtypeStruct((n_slots,), jnp.int32)],
        scratch_shapes=[pltpu.SemaphoreType.DMA(())],
        compiler_params=_cparams(("arbitrary",)),
        name="dispatch",
    )(pos_flat, xn)


def _combine_kernel(tw_ref, x1_ref, g_ref, *rest, n0):
    y_refs, (op_ref, os_ref) = rest[:TOP_K], rest[TOP_K:]
    i = pl.program_id(0)
    tw = tw_ref[...]
    tm = x1_ref.shape[0]
    lo_acc = [None] * SUBLANES
    hi_acc = [None] * SUBLANES
    for k in range(TOP_K):
        wk = tw[:, k:k + 1]
        for s in range(SUBLANES):
            lo, hi = _unpack_bf16_pair(_load_tile_rows(y_refs[k], tm, s))
            lo_acc[s] = wk * lo if k == 0 else lo_acc[s] + wk * lo
            hi_acc[s] = wk * hi if k == 0 else hi_acc[s] + wk * hi
    x2 = x1_ref[...] + jnp.concatenate(lo_acc + hi_acc, axis=1)
    var = jnp.mean(x2 * x2, axis=-1, keepdims=True)
    y = x2 * lax.rsqrt(var + EPS) * g_ref[...]

    @pl.when(i < n0)
    def _():
        op_ref[...] = y

    @pl.when(i >= n0)
    def _():
        os_ref[...] = y


def _combine(tw, x1, gain, yk, t_prompt, tm=256):
    T, D = x1.shape
    n0 = t_prompt // tm
    nblk = T // tm
    y_spec = lambda k: pl.BlockSpec((tm * SUBLANES, LANES), lambda i: (k * nblk + i, 0))
    return pl.pallas_call(
        functools.partial(_combine_kernel, n0=n0),
        grid=(T // tm,),
        in_specs=[
            pl.BlockSpec((tm, TOP_K), lambda i: (i, 0)),
            pl.BlockSpec((tm, D), lambda i: (i, 0)),
            pl.BlockSpec((1, D), lambda i: (0, 0)),
        ] + [y_spec(k) for k in range(TOP_K)],
        out_specs=[
            pl.BlockSpec((tm, D), lambda i: (jnp.minimum(i, n0 - 1), 0)),
            pl.BlockSpec((tm, D), lambda i: (jnp.maximum(i - n0, 0), 0)),
        ],
        out_shape=[jax.ShapeDtypeStruct((t_prompt, D), F32), jax.ShapeDtypeStruct((T - t_prompt, D), F32)],
        compiler_params=_cparams(("arbitrary",)),
        name="combine",
    )(tw, x1, gain.reshape(1, D), *([yk] * TOP_K))


def _gate_up_kernel(g0_ref, nt_ref, cnt_ref, w_ref, bg_ref, bu_ref, x_hbm, o_hbm,
                    wp_ref, xin_ref, xb_ref, obuf_ref, sin, sout, *, tn):
    j, e = pl.program_id(0), pl.program_id(1)
    tm = GROUP_TILE
    D = xb_ref.shape[1]
    half = MXU_DIM // 2
    n = nt_ref[e]
    row_base = g0_ref[e]
    cnt = cnt_ref[e]

    def in_copy(r, slot):
        rows = pl.ds(pl.multiple_of((row_base + r * tm) * SUBLANES, tm * SUBLANES), tm * SUBLANES)
        return pltpu.make_async_copy(x_hbm.at[rows, :], xin_ref.at[slot], sin.at[slot])

    def out_copy(r, slot):
        rows = pl.ds(pl.multiple_of(row_base + r * tm, tm), tm)
        cols = pl.ds(pl.multiple_of(j * (tn // 2), LANES), tn // 2)
        return pltpu.make_async_copy(obuf_ref.at[slot], o_hbm.at[rows, cols], sout.at[slot])

    def body(r, carry):
        slot = r % 2

        @pl.when(r + 1 < n)
        def _():
            in_copy(r + 1, 1 - slot).start(priority=1)

        in_copy(r, slot).wait()

        @pl.when(r >= 2)
        def _():
            out_copy(r - 2, slot).wait()

        valid = (r * tm + lax.broadcasted_iota(jnp.int32, (tm, 1), 0)) < cnt
        for s in range(SUBLANES):
            lo, hi = _unpack_bf16_pair(_load_tile_rows(xin_ref.at[slot], tm, s))
            xb_ref[:, s * LANES:(s + 1) * LANES] = jnp.where(valid, lo, 0.0).astype(BF16)
            xb_ref[:, D // 2 + s * LANES:D // 2 + (s + 1) * LANES] = jnp.where(valid, hi, 0.0).astype(BF16)
        for cb in range(tn // MXU_DIM):
            h = jnp.dot(xb_ref[...], wp_ref[:, cb * MXU_DIM:(cb + 1) * MXU_DIM], preferred_element_type=F32)
            hg = h[:, :half] + bg_ref[0, :, cb * half:(cb + 1) * half]
            hu = h[:, half:] + bu_ref[0, :, cb * half:(cb + 1) * half]
            gate = jnp.minimum(hg, SWIGLU_LIMIT)
            up = jnp.clip(hu, -SWIGLU_LIMIT, SWIGLU_LIMIT)
            act = gate * jax.nn.sigmoid(SWIGLU_ALPHA * gate) * (up + 1.0)
            obuf_ref[slot, :, cb * half:(cb + 1) * half] = act.astype(obuf_ref.dtype)
        out_copy(r, slot).start()
        return carry

    @pl.when(n > 0)
    def _():
        in_copy(0, 0).start(priority=1)
        r_i = lax.broadcasted_iota(jnp.int32, (MXU_DIM, MXU_DIM), 0)
        c_i = lax.broadcasted_iota(jnp.int32, (MXU_DIM, MXU_DIM), 1)
        perm = (((c_i < half) & (r_i == 2 * c_i)) | ((c_i >= half) & (r_i == 2 * (c_i - half) + 1))).astype(BF16)
        for cb in range(tn // MXU_DIM):
            for rb in range(D // 512):
                w = w_ref[0, rb * 512:(rb + 1) * 512, cb * MXU_DIM:(cb + 1) * MXU_DIM].astype(BF16)
                wp_ref[rb * 512:(rb + 1) * 512, cb * MXU_DIM:(cb + 1) * MXU_DIM] = jnp.dot(
                    w, perm, preferred_element_type=F32).astype(BF16)
        lax.fori_loop(0, n, body, 0)

        @pl.when(n >= 2)
        def _():
            out_copy(n - 2, n % 2).wait()

        out_copy(n - 1, (n - 1) % 2).wait()


def _gate_up(g0, ntiles, cnt, xs, w_gate_up, bg, bu, tn):
    P = xs.shape[0] // SUBLANES
    E, D, H2 = w_gate_up.shape
    tm = GROUP_TILE
    wmap = lambda j, e, *_: (e, 0, j)
    gs = pltpu.PrefetchScalarGridSpec(
        num_scalar_prefetch=3,
        grid=(H2 // tn, E),
        in_specs=[
            pl.BlockSpec((1, D, tn), wmap),
            pl.BlockSpec((1, 1, tn // 2), wmap),
            pl.BlockSpec((1, 1, tn // 2), wmap),
            pl.BlockSpec(memory_space=pl.ANY),
        ],
        out_specs=pl.BlockSpec(memory_space=pl.ANY),
        scratch_shapes=[
            pltpu.VMEM((D, tn), BF16),
            pltpu.VMEM((2, tm * SUBLANES, LANES), jnp.uint32),
            pltpu.VMEM((tm, D), BF16),
            pltpu.VMEM((2, tm, tn // 2), BF16),
            pltpu.SemaphoreType.DMA((2,)),
            pltpu.SemaphoreType.DMA((2,)),
        ],
    )
    return pl.pallas_call(
        functools.partial(_gate_up_kernel, tn=tn),
        grid_spec=gs,
        out_shape=jax.ShapeDtypeStruct((P, H2 // 2), BF16),
        compiler_params=_cparams(("arbitrary", "arbitrary")),
        name="moe_gate_up",
    )(g0, ntiles, cnt, w_gate_up, bg, bu, xs)


def _down_kernel(g0_ref, nt_ref, cnt_ref, dst_ref, w_ref, b_ref, a_hbm, y_hbm, wb_ref, ain_ref, obuf_ref, sin, sout):
    e = pl.program_id(0)
    tm = GROUP_TILE
    H, D = wb_ref.shape
    n = nt_ref[e]
    row_base = g0_ref[e]
    cnt = cnt_ref[e]
    n_real = y_hbm.shape[0] // SUBLANES - dst_ref.shape[0]

    def in_copy(r, slot):
        rows = pl.ds(pl.multiple_of(row_base + r * tm, tm), tm)
        return pltpu.make_async_copy(a_hbm.at[rows, :], ain_ref.at[slot], sin.at[slot])

    nblk = (D // 2) // MXU_DIM

    def compute(slot, out_ref, before_block=None):
        a = ain_ref[slot]
        for bi, c0 in enumerate(range(0, D // 2, MXU_DIM)):
            if before_block is not None:
                before_block(bi)
            c1 = D // 2 + c0
            lo = jnp.dot(a, wb_ref[:, c0:c0 + MXU_DIM], preferred_element_type=F32) + b_ref[0, :, c0:c0 + MXU_DIM]
            hi = jnp.dot(a, wb_ref[:, c1:c1 + MXU_DIM], preferred_element_type=F32) + b_ref[0, :, c1:c1 + MXU_DIM]
            packed = _pack_bf16_pair(lo, hi)
            for u in range(MXU_DIM // LANES):
                out_ref[pl.ds(c0 // LANES + u, tm, stride=SUBLANES), :] = packed[:, u * LANES:(u + 1) * LANES]

    def scatter_rows(q, so, part):
        base = row_base + q * tm
        for i in range(part * (tm // nblk), (part + 1) * (tm // nblk)):
            d = jnp.where(q * tm + i < cnt, dst_ref[base + i], n_real + base + i)
            pltpu.make_async_copy(obuf_ref.at[so, pl.ds(i * SUBLANES, SUBLANES), :],
                                  y_hbm.at[pl.ds(pl.multiple_of(d * SUBLANES, SUBLANES), SUBLANES), :],
                                  sout.at[so]).start(priority=i % 2)

    def scatter_tile(q, so):
        for part in range(nblk):
            scatter_rows(q, so, part)

    def wait_scatter(so):
        pltpu.make_async_copy(obuf_ref.at[so], y_hbm.at[pl.ds(0, tm * SUBLANES), :], sout.at[so]).wait()

    def body(r, carry):
        slot = r % 2

        @pl.when(r + 1 < n)
        def _():
            in_copy(r + 1, 1 - slot).start(priority=1)

        in_copy(r, slot).wait()

        @pl.when(r >= 2)
        def _():
            wait_scatter(slot)

        compute(slot, obuf_ref.at[slot], functools.partial(scatter_rows, r - 1, 1 - slot))
        return carry

    @pl.when(n > 0)
    def _():
        in_copy(0, 0).start(priority=1)
        for rb in range(H // 512):
            wb_ref[rb * 512:(rb + 1) * 512, :] = w_ref[0, rb * 512:(rb + 1) * 512, :].astype(BF16)

        @pl.when(n > 1)
        def _():
            in_copy(1, 1).start(priority=1)

        in_copy(0, 0).wait()
        compute(0, obuf_ref.at[0])
        lax.fori_loop(1, n, body, 0)
        so = (n - 1) % 2
        scatter_tile(n - 1, so)
        wait_scatter(so)

        @pl.when(n >= 2)
        def _():
            wait_scatter(1 - so)


def _down(g0, ntiles, cnt, slot_dst, n_rows, act, wd, bd):
    P, H = act.shape
    E, _, D = wd.shape
    tm = GROUP_TILE
    wmap = lambda e, *_: (e, 0, 0)
    gs = pltpu.PrefetchScalarGridSpec(
        num_scalar_prefetch=4,
        grid=(E,),
        in_specs=[
            pl.BlockSpec((1, H, D), wmap),
            pl.BlockSpec((1, 1, D), wmap),
            pl.BlockSpec(memory_space=pl.ANY),
        ],
        out_specs=pl.BlockSpec(memory_space=pl.ANY),
        scratch_shapes=[
            pltpu.VMEM((H, D), BF16),
            pltpu.VMEM((2, tm, H), BF16),
            pltpu.VMEM((2, tm * SUBLANES, LANES), jnp.uint32),
            pltpu.SemaphoreType.DMA((2,)),
            pltpu.SemaphoreType.DMA((2,)),
        ],
    )
    return pl.pallas_call(
        _down_kernel,
        grid_spec=gs,
        out_shape=jax.ShapeDtypeStruct((n_rows * SUBLANES, LANES), jnp.uint32),
        compiler_params=_cparams(("arbitrary",)),
        name="moe_down",
    )(g0, ntiles, cnt, slot_dst, wd, bd, act)


def _trunk(xp, xs, seq_shapes, norm_mix, w_in, w_gk_up_fwd, b_gk_fwd, w_gk_up_bwd, b_gk_bwd, gla_head_norm,
           w_fnet_out, w_gla_out, w_out, norm_ffn, w_router, b_router, w_gate_up, b_gate_up,
           w_down, b_down, norm_final):
    D = xp.shape[1]
    T = xp.shape[0] + xs.shape[0]
    fw = w_fnet_out.shape[0]
    dkk = w_gk_up_fwd.shape[1]
    dvv = w_gla_out.shape[0]
    dk, dv = dkk // GLA_HEADS, dvv // GLA_HEADS
    sizes = (fw, dkk, dkk, dvv, dvv, GATE_LOW_RANK, GATE_LOW_RANK, 2 * D)
    offs = np.concatenate([[0], np.cumsum(sizes)])
    span = lambda n: (int(offs[n]), int(offs[n + 1]))
    w_main, w_lr2 = _repack_w_in(w_in, tuple(span(n) for n in (4, 7, 3, 0, 1, 2)), (int(offs[5]), int(offs[7])))
    og_blk, g0_blk, g1_blk = 0, dvv // D, dvv // D + 1
    v_off = dvv + 2 * D
    u_off = v_off + dvv
    q_off = u_off + fw
    k_off = q_off + dkk

    proj, lr = _inproj(xp, xs, norm_mix, w_main, w_lr2)
    lr_f, lr_b = lr[:, :GATE_LOW_RANK], lr[:, GATE_LOW_RANK:]

    gd = fw // FNET_GROUPS
    cc, sc = _dft_mats(gd, gd ** -0.5)
    cs = jnp.concatenate([cc, sc], axis=1).astype(BF16)
    z = _chan_dft(proj, u_off // fw, fw, cs)
    fft, o_gla = None, None
    row0 = 0
    for (B, S) in seq_shapes:
        fft = _seq_dft(z, row0, B, S, prev=fft)
        o_gla = _gla(proj, lr_f, lr_b, w_gk_up_fwd, b_gk_fwd, w_gk_up_bwd, b_gk_bwd, row0, B, S,
                     q_off // dk, k_off // dk, v_off // dv, dk, dv, prev=o_gla)
        row0 += B * S

    merged = _merge(fft, o_gla, proj, og_blk, g0_blk, g1_blk, gla_head_norm,
                    w_fnet_out.astype(BF16), w_gla_out.astype(BF16))
    x1, xn2, idx, tw, rank, cnt = _outproj_router(merged, xp, xs, w_out.astype(BF16), norm_ffn,
                                                  _hi_lo(w_router), b_router)

    E = w_router.shape[1]
    cnt = cnt.reshape(E)
    gsz = ((cnt + GROUP_TILE - 1) // GROUP_TILE) * GROUP_TILE
    gend = jnp.cumsum(gsz)
    gstart = gend - gsz
    pos = (gstart[idx] + rank).reshape(-1).astype(jnp.int32)
    n_slots = T * TOP_K + E * GROUP_TILE
    g0 = gstart.astype(jnp.int32)
    ntiles = (gsz // GROUP_TILE).astype(jnp.int32)

    x_sorted, slot_dst = _dispatch(xn2, pos, n_slots)
    H = w_down.shape[1]
    bg = b_gate_up[:, 0::2].reshape(E, 1, H)
    bu = b_gate_up[:, 1::2].reshape(E, 1, H)
    act = _gate_up(g0, ntiles, cnt, x_sorted, w_gate_up, bg, bu, GATE_UP_TN)
    yk = _down(g0, ntiles, cnt, slot_dst, T * TOP_K + n_slots, act, w_down, b_down.reshape(E, 1, D))
    return _combine(tw, x1, norm_final, yk, xp.shape[0])


def kernel(x_prompt, x_sample, norm_mix, w_in, w_gk_up_fwd, b_gk_fwd, w_gk_up_bwd, b_gk_bwd, gla_head_norm,
           w_fnet_out, w_gla_out, w_out, norm_ffn, w_router, b_router, w_gate_up, b_gate_up, w_down,
           b_down, norm_final):
    D = x_prompt.shape[-1]
    shapes = (x_prompt.shape[:2], x_sample.shape[:2])
    yp, ys = _trunk(x_prompt.reshape(-1, D), x_sample.reshape(-1, D), shapes, norm_mix[0], w_in[0], w_gk_up_fwd[0], b_gk_fwd[0], w_gk_up_bwd[0], b_gk_bwd[0],
               gla_head_norm[0], w_fnet_out[0], w_gla_out[0], w_out[0], norm_ffn[0], w_router[0],
               b_router[0], w_gate_up[0], b_gate_up[0], w_down[0], b_down[0], norm_final)
    return (yp.reshape(x_prompt.shape), ys.reshape(x_sample.shape))
```
